```python
import math
import jax
import jax.numpy as jnp
from jax import lax
import numpy as np

D_MODEL = 1024
BATCH = 4
SEQ = 8192
DEPTH = 2
DEC_BATCH = 32
DEC_SEQ = 4
PAST_LEN = 16384
PAGE_SIZE = 128

N_EVEN = (DEPTH + 1) // 2
N_ODD = DEPTH // 2
SSD_HEADS = 16
SSD_HEAD_DIM = 64
SSD_INNER = SSD_HEADS * SSD_HEAD_DIM
SSD_GROUPS = 4
SSD_STATE = 64
SSD_CONV = 4
SSD_CONV_DIM = SSD_INNER + 2 * SSD_GROUPS * SSD_STATE
SSD_CHUNK = 128
NSA_HEADS = 16
NSA_KV_HEADS = 4
NSA_REP = NSA_HEADS // NSA_KV_HEADS
HEAD_DIM = 64
NSA_BLOCK = 64
NSA_TOPN = 16
NSA_WINDOW = 512
NSA_QBLOCK = 64
ROPE_THETA = 10000.0
S5_CH = 512
S5_GROUP_CH = 16
S5_GROUPS = S5_CH // S5_GROUP_CH
S5_STATE = 64
GLA_HEADS = 4
GLA_DK = 64
GLA_DV = 128
GLA_RANK = 16
GLA_TEMP = 16.0
GLA_CHUNK = 64
MOE_GROUPS = 4
MOE_PER_GROUP = 8
MOE_EXPERTS = MOE_GROUPS * MOE_PER_GROUP
MOE_TOPK = 2
MOE_FF = 256
MOE_BLOCK = 128
EPS = 1e-6
NEG = -1e30
BIG = 1e30
EV_SIZES = (SSD_INNER, SSD_CONV_DIM, SSD_HEADS, NSA_HEADS * HEAD_DIM, 6 * NSA_KV_HEADS * HEAD_DIM, 3 * NSA_HEADS)
EV_IN = sum(EV_SIZES)
EV_MIX = SSD_INNER + NSA_HEADS * HEAD_DIM
OD_SIZES = (S5_CH, GLA_HEADS * GLA_DK, GLA_HEADS * GLA_DK, GLA_HEADS * GLA_DV, GLA_HEADS * GLA_DV, GLA_RANK)
OD_IN = sum(OD_SIZES)
OD_MIX = S5_CH + GLA_HEADS * GLA_DV
EVEN_STATES = ('cmp_k', 'cmp_v', 'sel_k', 'sel_v', 'win_k', 'win_v', 'ssd', 'conv')
ODD_STATES = ('s5_re', 's5_im', 'gla')

kernel_name = 'hybrid_ssd_nsa_s5_gla_hmoe_step'


def rmsnorm(x, g):
    xf = x.astype(jnp.float32)
    y = xf * lax.rsqrt(jnp.mean(xf * xf, axis=-1, keepdims=True) + EPS)
    return (y * g.astype(jnp.float32)).astype(x.dtype)


def split_cols(x, sizes):
    out, o = [], 0
    for s in sizes:
        out.append(x[..., o:o + s])
        o += s
    return out


def ada_mod(c, w, b):
    m = (jax.nn.silu(c) @ w + b)[:, None, :]
    return jnp.split(m, 6, axis=-1)


def modulate(x, g, shift, scale):
    return rmsnorm(x, g) * (1.0 + scale) + shift


def rope(x, pos):
    half = HEAD_DIM // 2
    inv = ROPE_THETA ** (-jnp.arange(half, dtype=jnp.float32) / half)
    ang = pos.astype(jnp.float32)[:, None] * inv[None, :]
    cos, sin = jnp.cos(ang)[None, :, None, :], jnp.sin(ang)[None, :, None, :]
    xf = x.astype(jnp.float32)
    x1, x2 = xf[..., :half], xf[..., half:]
    return jnp.concatenate([x1 * cos - x2 * sin, x2 * cos + x1 * sin], axis=-1).astype(x.dtype)


def causal_conv(x, buf, w, b):
    T = x.shape[1]
    xp = jnp.concatenate([buf.astype(x.dtype), x], axis=1)
    y = b + sum(w[k] * xp[:, k:k + T] for k in range(SSD_CONV))
    return y, xp[:, xp.shape[1] - (SSD_CONV - 1):]


def ssd_scan(x, dt, a, bm, cm, h0):
    f32 = jnp.float32
    Bsz, T, H, P = x.shape
    L = SSD_CHUNK if T % SSD_CHUNK == 0 else T
    nc = T // L
    rep = H // SSD_GROUPS
    causal = jnp.tril(jnp.ones((L, L), dtype=bool))[None, :, :, None]

    def chunks(u):
        return jnp.swapaxes(u.reshape((Bsz, nc, L) + u.shape[2:]), 0, 1)

    def step(h, inp):
        xc, dtc, bc, cc = inp
        xc = xc.astype(f32)
        bh = jnp.repeat(bc.astype(f32), rep, axis=2)
        ch = jnp.repeat(cc.astype(f32), rep, axis=2)
        cs = jnp.cumsum(dtc * a, axis=1)
        seg = cs[:, :, None, :] - cs[:, None, :, :]
        decay = jnp.where(causal, jnp.exp(jnp.where(causal, seg, 0.0)), 0.0)
        scores = jnp.einsum('blhn,bshn->blsh', ch, bh) * decay * dtc[:, None, :, :]
        y = jnp.einsum('blsh,bshp->blhp', scores, xc)
        y = y + jnp.einsum('blhn,bhpn->blhp', ch, h) * jnp.exp(cs)[..., None]
        w_end = jnp.exp(cs[:, -1:, :] - cs) * dtc
        h = h * jnp.exp(cs[:, -1, :])[:, :, None, None] + jnp.einsum('bsh,bshp,bshn->bhpn', w_end, xc, bh)
        return h, y

    h, ys = lax.scan(step, h0.astype(f32), (chunks(x), chunks(dt), chunks(bm), chunks(cm)))
    return jnp.swapaxes(ys, 0, 1).reshape(Bsz, T, H, P), h


def mamba2_branch(z, xbc, dt_raw, conv_buf, h0, p):
    Bsz, T = z.shape[:2]
    xbc, conv_new = causal_conv(xbc, conv_buf, p['conv_w'], p['conv_b'])
    xbc = jax.nn.silu(xbc)
    gn = SSD_GROUPS * SSD_STATE
    xs = xbc[..., :SSD_INNER].reshape(Bsz, T, SSD_HEADS, SSD_HEAD_DIM)
    bm = xbc[..., SSD_INNER:SSD_INNER + gn].reshape(Bsz, T, SSD_GROUPS, SSD_STATE)
    cm = xbc[..., SSD_INNER + gn:].reshape(Bsz, T, SSD_GROUPS, SSD_STATE)
    dt = jax.nn.softplus((dt_raw + p['dt_bias']).astype(jnp.float32))
    a = -jnp.exp(p['a_log'].astype(jnp.float32))
    y, h_new = ssd_scan(xs, dt, a, bm, cm, h0)
    y = y + p['d_skip'].astype(jnp.float32)[:, None] * xs.astype(jnp.float32)
    y = y.reshape(Bsz, T, SSD_INNER) * jax.nn.silu(z.astype(jnp.float32))
    y = rmsnorm(y.reshape(Bsz, T, SSD_GROUPS, SSD_INNER // SSD_GROUPS), p['ssd_norm'].reshape(SSD_GROUPS, -1))
    return y.reshape(Bsz, T, SSD_INNER).astype(z.dtype), conv_new, h_new


def compress_blocks(rows, w):
    Bsz, T = rows.shape[:2]
    nb = -(-T // NSA_BLOCK)
    r = jnp.pad(rows, ((0, 0), (0, nb * NSA_BLOCK - T), (0, 0), (0, 0)))
    r = r.reshape(Bsz, nb, NSA_BLOCK, NSA_KV_HEADS, HEAD_DIM)
    return jnp.einsum('bnigd,id->bngd', r, w)


def nsa_core(q_r, q_c, t_pos, kc, vc, fetch_sel, kw, vw, kw_pos, gates):
    f32 = jnp.float32
    Bsz, Q = q_r.shape[:2]
    nb = kc.shape[1]
    scale = HEAD_DIM ** -0.5
    qr = q_r.reshape(Bsz, Q, NSA_KV_HEADS, NSA_REP, HEAD_DIM)
    qc = q_c.reshape(Bsz, Q, NSA_KV_HEADS, NSA_REP, HEAD_DIM)
    blocks = jnp.arange(nb)
    ok_c = ((blocks + 1) * NSA_BLOCK - 1)[None, :] <= t_pos[:, None]
    s = jnp.einsum('bqgrd,bngd->bgrqn', qc, kc).astype(f32) * scale
    p_c = jax.nn.softmax(jnp.where(ok_c, s, NEG), axis=-1) * ok_c
    o_c = jnp.einsum('bgrqn,bngd->bqgrd', p_c.astype(vc.dtype), vc)
    cur = (t_pos // NSA_BLOCK)[:, None]
    imp = jnp.sum(p_c, axis=2)
    forced = (blocks[None, :] == cur) | (blocks[None, :] == 0)
    imp = jnp.where(forced, BIG, jnp.where(blocks[None, :] < cur, imp, NEG))
    n_sel = min(NSA_TOPN, nb)
    top_v, top_i = lax.top_k(imp, n_sel)
    pos = top_i[..., None] * NSA_BLOCK + jnp.arange(NSA_BLOCK)
    ok_s = (top_v[..., None] > 0.5 * NEG) & (pos <= t_pos[None, None, :, None, None])
    ks, vs = fetch_sel(pos)
    ks = ks.reshape(Bsz, NSA_KV_HEADS, Q, n_sel * NSA_BLOCK, HEAD_DIM)
    vs = vs.reshape(Bsz, NSA_KV_HEADS, Q, n_sel * NSA_BLOCK, HEAD_DIM)
    ok_s = ok_s.reshape(Bsz, NSA_KV_HEADS, 1, Q, n_sel * NSA_BLOCK)
    s = jnp.einsum('bqgrd,bgqkd->bgrqk', qr, ks).astype(f32) * scale
    p_s = jax.nn.softmax(jnp.where(ok_s, s, NEG), axis=-1)
    o_s = jnp.einsum('bgrqk,bgqkd->bqgrd', p_s.astype(vs.dtype), vs)
    kp = kw_pos[None, :]
    ok_w = (kp <= t_pos[:, None]) & (kp > t_pos[:, None] - NSA_WINDOW) & (kp >= 0)
    s = jnp.einsum('bqgrd,bkgd->bgrqk', qr, kw).astype(f32) * scale
    p_w = jax.nn.softmax(jnp.where(ok_w, s, NEG), axis=-1)
    o_w = jnp.einsum('bgrqk,bkgd->bqgrd', p_w.astype(vw.dtype), vw)
    g = gates.reshape(Bsz, Q, NSA_KV_HEADS, NSA_REP, 3)
    o = g[..., 0:1] * o_c + g[..., 1:2] * o_s + g[..., 2:3] * o_w
    return o.reshape(Bsz, Q, NSA_HEADS, HEAD_DIM)


def nsa_prompt(q_r, q_c, kc_rows, vc_rows, ks_rows, vs_rows, kw_rows, vw_rows, gates, cmp_w):
    Bsz, T = q_r.shape[:2]
    kc = compress_blocks(kc_rows, cmp_w[0])
    vc = compress_blocks(vc_rows, cmp_w[1])
    pad = ((0, 0), (NSA_WINDOW, 0), (0, 0), (0, 0))
    kw_pad, vw_pad = jnp.pad(kw_rows, pad), jnp.pad(vw_rows, pad)
    bidx = jnp.arange(Bsz)[:, None, None, None, None]
    gidx = jnp.arange(NSA_KV_HEADS)[None, :, None, None, None]

    def fetch(pos):
        pc = jnp.clip(pos, 0, T - 1)
        return ks_rows[bidx, pc, gidx], vs_rows[bidx, pc, gidx]

    qb = NSA_QBLOCK

    def block(i):
        s0 = i * qb
        sl = lambda u: lax.dynamic_slice_in_dim(u, s0, qb, axis=1)
        kw = lax.dynamic_slice_in_dim(kw_pad, s0, NSA_WINDOW + qb, axis=1)
        vw = lax.dynamic_slice_in_dim(vw_pad, s0, NSA_WINDOW + qb, axis=1)
        kw_pos = s0 - NSA_WINDOW + jnp.arange(NSA_WINDOW + qb)
        t_pos = s0 + jnp.arange(qb)
        return nsa_core(sl(q_r), sl(q_c), t_pos, kc, vc, fetch, kw, vw, kw_pos, sl(gates))

    o = lax.map(block, jnp.arange(T // qb))
    return jnp.moveaxis(o, 0, 1).reshape(Bsz, T, NSA_HEADS, HEAD_DIM)


def nsa_sample(q_r, q_c, kc_rows, vc_rows, ks_rows, vs_rows, kw, vw, kw_pos, gates, cmp_w, t_pos, cache, li):
    Bsz, T = q_r.shape[:2]
    pt = cache['page_table']
    n_pages = pt.shape[1]
    past = n_pages * PAGE_SIZE
    ck = cache['cmp_k'][li, pt].reshape(Bsz, past, NSA_KV_HEADS, HEAD_DIM)
    cv = cache['cmp_v'][li, pt].reshape(Bsz, past, NSA_KV_HEADS, HEAD_DIM)
    kc = jnp.concatenate([compress_blocks(ck, cmp_w[0]), compress_blocks(kc_rows, cmp_w[0])], axis=1)
    vc = jnp.concatenate([compress_blocks(cv, cmp_w[1]), compress_blocks(vc_rows, cmp_w[1])], axis=1)
    bidx = jnp.arange(Bsz)[:, None, None, None, None]
    gidx = jnp.arange(NSA_KV_HEADS)[None, :, None, None, None]

    def fetch(pos):
        phys = pt[bidx, jnp.clip(pos // PAGE_SIZE, 0, n_pages - 1)]
        off = pos % PAGE_SIZE
        new_i = jnp.clip(pos - past, 0, T - 1)
        in_past = (pos < past)[..., None]
        k = jnp.where(in_past, cache['sel_k'][li, phys, off, gidx], ks_rows[bidx, new_i, gidx])
        v = jnp.where(in_past, cache['sel_v'][li, phys, off, gidx], vs_rows[bidx, new_i, gidx])
        return k, v

    return nsa_core(q_r, q_c, t_pos, kc, vc, fetch, kw, vw, kw_pos, gates)


def even_mixer(h, p, pos, conv_buf, ssd_h0, li, cache):
    Bsz, T, _ = h.shape
    z, xbc, dt_raw, q, kv, g = split_cols(h @ p['w_in'], EV_SIZES)
    y_a, conv_new, ssd_new = mamba2_branch(z, xbc, dt_raw, conv_buf, ssd_h0, p)
    q = rmsnorm(q.reshape(Bsz, T, NSA_HEADS, HEAD_DIM), p['q_norm'])
    kv = kv.reshape(Bsz, T, 6, NSA_KV_HEADS, HEAD_DIM)
    k_cmp = rmsnorm(kv[:, :, 0], p['k_norm'][0])
    v_cmp = kv[:, :, 1]
    k_sel = rope(rmsnorm(kv[:, :, 2], p['k_norm'][1]), pos)
    v_sel = kv[:, :, 3]
    k_win = rope(rmsnorm(kv[:, :, 4], p['k_norm'][2]), pos)
    v_win = kv[:, :, 5]
    q_r = rope(q, pos)
    gates = jax.nn.sigmoid(g.reshape(Bsz, T, NSA_HEADS, 3))
    if cache is None:
        o_b = nsa_prompt(q_r, q, k_cmp, v_cmp, k_sel, v_sel, k_win, v_win, gates, p['cmp_w'])
        keep = min(NSA_WINDOW, T)
        wk, wv = k_win[:, T - keep:], v_win[:, T - keep:]
    else:
        wbuf_k, wbuf_v = cache['win_k'][li], cache['win_v'][li]
        wb = wbuf_k.shape[1]
        kw = jnp.concatenate([wbuf_k, k_win], axis=1)
        vw = jnp.concatenate([wbuf_v, v_win], axis=1)
        kw_pos = PAST_LEN - wb + jnp.arange(wb + T)
        o_b = nsa_sample(q_r, q, k_cmp, v_cmp, k_sel, v_sel, kw, vw, kw_pos, gates, p['cmp_w'], pos, cache, li)
        keep = min(NSA_WINDOW, PAST_LEN + T)
        wk, wv = kw[:, wb + T - keep:], vw[:, wb + T - keep:]
    y = jnp.concatenate([y_a, o_b.reshape(Bsz, T, -1)], axis=-1) @ p['w_out']
    return y, (k_cmp, v_cmp, k_sel, v_sel, wk, wv, ssd_new, conv_new)


def complex_affine_combine(e1, e2):
    a1r, a1i, b1r, b1i = e1
    a2r, a2i, b2r, b2i = e2
    return (a2r * a1r - a2i * a1i, a2r * a1i + a2i * a1r,
            a2r * b1r - a2i * b1i + b2r, a2r * b1i + a2i * b1r + b2i)


def s5_branch(u, h0_re, h0_im, p):
    f32 = jnp.float32
    Bsz, T, _ = u.shape
    ug = u.reshape(Bsz, T, S5_GROUPS, S5_GROUP_CH).astype(f32)
    a_re, a_im = p['a_re'].astype(f32), p['a_im'].astype(f32)
    dt = jnp.exp(p['log_dt'].astype(f32))[:, None]
    lr, li = a_re * dt, a_im * dt
    ab_re, ab_im = jnp.exp(lr) * jnp.cos(li), jnp.exp(lr) * jnp.sin(li)
    den = a_re * a_re + a_im * a_im
    nr = ab_re - 1.0
    f_re = (nr * a_re + ab_im * a_im) / den
    f_im = (ab_im * a_re - nr * a_im) / den
    b_re, b_im = p['b_re'].astype(f32), p['b_im'].astype(f32)
    bb_re = f_re[..., None] * b_re - f_im[..., None] * b_im
    bb_im = f_re[..., None] * b_im + f_im[..., None] * b_re
    bu_re = jnp.einsum('gpc,btgc->btgp', bb_re, ug)
    bu_im = jnp.einsum('gpc,btgc->btgp', bb_im, ug)
    shape = bu_re.shape
    elems = (jnp.broadcast_to(ab_re, shape), jnp.broadcast_to(ab_im, shape), bu_re, bu_im)
    _, _, xr, xi = lax.associative_scan(complex_affine_combine, elems, axis=1)
    k = jnp.arange(1, T + 1, dtype=f32)[:, None, None]
    mag, ang = jnp.exp(lr * k), li * k
    pw_re, pw_im = mag * jnp.cos(ang), mag * jnp.sin(ang)
    h0r, h0i = h0_re.astype(f32)[:, None], h0_im.astype(f32)[:, None]
    xr = xr + pw_re * h0r - pw_im * h0i
    xi = xi + pw_re * h0i + pw_im * h0r
    y = (jnp.einsum('gcp,btgp->btgc', p['c_re'].astype(f32), xr)
         - jnp.einsum('gcp,btgp->btgc', p['c_im'].astype(f32), xi)
         + p['d'].astype(f32).reshape(S5_GROUPS, S5_GROUP_CH) * ug)
    y = jax.nn.gelu(y.reshape(Bsz, T, S5_CH))
    y = y * jax.nn.sigmoid(y @ p['glu_w'].astype(f32) + p['glu_b'].astype(f32))
    return y.astype(u.dtype), xr[:, -1], xi[:, -1]


def gla_scan(q, k, v, log_a, s0):
    f32 = jnp.float32
    Bsz, T, H, _ = q.shape
    L = GLA_CHUNK if T % GLA_CHUNK == 0 else T
    nc = T // L
    causal = jnp.tril(jnp.ones((L, L), dtype=bool))[None, :, :, None, None]

    def chunks(u):
        return jnp.swapaxes(u.astype(f32).reshape((Bsz, nc, L) + u.shape[2:]), 0, 1)

    def step(s, inp):
        qc, kc, vc, lac = inp
        b = jnp.cumsum(lac, axis=1)
        rel = jnp.where(causal, jnp.exp(jnp.where(causal, b[:, :, None] - b[:, None], 0.0)), 0.0)
        att = jnp.einsum('bthd,btshd,bshd->bhts', qc, rel, kc)
        o = jnp.einsum('bhts,bshv->bthv', att, vc) + jnp.einsum('bthd,bhdv->bthv', qc * jnp.exp(b), s)
        s = s * jnp.exp(b[:, -1])[..., None] + jnp.einsum('bshd,bshv->bhdv', kc * jnp.exp(b[:, -1:] - b), vc)
        return s, o

    s, outs = lax.scan(step, s0.astype(f32), (chunks(q), chunks(k), chunks(v), chunks(log_a)))
    return jnp.swapaxes(outs, 0, 1).reshape(Bsz, T, H, -1), s


def odd_mixer(h, p, s5_re0, s5_im0, gla0):
    Bsz, T, _ = h.shape
    u, q, k, v, g, alr = split_cols(h @ p['w_in'], OD_SIZES)
    y_c, s5_re, s5_im = s5_branch(u, s5_re0, s5_im0, p)
    shp = (Bsz, T, GLA_HEADS, -1)
    log_a = jax.nn.log_sigmoid((alr @ p['wa2'] + p['ba']).astype(jnp.float32)) / GLA_TEMP
    o, s_new = gla_scan(q.reshape(shp) * GLA_DK ** -0.5, k.reshape(shp), v.reshape(shp), log_a.reshape(shp), gla0)
    o = rmsnorm(o, p['gla_norm']) * jax.nn.silu(g.reshape(shp).astype(jnp.float32))
    y = jnp.concatenate([y_c, o.reshape(Bsz, T, -1).astype(h.dtype)], axis=-1) @ p['w_out']
    return y, (s5_re, s5_im, s_new)


def moe_dispatch(xf, e, w, m):
    n, d = xf.shape
    a = n * MOE_TOPK
    flat_e = e.reshape(a)
    flat_tok = jnp.arange(a, dtype=jnp.int32) // MOE_TOPK
    flat_w = w.reshape(a)
    order = jnp.argsort(flat_e)
    se, stok, sw = flat_e[order], flat_tok[order], flat_w[order]
    counts = jnp.zeros((MOE_EXPERTS,), jnp.int32).at[flat_e].add(1)
    starts = jnp.cumsum(counts) - counts
    pcounts = (counts + MOE_BLOCK - 1) // MOE_BLOCK * MOE_BLOCK
    pends = jnp.cumsum(pcounts)
    pstarts = pends - pcounts
    dest = pstarts[se] + jnp.arange(a, dtype=jnp.int32) - starts[se]
    nblk = -(-a // MOE_BLOCK) + MOE_EXPERTS
    tok_buf = jnp.full((nblk * MOE_BLOCK,), n, jnp.int32).at[dest].set(stok)
    w_buf = jnp.zeros((nblk * MOE_BLOCK,), xf.dtype).at[dest].set(sw)
    blk_e = jnp.minimum(jnp.searchsorted(pends, jnp.arange(nblk, dtype=jnp.int32) * MOE_BLOCK, side='right'), MOE_EXPERTS - 1)
    x_pad = jnp.concatenate([xf, jnp.zeros((1, d), xf.dtype)], axis=0)

    def run_block(bi):
        ids = lax.dynamic_slice_in_dim(tok_buf, bi * MOE_BLOCK, MOE_BLOCK)
        wb = lax.dynamic_slice_in_dim(w_buf, bi * MOE_BLOCK, MOE_BLOCK)
        ex = blk_e[bi]
        xb = x_pad[ids]
        hid = jax.nn.silu(xb @ m['w1'][ex]) * (xb @ m['w3'][ex])
        return (hid @ m['w2'][ex]) * wb[:, None]

    ys = lax.map(run_block, jnp.arange(nblk))
    out = jnp.zeros((n + 1, d), xf.dtype).at[tok_buf].add(ys.reshape(-1, d))
    return out[:n]


def hier_moe(h, m):
    Bsz, T, D = h.shape
    xf = h.reshape(Bsz * T, D)
    n = xf.shape[0]
    f32 = jnp.float32
    lc = (xf @ m['wc'] + m['bc']).astype(f32)
    gsel = jnp.argmax(lc, axis=-1).astype(jnp.int32)
    gc = jnp.take_along_axis(jax.nn.softmax(lc, axis=-1), gsel[:, None], axis=-1)
    lf = (xf @ m['wf'] + m['bf']).astype(f32).reshape(n, MOE_GROUPS, MOE_PER_GROUP)
    lf = jnp.take_along_axis(lf, gsel[:, None, None], axis=1)[:, 0]
    tv, ti = lax.top_k(jax.nn.softmax(lf, axis=-1), MOE_TOPK)
    w = (gc * tv / jnp.sum(tv, axis=-1, keepdims=True)).astype(h.dtype)
    e = gsel[:, None] * MOE_PER_GROUP + ti.astype(jnp.int32)
    return moe_dispatch(xf, e, w, m).reshape(Bsz, T, D)


def setup_inputs(seed: int = 0) -> dict:
    key = jax.random.key(seed)
    keys = iter(jax.random.split(key, 96))
    f32 = jnp.float32

    def nrm(shape, scale):
        return scale * jax.random.normal(next(keys), shape, f32)

    def unif(shape, lo, hi):
        return jax.random.uniform(next(keys), shape, f32, lo, hi)

    n_pages = PAST_LEN // PAGE_SIZE
    n_used = DEC_BATCH * n_pages
    n_phys = n_used + n_used // 4
    page_table = jax.random.permutation(next(keys), n_phys)[:n_used].reshape(DEC_BATCH, n_pages).astype(jnp.int32)
    pool = (N_EVEN, n_phys, PAGE_SIZE, NSA_KV_HEADS, HEAD_DIM)
    wb = min(NSA_WINDOW, PAST_LEN)
    win = (N_EVEN, DEC_BATCH, wb, NSA_KV_HEADS, HEAD_DIM)
    dt0 = jnp.exp(unif((N_EVEN, SSD_HEADS), math.log(1e-3), math.log(1e-1)))
    d = D_MODEL
    return {
        'x_prompt': nrm((BATCH, SEQ, d), 1.0),
        'x_sample': nrm((DEC_BATCH, DEC_SEQ, d), 1.0),
        'cache_cmp_k': nrm(pool, 1.0),
        'cache_cmp_v': nrm(pool, 1.0),
        'cache_sel_k': nrm(pool, 1.0),
        'cache_sel_v': nrm(pool, 1.0),
        'cache_win_k': nrm(win, 1.0),
        'cache_win_v': nrm(win, 1.0),
        'state_ssd': nrm((N_EVEN, DEC_BATCH, SSD_HEADS, SSD_HEAD_DIM, SSD_STATE), 0.1),
        'state_conv': nrm((N_EVEN, DEC_BATCH, SSD_CONV - 1, SSD_CONV_DIM), 1.0),
        'state_s5_re': nrm((N_ODD, DEC_BATCH, S5_GROUPS, S5_STATE), 0.1),
        'state_s5_im': nrm((N_ODD, DEC_BATCH, S5_GROUPS, S5_STATE), 0.1),
        'state_gla': nrm((N_ODD, DEC_BATCH, GLA_HEADS, GLA_DK, GLA_DV), 0.1),
        'page_table': page_table,
        'c_prompt': nrm((BATCH, d), 1.0),
        'c_sample': nrm((DEC_BATCH, d), 1.0),
        'ada_w': nrm((DEPTH, d, 6 * d), 0.5 * d ** -0.5),
        'ada_b': nrm((DEPTH, 6 * d), 0.02),
        'norm_mix': 1.0 + nrm((DEPTH, d), 0.02),
        'norm_ffn': 1.0 + nrm((DEPTH, d), 0.02),
        'ev_w_in': nrm((N_EVEN, d, EV_IN), d ** -0.5),
        'ev_w_out': nrm((N_EVEN, EV_MIX, d), EV_MIX ** -0.5),
        'ev_conv_w': nrm((N_EVEN, SSD_CONV, SSD_CONV_DIM), 0.5),
        'ev_conv_b': nrm((N_EVEN, SSD_CONV_DIM), 0.02),
        'ev_dt_bias': dt0 + jnp.log(-jnp.expm1(-dt0)),
        'ev_a_log': jnp.log(unif((N_EVEN, SSD_HEADS), 1.0, 16.0)),
        'ev_d_skip': 1.0 + nrm((N_EVEN, SSD_HEADS), 0.1),
        'ev_ssd_norm': 1.0 + nrm((N_EVEN, SSD_INNER), 0.02),
        'ev_q_norm': 1.0 + nrm((N_EVEN, HEAD_DIM), 0.02),
        'ev_k_norm': 1.0 + nrm((N_EVEN, 3, HEAD_DIM), 0.02),
        'ev_cmp_w': (1.0 + nrm((N_EVEN, 2, NSA_BLOCK, HEAD_DIM), 0.1)) * NSA_BLOCK ** -0.5,
        'od_w_in': nrm((N_ODD, d, OD_IN), d ** -0.5),
        'od_w_out': nrm((N_ODD, OD_MIX, d), OD_MIX ** -0.5),
        'od_s5_a_re': -0.5 + nrm((N_ODD, S5_GROUPS, S5_STATE), 0.01),
        'od_s5_a_im': math.pi * jnp.arange(S5_STATE, dtype=f32) + nrm((N_ODD, S5_GROUPS, S5_STATE), 0.01),
        'od_s5_log_dt': unif((N_ODD, S5_GROUPS), math.log(1e-3), math.log(1e-1)),
        'od_s5_b_re': nrm((N_ODD, S5_GROUPS, S5_STATE, S5_GROUP_CH), (2 * S5_GROUP_CH) ** -0.5),
        'od_s5_b_im': nrm((N_ODD, S5_GROUPS, S5_STATE, S5_GROUP_CH), (2 * S5_GROUP_CH) ** -0.5),
        'od_s5_c_re': nrm((N_ODD, S5_GROUPS, S5_GROUP_CH, S5_STATE), S5_STATE ** -0.5),
        'od_s5_c_im': nrm((N_ODD, S5_GROUPS, S5_GROUP_CH, S5_STATE), S5_STATE ** -0.5),
        'od_s5_d': nrm((N_ODD, S5_CH), 0.5),
        'od_glu_w': nrm((N_ODD, S5_CH, S5_CH), S5_CH ** -0.5),
        'od_glu_b': nrm((N_ODD, S5_CH), 0.02),
        'od_gla_wa2': nrm((N_ODD, GLA_RANK, GLA_HEADS * GLA_DK), GLA_RANK ** -0.5),
        'od_gla_ba': nrm((N_ODD, GLA_HEADS * GLA_DK), 0.1),
        'od_gla_norm': 1.0 + nrm((N_ODD, GLA_DV), 0.02),
        'moe_wc': nrm((DEPTH, d, MOE_GROUPS), d ** -0.5),
        'moe_bc': nrm((DEPTH, MOE_GROUPS), 0.01),
        'moe_wf': nrm((DEPTH, d, MOE_EXPERTS), d ** -0.5),
        'moe_bf': nrm((DEPTH, MOE_EXPERTS), 0.01),
        'moe_w1': nrm((DEPTH, MOE_EXPERTS, d, MOE_FF), d ** -0.5),
        'moe_w3': nrm((DEPTH, MOE_EXPERTS, d, MOE_FF), d ** -0.5),
        'moe_w2': nrm((DEPTH, MOE_EXPERTS, MOE_FF, d), MOE_FF ** -0.5),
    }


def reference(x_prompt, x_sample, cache_cmp_k, cache_cmp_v, cache_sel_k, cache_sel_v, cache_win_k, cache_win_v,
              state_ssd, state_conv, state_s5_re, state_s5_im, state_gla, page_table, c_prompt, c_sample,
              ada_w, ada_b, norm_mix, norm_ffn,
              ev_w_in, ev_w_out, ev_conv_w, ev_conv_b, ev_dt_bias, ev_a_log, ev_d_skip, ev_ssd_norm,
              ev_q_norm, ev_k_norm, ev_cmp_w,
              od_w_in, od_w_out, od_s5_a_re, od_s5_a_im, od_s5_log_dt, od_s5_b_re, od_s5_b_im,
              od_s5_c_re, od_s5_c_im, od_s5_d, od_glu_w, od_glu_b, od_gla_wa2, od_gla_ba, od_gla_norm,
              moe_wc, moe_bc, moe_wf, moe_bf, moe_w1, moe_w3, moe_w2):
    cache = {'cmp_k': cache_cmp_k, 'cmp_v': cache_cmp_v, 'sel_k': cache_sel_k, 'sel_v': cache_sel_v,
             'win_k': cache_win_k, 'win_v': cache_win_v, 'page_table': page_table}
    bp, tp = x_prompt.shape[:2]
    ts = x_sample.shape[1]
    pos_p = jnp.arange(tp)
    pos_s = PAST_LEN + jnp.arange(ts)
    sp = {n: [] for n in EVEN_STATES + ODD_STATES}
    ss = {n: [] for n in EVEN_STATES + ODD_STATES}
    xp, xs = x_prompt, x_sample
    for l in range(DEPTH):
        i = l // 2
        mp = ada_mod(c_prompt, ada_w[l], ada_b[l])
        ms = ada_mod(c_sample, ada_w[l], ada_b[l])
        hp = modulate(xp, norm_mix[l], mp[0], mp[1])
        hs = modulate(xs, norm_mix[l], ms[0], ms[1])
        if l % 2 == 0:
            p = {'w_in': ev_w_in[i], 'w_out': ev_w_out[i], 'conv_w': ev_conv_w[i], 'conv_b': ev_conv_b[i],
                 'dt_bias': ev_dt_bias[i], 'a_log': ev_a_log[i], 'd_skip': ev_d_skip[i],
                 'ssd_norm': ev_ssd_norm[i], 'q_norm': ev_q_norm[i], 'k_norm': ev_k_norm[i], 'cmp_w': ev_cmp_w[i]}
            conv0 = jnp.zeros((bp, SSD_CONV - 1, SSD_CONV_DIM), xp.dtype)
            ssd0 = jnp.zeros((bp, SSD_HEADS, SSD_HEAD_DIM, SSD_STATE), jnp.float32)
            yp, st_p = even_mixer(hp, p, pos_p, conv0, ssd0, i, None)
            ys, st_s = even_mixer(hs, p, pos_s, state_conv[i], state_ssd[i], i, cache)
            names = EVEN_STATES
        else:
            p = {'w_in': od_w_in[i], 'w_out': od_w_out[i], 'a_re': od_s5_a_re[i], 'a_im': od_s5_a_im[i],
                 'log_dt': od_s5_log_dt[i], 'b_re': od_s5_b_re[i], 'b_im': od_s5_b_im[i],
                 'c_re': od_s5_c_re[i], 'c_im': od_s5_c_im[i], 'd': od_s5_d[i], 'glu_w': od_glu_w[i],
                 'glu_b': od_glu_b[i], 'wa2': od_gla_wa2[i], 'ba': od_gla_ba[i], 'gla_norm': od_gla_norm[i]}
            s50 = jnp.zeros((bp, S5_GROUPS, S5_STATE), jnp.float32)
            gla0 = jnp.zeros((bp, GLA_HEADS, GLA_DK, GLA_DV), jnp.float32)
            yp, st_p = odd_mixer(hp, p, s50, s50, gla0)
            ys, st_s = odd_mixer(hs, p, state_s5_re[i], state_s5_im[i], state_gla[i])
            names = ODD_STATES
        for n, a_p, a_s in zip(names, st_p, st_s):
            sp[n].append(a_p.astype(x_prompt.dtype))
            ss[n].append(a_s.astype(x_sample.dtype))
        xp = xp + mp[2] * yp
        xs = xs + ms[2] * ys
        m = {'wc': moe_wc[l], 'bc': moe_bc[l], 'wf': moe_wf[l], 'bf': moe_bf[l],
             'w1': moe_w1[l], 'w3': moe_w3[l], 'w2': moe_w2[l]}
        xp = xp + mp[5] * hier_moe(modulate(xp, norm_ffn[l], mp[3], mp[4]), m)
        xs = xs + ms[5] * hier_moe(modulate(xs, norm_ffn[l], ms[3], ms[4]), m)
    return (xp, xs,
            jnp.stack(sp['cmp_k']), jnp.stack(ss['cmp_k']), jnp.stack(sp['cmp_v']), jnp.stack(ss['cmp_v']),
            jnp.stack(sp['sel_k']), jnp.stack(ss['sel_k']), jnp.stack(sp['sel_v']), jnp.stack(ss['sel_v']),
            jnp.stack(sp['win_k']), jnp.stack(ss['win_k']), jnp.stack(sp['win_v']), jnp.stack(ss['win_v']),
            jnp.stack(sp['ssd']), jnp.stack(ss['ssd']), jnp.stack(sp['conv']), jnp.stack(ss['conv']),
            jnp.stack(sp['s5_re']), jnp.stack(ss['s5_re']), jnp.stack(sp['s5_im']), jnp.stack(ss['s5_im']),
            jnp.stack(sp['gla']), jnp.stack(ss['gla']))
```

```python
import functools
import math

import jax
import jax.numpy as jnp
from jax import lax
from jax.experimental import pallas as pl
from jax.experimental.pallas import tpu as pltpu

f32 = jnp.float32
bf16 = jnp.bfloat16
i32 = jnp.int32

D = 1024
PAGE = 128
SSD_HEADS, SSD_HD, SSD_INNER, SSD_GROUPS, SSD_STATE, SSD_CONV = 16, 64, 1024, 4, 64, 4
SSD_CONV_DIM = SSD_INNER + 2 * SSD_GROUPS * SSD_STATE
NSA_HEADS, NSA_KVH, NSA_REP, HD, NSA_BLOCK, NSA_TOPN, NSA_WINDOW = 16, 4, 4, 64, 64, 16, 512
ROPE_THETA = 10000.0
S5_CH, S5_GCH, S5_GROUPS, S5_STATE = 512, 16, 32, 64
S5_N = S5_GROUPS * S5_STATE
GLA_HEADS, GLA_DK, GLA_DV, GLA_RANK, GLA_TEMP = 4, 64, 128, 16, 16.0
MOE_GROUPS, MOE_PER_GROUP, MOE_EXPERTS, MOE_FF = 4, 8, 32, 256
MOE_ROWS = 256
EPS = 1e-6
NEG = -1e30
BIG = 1e30
GONE = -3e38
LANES = 128
CHUNK = 128
VMEM_LIMIT = 48 * 2**20


def _cp(sem, vmem=VMEM_LIMIT):
    return pltpu.CompilerParams(dimension_semantics=sem, vmem_limit_bytes=vmem)


def _sigmoid(x):
    return 1.0 / (1.0 + jnp.exp(-x))


def _softplus(x):
    return jnp.maximum(x, 0.0) + jnp.log1p(jnp.exp(-jnp.abs(x)))


def _dotf(a, b):
    return jnp.dot(a, b, preferred_element_type=f32)


def _dot_nt(a, b):
    return lax.dot_general(a, b, (((1,), (1,)), ((), ())), preferred_element_type=f32)


def _dot_tn(a, b):
    return lax.dot_general(a, b, (((0,), (0,)), ((), ())), preferred_element_type=f32)


def _split2(a):
    h = a.astype(bf16)
    return h, (a - h.astype(f32)).astype(bf16)


def _split3(a):
    h = a.astype(bf16)
    r = a - h.astype(f32)
    m = r.astype(bf16)
    return h, m, (r - m.astype(f32)).astype(bf16)


def _dot3_r(a, w):
    h, m, l = _split3(a)
    return _dotf(h, w) + _dotf(m, w) + _dotf(l, w)


def _dot3_l(w, a):
    h, m, l = _split3(a)
    return _dotf(w, h) + _dotf(w, m) + _dotf(w, l)


def _iota(shape, dim):
    return lax.broadcasted_iota(i32, shape, dim)


def _tri(n):
    return _iota((n, n), 0) >= _iota((n, n), 1)


def _modulate(x, sh, sc, g):
    ms = jnp.mean(x * x, axis=-1, keepdims=True)
    return (x * lax.rsqrt(ms + EPS) * g) * (1.0 + sc) + sh


def _mod_spec(mod, tm):
    if mod.shape[1] == 1:
        return pl.BlockSpec((None, 1, D), lambda b, i: (b, 0, 0))
    return pl.BlockSpec((None, tm, D), lambda b, i: (b, i, 0))


def _full(a):
    n = a.ndim
    return pl.BlockSpec(a.shape, lambda *_: (0,) * n)


def _ada_kernel(c_ref, w_ref, b_ref, o_ref):
    c = c_ref[...]
    a = c * _sigmoid(c)
    ah, al = _split2(a)
    wh, wl = _split2(w_ref[...])
    o_ref[...] = _dotf(ah, wh) + _dotf(al, wh) + _dotf(ah, wl) + b_ref[...]


def _ada(c_all, ada_w, ada_b):
    depth, bc, tn = ada_w.shape[0], c_all.shape[0], 1536
    return pl.pallas_call(
        _ada_kernel, grid=(depth, 6 * D // tn),
        in_specs=[pl.BlockSpec((bc, D), lambda l, j: (0, 0)),
                  pl.BlockSpec((None, D, tn), lambda l, j: (l, 0, j)),
                  pl.BlockSpec((None, 1, tn), lambda l, j: (l, 0, j))],
        out_specs=pl.BlockSpec((None, bc, tn), lambda l, j: (l, 0, j)),
        out_shape=jax.ShapeDtypeStruct((depth, bc, 6 * D), f32),
        compiler_params=_cp(("parallel", "parallel")), name="ada")(c_all, ada_w, ada_b.reshape(depth, 1, 6 * D))


def _proj_kernel(x_ref, sh_ref, sc_ref, g_ref, *refs, nseg):
    h = _modulate(x_ref[...], sh_ref[...], sc_ref[...], g_ref[...]).astype(bf16)
    for i in range(nseg):
        refs[nseg + i][...] = _dotf(h, refs[i][...])


def _mod_proj(x, sh, sc, g, ws, name):
    b, t, _ = x.shape
    tm = min(256, t)
    nseg = len(ws)
    return pl.pallas_call(
        functools.partial(_proj_kernel, nseg=nseg), grid=(b, t // tm),
        in_specs=[pl.BlockSpec((None, tm, D), lambda b, i: (b, i, 0)), _mod_spec(sh, tm), _mod_spec(sc, tm),
                  pl.BlockSpec((1, D), lambda b, i: (0, 0))] + [_full(w) for w in ws],
        out_specs=[pl.BlockSpec((None, tm, w.shape[1]), lambda b, i: (b, i, 0)) for w in ws],
        out_shape=[jax.ShapeDtypeStruct((b, t, w.shape[1]), f32) for w in ws],
        compiler_params=_cp(("parallel", "parallel")), name=name)(x, sh, sc, g, *ws)


def _outproj_kernel(x_ref, gate_ref, *refs, nseg):
    acc = _dotf(refs[0][...].astype(bf16), refs[nseg][...])
    for i in range(1, nseg):
        acc = acc + _dotf(refs[i][...].astype(bf16), refs[nseg + i][...])
    refs[2 * nseg][...] = x_ref[...] + gate_ref[...] * acc


def _out_proj(x, gate, acts, ws, name):
    b, t, _ = x.shape
    tm = min(512, t)
    nseg = len(acts)
    return pl.pallas_call(
        functools.partial(_outproj_kernel, nseg=nseg), grid=(b, t // tm),
        in_specs=[pl.BlockSpec((None, tm, D), lambda b, i: (b, i, 0)), _mod_spec(gate, tm)]
        + [pl.BlockSpec((None, tm, a.shape[2]), lambda b, i: (b, i, 0)) for a in acts] + [_full(w) for w in ws],
        out_specs=pl.BlockSpec((None, tm, D), lambda b, i: (b, i, 0)),
        out_shape=jax.ShapeDtypeStruct((b, t, D), f32),
        compiler_params=_cp(("parallel", "parallel")), name=name)(x, gate, *acts, *ws)


def _ssd_kernel(xbc_ref, dt_ref, z_ref, conv0_ref, h0_ref, cw_ref, cb_ref, dtb_ref, a_ref, dsk_ref, nrm_ref,
                y_ref, convn_ref, hn_ref, xpad, hst, ybuf, *, nc, t_valid):
    L = CHUNK
    c = pl.program_id(1)

    @pl.when(c == 0)
    def _():
        hst[...] = h0_ref[...]
        xpad[0:8, :] = conv0_ref[...]

    xpad[8:8 + L, :] = xbc_ref[...]
    acc = cb_ref[...] + cw_ref[0:1, :] * xpad[5:5 + L, :]
    for k in range(1, SSD_CONV):
        acc = acc + cw_ref[k:k + 1, :] * xpad[5 + k:5 + k + L, :]
    tv_last = t_valid - (nc - 1) * L
    convn_ref[...] = xpad[tv_last:tv_last + 8, :]
    xpad[0:8, :] = xpad[L:L + 8, :]

    xc = acc * _sigmoid(acc)
    xs = xc[:, :SSD_INNER]
    bm = xc[:, SSD_INNER:SSD_INNER + 256]
    cm = xc[:, SSD_INNER + 256:]
    dt = _softplus(dt_ref[...] + dtb_ref[...])
    if t_valid < nc * L:
        dt = jnp.where(c * L + _iota((L, LANES), 0) < t_valid, dt, 0.0)
    causal = _tri(L)
    trib = causal.astype(bf16)
    cs = _dot3_l(trib, dt * a_ref[...])
    cs_t, dt_t = cs.T, dt.T
    wend_t = jnp.exp(cs_t[:, L - 1:L] - cs_t) * dt_t
    ecs = jnp.exp(cs)
    xs_t = xs.T
    for g in range(SSD_GROUPS):
        bg = bm[:, g * 64:(g + 1) * 64].astype(bf16)
        cg = cm[:, g * 64:(g + 1) * 64].astype(bf16)
        gmat = _dot_nt(cg, bg)
        for r in range(SSD_HEADS // SSD_GROUPS):
            h = g * (SSD_HEADS // SSD_GROUPS) + r
            seg = cs[:, h:h + 1] - cs_t[h:h + 1, :]
            dec = jnp.where(causal, jnp.exp(jnp.where(causal, seg, 0.0)), 0.0)
            sc = (gmat * dec * dt_t[h:h + 1, :]).astype(bf16)
            hs = hst[h]
            yh = _dotf(sc, xs[:, h * 64:(h + 1) * 64].astype(bf16)) + _dot_nt(cg, hs.astype(bf16)) * ecs[:, h:h + 1]
            ybuf[:, h * 64:(h + 1) * 64] = yh
            xw = (xs_t[h * 64:(h + 1) * 64, :] * wend_t[h:h + 1, :]).astype(bf16)
            hst[h] = hs * ecs[L - 1:L, h:h + 1] + _dotf(xw, bg)
    zz = z_ref[...]
    y = (ybuf[...] + dsk_ref[...] * xs) * (zz * _sigmoid(zz))
    gw = SSD_INNER // SSD_GROUPS
    for g in range(SSD_GROUPS):
        s = y[:, g * gw:(g + 1) * gw]
        ms = jnp.mean(s * s, axis=-1, keepdims=True)
        y_ref[:, g * gw:(g + 1) * gw] = s * lax.rsqrt(ms + EPS) * nrm_ref[:, g * gw:(g + 1) * gw]

    @pl.when(c == nc - 1)
    def _():
        hn_ref[...] = hst[...]


def _ssd(xbc, sm, z, conv0, h0, prm, t_valid):
    b, t, _ = xbc.shape
    nc = t // CHUNK
    tok = lambda w: pl.BlockSpec((None, CHUNK, w), lambda b, c: (b, c, 0))
    return pl.pallas_call(
        functools.partial(_ssd_kernel, nc=nc, t_valid=t_valid), grid=(b, nc),
        in_specs=[tok(SSD_CONV_DIM), tok(LANES), tok(SSD_INNER),
                  pl.BlockSpec((None, 8, SSD_CONV_DIM), lambda b, c: (b, 0, 0)),
                  pl.BlockSpec((None, SSD_HEADS, SSD_HD, SSD_STATE), lambda b, c: (b, 0, 0, 0))]
        + [_full(p) for p in prm],
        out_specs=[tok(SSD_INNER), pl.BlockSpec((None, 8, SSD_CONV_DIM), lambda b, c: (b, 0, 0)),
                   pl.BlockSpec((None, SSD_HEADS, SSD_HD, SSD_STATE), lambda b, c: (b, 0, 0, 0))],
        out_shape=[jax.ShapeDtypeStruct((b, t, SSD_INNER), f32), jax.ShapeDtypeStruct((b, 8, SSD_CONV_DIM), f32),
                   jax.ShapeDtypeStruct((b, SSD_HEADS, SSD_HD, SSD_STATE), f32)],
        scratch_shapes=[pltpu.VMEM((CHUNK + 8, SSD_CONV_DIM), f32), pltpu.VMEM((SSD_HEADS, SSD_HD, SSD_STATE), f32),
                        pltpu.VMEM((CHUNK, SSD_INNER), f32)],
        compiler_params=_cp(("parallel", "arbitrary")), name="ssd")(xbc, sm, z, conv0, h0, *prm)


def _seg_rinv(x, seg_ref, exp_ref):
    x2 = x * x
    h, l = _split2(x2)
    ss = _dotf(h, seg_ref[...]) + _dotf(l, seg_ref[...])
    return _dot3_r(lax.rsqrt(ss * (1.0 / HD) + EPS), exp_ref[...])


def _rope(x, cosf, sinf):
    w = x.shape[1]
    first = (_iota(x.shape, 1) % HD) < (HD // 2)
    rot = jnp.where(first, pltpu.roll(x, w - HD // 2, 1), pltpu.roll(x, HD // 2, 1))
    return x * cosf + rot * sinf


def _nsaprep_kernel(q_ref, kv_ref, sm_ref, cos_ref, sin_ref, segq_ref, expq_ref, segk_ref, expk_ref, gq_ref, gk_ref,
                    vm_ref, cw_ref, *outs, tm, compress):
    qc_ref, qr_ref, kcmp_ref, vcmp_ref, ksel_ref, vsel_ref, kwin_ref, vwin_ref, gate_ref = outs[:9]
    i = pl.program_id(1)
    cos1, sin1 = cos_ref[...], sin_ref[...]
    q = q_ref[...]
    qn = q * _seg_rinv(q, segq_ref, expq_ref) * gq_ref[...]
    qc_ref[...] = (qn * 0.125).astype(bf16)
    qr_ref[...] = (_rope(qn, jnp.concatenate([cos1] * 8, axis=1), jnp.concatenate([sin1] * 8, axis=1)) * 0.125).astype(bf16)
    kv = kv_ref[...]
    kvn = kv * (_seg_rinv(kv, segk_ref, expk_ref) * gk_ref[...] + vm_ref[...])
    cos2, sin2 = jnp.concatenate([cos1] * 2, axis=1), jnp.concatenate([sin1] * 2, axis=1)
    kcmp, vcmp = kvn[:, 0:256], kvn[:, 256:512]
    ksel, vsel = _rope(kvn[:, 512:768], cos2, sin2), kvn[:, 768:1024]
    kwin, vwin = _rope(kvn[:, 1024:1280], cos2, sin2), kvn[:, 1280:1536]
    kcmp_ref[...] = kcmp
    vcmp_ref[...] = vcmp
    ksel_ref[...] = ksel
    vsel_ref[...] = vsel
    kwin_ref[...] = kwin
    vwin_ref[...] = vwin
    gate_ref[...] = _sigmoid(sm_ref[...])
    if compress:
        kaug_ref, vsa_ref, kwh_ref, vwa_ref, kch_ref, vch_ref = outs[9:]
        blk = (i * tm + _iota((tm, LANES), 0)) // NSA_BLOCK
        onehot = (_iota((tm, LANES), 1) == blk).astype(bf16)
        ones = (_iota((tm, HD), 1) == 0).astype(bf16)
        zero = jnp.zeros((tm, HD), bf16)
        kcw = (kcmp.reshape(tm // NSA_BLOCK, NSA_BLOCK, 256) * cw_ref[0][None]).sum(axis=1)
        vcw = (vcmp.reshape(tm // NSA_BLOCK, NSA_BLOCK, 256) * cw_ref[1][None]).sum(axis=1)
        for g in range(NSA_KVH):
            sl = slice(g * HD, (g + 1) * HD)
            kaug_ref[g, :, 0:HD] = ksel[:, sl].astype(bf16)
            kaug_ref[g, :, HD:2 * HD] = zero
            kaug_ref[g, :, 2 * HD:] = onehot
            vsa_ref[g, :, 0:HD] = vsel[:, sl].astype(bf16)
            vsa_ref[g, :, HD:] = ones
            kwh_ref[g] = kwin[:, sl].astype(bf16)
            vwa_ref[g, :, 0:HD] = vwin[:, sl].astype(bf16)
            vwa_ref[g, :, HD:] = ones
            kch_ref[g] = kcw[:, sl]
            vch_ref[g] = vcw[:, sl]


def _nsa_prep(q, kv, sm, cos_t, sin_t, prm, compress):
    b, t, _ = q.shape
    tm = min(512, t)
    tok = lambda w: pl.BlockSpec((None, tm, w), lambda b, i: (b, i, 0))
    tab = pl.BlockSpec((tm, LANES), lambda b, i: (i, 0))
    hm = lambda w: pl.BlockSpec((None, NSA_KVH, tm, w), lambda b, i: (b, 0, i, 0))
    out_specs = [tok(D), tok(D)] + [tok(256)] * 6 + [tok(LANES)]
    out_shape = [jax.ShapeDtypeStruct((b, t, D), bf16)] * 2 + [jax.ShapeDtypeStruct((b, t, 256), f32)] * 6 \
        + [jax.ShapeDtypeStruct((b, t, LANES), f32)]
    if compress:
        nbt = tm // NSA_BLOCK
        cspec = pl.BlockSpec((None, NSA_KVH, nbt, HD), lambda b, i: (b, 0, i, 0))
        out_specs += [hm(256), hm(LANES), hm(HD), hm(LANES), cspec, cspec]
        out_shape += [jax.ShapeDtypeStruct((b, NSA_KVH, t, 256), bf16), jax.ShapeDtypeStruct((b, NSA_KVH, t, LANES), bf16),
                      jax.ShapeDtypeStruct((b, NSA_KVH, t, HD), bf16), jax.ShapeDtypeStruct((b, NSA_KVH, t, LANES), bf16),
                      jax.ShapeDtypeStruct((b, NSA_KVH, t // NSA_BLOCK, HD), f32),
                      jax.ShapeDtypeStruct((b, NSA_KVH, t // NSA_BLOCK, HD), f32)]
    return pl.pallas_call(
        functools.partial(_nsaprep_kernel, tm=tm, compress=compress), grid=(b, t // tm),
        in_specs=[tok(D), tok(1536), tok(LANES), tab, tab] + [_full(p) for p in prm],
        out_specs=out_specs, out_shape=out_shape,
        compiler_params=_cp(("parallel", "parallel")), name="nsa_prep")(q, kv, sm, cos_t, sin_t, *prm)


def _cmp_branch(q_heads, kc, vc, tpos):
    nb, nq = kc.shape[0], tpos.shape[1]
    nrow = _iota((nb, nq), 0)
    ok = ((nrow + 1) * NSA_BLOCK - 1) <= tpos
    imp = jnp.zeros((nb, nq), f32)
    outs = []
    for q in q_heads:
        s = jnp.where(ok, _dot_nt(kc, q), NEG)
        m = jnp.max(s, axis=0, keepdims=True)
        e = jnp.where(ok, jnp.exp(s - m), 0.0)
        l = jnp.sum(e, axis=0, keepdims=True)
        p = e / jnp.where(l > 0.0, l, 1.0)
        imp = imp + p
        outs.append(_dot_tn(p.astype(bf16), vc))
    cur = tpos // NSA_BLOCK
    impm = jnp.where((nrow == cur) | (nrow == 0), BIG, jnp.where(nrow < cur, imp, NEG))

    def pick(_, carry):
        v, sel = carry
        mx = jnp.max(v, axis=0, keepdims=True)
        idx = jnp.min(jnp.where(v == mx, nrow, nb), axis=0, keepdims=True)
        hit = nrow == idx
        return jnp.where(hit, GONE, v), jnp.where(hit, 1.0, sel)

    _, sel = lax.fori_loop(0, NSA_TOPN, pick, (impm, jnp.zeros((nb, nq), f32)))
    neg = jnp.where((sel > 0.5) & (impm > 0.5 * NEG), 0.0, NEG)
    return outs, neg


def _flash(lhs, k_ref, v_ref, lo, hi, trow, tk, m_ref, acc_ref, window):
    m_ref[...] = jnp.full(m_ref.shape, NEG, f32)
    acc_ref[...] = jnp.zeros(acc_ref.shape, f32)

    def body(kt, _):
        ks = pl.multiple_of(kt * tk, tk)
        s = _dot_nt(lhs, k_ref[pl.ds(ks, tk), :])
        kpos = ks + _iota((1, tk), 1)
        keep = kpos <= trow
        if window:
            keep = keep & (kpos > trow - NSA_WINDOW)
        s = jnp.where(keep, s, NEG)
        m_old = m_ref[...]
        m_new = jnp.maximum(m_old, jnp.max(s, axis=1, keepdims=True))
        p = jnp.exp(s - m_new)
        acc_ref[...] = jnp.exp(m_old - m_new) * acc_ref[...] + _dotf(p.astype(bf16), v_ref[pl.ds(ks, tk), :])
        m_ref[...] = m_new
        return 0

    lax.fori_loop(lo, hi, body, 0)
    acc = acc_ref[...]
    return acc[:, 0:HD] / acc[:, HD:HD + 1]


def _nsa_kernel(qc_ref, qr_ref, kc_ref, vc_ref, kaug_ref, vsa_ref, kw_ref, vwa_ref, gate_ref, gexp_ref, o_ref,
                lhs_ref, m_ref, acc_ref, *, tq, tk):
    qi = pl.program_id(2)
    t0 = qi * tq
    tpos = t0 + _iota((1, tq), 1)
    qh = [qc_ref[:, r * HD:(r + 1) * HD] for r in range(NSA_REP)]
    oc, neg = _cmp_branch(qh, kc_ref[...].astype(bf16), vc_ref[...].astype(bf16), tpos)
    negq = neg.T.astype(bf16)
    for r in range(NSA_REP):
        rows = slice(r * tq, (r + 1) * tq)
        lhs_ref[rows, 0:HD] = qr_ref[:, r * HD:(r + 1) * HD]
        lhs_ref[rows, HD:2 * HD] = jnp.zeros((tq, HD), bf16)
        lhs_ref[rows, 2 * HD:] = negq
    trow = t0 + _iota((NSA_REP * tq, 1), 0) % tq
    hi = (t0 + tq) // tk
    o_s = _flash(lhs_ref[...], kaug_ref, vsa_ref, 0, hi, trow, tk, m_ref, acc_ref, False)
    lo = jnp.maximum(t0 - (NSA_WINDOW - 1), 0) // tk
    o_w = _flash(lhs_ref[:, 0:HD], kw_ref, vwa_ref, lo, hi, trow, tk, m_ref, acc_ref, True)
    gates = gate_ref[...]
    unstack = lambda a: jnp.concatenate([a[r * tq:(r + 1) * tq] for r in range(NSA_REP)], axis=1)
    o = _dot3_r(gates, gexp_ref[0]) * jnp.concatenate(oc, axis=1) \
        + _dot3_r(gates, gexp_ref[1]) * unstack(o_s) + _dot3_r(gates, gexp_ref[2]) * unstack(o_w)
    o_ref[...] = o.astype(bf16)


def _nsa_prompt(qc, qr, kch, vch, kaug, vsa, kwh, vwa, gates, gexp):
    b, t, _ = qc.shape
    tq = tk = min(256, t)
    nbp = kch.shape[2]
    qspec = pl.BlockSpec((None, tq, 256), lambda b, g, i: (b, i, g))
    kvspec = lambda n, w: pl.BlockSpec((None, None, n, w), lambda b, g, i: (b, g, 0, 0))
    return pl.pallas_call(
        functools.partial(_nsa_kernel, tq=tq, tk=tk), grid=(b, NSA_KVH, t // tq),
        in_specs=[qspec, qspec, kvspec(nbp, HD), kvspec(nbp, HD), kvspec(t, 256), kvspec(t, LANES), kvspec(t, HD),
                  kvspec(t, LANES), pl.BlockSpec((None, tq, LANES), lambda b, g, i: (b, i, 0)),
                  pl.BlockSpec((None, 3, LANES, 256), lambda b, g, i: (g, 0, 0, 0))],
        out_specs=qspec, out_shape=jax.ShapeDtypeStruct((b, t, D), bf16),
        scratch_shapes=[pltpu.VMEM((NSA_REP * tq, 256), bf16), pltpu.VMEM((NSA_REP * tq, 1), f32),
                        pltpu.VMEM((NSA_REP * tq, LANES), f32)],
        compiler_params=_cp(("parallel", "parallel", "arbitrary")), name="nsa_prompt")(
            qc, qr, kch, vch, kaug, vsa, kwh, vwa, gates, gexp)


def _pagecmp_kernel(pt_ref, k_ref, v_ref, w_ref, kc_ref, vc_ref):
    w = w_ref[...]
    kc_ref[...] = jnp.sum(k_ref[...].reshape(2, NSA_BLOCK, NSA_KVH, HD) * w[0][None], axis=1)
    vc_ref[...] = jnp.sum(v_ref[...].reshape(2, NSA_BLOCK, NSA_KVH, HD) * w[1][None], axis=1)


def _page_compress(page_table, cache_k, cache_v, li, cmp_w):
    bs, npg = page_table.shape
    page = pl.BlockSpec((None, None, PAGE, NSA_KVH, HD), lambda b, j, pt: (li, pt[b, j], 0, 0, 0))
    out = pl.BlockSpec((None, 2, NSA_KVH, HD), lambda b, j, pt: (b, j, 0, 0))
    shp = jax.ShapeDtypeStruct((bs, 2 * npg, NSA_KVH, HD), f32)
    return pl.pallas_call(
        _pagecmp_kernel,
        grid_spec=pltpu.PrefetchScalarGridSpec(
            num_scalar_prefetch=1, grid=(bs, npg),
            in_specs=[page, page, pl.BlockSpec((2, NSA_BLOCK, 1, HD), lambda b, j, pt: (0, 0, 0, 0))],
            out_specs=[out, out]),
        out_shape=[shp, shp], compiler_params=_cp(("parallel", "arbitrary")), name="page_compress")(
            page_table, cache_k, cache_v, cmp_w.reshape(2, NSA_BLOCK, 1, HD))


def _ssel_kernel(qc_ref, kc_ref, vc_ref, kn_ref, vn_ref, w_ref, mrep_ref, oc_ref, neg_ref, *, past, ts):
    nbp = kc_ref.shape[1]
    nb_past = past // NSA_BLOCK
    col = _iota((1, LANES), 1)
    tpos = past + col % 8
    newrow = (_iota((nbp, 1), 0) == nb_past).astype(f32)
    tmask = (_iota((8, 1), 0) < ts).astype(f32)
    for g in range(NSA_KVH):
        sl = slice(g * HD, (g + 1) * HD)
        kc_new = jnp.sum(kn_ref[:, sl] * tmask * w_ref[0, 0:8, :], axis=0, keepdims=True)
        vc_new = jnp.sum(vn_ref[:, sl] * tmask * w_ref[1, 0:8, :], axis=0, keepdims=True)
        kc = (kc_ref[g] + newrow * kc_new).astype(bf16)
        vc = (vc_ref[g] + newrow * vc_new).astype(bf16)
        q = qc_ref[g]
        nrow = _iota((nbp, LANES), 0)
        ok = ((nrow + 1) * NSA_BLOCK - 1) <= tpos
        s = jnp.where(ok, _dot_nt(kc, q), NEG)
        m = jnp.max(s, axis=0, keepdims=True)
        e = jnp.where(ok, jnp.exp(s - m), 0.0)
        l = jnp.sum(e, axis=0, keepdims=True)
        p = e / jnp.where(l > 0.0, l, 1.0)
        oc_ref[g] = _dot_tn(p.astype(bf16), vc)
        imp = _dot3_r(p, mrep_ref[...])
        cur = tpos // NSA_BLOCK
        impm = jnp.where((nrow == cur) | (nrow == 0), BIG, jnp.where(nrow < cur, imp, NEG))

        def pick(_, carry):
            v, sel = carry
            mx = jnp.max(v, axis=0, keepdims=True)
            idx = jnp.min(jnp.where(v == mx, nrow, nbp), axis=0, keepdims=True)
            hit = nrow == idx
            return jnp.where(hit, GONE, v), jnp.where(hit, 1.0, sel)

        _, sel = lax.fori_loop(0, NSA_TOPN, pick, (impm, jnp.zeros((nbp, LANES), f32)))
        neg_ref[g] = jnp.where((sel > 0.5) & (impm > 0.5 * NEG), 0.0, NEG)


def _sample_select(qc_st, kc_h, vc_h, kn, vn, cmp_w, mrep, past, ts):
    bs, _, nbp, _ = kc_h.shape
    b4 = lambda n, w: pl.BlockSpec((None, NSA_KVH, n, w), lambda b: (b, 0, 0, 0))
    return pl.pallas_call(
        functools.partial(_ssel_kernel, past=past, ts=ts), grid=(bs,),
        in_specs=[b4(LANES, HD), b4(nbp, HD), b4(nbp, HD), pl.BlockSpec((None, 8, 256), lambda b: (b, 0, 0)),
                  pl.BlockSpec((None, 8, 256), lambda b: (b, 0, 0)), _full(cmp_w), _full(mrep)],
        out_specs=[b4(LANES, HD), b4(nbp, LANES)],
        out_shape=[jax.ShapeDtypeStruct((bs, NSA_KVH, LANES, HD), f32), jax.ShapeDtypeStruct((bs, NSA_KVH, nbp, LANES), f32)],
        compiler_params=_cp(("parallel",)), name="sample_select")(qc_st, kc_h, vc_h, kn, vn, cmp_w, mrep)


def _spage_kernel(pt_ref, q_ref, k_ref, v_ref, neg_ref, kn_ref, vn_ref, o_ref, m_s, l_s, acc_s, *, npg, ts):
    j = pl.program_id(1)
    tt = _iota((1, LANES), 1) % 8

    @pl.when(j == 0)
    def _():
        for g in range(NSA_KVH):
            kn = kn_ref[:, g * HD:(g + 1) * HD].astype(bf16)
            vn = vn_ref[:, g * HD:(g + 1) * HD].astype(bf16)
            i_row = _iota((8, LANES), 0)
            s = jnp.where((i_row <= tt) & (i_row < ts), _dot_nt(kn, q_ref[g]), NEG)
            m = jnp.max(s, axis=0, keepdims=True)
            p = jnp.exp(s - m)
            m_s[g] = m
            l_s[g] = jnp.sum(p, axis=0, keepdims=True)
            acc_s[g] = _dot_tn(vn, p.astype(bf16))

    half = _iota((PAGE, 1), 0) < NSA_BLOCK
    for g in range(NSA_KVH):
        kg = k_ref[:, g, :].astype(bf16)
        vg = v_ref[:, g, :].astype(bf16)
        s = _dot_nt(kg, q_ref[g]) + jnp.where(half, neg_ref[g, 0:1, :], neg_ref[g, 1:2, :])
        m_old = m_s[g]
        m_new = jnp.maximum(m_old, jnp.max(s, axis=0, keepdims=True))
        p = jnp.exp(s - m_new)
        alpha = jnp.exp(m_old - m_new)
        l_s[g] = alpha * l_s[g] + jnp.sum(p, axis=0, keepdims=True)
        acc_s[g] = alpha * acc_s[g] + _dot_tn(vg, p.astype(bf16))
        m_s[g] = m_new

    @pl.when(j == npg - 1)
    def _():
        for g in range(NSA_KVH):
            o_ref[g] = acc_s[g] / l_s[g]


def _sample_selected(page_table, qr_st, cache_k, cache_v, li, neg5, kn, vn, ts):
    bs, npg = page_table.shape
    page = pl.BlockSpec((None, None, PAGE, NSA_KVH, HD), lambda b, j, pt: (li, pt[b, j], 0, 0, 0))
    return pl.pallas_call(
        functools.partial(_spage_kernel, npg=npg, ts=ts),
        grid_spec=pltpu.PrefetchScalarGridSpec(
            num_scalar_prefetch=1, grid=(bs, npg),
            in_specs=[pl.BlockSpec((None, NSA_KVH, LANES, HD), lambda b, j, pt: (b, 0, 0, 0)), page, page,
                      pl.BlockSpec((None, NSA_KVH, None, 2, LANES), lambda b, j, pt: (b, 0, j, 0, 0)),
                      pl.BlockSpec((None, 8, 256), lambda b, j, pt: (b, 0, 0)),
                      pl.BlockSpec((None, 8, 256), lambda b, j, pt: (b, 0, 0))],
            out_specs=pl.BlockSpec((None, NSA_KVH, HD, LANES), lambda b, j, pt: (b, 0, 0, 0)),
            scratch_shapes=[pltpu.VMEM((NSA_KVH, 1, LANES), f32), pltpu.VMEM((NSA_KVH, 1, LANES), f32),
                            pltpu.VMEM((NSA_KVH, HD, LANES), f32)]),
        out_shape=jax.ShapeDtypeStruct((bs, NSA_KVH, HD, LANES), f32),
        compiler_params=_cp(("parallel", "arbitrary")), name="sample_selected")(
            page_table, qr_st, cache_k, cache_v, neg5, kn, vn)


def _swin_kernel(q_ref, wk_ref, wv_ref, kn_ref, vn_ref, kn4_ref, vn4_ref, o_ref, wko_ref, wvo_ref, *, ts, wb):
    tt = _iota((1, LANES), 1) % 8
    i_old = _iota((wb, LANES), 0)
    i_new = _iota((8, LANES), 0)
    for g in range(NSA_KVH):
        q = q_ref[g]
        s1 = jnp.where(i_old > tt + (wb - NSA_WINDOW), _dot_nt(wk_ref[:, g, :].astype(bf16), q), NEG)
        s2 = jnp.where((i_new <= tt) & (i_new < ts), _dot_nt(kn_ref[:, g * HD:(g + 1) * HD].astype(bf16), q), NEG)
        m = jnp.maximum(jnp.max(s1, axis=0, keepdims=True), jnp.max(s2, axis=0, keepdims=True))
        p1, p2 = jnp.exp(s1 - m), jnp.exp(s2 - m)
        l = jnp.sum(p1, axis=0, keepdims=True) + jnp.sum(p2, axis=0, keepdims=True)
        acc = _dot_tn(wv_ref[:, g, :].astype(bf16), p1.astype(bf16)) \
            + _dot_tn(vn_ref[:, g * HD:(g + 1) * HD].astype(bf16), p2.astype(bf16))
        o_ref[g] = acc / l
    wko_ref[0:wb - ts] = wk_ref[ts:wb]
    wko_ref[wb - ts:wb] = kn4_ref[0:ts]
    wvo_ref[0:wb - ts] = wv_ref[ts:wb]
    wvo_ref[wb - ts:wb] = vn4_ref[0:ts]


def _sample_window(qr_st, win_k, win_v, li, kn, vn, kn4, vn4, ts):
    bs, wb = win_k.shape[1], win_k.shape[2]
    wspec = pl.BlockSpec((None, None, wb, NSA_KVH, HD), lambda b: (li, b, 0, 0, 0))
    ospec = pl.BlockSpec((None, wb, NSA_KVH, HD), lambda b: (b, 0, 0, 0))
    r8 = pl.BlockSpec((None, 8, 256), lambda b: (b, 0, 0))
    r4 = pl.BlockSpec((None, 8, NSA_KVH, HD), lambda b: (b, 0, 0, 0))
    return pl.pallas_call(
        functools.partial(_swin_kernel, ts=ts, wb=wb), grid=(bs,),
        in_specs=[pl.BlockSpec((None, NSA_KVH, LANES, HD), lambda b: (b, 0, 0, 0)), wspec, wspec, r8, r8, r4, r4],
        out_specs=[pl.BlockSpec((None, NSA_KVH, HD, LANES), lambda b: (b, 0, 0, 0)), ospec, ospec],
        out_shape=[jax.ShapeDtypeStruct((bs, NSA_KVH, HD, LANES), f32), jax.ShapeDtypeStruct((bs, wb, NSA_KVH, HD), f32),
                   jax.ShapeDtypeStruct((bs, wb, NSA_KVH, HD), f32)],
        compiler_params=_cp(("parallel",)), name="sample_window")(qr_st, win_k, win_v, kn, vn, kn4, vn4)


def _gelu_tanh(x):
    return 0.5 * x * (1.0 + jnp.tanh(math.sqrt(2.0 / math.pi) * (x + 0.044715 * (x * x * x))))


def _s5_kernel(u_ref, h0r_ref, h0i_ref, bd_ref, pwr_ref, pwi_ref, cdr_ref, cdi_ref, d_ref, gw_ref, gb_ref,
               y_ref, sr_ref, si_ref, xr_s, xi_s, cr_s, ci_s, *, tc, srow):
    c = pl.program_id(1)

    @pl.when(c == 0)
    def _():
        cr_s[...] = jnp.broadcast_to(h0r_ref[...], (8, S5_N))
        ci_s[...] = jnp.broadcast_to(h0i_ref[...], (8, S5_N))

    u = u_ref[...]
    bu = _dotf(u.astype(bf16), bd_ref[...])
    xr_s[...] = bu[:, :S5_N]
    xi_s[...] = bu[:, S5_N:]
    rowi = _iota((8, S5_N), 0)

    def blk(j, carry):
        cr, ci = carry
        rs = pl.multiple_of(j * 8, 8)
        yr, yi = xr_s[pl.ds(rs, 8), :], xi_s[pl.ds(rs, 8), :]
        for sh in (1, 2, 4):
            ar, ai = pwr_ref[sh - 1:sh, :], pwi_ref[sh - 1:sh, :]
            sr = jnp.where(rowi >= sh, pltpu.roll(yr, sh, 0), 0.0)
            si = jnp.where(rowi >= sh, pltpu.roll(yi, sh, 0), 0.0)
            yr, yi = yr + ar * sr - ai * si, yi + ar * si + ai * sr
        pr, pi = pwr_ref[...], pwi_ref[...]
        xr = yr + pr * cr - pi * ci
        xi = yi + pr * ci + pi * cr
        xr_s[pl.ds(rs, 8), :] = xr
        xi_s[pl.ds(rs, 8), :] = xi
        return jnp.broadcast_to(xr[7:8, :], (8, S5_N)), jnp.broadcast_to(xi[7:8, :], (8, S5_N))

    cr, ci = lax.fori_loop(0, tc // 8, blk, (cr_s[...], ci_s[...]))
    cr_s[...] = cr
    ci_s[...] = ci
    sr_ref[...] = xr_s[srow:srow + 8, :]
    si_ref[...] = xi_s[srow:srow + 8, :]
    y = _dotf(xr_s[...].astype(bf16), cdr_ref[...]) + _dotf(xi_s[...].astype(bf16), cdi_ref[...]) + d_ref[...] * u
    y = _gelu_tanh(y)
    y_ref[...] = y * _sigmoid(_dotf(y.astype(bf16), gw_ref[...]) + gb_ref[...])


def _s5(u, h0r, h0i, prm, t_valid):
    b, t, _ = u.shape
    tc = min(256, t)
    assert (t_valid - 1) // tc == t // tc - 1
    srow = ((t_valid - 1) % tc) // 8 * 8
    st = pl.BlockSpec((None, 8, S5_N), lambda b, c: (b, 0, 0))
    h0 = pl.BlockSpec((None, 1, S5_N), lambda b, c: (b, 0, 0))
    return pl.pallas_call(
        functools.partial(_s5_kernel, tc=tc, srow=srow), grid=(b, t // tc),
        in_specs=[pl.BlockSpec((None, tc, S5_CH), lambda b, c: (b, c, 0)), h0, h0] + [_full(p) for p in prm],
        out_specs=[pl.BlockSpec((None, tc, S5_CH), lambda b, c: (b, c, 0)), st, st],
        out_shape=[jax.ShapeDtypeStruct((b, t, S5_CH), f32), jax.ShapeDtypeStruct((b, 8, S5_N), f32),
                   jax.ShapeDtypeStruct((b, 8, S5_N), f32)],
        scratch_shapes=[pltpu.VMEM((tc, S5_N), f32), pltpu.VMEM((tc, S5_N), f32), pltpu.VMEM((8, S5_N), f32),
                        pltpu.VMEM((8, S5_N), f32)],
        compiler_params=_cp(("parallel", "arbitrary")), name="s5")(u, h0r, h0i, *prm)


def _gla_kernel(q_ref, k_ref, v_ref, g_ref, sm_ref, s0_ref, wa_ref, ba_ref, gn_ref, o_ref, sn_ref, st, *, nc, t_valid):
    L = CHUNK
    c = pl.program_id(1)

    @pl.when(c == 0)
    def _():
        st[...] = s0_ref[...]

    x = _dotf(sm_ref[...].astype(bf16), wa_ref[...]) + ba_ref[...]
    la = (jnp.minimum(x, 0.0) - jnp.log1p(jnp.exp(-jnp.abs(x)))) * (1.0 / GLA_TEMP)
    k = k_ref[...]
    if t_valid < nc * L:
        valid = c * L + _iota((L, 1), 0) < t_valid
        la = jnp.where(valid, la, 0.0)
        k = jnp.where(valid, k, 0.0)
    causal = _tri(L)
    b = _dot3_l(causal.astype(bf16), la)
    bl = b[L - 1:L, :]
    qe = (q_ref[...] * (GLA_DK ** -0.5) * jnp.exp(b)).astype(bf16)
    ke = (k * jnp.exp(-b)).astype(bf16)
    kl = (k * jnp.exp(bl - b)).astype(bf16)
    ebl = jnp.exp(bl)
    v, gg, gn = v_ref[...], g_ref[...], gn_ref[...]
    for h in range(GLA_HEADS):
        sl = slice(h * GLA_DK, (h + 1) * GLA_DK)
        vl = slice(h * GLA_DV, (h + 1) * GLA_DV)
        att = jnp.where(causal, _dot_nt(qe[:, sl], ke[:, sl]), 0.0)
        vh = v[:, vl].astype(bf16)
        s_t = st[h]
        o = _dotf(att.astype(bf16), vh) + _dot_nt(qe[:, sl], s_t.astype(bf16))
        st[h] = s_t * ebl[:, sl] + _dot_tn(vh, kl[:, sl])
        ms = jnp.mean(o * o, axis=-1, keepdims=True)
        gh = gg[:, vl]
        o_ref[:, vl] = o * lax.rsqrt(ms + EPS) * gn * (gh * _sigmoid(gh))

    @pl.when(c == nc - 1)
    def _():
        sn_ref[...] = st[...]


def _gla(q, k, v, g, sm, s0t, prm, t_valid):
    b, t, _ = q.shape
    nc = t // CHUNK
    tok = lambda w: pl.BlockSpec((None, CHUNK, w), lambda b, c: (b, c, 0))
    st = pl.BlockSpec((None, GLA_HEADS, GLA_DV, GLA_DK), lambda b, c: (b, 0, 0, 0))
    return pl.pallas_call(
        functools.partial(_gla_kernel, nc=nc, t_valid=t_valid), grid=(b, nc),
        in_specs=[tok(256), tok(256), tok(512), tok(512), tok(LANES), st] + [_full(p) for p in prm],
        out_specs=[tok(512), st],
        out_shape=[jax.ShapeDtypeStruct((b, t, 512), f32), jax.ShapeDtypeStruct((b, GLA_HEADS, GLA_DV, GLA_DK), f32)],
        scratch_shapes=[pltpu.VMEM((GLA_HEADS, GLA_DV, GLA_DK), f32)],
        compiler_params=_cp(("parallel", "arbitrary")), name="gla")(q, k, v, g, sm, s0t, *prm)


def _router_kernel(x_ref, sh_ref, sc_ref, g_ref, w_ref, b_ref, h_ref, route_ref, cnt_ref):
    first = (pl.program_id(0) == 0) & (pl.program_id(1) == 0)
    h = _modulate(x_ref[...], sh_ref[...], sc_ref[...], g_ref[...])
    h_ref[...] = h
    hh, hl = _split2(h)
    wh, wl = _split2(w_ref[...])
    lg = _dotf(hh, wh) + _dotf(hl, wh) + _dotf(hh, wl) + b_ref[...]
    lane = _iota(lg.shape, 1)
    big = 4 * LANES
    coarse = lane < MOE_GROUPS
    lc = jnp.where(coarse, lg, GONE)
    mx = jnp.max(lc, axis=1, keepdims=True)
    gsel = jnp.min(jnp.where(lc == mx, lane, big), axis=1, keepdims=True)
    gc = 1.0 / jnp.sum(jnp.where(coarse, jnp.exp(lg - mx), 0.0), axis=1, keepdims=True)
    base = MOE_GROUPS + gsel * MOE_PER_GROUP
    fine = (lane >= base) & (lane < base + MOE_PER_GROUP)
    mf = jnp.max(jnp.where(fine, lg, GONE), axis=1, keepdims=True)
    ef = jnp.where(fine, jnp.exp(lg - mf), 0.0)
    pf = ef / jnp.sum(ef, axis=1, keepdims=True)
    cand = jnp.where(fine, pf, -1.0)
    v1 = jnp.max(cand, axis=1, keepdims=True)
    i1 = jnp.min(jnp.where(cand == v1, lane, big), axis=1, keepdims=True)
    cand = jnp.where(lane == i1, -1.0, cand)
    v2 = jnp.max(cand, axis=1, keepdims=True)
    i2 = jnp.min(jnp.where(cand == v2, lane, big), axis=1, keepdims=True)
    e1, e2 = i1 - MOE_GROUPS, i2 - MOE_GROUPS
    w1, w2 = gc * v1 / (v1 + v2), gc * v2 / (v1 + v2)
    route_ref[...] = jnp.where(lane == 0, e1.astype(f32), jnp.where(lane == 1, e2.astype(f32),
                               jnp.where(lane == 2, w1, jnp.where(lane == 3, w2, 0.0))))
    cnt = jnp.sum((lane == e1).astype(f32) + (lane == e2).astype(f32), axis=0, keepdims=True)

    @pl.when(first)
    def _():
        cnt_ref[...] = jnp.zeros_like(cnt_ref)

    cnt_ref[...] += jnp.broadcast_to(cnt, cnt_ref.shape)


def _router(x, sh, sc, g, wr, br):
    b, t, _ = x.shape
    tm = min(256, t)
    return pl.pallas_call(
        _router_kernel, grid=(b, t // tm),
        in_specs=[pl.BlockSpec((None, tm, D), lambda b, i: (b, i, 0)), _mod_spec(sh, tm), _mod_spec(sc, tm),
                  pl.BlockSpec((1, D), lambda b, i: (0, 0)), _full(wr), _full(br)],
        out_specs=[pl.BlockSpec((None, tm, D), lambda b, i: (b, i, 0)), pl.BlockSpec((None, tm, LANES), lambda b, i: (b, i, 0)),
                   pl.BlockSpec((8, LANES), lambda b, i: (0, 0))],
        out_shape=[jax.ShapeDtypeStruct((b, t, D), f32), jax.ShapeDtypeStruct((b, t, LANES), f32),
                   jax.ShapeDtypeStruct((8, LANES), f32)],
        compiler_params=_cp(("arbitrary", "arbitrary")), name="moe_router")(x, sh, sc, g, wr, br)


def _plan_kernel(route_ref, ps_ref, dest_ref, run_s, *, tm):
    @pl.when(pl.program_id(0) == 0)
    def _():
        run_s[...] = jnp.zeros_like(run_s)

    route = route_ref[...]
    lane = _iota((tm, LANES), 1).astype(f32)
    oh1 = (lane == route[:, 0:1]).astype(f32)
    oh2 = (lane == route[:, 1:2]).astype(f32)
    tot = oh1 + oh2
    strict = (_iota((tm, tm), 0) > _iota((tm, tm), 1)).astype(bf16)
    before = _dotf(strict, tot.astype(bf16)) + run_s[0:1, :] + ps_ref[0:1, :]
    d1 = jnp.sum(oh1 * before, axis=1, keepdims=True)
    d2 = jnp.sum(oh2 * before, axis=1, keepdims=True)
    lane_i = _iota((tm, LANES), 1)
    dest_ref[...] = jnp.where(lane_i == 0, d1, jnp.where(lane_i == 1, d2, 0.0)).astype(i32)
    run_s[...] += jnp.broadcast_to(jnp.sum(tot, axis=0, keepdims=True), run_s.shape)


def _plan(route, pstart):
    n = route.shape[0]
    tm = min(256, n)
    return pl.pallas_call(
        functools.partial(_plan_kernel, tm=tm), grid=(n // tm,),
        in_specs=[pl.BlockSpec((tm, LANES), lambda i: (i, 0)), pl.BlockSpec((8, LANES), lambda i: (0, 0))],
        out_specs=pl.BlockSpec((tm, LANES), lambda i: (i, 0)),
        out_shape=jax.ShapeDtypeStruct((n, LANES), i32),
        scratch_shapes=[pltpu.VMEM((8, LANES), f32)],
        compiler_params=_cp(("arbitrary",)), name="moe_plan")(route, pstart)


def _dispatch_kernel(dest_ref, h_ref, xs_in, xs_out, sem, *, tm):
    del xs_in

    def row_copy(r, d):
        return pltpu.make_async_copy(h_ref.at[pl.ds(r, 1), :], xs_out.at[pl.ds(d, 1), :], sem)

    def issue(r, _):
        row_copy(r, dest_ref[2 * r]).start()
        row_copy(r, dest_ref[2 * r + 1]).start()
        return 0

    lax.fori_loop(0, tm, issue, 0)

    def drain(r, _):
        row_copy(0, 0).wait()
        row_copy(0, 0).wait()
        return 0

    lax.fori_loop(0, tm, drain, 0)


def _dispatch(h, dest_flat, nrows):
    n = h.shape[0]
    tm = min(256, n)
    return pl.pallas_call(
        functools.partial(_dispatch_kernel, tm=tm), grid=(n // tm,),
        in_specs=[pl.BlockSpec((2 * tm,), lambda i: (i,), memory_space=pltpu.SMEM),
                  pl.BlockSpec((tm, D), lambda i: (i, 0)), pl.BlockSpec(memory_space=pl.ANY)],
        out_specs=pl.BlockSpec(memory_space=pl.ANY),
        out_shape=jax.ShapeDtypeStruct((nrows, D), f32),
        scratch_shapes=[pltpu.SemaphoreType.DMA(())],
        input_output_aliases={2: 0},
        compiler_params=_cp(("arbitrary",)), name="moe_dispatch")(dest_flat, h, jnp.zeros((nrows, D), f32))


def _ffn_kernel(be_ref, nu_ref, x_ref, w1_ref, w3_ref, w2_ref, o_ref):
    i = pl.program_id(0)

    @pl.when(i < nu_ref[0])
    def _():
        x = x_ref[...].astype(bf16)
        a = _dotf(x, w1_ref[...].astype(bf16))
        b = _dotf(x, w3_ref[...].astype(bf16))
        hid = ((a * _sigmoid(a)) * b).astype(bf16)
        o_ref[...] = _dotf(hid, w2_ref[...].astype(bf16))

    @pl.when(i >= nu_ref[0])
    def _():
        o_ref[...] = jnp.zeros_like(o_ref)


def _ffn(xs, blk_e, nused, w1, w3, w2, l):
    nblk = xs.shape[0] // MOE_ROWS
    return pl.pallas_call(
        _ffn_kernel,
        grid_spec=pltpu.PrefetchScalarGridSpec(
            num_scalar_prefetch=2, grid=(nblk,),
            in_specs=[pl.BlockSpec((MOE_ROWS, D), lambda i, be, nu: (i, 0)),
                      pl.BlockSpec((None, None, D, MOE_FF), lambda i, be, nu: (l, be[i], 0, 0)),
                      pl.BlockSpec((None, None, D, MOE_FF), lambda i, be, nu: (l, be[i], 0, 0)),
                      pl.BlockSpec((None, None, MOE_FF, D), lambda i, be, nu: (l, be[i], 0, 0))],
            out_specs=pl.BlockSpec((MOE_ROWS, D), lambda i, be, nu: (i, 0))),
        out_shape=jax.ShapeDtypeStruct(xs.shape, f32),
        compiler_params=_cp(("arbitrary",)), name="moe_ffn")(blk_e, nused, xs, w1, w3, w2)


def _combine_kernel(dest_ref, x_ref, gate_ref, route_ref, ys_ref, o_ref, buf_a, buf_b, sem, *, tm):
    def row_copy(d, buf, r):
        return pltpu.make_async_copy(ys_ref.at[pl.ds(d, 1), :], buf.at[pl.ds(r, 1), :], sem)

    def issue(r, _):
        row_copy(dest_ref[2 * r], buf_a, r).start()
        row_copy(dest_ref[2 * r + 1], buf_b, r).start()
        return 0

    lax.fori_loop(0, tm, issue, 0)

    def drain(r, _):
        row_copy(0, buf_a, 0).wait()
        row_copy(0, buf_b, 0).wait()
        return 0

    lax.fori_loop(0, tm, drain, 0)
    route = route_ref[...]
    o_ref[...] = x_ref[...] + gate_ref[...] * (route[:, 2:3] * buf_a[...] + route[:, 3:4] * buf_b[...])


def _combine(x, gate, route, ys, dest_flat):
    b, t, _ = x.shape
    tm = min(256, t)
    nt = t // tm
    return pl.pallas_call(
        functools.partial(_combine_kernel, tm=tm), grid=(b, nt),
        in_specs=[pl.BlockSpec((2 * tm,), lambda b, i: (b * nt + i,), memory_space=pltpu.SMEM),
                  pl.BlockSpec((None, tm, D), lambda b, i: (b, i, 0)), _mod_spec(gate, tm),
                  pl.BlockSpec((None, tm, LANES), lambda b, i: (b, i, 0)), pl.BlockSpec(memory_space=pl.ANY)],
        out_specs=pl.BlockSpec((None, tm, D), lambda b, i: (b, i, 0)),
        out_shape=jax.ShapeDtypeStruct((b, t, D), f32),
        scratch_shapes=[pltpu.VMEM((tm, D), f32), pltpu.VMEM((tm, D), f32), pltpu.SemaphoreType.DMA(())],
        compiler_params=_cp(("arbitrary", "arbitrary")), name="moe_combine")(dest_flat, x, gate, route, ys)


def _moe(x, sh, sc, gate, g, wr, br, w1, w3, w2, l):
    b, t, _ = x.shape
    n = b * t
    h, route, cnt = _router(x, sh, sc, g, wr, br)
    counts = cnt[0, :MOE_EXPERTS].astype(i32)
    pcounts = (counts + MOE_ROWS - 1) // MOE_ROWS * MOE_ROWS
    pends = jnp.cumsum(pcounts)
    pstarts = pends - pcounts
    nblk = -(-2 * n // MOE_ROWS) + MOE_EXPERTS
    blk_e = jnp.minimum(jnp.searchsorted(pends, jnp.arange(nblk, dtype=i32) * MOE_ROWS, side='right'),
                        MOE_EXPERTS - 1).astype(i32)
    nused = (pends[-1:] // MOE_ROWS).astype(i32)
    ps = jnp.zeros((8, LANES), f32).at[:, :MOE_EXPERTS].set(pstarts.astype(f32)[None])
    dest = _plan(route.reshape(n, LANES), ps)
    dest_flat = dest[:, :2].reshape(2 * n)
    xs = _dispatch(h.reshape(n, D), dest_flat, nblk * MOE_ROWS)
    ys = _ffn(xs, blk_e, nused, w1, w3, w2, l)
    return _combine(x, gate, route, ys, dest_flat)


def _rope_tables(pos):
    half = HD // 2
    inv = ROPE_THETA ** (-jnp.arange(half, dtype=f32) / half)
    ang = pos.astype(f32)[:, None] * inv[None, :]
    cos, sin = jnp.cos(ang), jnp.sin(ang)
    return jnp.tile(jnp.concatenate([cos, cos], axis=1), (1, 2)), jnp.tile(jnp.concatenate([-sin, sin], axis=1), (1, 2))


def _seg_mats(width):
    nseg = width // HD
    seg = (jnp.arange(width)[:, None] // HD == jnp.arange(LANES)[None, :]).astype(bf16)
    return seg, seg.T


def _even_params(i, ev_w_in, ev_w_out, ev_conv_w, ev_conv_b, ev_dt_bias, ev_a_log, ev_d_skip, ev_ssd_norm, ev_q_norm,
                 ev_k_norm, ev_cmp_w):
    w = ev_w_in[i]
    o = [0, 1024, 2560, 2576, 3600, 5136, 5184]
    small = jnp.concatenate([w[:, o[2]:o[3]], w[:, o[5]:o[6]], jnp.zeros((D, 64), f32)], axis=1)
    ws = [w[:, o[0]:o[1]], w[:, o[1]:o[2]], w[:, o[3]:o[4]], w[:, o[4]:o[5]], small]
    pad = lambda v: jnp.zeros((1, LANES), f32).at[0, :v.shape[0]].set(v)
    ssd = [ev_conv_w[i], ev_conv_b[i][None], pad(ev_dt_bias[i]), pad(-jnp.exp(ev_a_log[i])),
           jnp.repeat(ev_d_skip[i], SSD_HD)[None], ev_ssd_norm[i][None]]
    segq, expq = _seg_mats(D)
    segk, expk = _seg_mats(1536)
    kn = ev_k_norm[i]
    z4 = jnp.zeros((256,), f32)
    gk = jnp.concatenate([jnp.tile(kn[0], 4), z4, jnp.tile(kn[1], 4), z4, jnp.tile(kn[2], 4), z4])[None]
    vm = jnp.concatenate([z4, z4 + 1, z4, z4 + 1, z4, z4 + 1])[None]
    cw = jnp.tile(ev_cmp_w[i], (1, 1, 4))
    prep = [segq, expq, segk, expk, jnp.tile(ev_q_norm[i], 16)[None], gk, vm, cw]
    lane = jnp.arange(LANES)[None, None, :, None]
    col = jnp.arange(256)[None, None, None, :]
    gidx = jnp.arange(NSA_KVH)[:, None, None, None]
    jidx = jnp.arange(3)[None, :, None, None]
    gexp = (lane == 16 + ((gidx * NSA_REP + col // HD) * 3 + jidx)).astype(bf16)
    wo = ev_w_out[i]
    return dict(ws=[a.astype(bf16) for a in ws], ssd=ssd, prep=prep, gexp=gexp, cmp_w=ev_cmp_w[i],
                wo=[wo[:SSD_INNER].astype(bf16), wo[SSD_INNER:].astype(bf16)])


def _odd_params(i, od_w_in, od_w_out, a_re, a_im, log_dt, b_re, b_im, c_re, c_im, d, glu_w, glu_b, wa2, ba, gnorm):
    w = od_w_in[i]
    o = [0, 512, 768, 1024, 1536, 2048, 2064]
    small = jnp.concatenate([w[:, o[5]:o[6]], jnp.zeros((D, LANES - GLA_RANK), f32)], axis=1)
    ws = [w[:, o[k]:o[k + 1]] for k in range(5)] + [small]
    are, aim = a_re[i], a_im[i]
    dt = jnp.exp(log_dt[i])[:, None]
    lr, li = are * dt, aim * dt
    ab_re, ab_im = jnp.exp(lr) * jnp.cos(li), jnp.exp(lr) * jnp.sin(li)
    den = are * are + aim * aim
    nr = ab_re - 1.0
    f_re = (nr * are + ab_im * aim) / den
    f_im = (ab_im * are - nr * aim) / den
    bb_re = f_re[..., None] * b_re[i] - f_im[..., None] * b_im[i]
    bb_im = f_re[..., None] * b_im[i] + f_im[..., None] * b_re[i]
    eye = jnp.eye(S5_GROUPS, dtype=f32)
    bdiag = lambda m: jnp.einsum('gpc,gh->gchp', m, eye).reshape(S5_CH, S5_N)
    cdiag = lambda m: jnp.einsum('gcp,gh->gphc', m, eye).reshape(S5_N, S5_CH)
    bd = jnp.concatenate([bdiag(bb_re), bdiag(bb_im)], axis=1).astype(bf16)
    kk = jnp.arange(1, 9, dtype=f32)[:, None, None]
    mag, ang = jnp.exp(lr[None] * kk), li[None] * kk
    pwr, pwi = (mag * jnp.cos(ang)).reshape(8, S5_N), (mag * jnp.sin(ang)).reshape(8, S5_N)
    s5 = [bd, pwr, pwi, cdiag(c_re[i]).astype(bf16), cdiag(-c_im[i]).astype(bf16), d[i][None],
          glu_w[i].astype(bf16), glu_b[i][None]]
    wa = jnp.zeros((LANES, 256), f32).at[:GLA_RANK].set(wa2[i]).astype(bf16)
    gla = [wa, ba[i][None], gnorm[i][None]]
    wo = od_w_out[i]
    return dict(ws=[a.astype(bf16) for a in ws], s5=s5, gla=gla, wo=[wo[:S5_CH].astype(bf16), wo[S5_CH:].astype(bf16)])


def _pad_t(a, t):
    return jnp.pad(a, ((0, 0), (0, t - a.shape[1])) + ((0, 0),) * (a.ndim - 2))


def _even_prompt(x, sh, sc, gate, g, prm):
    b, t, _ = x.shape
    z, xbc, q, kv, sm = _mod_proj(x, sh, sc, g, prm['ws'], "even_in")
    ya, convn, ssdn = _ssd(xbc, sm, z, jnp.zeros((b, 8, SSD_CONV_DIM), f32),
                           jnp.zeros((b, SSD_HEADS, SSD_HD, SSD_STATE), f32), prm['ssd'], t)
    cos_t, sin_t = _rope_tables(jnp.arange(t))
    (qc, qr, kcmp, vcmp, ksel, vsel, kwin, vwin, gates, kaug, vsa, kwh, vwa, kch, vch) = _nsa_prep(
        q, kv, sm, cos_t, sin_t, prm['prep'], True)
    nb = t // NSA_BLOCK
    padb = lambda a: jnp.pad(a, ((0, 0), (0, 0), (0, LANES - nb), (0, 0)))
    ob = _nsa_prompt(qc, qr, padb(kch), padb(vch), kaug, vsa, kwh, vwa, gates, prm['gexp'])
    xn = _out_proj(x, gate, [ya, ob], prm['wo'], "even_out")
    r5 = lambda a: a.reshape(b, t, NSA_KVH, HD)
    keep = min(NSA_WINDOW, t)
    st = (r5(kcmp), r5(vcmp), r5(ksel), r5(vsel), r5(kwin)[:, t - keep:], r5(vwin)[:, t - keep:], ssdn, convn[:, 5:8])
    return xn, st


def _stack_q(qb, bs, ts):
    q = qb.reshape(bs, ts, NSA_KVH, NSA_REP, HD)
    q = jnp.pad(q, ((0, 0), (0, 8 - ts), (0, 0), (0, 0), (0, 0)))
    q = q.transpose(0, 2, 3, 1, 4).reshape(bs, NSA_KVH, NSA_REP * 8, HD)
    return jnp.pad(q, ((0, 0), (0, 0), (0, LANES - NSA_REP * 8), (0, 0)))


def _unstack_o(o, bs, ts, transposed):
    if transposed:
        o = jnp.swapaxes(o, 2, 3)
    o = o[:, :, :NSA_REP * 8].reshape(bs, NSA_KVH, NSA_REP, 8, HD)[:, :, :, :ts]
    return o.transpose(0, 3, 1, 2, 4).reshape(1, bs * ts, D)


def _even_sample(x, sh, sc, gate, g, prm, bs, ts, li, conv_state, ssd_state, page_table, cmp_k, cmp_v, sel_k, sel_v,
                 win_k, win_v):
    n = bs * ts
    npg = page_table.shape[1]
    past = npg * PAGE
    z, xbc, q, kv, sm = _mod_proj(x, sh, sc, g, prm['ws'], "even_in_s")
    seq = lambda a: _pad_t(a.reshape(bs, ts, a.shape[-1]), CHUNK)
    conv0 = jnp.pad(conv_state, ((0, 0), (5, 0), (0, 0)))
    ya, convn, ssdn = _ssd(seq(xbc), seq(sm), seq(z), conv0, ssd_state, prm['ssd'], ts)
    ya = ya[:, :ts].reshape(1, n, SSD_INNER)
    cos_t, sin_t = _rope_tables(past + jnp.arange(n) % ts)
    qc, qr, kcmp, vcmp, ksel, vsel, kwin, vwin, gates = _nsa_prep(q, kv, sm, cos_t, sin_t, prm['prep'], False)
    r8 = lambda a: _pad_t(a.reshape(bs, ts, 256), 8)
    qc_st, qr_st = _stack_q(qc, bs, ts), _stack_q(qr, bs, ts)
    kcp, vcp = _page_compress(page_table, cmp_k, cmp_v, li, prm['cmp_w'])
    nb_past = 2 * npg
    nbp = -(-(nb_past + 1) // LANES) * LANES
    hmaj = lambda a: jnp.pad(a.transpose(0, 2, 1, 3), ((0, 0), (0, 0), (0, nbp - nb_past), (0, 0)))
    cidx = jnp.arange(LANES)
    mrep = ((cidx[:, None] < NSA_REP * 8) & (cidx[None, :] < NSA_REP * 8)
            & (cidx[:, None] % 8 == cidx[None, :] % 8)).astype(bf16)
    oc, neg = _sample_select(qc_st, hmaj(kcp), hmaj(vcp), r8(kcmp), r8(vcmp), prm['cmp_w'], mrep, past, ts)
    neg5 = neg[:, :, :nb_past].reshape(bs, NSA_KVH, npg, 2, LANES)
    os_t = _sample_selected(page_table, qr_st, sel_k, sel_v, li, neg5, r8(ksel), r8(vsel), ts)
    r84 = lambda a: r8(a).reshape(bs, 8, NSA_KVH, HD)
    ow_t, wkn, wvn = _sample_window(qr_st, win_k, win_v, li, r8(kwin), r8(vwin), r84(kwin), r84(vwin), ts)
    gt = gates[0, :, 16:16 + 3 * NSA_HEADS].reshape(1, n, NSA_HEADS, 3)
    gx = lambda j: jnp.repeat(gt[..., j], HD, axis=-1)
    ob = gx(0) * _unstack_o(oc, bs, ts, False) + gx(1) * _unstack_o(os_t, bs, ts, True) + gx(2) * _unstack_o(ow_t, bs, ts, True)
    xn = _out_proj(x, gate, [ya, ob], prm['wo'], "even_out_s")
    r5 = lambda a: a.reshape(bs, ts, NSA_KVH, HD)
    st = (r5(kcmp), r5(vcmp), r5(ksel), r5(vsel), wkn, wvn, ssdn, convn[:, 5:8])
    return xn, st


def _odd_layer(x, sh, sc, gate, g, prm, bs, ts, s5r0, s5i0, gla0):
    u, q, k, v, gg, sm = _mod_proj(x, sh, sc, g, prm['ws'], "odd_in")
    tp = -(-ts // CHUNK) * CHUNK
    seq = lambda a: _pad_t(a.reshape(bs, ts, a.shape[-1]), tp)
    yc, sr, si = _s5(seq(u), s5r0.reshape(bs, 1, S5_N), s5i0.reshape(bs, 1, S5_N), prm['s5'], ts)
    ridx = (ts - 1) % 8
    og, gn = _gla(seq(q), seq(k), seq(v), seq(gg), seq(sm), jnp.swapaxes(gla0, 2, 3), prm['gla'], ts)
    unseq = lambda a: a[:, :ts].reshape(x.shape[0], x.shape[1], a.shape[-1])
    xn = _out_proj(x, gate, [unseq(yc), unseq(og)], prm['wo'], "odd_out")
    st = (sr[:, ridx].reshape(bs, S5_GROUPS, S5_STATE), si[:, ridx].reshape(bs, S5_GROUPS, S5_STATE), jnp.swapaxes(gn, 2, 3))
    return xn, st


def kernel(x_prompt, x_sample, cache_cmp_k, cache_cmp_v, cache_sel_k, cache_sel_v, cache_win_k, cache_win_v, state_ssd, state_conv, state_s5_re, state_s5_im, state_gla, page_table, c_prompt, c_sample, ada_w, ada_b, norm_mix, norm_ffn, ev_w_in, ev_w_out, ev_conv_w, ev_conv_b, ev_dt_bias, ev_a_log, ev_d_skip, ev_ssd_norm, ev_q_norm, ev_k_norm, ev_cmp_w, od_w_in, od_w_out, od_s5_a_re, od_s5_a_im, od_s5_log_dt, od_s5_b_re, od_s5_b_im, od_s5_c_re, od_s5_c_im, od_s5_d, od_glu_w, od_glu_b, od_gla_wa2, od_gla_ba, od_gla_norm, moe_wc, moe_bc, moe_wf, moe_bf, moe_w1, moe_w3, moe_w2):
    bp, tp, _ = x_prompt.shape
    bs, ts, _ = x_sample.shape
    ns = bs * ts
    depth = ada_w.shape[0]
    bc = -(-(bp + bs) // 8) * 8
    c_all = jnp.zeros((bc, D), f32).at[:bp].set(c_prompt).at[bp:bp + bs].set(c_sample)
    mods = _ada(c_all, ada_w, ada_b)
    xp, xs = x_prompt, x_sample.reshape(1, ns, D)
    sp = {}
    ss = {}
    for l in range(depth):
        i = l // 2
        mp = [m[:, None, :] for m in jnp.split(mods[l, :bp], 6, axis=-1)]
        ms = [jnp.repeat(m, ts, axis=0)[None] for m in jnp.split(mods[l, bp:bp + bs], 6, axis=-1)]
        gm, gf = norm_mix[l][None], norm_ffn[l][None]
        if l % 2 == 0:
            prm = _even_params(i, ev_w_in, ev_w_out, ev_conv_w, ev_conv_b, ev_dt_bias, ev_a_log, ev_d_skip, ev_ssd_norm,
                               ev_q_norm, ev_k_norm, ev_cmp_w)
            xp, st_p = _even_prompt(xp, mp[0], mp[1], mp[2], gm, prm)
            xs, st_s = _even_sample(xs, ms[0], ms[1], ms[2], gm, prm, bs, ts, i, state_conv[i], state_ssd[i], page_table,
                                    cache_cmp_k, cache_cmp_v, cache_sel_k, cache_sel_v, cache_win_k, cache_win_v)
            names = ('cmp_k', 'cmp_v', 'sel_k', 'sel_v', 'win_k', 'win_v', 'ssd', 'conv')
        else:
            prm = _odd_params(i, od_w_in, od_w_out, od_s5_a_re, od_s5_a_im, od_s5_log_dt, od_s5_b_re, od_s5_b_im,
                              od_s5_c_re, od_s5_c_im, od_s5_d, od_glu_w, od_glu_b, od_gla_wa2, od_gla_ba, od_gla_norm)
            zs = jnp.zeros((bp, S5_GROUPS, S5_STATE), f32)
            xp, st_p = _odd_layer(xp, mp[0], mp[1], mp[2], gm, prm, bp, tp, zs, zs,
                                  jnp.zeros((bp, GLA_HEADS, GLA_DK, GLA_DV), f32))
            xs, st_s = _odd_layer(xs, ms[0], ms[1], ms[2], gm, prm, bs, ts, state_s5_re[i], state_s5_im[i], state_gla[i])
            names = ('s5_re', 's5_im', 'gla')
        for nm, a_p, a_s in zip(names, st_p, st_s):
            sp.setdefault(nm, []).append(a_p)
            ss.setdefault(nm, []).append(a_s)
        wr = jnp.zeros((D, LANES), f32).at[:, :MOE_GROUPS].set(moe_wc[l]).at[:, MOE_GROUPS:MOE_GROUPS + MOE_EXPERTS].set(moe_wf[l])
        br = jnp.zeros((1, LANES), f32).at[0, :MOE_GROUPS].set(moe_bc[l]).at[0, MOE_GROUPS:MOE_GROUPS + MOE_EXPERTS].set(moe_bf[l])
        xp = _moe(xp, mp[3], mp[4], mp[5], gf, wr, br, moe_w1, moe_w3, moe_w2, l)
        xs = _moe(xs, ms[3], ms[4], ms[5], gf, wr, br, moe_w1, moe_w3, moe_w2, l)
    order = ('cmp_k', 'cmp_v', 'sel_k', 'sel_v', 'win_k', 'win_v', 'ssd', 'conv', 's5_re', 's5_im', 'gla')
    outs = [xp, xs.reshape(bs, ts, D)]
    for nm in order:
        outs += [jnp.stack(sp[nm]), jnp.stack(ss[nm])]
    return tuple(outs)
```

```python
import functools
import math

import jax
import jax.numpy as jnp
from jax import lax
from jax.experimental import pallas as pl
from jax.experimental.pallas import tpu as pltpu

f32 = jnp.float32
bf16 = jnp.bfloat16
i32 = jnp.int32

D = 1024
PAGE = 128
SSD_HEADS, SSD_HD, SSD_INNER, SSD_GROUPS, SSD_STATE, SSD_CONV = 16, 64, 1024, 4, 64, 4
SSD_CONV_DIM = SSD_INNER + 2 * SSD_GROUPS * SSD_STATE
NSA_HEADS, NSA_KVH, NSA_REP, HD, NSA_BLOCK, NSA_TOPN, NSA_WINDOW = 16, 4, 4, 64, 64, 16, 512
ROPE_THETA = 10000.0
S5_CH, S5_GCH, S5_GROUPS, S5_STATE = 512, 16, 32, 64
S5_N = S5_GROUPS * S5_STATE
GLA_HEADS, GLA_DK, GLA_DV, GLA_RANK, GLA_TEMP = 4, 64, 128, 16, 16.0
MOE_GROUPS, MOE_PER_GROUP, MOE_EXPERTS, MOE_FF = 4, 8, 32, 256
MOE_ROWS = 256
EPS = 1e-6
NEG = -1e30
BIG = 1e30
GONE = -3e38
LANES = 128
CHUNK = 128
FLASH_SPLIT = 1
VMEM_LIMIT = 48 * 2**20


def _cp(sem, vmem=VMEM_LIMIT):
    return pltpu.CompilerParams(dimension_semantics=sem, vmem_limit_bytes=vmem)


def _sigmoid(x):
    return 1.0 / (1.0 + jnp.exp(-x))


def _softplus(x):
    return jnp.maximum(x, 0.0) + jnp.log1p(jnp.exp(-jnp.abs(x)))


def _dotf(a, b):
    return jnp.dot(a, b, preferred_element_type=f32)


def _dot_nt(a, b):
    return lax.dot_general(a, b, (((1,), (1,)), ((), ())), preferred_element_type=f32)


def _dot_tn(a, b):
    return lax.dot_general(a, b, (((0,), (0,)), ((), ())), preferred_element_type=f32)


def _split2(a):
    h = a.astype(bf16)
    return h, (a - h.astype(f32)).astype(bf16)


def _split3(a):
    h = a.astype(bf16)
    r = a - h.astype(f32)
    m = r.astype(bf16)
    return h, m, (r - m.astype(f32)).astype(bf16)


def _dot3_r(a, w):
    h, m, l = _split3(a)
    return _dotf(h, w) + _dotf(m, w) + _dotf(l, w)


def _dot3_l(w, a):
    h, m, l = _split3(a)
    return _dotf(w, h) + _dotf(w, m) + _dotf(w, l)


def _iota(shape, dim):
    return lax.broadcasted_iota(i32, shape, dim)


def _tri(n):
    return _iota((n, n), 0) >= _iota((n, n), 1)


def _modulate(x, sh, sc, g):
    ms = jnp.mean(x * x, axis=-1, keepdims=True)
    return (x * lax.rsqrt(ms + EPS) * g) * (1.0 + sc) + sh


def _mod_spec(mod, tm):
    if mod.shape[1] == 1:
        return pl.BlockSpec((None, 1, D), lambda b, i: (b, 0, 0))
    return pl.BlockSpec((None, tm, D), lambda b, i: (b, i, 0))


def _full(a):
    n = a.ndim
    return pl.BlockSpec(a.shape, lambda *_: (0,) * n)


def _ada_kernel(c_ref, w_ref, b_ref, o_ref):
    c = c_ref[...]
    a = c * _sigmoid(c)
    ah, al = _split2(a)
    wh, wl = _split2(w_ref[...])
    o_ref[...] = _dotf(ah, wh) + _dotf(al, wh) + _dotf(ah, wl) + b_ref[...]


def _ada(c_all, ada_w, ada_b):
    depth, bc, tn = ada_w.shape[0], c_all.shape[0], 1536
    return pl.pallas_call(
        _ada_kernel, grid=(depth, 6 * D // tn),
        in_specs=[pl.BlockSpec((bc, D), lambda l, j: (0, 0)),
                  pl.BlockSpec((None, D, tn), lambda l, j: (l, 0, j)),
                  pl.BlockSpec((None, 1, tn), lambda l, j: (l, 0, j))],
        out_specs=pl.BlockSpec((None, bc, tn), lambda l, j: (l, 0, j)),
        out_shape=jax.ShapeDtypeStruct((depth, bc, 6 * D), f32),
        compiler_params=_cp(("parallel", "parallel")), name="ada")(c_all, ada_w, ada_b.reshape(depth, 1, 6 * D))


def _proj_kernel(x_ref, sh_ref, sc_ref, g_ref, *refs, nseg):
    h = _modulate(x_ref[...], sh_ref[...], sc_ref[...], g_ref[...]).astype(bf16)
    for i in range(nseg):
        refs[nseg + i][...] = _dotf(h, refs[i][...])


def _mod_proj(x, sh, sc, g, ws, name):
    b, t, _ = x.shape
    tm = min(256, t)
    nseg = len(ws)
    return pl.pallas_call(
        functools.partial(_proj_kernel, nseg=nseg), grid=(b, t // tm),
        in_specs=[pl.BlockSpec((None, tm, D), lambda b, i: (b, i, 0)), _mod_spec(sh, tm), _mod_spec(sc, tm),
                  pl.BlockSpec((1, D), lambda b, i: (0, 0))] + [_full(w) for w in ws],
        out_specs=[pl.BlockSpec((None, tm, w.shape[1]), lambda b, i: (b, i, 0)) for w in ws],
        out_shape=[jax.ShapeDtypeStruct((b, t, w.shape[1]), f32) for w in ws],
        compiler_params=_cp(("parallel", "parallel")), name=name)(x, sh, sc, g, *ws)


def _outproj_kernel(x_ref, gate_ref, *refs, nseg):
    acc = _dotf(refs[0][...].astype(bf16), refs[nseg][...])
    for i in range(1, nseg):
        acc = acc + _dotf(refs[i][...].astype(bf16), refs[nseg + i][...])
    refs[2 * nseg][...] = x_ref[...] + gate_ref[...] * acc


def _out_proj(x, gate, acts, ws, name):
    b, t, _ = x.shape
    tm = min(512, t)
    nseg = len(acts)
    return pl.pallas_call(
        functools.partial(_outproj_kernel, nseg=nseg), grid=(b, t // tm),
        in_specs=[pl.BlockSpec((None, tm, D), lambda b, i: (b, i, 0)), _mod_spec(gate, tm)]
        + [pl.BlockSpec((None, tm, a.shape[2]), lambda b, i: (b, i, 0)) for a in acts] + [_full(w) for w in ws],
        out_specs=pl.BlockSpec((None, tm, D), lambda b, i: (b, i, 0)),
        out_shape=jax.ShapeDtypeStruct((b, t, D), f32),
        compiler_params=_cp(("parallel", "parallel")), name=name)(x, gate, *acts, *ws)


def _ssd_kernel(xbc_ref, dt_ref, z_ref, conv0_ref, h0_ref, cw_ref, cb_ref, dtb_ref, a_ref, dsk_ref, nrm_ref,
                y_ref, convn_ref, hn_ref, xpad, hst, ybuf, *, nc, t_valid):
    L = CHUNK
    c = pl.program_id(1)

    @pl.when(c == 0)
    def _():
        hst[...] = h0_ref[...]
        xpad[0:8, :] = conv0_ref[...]

    xpad[8:8 + L, :] = xbc_ref[...]
    acc = cb_ref[...] + cw_ref[0:1, :] * xpad[5:5 + L, :]
    for k in range(1, SSD_CONV):
        acc = acc + cw_ref[k:k + 1, :] * xpad[5 + k:5 + k + L, :]
    tv_last = t_valid - (nc - 1) * L
    convn_ref[...] = xpad[tv_last:tv_last + 8, :]
    xpad[0:8, :] = xpad[L:L + 8, :]

    xc = acc * _sigmoid(acc)
    xs = xc[:, :SSD_INNER]
    bm = xc[:, SSD_INNER:SSD_INNER + 256]
    cm = xc[:, SSD_INNER + 256:]
    dt = _softplus(dt_ref[...] + dtb_ref[...])
    if t_valid < nc * L:
        dt = jnp.where(c * L + _iota((L, LANES), 0) < t_valid, dt, 0.0)
    causal = _tri(L)
    trib = causal.astype(bf16)
    cs = _dot3_l(trib, dt * a_ref[...])
    cs_t, dt_t = cs.T, dt.T
    wend_t = jnp.exp(cs_t[:, L - 1:L] - cs_t) * dt_t
    ecs = jnp.exp(cs)
    xs_t = xs.T
    for g in range(SSD_GROUPS):
        bg = bm[:, g * 64:(g + 1) * 64].astype(bf16)
        cg = cm[:, g * 64:(g + 1) * 64].astype(bf16)
        gmat = _dot_nt(cg, bg)
        for r in range(SSD_HEADS // SSD_GROUPS):
            h = g * (SSD_HEADS // SSD_GROUPS) + r
            seg = cs[:, h:h + 1] - cs_t[h:h + 1, :]
            dec = jnp.where(causal, jnp.exp(jnp.where(causal, seg, 0.0)), 0.0)
            sc = (gmat * dec * dt_t[h:h + 1, :]).astype(bf16)
            hs = hst[h]
            yh = _dotf(sc, xs[:, h * 64:(h + 1) * 64].astype(bf16)) + _dot_nt(cg, hs.astype(bf16)) * ecs[:, h:h + 1]
            ybuf[:, h * 64:(h + 1) * 64] = yh
            xw = (xs_t[h * 64:(h + 1) * 64, :] * wend_t[h:h + 1, :]).astype(bf16)
            hst[h] = hs * ecs[L - 1:L, h:h + 1] + _dotf(xw, bg)
    zz = z_ref[...]
    y = (ybuf[...] + dsk_ref[...] * xs) * (zz * _sigmoid(zz))
    gw = SSD_INNER // SSD_GROUPS
    for g in range(SSD_GROUPS):
        s = y[:, g * gw:(g + 1) * gw]
        ms = jnp.mean(s * s, axis=-1, keepdims=True)
        y_ref[:, g * gw:(g + 1) * gw] = s * lax.rsqrt(ms + EPS) * nrm_ref[:, g * gw:(g + 1) * gw]

    @pl.when(c == nc - 1)
    def _():
        hn_ref[...] = hst[...]


def _ssd(xbc, sm, z, conv0, h0, prm, t_valid):
    b, t, _ = xbc.shape
    nc = t // CHUNK
    tok = lambda w: pl.BlockSpec((None, CHUNK, w), lambda b, c: (b, c, 0))
    return pl.pallas_call(
        functools.partial(_ssd_kernel, nc=nc, t_valid=t_valid), grid=(b, nc),
        in_specs=[tok(SSD_CONV_DIM), tok(LANES), tok(SSD_INNER),
                  pl.BlockSpec((None, 8, SSD_CONV_DIM), lambda b, c: (b, 0, 0)),
                  pl.BlockSpec((None, SSD_HEADS, SSD_HD, SSD_STATE), lambda b, c: (b, 0, 0, 0))]
        + [_full(p) for p in prm],
        out_specs=[tok(SSD_INNER), pl.BlockSpec((None, 8, SSD_CONV_DIM), lambda b, c: (b, 0, 0)),
                   pl.BlockSpec((None, SSD_HEADS, SSD_HD, SSD_STATE), lambda b, c: (b, 0, 0, 0))],
        out_shape=[jax.ShapeDtypeStruct((b, t, SSD_INNER), f32), jax.ShapeDtypeStruct((b, 8, SSD_CONV_DIM), f32),
                   jax.ShapeDtypeStruct((b, SSD_HEADS, SSD_HD, SSD_STATE), f32)],
        scratch_shapes=[pltpu.VMEM((CHUNK + 8, SSD_CONV_DIM), f32), pltpu.VMEM((SSD_HEADS, SSD_HD, SSD_STATE), f32),
                        pltpu.VMEM((CHUNK, SSD_INNER), f32)],
        compiler_params=_cp(("parallel", "arbitrary")), name="ssd")(xbc, sm, z, conv0, h0, *prm)


def _seg_rinv(x, seg_ref, exp_ref):
    x2 = x * x
    h, l = _split2(x2)
    ss = _dotf(h, seg_ref[...]) + _dotf(l, seg_ref[...])
    return _dot3_r(lax.rsqrt(ss * (1.0 / HD) + EPS), exp_ref[...])


def _rope(x, cosf, sinf):
    w = x.shape[1]
    first = (_iota(x.shape, 1) % HD) < (HD // 2)
    rot = jnp.where(first, pltpu.roll(x, w - HD // 2, 1), pltpu.roll(x, HD // 2, 1))
    return x * cosf + rot * sinf


def _nsaprep_kernel(q_ref, kv_ref, sm_ref, cos_ref, sin_ref, segq_ref, expq_ref, segk_ref, expk_ref, gq_ref, gk_ref,
                    vm_ref, cw_ref, *outs, tm, compress):
    qc_ref, qr_ref, kcmp_ref, vcmp_ref, ksel_ref, vsel_ref, kwin_ref, vwin_ref, gate_ref = outs[:9]
    i = pl.program_id(1)
    cos1, sin1 = cos_ref[...], sin_ref[...]
    q = q_ref[...]
    qn = q * _seg_rinv(q, segq_ref, expq_ref) * gq_ref[...]
    qc_ref[...] = (qn * 0.125).astype(bf16)
    qr_ref[...] = (_rope(qn, jnp.concatenate([cos1] * 8, axis=1), jnp.concatenate([sin1] * 8, axis=1)) * 0.125).astype(bf16)
    kv = kv_ref[...]
    kvn = kv * (_seg_rinv(kv, segk_ref, expk_ref) * gk_ref[...] + vm_ref[...])
    cos2, sin2 = jnp.concatenate([cos1] * 2, axis=1), jnp.concatenate([sin1] * 2, axis=1)
    kcmp, vcmp = kvn[:, 0:256], kvn[:, 256:512]
    ksel, vsel = _rope(kvn[:, 512:768], cos2, sin2), kvn[:, 768:1024]
    kwin, vwin = _rope(kvn[:, 1024:1280], cos2, sin2), kvn[:, 1280:1536]
    kcmp_ref[...] = kcmp
    vcmp_ref[...] = vcmp
    ksel_ref[...] = ksel
    vsel_ref[...] = vsel
    kwin_ref[...] = kwin
    vwin_ref[...] = vwin
    gate_ref[...] = _sigmoid(sm_ref[...])
    if compress:
        kaug_ref, vsa_ref, kwh_ref, vwa_ref, kch_ref, vch_ref = outs[9:]
        blk = (i * tm + _iota((tm, LANES), 0)) // NSA_BLOCK
        onehot = (_iota((tm, LANES), 1) == blk).astype(bf16)
        ones_t = (_iota((HD, tm), 0) == 0).astype(bf16)
        zero = jnp.zeros((tm, HD), bf16)
        kcw = (kcmp.reshape(tm // NSA_BLOCK, NSA_BLOCK, 256) * cw_ref[0][None]).sum(axis=1)
        vcw = (vcmp.reshape(tm // NSA_BLOCK, NSA_BLOCK, 256) * cw_ref[1][None]).sum(axis=1)
        vsel_t, vwin_t = vsel.T, vwin.T
        for g in range(NSA_KVH):
            sl = slice(g * HD, (g + 1) * HD)
            kaug_ref[g, :, 0:HD] = ksel[:, sl].astype(bf16)
            kaug_ref[g, :, HD:2 * HD] = zero
            kaug_ref[g, :, 2 * HD:] = onehot
            vsa_ref[g, 0:HD, :] = vsel_t[sl, :].astype(bf16)
            vsa_ref[g, HD:, :] = ones_t
            kwh_ref[g] = kwin[:, sl].astype(bf16)
            vwa_ref[g, 0:HD, :] = vwin_t[sl, :].astype(bf16)
            vwa_ref[g, HD:, :] = ones_t
            kch_ref[g] = kcw[:, sl]
            vch_ref[g] = vcw[:, sl]


def _nsa_prep(q, kv, sm, cos_t, sin_t, prm, compress):
    b, t, _ = q.shape
    tm = min(512, t)
    tok = lambda w: pl.BlockSpec((None, tm, w), lambda b, i: (b, i, 0))
    tab = pl.BlockSpec((tm, LANES), lambda b, i: (i, 0))
    hm = lambda w: pl.BlockSpec((None, NSA_KVH, tm, w), lambda b, i: (b, 0, i, 0))
    out_specs = [tok(D), tok(D)] + [tok(256)] * 6 + [tok(LANES)]
    out_shape = [jax.ShapeDtypeStruct((b, t, D), bf16)] * 2 + [jax.ShapeDtypeStruct((b, t, 256), f32)] * 6 \
        + [jax.ShapeDtypeStruct((b, t, LANES), f32)]
    if compress:
        nbt = tm // NSA_BLOCK
        cspec = pl.BlockSpec((None, NSA_KVH, nbt, HD), lambda b, i: (b, 0, i, 0))
        vt = pl.BlockSpec((None, NSA_KVH, LANES, tm), lambda b, i: (b, 0, 0, i))
        out_specs += [hm(256), vt, hm(HD), vt, cspec, cspec]
        out_shape += [jax.ShapeDtypeStruct((b, NSA_KVH, t, 256), bf16), jax.ShapeDtypeStruct((b, NSA_KVH, LANES, t), bf16),
                      jax.ShapeDtypeStruct((b, NSA_KVH, t, HD), bf16), jax.ShapeDtypeStruct((b, NSA_KVH, LANES, t), bf16),
                      jax.ShapeDtypeStruct((b, NSA_KVH, t // NSA_BLOCK, HD), f32),
                      jax.ShapeDtypeStruct((b, NSA_KVH, t // NSA_BLOCK, HD), f32)]
    return pl.pallas_call(
        functools.partial(_nsaprep_kernel, tm=tm, compress=compress), grid=(b, t // tm),
        in_specs=[tok(D), tok(1536), tok(LANES), tab, tab] + [_full(p) for p in prm],
        out_specs=out_specs, out_shape=out_shape,
        compiler_params=_cp(("parallel", "parallel")), name="nsa_prep")(q, kv, sm, cos_t, sin_t, *prm)


def _cmp_branch(q_heads, kc, vc, tpos):
    nb, nq = kc.shape[0], tpos.shape[1]
    nrow = _iota((nb, nq), 0)
    ok = ((nrow + 1) * NSA_BLOCK - 1) <= tpos
    imp = jnp.zeros((nb, nq), f32)
    outs = []
    for q in q_heads:
        s = jnp.where(ok, _dot_nt(kc, q), NEG)
        m = jnp.max(s, axis=0, keepdims=True)
        e = jnp.where(ok, jnp.exp(s - m), 0.0)
        l = jnp.sum(e, axis=0, keepdims=True)
        p = e / jnp.where(l > 0.0, l, 1.0)
        imp = imp + p
        outs.append(_dot_tn(p.astype(bf16), vc))
    cur = tpos // NSA_BLOCK
    impm = jnp.where((nrow == cur) | (nrow == 0), BIG, jnp.where(nrow < cur, imp, NEG))

    def pick(_, carry):
        v, sel = carry
        mx = jnp.max(v, axis=0, keepdims=True)
        idx = jnp.min(jnp.where(v == mx, nrow, nb), axis=0, keepdims=True)
        hit = nrow == idx
        return jnp.where(hit, GONE, v), jnp.where(hit, 1.0, sel)

    _, sel = lax.fori_loop(0, NSA_TOPN, pick, (impm, jnp.zeros((nb, nq), f32)))
    neg = jnp.where((sel > 0.5) & (impm > 0.5 * NEG), 0.0, NEG)
    return outs, neg


def _flash_t(lhs, k_ref, vt_ref, lo, hi, trow, tk, m_ref, acc_ref, window):
    m_ref[...] = jnp.full(m_ref.shape, NEG, f32)
    acc_ref[...] = jnp.zeros(acc_ref.shape, f32)
    nq = lhs.shape[0]
    qs = nq // FLASH_SPLIT

    def step(kt, masked):
        ks = pl.multiple_of(kt * tk, tk)
        kt_tile = k_ref[pl.ds(ks, tk), :]
        vt_tile = vt_ref[:, pl.ds(ks, tk)]
        for h in range(FLASH_SPLIT):
            cols = slice(h * qs, (h + 1) * qs)
            s = _dot_nt(kt_tile, lhs[cols])
            if masked:
                kpos = ks + _iota((tk, 1), 0)
                keep = kpos <= trow[:, cols]
                if window:
                    keep = keep & (kpos > trow[:, cols] - NSA_WINDOW)
                s = jnp.where(keep, s, NEG)
            m_old = m_ref[:, cols]
            m_new = jnp.maximum(m_old, jnp.max(s, axis=0, keepdims=True))
            p = jnp.exp(s - m_new).astype(bf16)
            acc_ref[:, cols] = jnp.exp(m_old - m_new) * acc_ref[:, cols] + _dotf(vt_tile, p)
            m_ref[:, cols] = m_new

    def body(masked):
        def f(kt, _):
            step(kt, masked)
            return 0
        return f

    if window:
        lax.fori_loop(lo, hi, body(True), 0)
    else:
        lax.fori_loop(lo, hi - 1, body(False), 0)
        step(hi - 1, True)
    acc = acc_ref[...]
    return acc[0:HD, :] / acc[HD:HD + 1, :]


def _nsa_kernel(qc_ref, qr_ref, kc_ref, vc_ref, kaug_ref, vsa_ref, kw_ref, vwa_ref, gate_ref, gexp_ref, o_ref,
                lhs_ref, m_ref, acc_ref, *, tq, tk):
    qi = pl.program_id(2)
    t0 = qi * tq
    tpos = t0 + _iota((1, tq), 1)
    qh = [qc_ref[:, r * HD:(r + 1) * HD] for r in range(NSA_REP)]
    oc, neg = _cmp_branch(qh, kc_ref[...].astype(bf16), vc_ref[...].astype(bf16), tpos)
    negq = neg.T.astype(bf16)
    for r in range(NSA_REP):
        rows = slice(r * tq, (r + 1) * tq)
        lhs_ref[rows, 0:HD] = qr_ref[:, r * HD:(r + 1) * HD]
        lhs_ref[rows, HD:2 * HD] = jnp.zeros((tq, HD), bf16)
        lhs_ref[rows, 2 * HD:] = negq
    trow = t0 + _iota((1, NSA_REP * tq), 1) % tq
    hi = (t0 + tq) // tk
    os_t = _flash_t(lhs_ref[...], kaug_ref, vsa_ref, 0, hi, trow, tk, m_ref, acc_ref, False)
    lo = jnp.maximum(t0 - (NSA_WINDOW - 1), 0) // tk
    ow_t = _flash_t(lhs_ref[:, 0:HD], kw_ref, vwa_ref, lo, hi, trow, tk, m_ref, acc_ref, True)
    gates = gate_ref[...]
    unstack = lambda a: jnp.concatenate([a[:, r * tq:(r + 1) * tq] for r in range(NSA_REP)], axis=0).T
    o = _dot3_r(gates, gexp_ref[0]) * jnp.concatenate(oc, axis=1) \
        + _dot3_r(gates, gexp_ref[1]) * unstack(os_t) + _dot3_r(gates, gexp_ref[2]) * unstack(ow_t)
    o_ref[...] = o.astype(bf16)


def _nsa_prompt(qc, qr, kch, vch, kaug, vsa, kwh, vwa, gates, gexp):
    b, t, _ = qc.shape
    tq = tk = min(256, t)
    nbp = kch.shape[2]
    qspec = pl.BlockSpec((None, tq, 256), lambda b, g, i: (b, i, g))
    kvspec = lambda n, w: pl.BlockSpec((None, None, n, w), lambda b, g, i: (b, g, 0, 0))
    return pl.pallas_call(
        functools.partial(_nsa_kernel, tq=tq, tk=tk), grid=(b, NSA_KVH, t // tq),
        in_specs=[qspec, qspec, kvspec(nbp, HD), kvspec(nbp, HD), kvspec(t, 256), kvspec(LANES, t), kvspec(t, HD),
                  kvspec(LANES, t), pl.BlockSpec((None, tq, LANES), lambda b, g, i: (b, i, 0)),
                  pl.BlockSpec((None, 3, LANES, 256), lambda b, g, i: (g, 0, 0, 0))],
        out_specs=qspec, out_shape=jax.ShapeDtypeStruct((b, t, D), bf16),
        scratch_shapes=[pltpu.VMEM((NSA_REP * tq, 256), bf16), pltpu.VMEM((1, NSA_REP * tq), f32),
                        pltpu.VMEM((LANES, NSA_REP * tq), f32)],
        compiler_params=_cp(("parallel", "parallel", "arbitrary")), name="nsa_prompt")(
            qc, qr, kch, vch, kaug, vsa, kwh, vwa, gates, gexp)


PCH = 8


def _chunk_copies(cache_ref, li, pt_ref, b, c, buf, slot, sem):
    return [pltpu.make_async_copy(cache_ref.at[li, pt_ref[b, c * PCH + p]], buf.at[slot, p], sem.at[slot])
            for p in range(PCH)]


def _pagecmp_kernel(pt_ref, wk_ref, wv_ref, ck_ref, cv_ref, kc_ref, vc_ref, kbuf, vbuf, sem, *, li, npg):
    b = pl.program_id(0)
    nch = npg // PCH
    nbo = kc_ref.shape[1]

    def copies(c, slot):
        return _chunk_copies(ck_ref, li, pt_ref, b, c, kbuf, slot, sem) + _chunk_copies(cv_ref, li, pt_ref, b, c, vbuf, slot, sem)

    for cp in copies(0, 0):
        cp.start()
    kc_ref[...] = jnp.zeros(kc_ref.shape, f32)
    vc_ref[...] = jnp.zeros(vc_ref.shape, f32)
    rowblk = _iota((PAGE, nbo), 0) // NSA_BLOCK
    col = _iota((PAGE, nbo), 1)

    def body(c, _):
        slot = c % 2

        @pl.when(c + 1 < nch)
        def _():
            for cp in copies(c + 1, 1 - slot):
                cp.start()

        for cp in copies(c, slot):
            cp.wait()
        for p in range(PCH):
            place = (col == 2 * (c * PCH + p) + rowblk).astype(bf16)
            kc_ref[...] += _dotf((kbuf[slot, p] * wk_ref[...]).astype(bf16), place)
            vc_ref[...] += _dotf((vbuf[slot, p] * wv_ref[...]).astype(bf16), place)
        return 0

    lax.fori_loop(0, nch, body, 0)


def _page_compress(page_table, cache_k, cache_v, li, wk_t, wv_t):
    bs, npg = page_table.shape
    nbo = -(-2 * npg // LANES) * LANES
    out = pl.BlockSpec((None, 256, nbo), lambda b, pt: (b, 0, 0))
    shp = jax.ShapeDtypeStruct((bs, 256, nbo), f32)
    anyspec = pl.BlockSpec(memory_space=pl.ANY)
    return pl.pallas_call(
        functools.partial(_pagecmp_kernel, li=li, npg=npg),
        grid_spec=pltpu.PrefetchScalarGridSpec(
            num_scalar_prefetch=1, grid=(bs,),
            in_specs=[pl.BlockSpec((256, PAGE), lambda b, pt: (0, 0)), pl.BlockSpec((256, PAGE), lambda b, pt: (0, 0)),
                      anyspec, anyspec],
            out_specs=[out, out],
            scratch_shapes=[pltpu.VMEM((2, PCH, 256, PAGE), f32), pltpu.VMEM((2, PCH, 256, PAGE), f32),
                            pltpu.SemaphoreType.DMA((2,))]),
        out_shape=[shp, shp], compiler_params=_cp(("arbitrary",)), name="page_compress")(
            page_table, wk_t, wv_t, cache_k, cache_v)


def _ssel_kernel(qc_ref, kc_ref, vc_ref, kn_ref, vn_ref, w_ref, mrep_ref, oc_ref, neg_ref, *, past, ts):
    nbp = kc_ref.shape[1]
    nb_past = past // NSA_BLOCK
    col = _iota((1, LANES), 1)
    tpos = past + col % 8
    newrow = (_iota((nbp, 1), 0) == nb_past).astype(f32)
    tmask = (_iota((8, 1), 0) < ts).astype(f32)
    for g in range(NSA_KVH):
        sl = slice(g * HD, (g + 1) * HD)
        kc_new = jnp.sum(kn_ref[:, sl] * tmask * w_ref[0, 0:8, :], axis=0, keepdims=True)
        vc_new = jnp.sum(vn_ref[:, sl] * tmask * w_ref[1, 0:8, :], axis=0, keepdims=True)
        kc = (kc_ref[g] + newrow * kc_new).astype(bf16)
        vc = (vc_ref[g] + newrow * vc_new).astype(bf16)
        q = qc_ref[g]
        nrow = _iota((nbp, LANES), 0)
        ok = ((nrow + 1) * NSA_BLOCK - 1) <= tpos
        s = jnp.where(ok, _dot_nt(kc, q), NEG)
        m = jnp.max(s, axis=0, keepdims=True)
        e = jnp.where(ok, jnp.exp(s - m), 0.0)
        l = jnp.sum(e, axis=0, keepdims=True)
        p = e / jnp.where(l > 0.0, l, 1.0)
        oc_ref[g] = _dot_tn(p.astype(bf16), vc)
        imp = _dot3_r(p, mrep_ref[...])
        cur = tpos // NSA_BLOCK
        impm = jnp.where((nrow == cur) | (nrow == 0), BIG, jnp.where(nrow < cur, imp, NEG))

        def pick(_, carry):
            v, sel = carry
            mx = jnp.max(v, axis=0, keepdims=True)
            idx = jnp.min(jnp.where(v == mx, nrow, nbp), axis=0, keepdims=True)
            hit = nrow == idx
            return jnp.where(hit, GONE, v), jnp.where(hit, 1.0, sel)

        _, sel = lax.fori_loop(0, NSA_TOPN, pick, (impm, jnp.zeros((nbp, LANES), f32)))
        neg_ref[g] = jnp.where((sel > 0.5) & (impm > 0.5 * NEG), 0.0, NEG)


def _sample_select(qc_st, kc_h, vc_h, kn, vn, cmp_w, mrep, past, ts):
    bs, _, nbp, _ = kc_h.shape
    b4 = lambda n, w: pl.BlockSpec((None, NSA_KVH, n, w), lambda b: (b, 0, 0, 0))
    return pl.pallas_call(
        functools.partial(_ssel_kernel, past=past, ts=ts), grid=(bs,),
        in_specs=[b4(LANES, HD), b4(nbp, HD), b4(nbp, HD), pl.BlockSpec((None, 8, 256), lambda b: (b, 0, 0)),
                  pl.BlockSpec((None, 8, 256), lambda b: (b, 0, 0)), _full(cmp_w), _full(mrep)],
        out_specs=[b4(LANES, HD), b4(nbp, LANES)],
        out_shape=[jax.ShapeDtypeStruct((bs, NSA_KVH, LANES, HD), f32), jax.ShapeDtypeStruct((bs, NSA_KVH, nbp, LANES), f32)],
        compiler_params=_cp(("parallel",)), name="sample_select")(qc_st, kc_h, vc_h, kn, vn, cmp_w, mrep)


def _new_rows_mask(ts):
    tt = _iota((LANES, LANES), 0) % 8
    lane = _iota((LANES, LANES), 1)
    return (lane <= tt) & (lane < ts)


def _diag_blocks(o_ref, acc, l):
    nr = LANES // NSA_KVH
    for g in range(NSA_KVH):
        o_ref[g] = acc[g * nr:(g + 1) * nr, g * HD:(g + 1) * HD] / l[g * nr:(g + 1) * nr, :]


def _spage_kernel(pt_ref, q_ref, slab_ref, e_ref, kn_ref, vn_ref, ck_ref, cv_ref, o_ref, kbuf, vbuf, sem, m_s, l_s, acc_s,
                  *, li, npg, ts):
    b = pl.program_id(0)
    nch = npg // PCH

    def copies(c, slot):
        return _chunk_copies(ck_ref, li, pt_ref, b, c, kbuf, slot, sem) + _chunk_copies(cv_ref, li, pt_ref, b, c, vbuf, slot, sem)

    for cp in copies(0, 0):
        cp.start()
    q = q_ref[...]
    s = jnp.where(_new_rows_mask(ts), _dotf(q, kn_ref[...].astype(bf16)), NEG)
    m = jnp.max(s, axis=1, keepdims=True)
    p = jnp.exp(s - m)
    m_s[...] = m
    l_s[...] = jnp.sum(p, axis=1, keepdims=True)
    acc_s[...] = _dot_nt(p.astype(bf16), vn_ref[...].astype(bf16))

    def body(c, _):
        slot = c % 2

        @pl.when(c + 1 < nch)
        def _():
            for cp in copies(c + 1, 1 - slot):
                cp.start()

        for cp in copies(c, slot):
            cp.wait()
        bias = _dotf(slab_ref[c], e_ref[...])
        s = jnp.concatenate([_dotf(q, kbuf[slot, p].astype(bf16)) for p in range(PCH)], axis=1) + bias
        m_old = m_s[...]
        m_new = jnp.maximum(m_old, jnp.max(s, axis=1, keepdims=True))
        p = jnp.exp(s - m_new)
        alpha = jnp.exp(m_old - m_new)
        l_s[...] = alpha * l_s[...] + jnp.sum(p, axis=1, keepdims=True)
        pb = p.astype(bf16)
        pv = _dot_nt(pb[:, 0:PAGE], vbuf[slot, 0].astype(bf16))
        for j in range(1, PCH):
            pv = pv + _dot_nt(pb[:, j * PAGE:(j + 1) * PAGE], vbuf[slot, j].astype(bf16))
        acc_s[...] = alpha * acc_s[...] + pv
        m_s[...] = m_new
        return 0

    lax.fori_loop(0, nch, body, 0)
    _diag_blocks(o_ref, acc_s[...], l_s[...])


def _sample_selected(page_table, qbd, slab, expand, kn_t, vn_t, cache_k, cache_v, li, ts):
    bs, npg = page_table.shape
    nch = npg // PCH
    per_b = lambda shp: pl.BlockSpec((None,) + shp, lambda b, pt: (b,) + (0,) * len(shp))
    anyspec = pl.BlockSpec(memory_space=pl.ANY)
    return pl.pallas_call(
        functools.partial(_spage_kernel, li=li, npg=npg, ts=ts),
        grid_spec=pltpu.PrefetchScalarGridSpec(
            num_scalar_prefetch=1, grid=(bs,),
            in_specs=[per_b((LANES, 256)), per_b((nch, LANES, LANES)),
                      pl.BlockSpec((LANES, PCH * PAGE), lambda b, pt: (0, 0)), per_b((256, LANES)), per_b((256, LANES)),
                      anyspec, anyspec],
            out_specs=per_b((NSA_KVH, LANES // NSA_KVH, HD)),
            scratch_shapes=[pltpu.VMEM((2, PCH, 256, PAGE), f32), pltpu.VMEM((2, PCH, 256, PAGE), f32),
                            pltpu.SemaphoreType.DMA((2,)), pltpu.VMEM((LANES, 1), f32), pltpu.VMEM((LANES, 1), f32),
                            pltpu.VMEM((LANES, 256), f32)]),
        out_shape=jax.ShapeDtypeStruct((bs, NSA_KVH, LANES // NSA_KVH, HD), f32),
        compiler_params=_cp(("arbitrary",)), name="sample_selected")(
            page_table, qbd, slab, expand, kn_t, vn_t, cache_k, cache_v)


def _swin_kernel(q_ref, wk_ref, wv_ref, kn_ref, vn_ref, o_ref, wko_ref, wvo_ref, *, ts, wb):
    q = q_ref[...]
    wk, wv, kn, vn = wk_ref[...], wv_ref[...], kn_ref[...], vn_ref[...]
    tt = _iota((LANES, wb), 0) % 8
    s1 = jnp.where(_iota((LANES, wb), 1) > tt + (wb - NSA_WINDOW), _dotf(q, wk.astype(bf16)), NEG)
    s2 = jnp.where(_new_rows_mask(ts), _dotf(q, kn.astype(bf16)), NEG)
    m = jnp.maximum(jnp.max(s1, axis=1, keepdims=True), jnp.max(s2, axis=1, keepdims=True))
    p1, p2 = jnp.exp(s1 - m), jnp.exp(s2 - m)
    l = jnp.sum(p1, axis=1, keepdims=True) + jnp.sum(p2, axis=1, keepdims=True)
    acc = _dot_nt(p1.astype(bf16), wv.astype(bf16)) + _dot_nt(p2.astype(bf16), vn.astype(bf16))
    _diag_blocks(o_ref, acc, l)
    tail = _iota((256, wb), 1) >= wb - ts
    pad = jnp.zeros((256, wb - LANES), f32)
    wko_ref[...] = jnp.where(tail, jnp.concatenate([pad, pltpu.roll(kn, LANES - ts, 1)], axis=1), pltpu.roll(wk, wb - ts, 1))
    wvo_ref[...] = jnp.where(tail, jnp.concatenate([pad, pltpu.roll(vn, LANES - ts, 1)], axis=1), pltpu.roll(wv, wb - ts, 1))


def _sample_window(qbd, win_k, win_v, li, kn_t, vn_t, ts):
    bs, wb = win_k.shape[1], win_k.shape[3]
    wspec = pl.BlockSpec((None, None, 256, wb), lambda b: (li, b, 0, 0))
    per_b = lambda shp: pl.BlockSpec((None,) + shp, lambda b: (b,) + (0,) * len(shp))
    return pl.pallas_call(
        functools.partial(_swin_kernel, ts=ts, wb=wb), grid=(bs,),
        in_specs=[per_b((LANES, 256)), wspec, wspec, per_b((256, LANES)), per_b((256, LANES))],
        out_specs=[per_b((NSA_KVH, LANES // NSA_KVH, HD)), per_b((256, wb)), per_b((256, wb))],
        out_shape=[jax.ShapeDtypeStruct((bs, NSA_KVH, LANES // NSA_KVH, HD), f32), jax.ShapeDtypeStruct((bs, 256, wb), f32),
                   jax.ShapeDtypeStruct((bs, 256, wb), f32)],
        compiler_params=_cp(("parallel",)), name="sample_window")(qbd, win_k, win_v, kn_t, vn_t)


def _gelu_tanh(x):
    return 0.5 * x * (1.0 + jnp.tanh(math.sqrt(2.0 / math.pi) * (x + 0.044715 * (x * x * x))))


def _s5_kernel(u_ref, h0r_ref, h0i_ref, bd_ref, pwr_ref, pwi_ref, cdr_ref, cdi_ref, d_ref, gw_ref, gb_ref,
               y_ref, sr_ref, si_ref, xr_s, xi_s, cr_s, ci_s, *, tc, srow):
    c = pl.program_id(1)

    @pl.when(c == 0)
    def _():
        cr_s[...] = jnp.broadcast_to(h0r_ref[...], (8, S5_N))
        ci_s[...] = jnp.broadcast_to(h0i_ref[...], (8, S5_N))

    u = u_ref[...]
    bu = _dotf(u.astype(bf16), bd_ref[...])
    xr_s[...] = bu[:, :S5_N]
    xi_s[...] = bu[:, S5_N:]
    rowi = _iota((8, S5_N), 0)

    def blk(j, carry):
        cr, ci = carry
        rs = pl.multiple_of(j * 8, 8)
        yr, yi = xr_s[pl.ds(rs, 8), :], xi_s[pl.ds(rs, 8), :]
        for sh in (1, 2, 4):
            ar, ai = pwr_ref[sh - 1:sh, :], pwi_ref[sh - 1:sh, :]
            sr = jnp.where(rowi >= sh, pltpu.roll(yr, sh, 0), 0.0)
            si = jnp.where(rowi >= sh, pltpu.roll(yi, sh, 0), 0.0)
            yr, yi = yr + ar * sr - ai * si, yi + ar * si + ai * sr
        pr, pi = pwr_ref[...], pwi_ref[...]
        xr = yr + pr * cr - pi * ci
        xi = yi + pr * ci + pi * cr
        xr_s[pl.ds(rs, 8), :] = xr
        xi_s[pl.ds(rs, 8), :] = xi
        return jnp.broadcast_to(xr[7:8, :], (8, S5_N)), jnp.broadcast_to(xi[7:8, :], (8, S5_N))

    cr, ci = lax.fori_loop(0, tc // 8, blk, (cr_s[...], ci_s[...]))
    cr_s[...] = cr
    ci_s[...] = ci
    sr_ref[...] = xr_s[srow:srow + 8, :]
    si_ref[...] = xi_s[srow:srow + 8, :]
    y = _dotf(xr_s[...].astype(bf16), cdr_ref[...]) + _dotf(xi_s[...].astype(bf16), cdi_ref[...]) + d_ref[...] * u
    y = _gelu_tanh(y)
    y_ref[...] = y * _sigmoid(_dotf(y.astype(bf16), gw_ref[...]) + gb_ref[...])


def _s5(u, h0r, h0i, prm, t_valid):
    b, t, _ = u.shape
    tc = min(256, t)
    assert (t_valid - 1) // tc == t // tc - 1
    srow = ((t_valid - 1) % tc) // 8 * 8
    st = pl.BlockSpec((None, 8, S5_N), lambda b, c: (b, 0, 0))
    h0 = pl.BlockSpec((None, 1, S5_N), lambda b, c: (b, 0, 0))
    return pl.pallas_call(
        functools.partial(_s5_kernel, tc=tc, srow=srow), grid=(b, t // tc),
        in_specs=[pl.BlockSpec((None, tc, S5_CH), lambda b, c: (b, c, 0)), h0, h0] + [_full(p) for p in prm],
        out_specs=[pl.BlockSpec((None, tc, S5_CH), lambda b, c: (b, c, 0)), st, st],
        out_shape=[jax.ShapeDtypeStruct((b, t, S5_CH), f32), jax.ShapeDtypeStruct((b, 8, S5_N), f32),
                   jax.ShapeDtypeStruct((b, 8, S5_N), f32)],
        scratch_shapes=[pltpu.VMEM((tc, S5_N), f32), pltpu.VMEM((tc, S5_N), f32), pltpu.VMEM((8, S5_N), f32),
                        pltpu.VMEM((8, S5_N), f32)],
        compiler_params=_cp(("parallel", "arbitrary")), name="s5")(u, h0r, h0i, *prm)


def _gla_kernel(q_ref, k_ref, v_ref, g_ref, sm_ref, s0_ref, wa_ref, ba_ref, gn_ref, o_ref, sn_ref, st, *, nc, t_valid):
    L = CHUNK
    c = pl.program_id(1)

    @pl.when(c == 0)
    def _():
        st[...] = s0_ref[...]

    x = _dotf(sm_ref[...].astype(bf16), wa_ref[...]) + ba_ref[...]
    la = (jnp.minimum(x, 0.0) - jnp.log1p(jnp.exp(-jnp.abs(x)))) * (1.0 / GLA_TEMP)
    k = k_ref[...]
    if t_valid < nc * L:
        valid = c * L + _iota((L, 1), 0) < t_valid
        la = jnp.where(valid, la, 0.0)
        k = jnp.where(valid, k, 0.0)
    causal = _tri(L)
    b = _dot3_l(causal.astype(bf16), la)
    bl = b[L - 1:L, :]
    qe = (q_ref[...] * (GLA_DK ** -0.5) * jnp.exp(b)).astype(bf16)
    ke = (k * jnp.exp(-b)).astype(bf16)
    kl = (k * jnp.exp(bl - b)).astype(bf16)
    ebl = jnp.exp(bl)
    v, gg, gn = v_ref[...], g_ref[...], gn_ref[...]
    for h in range(GLA_HEADS):
        sl = slice(h * GLA_DK, (h + 1) * GLA_DK)
        vl = slice(h * GLA_DV, (h + 1) * GLA_DV)
        att = jnp.where(causal, _dot_nt(qe[:, sl], ke[:, sl]), 0.0)
        vh = v[:, vl].astype(bf16)
        s_t = st[h]
        o = _dotf(att.astype(bf16), vh) + _dot_nt(qe[:, sl], s_t.astype(bf16))
        st[h] = s_t * ebl[:, sl] + _dot_tn(vh, kl[:, sl])
        ms = jnp.mean(o * o, axis=-1, keepdims=True)
        gh = gg[:, vl]
        o_ref[:, vl] = o * lax.rsqrt(ms + EPS) * gn * (gh * _sigmoid(gh))

    @pl.when(c == nc - 1)
    def _():
        sn_ref[...] = st[...]


def _gla(q, k, v, g, sm, s0t, prm, t_valid):
    b, t, _ = q.shape
    nc = t // CHUNK
    tok = lambda w: pl.BlockSpec((None, CHUNK, w), lambda b, c: (b, c, 0))
    st = pl.BlockSpec((None, GLA_HEADS, GLA_DV, GLA_DK), lambda b, c: (b, 0, 0, 0))
    return pl.pallas_call(
        functools.partial(_gla_kernel, nc=nc, t_valid=t_valid), grid=(b, nc),
        in_specs=[tok(256), tok(256), tok(512), tok(512), tok(LANES), st] + [_full(p) for p in prm],
        out_specs=[tok(512), st],
        out_shape=[jax.ShapeDtypeStruct((b, t, 512), f32), jax.ShapeDtypeStruct((b, GLA_HEADS, GLA_DV, GLA_DK), f32)],
        scratch_shapes=[pltpu.VMEM((GLA_HEADS, GLA_DV, GLA_DK), f32)],
        compiler_params=_cp(("parallel", "arbitrary")), name="gla")(q, k, v, g, sm, s0t, *prm)


def _router_kernel(x_ref, sh_ref, sc_ref, g_ref, w_ref, b_ref, h_ref, route_ref, cnt_ref):
    first = (pl.program_id(0) == 0) & (pl.program_id(1) == 0)
    h = _modulate(x_ref[...], sh_ref[...], sc_ref[...], g_ref[...])
    h_ref[...] = h
    hh, hl = _split2(h)
    wh, wl = _split2(w_ref[...])
    lg = _dotf(hh, wh) + _dotf(hl, wh) + _dotf(hh, wl) + b_ref[...]
    lane = _iota(lg.shape, 1)
    big = 4 * LANES
    coarse = lane < MOE_GROUPS
    lc = jnp.where(coarse, lg, GONE)
    mx = jnp.max(lc, axis=1, keepdims=True)
    gsel = jnp.min(jnp.where(lc == mx, lane, big), axis=1, keepdims=True)
    gc = 1.0 / jnp.sum(jnp.where(coarse, jnp.exp(lg - mx), 0.0), axis=1, keepdims=True)
    base = MOE_GROUPS + gsel * MOE_PER_GROUP
    fine = (lane >= base) & (lane < base + MOE_PER_GROUP)
    mf = jnp.max(jnp.where(fine, lg, GONE), axis=1, keepdims=True)
    ef = jnp.where(fine, jnp.exp(lg - mf), 0.0)
    pf = ef / jnp.sum(ef, axis=1, keepdims=True)
    cand = jnp.where(fine, pf, -1.0)
    v1 = jnp.max(cand, axis=1, keepdims=True)
    i1 = jnp.min(jnp.where(cand == v1, lane, big), axis=1, keepdims=True)
    cand = jnp.where(lane == i1, -1.0, cand)
    v2 = jnp.max(cand, axis=1, keepdims=True)
    i2 = jnp.min(jnp.where(cand == v2, lane, big), axis=1, keepdims=True)
    e1, e2 = i1 - MOE_GROUPS, i2 - MOE_GROUPS
    w1, w2 = gc * v1 / (v1 + v2), gc * v2 / (v1 + v2)
    route_ref[...] = jnp.where(lane == 0, e1.astype(f32), jnp.where(lane == 1, e2.astype(f32),
                               jnp.where(lane == 2, w1, jnp.where(lane == 3, w2, 0.0))))
    cnt = jnp.sum((lane == e1).astype(f32) + (lane == e2).astype(f32), axis=0, keepdims=True)

    @pl.when(first)
    def _():
        cnt_ref[...] = jnp.zeros_like(cnt_ref)

    cnt_ref[...] += jnp.broadcast_to(cnt, cnt_ref.shape)


def _router(x, sh, sc, g, wr, br):
    b, t, _ = x.shape
    tm = min(256, t)
    return pl.pallas_call(
        _router_kernel, grid=(b, t // tm),
        in_specs=[pl.BlockSpec((None, tm, D), lambda b, i: (b, i, 0)), _mod_spec(sh, tm), _mod_spec(sc, tm),
                  pl.BlockSpec((1, D), lambda b, i: (0, 0)), _full(wr), _full(br)],
        out_specs=[pl.BlockSpec((None, tm, D), lambda b, i: (b, i, 0)), pl.BlockSpec((None, tm, LANES), lambda b, i: (b, i, 0)),
                   pl.BlockSpec((8, LANES), lambda b, i: (0, 0))],
        out_shape=[jax.ShapeDtypeStruct((b, t, D), f32), jax.ShapeDtypeStruct((b, t, LANES), f32),
                   jax.ShapeDtypeStruct((8, LANES), f32)],
        compiler_params=_cp(("arbitrary", "arbitrary")), name="moe_router")(x, sh, sc, g, wr, br)


def _plan_kernel(route_ref, ps_ref, dest_ref, run_s, *, tm):
    @pl.when(pl.program_id(0) == 0)
    def _():
        run_s[...] = jnp.zeros_like(run_s)

    route = route_ref[...]
    lane = _iota((tm, LANES), 1).astype(f32)
    oh1 = (lane == route[:, 0:1]).astype(f32)
    oh2 = (lane == route[:, 1:2]).astype(f32)
    tot = oh1 + oh2
    strict = (_iota((tm, tm), 0) > _iota((tm, tm), 1)).astype(bf16)
    before = _dotf(strict, tot.astype(bf16)) + run_s[0:1, :] + ps_ref[0:1, :]
    d1 = jnp.sum(oh1 * before, axis=1, keepdims=True)
    d2 = jnp.sum(oh2 * before, axis=1, keepdims=True)
    lane_i = _iota((tm, LANES), 1)
    dest_ref[...] = jnp.where(lane_i == 0, d1, jnp.where(lane_i == 1, d2, 0.0)).astype(i32)
    run_s[...] += jnp.broadcast_to(jnp.sum(tot, axis=0, keepdims=True), run_s.shape)


def _plan(route, pstart):
    n = route.shape[0]
    tm = min(256, n)
    return pl.pallas_call(
        functools.partial(_plan_kernel, tm=tm), grid=(n // tm,),
        in_specs=[pl.BlockSpec((tm, LANES), lambda i: (i, 0)), pl.BlockSpec((8, LANES), lambda i: (0, 0))],
        out_specs=pl.BlockSpec((tm, LANES), lambda i: (i, 0)),
        out_shape=jax.ShapeDtypeStruct((n, LANES), i32),
        scratch_shapes=[pltpu.VMEM((8, LANES), f32)],
        compiler_params=_cp(("arbitrary",)), name="moe_plan")(route, pstart)


def _dispatch_kernel(dest_ref, h_ref, xs_in, xs_out, sem, *, tm):
    del xs_in

    def row_copy(r, d):
        return pltpu.make_async_copy(h_ref.at[pl.ds(r, 1), :], xs_out.at[pl.ds(d, 1), :], sem)

    def issue(r, _):
        row_copy(r, dest_ref[2 * r]).start()
        row_copy(r, dest_ref[2 * r + 1]).start()
        return 0

    lax.fori_loop(0, tm, issue, 0)

    def drain(r, _):
        row_copy(0, 0).wait()
        row_copy(0, 0).wait()
        return 0

    lax.fori_loop(0, tm, drain, 0)


def _dispatch(h, dest_flat, nrows):
    n = h.shape[0]
    tm = min(256, n)
    return pl.pallas_call(
        functools.partial(_dispatch_kernel, tm=tm), grid=(n // tm,),
        in_specs=[pl.BlockSpec((2 * tm,), lambda i: (i,), memory_space=pltpu.SMEM),
                  pl.BlockSpec((tm, D), lambda i: (i, 0)), pl.BlockSpec(memory_space=pl.ANY)],
        out_specs=pl.BlockSpec(memory_space=pl.ANY),
        out_shape=jax.ShapeDtypeStruct((nrows, D), f32),
        scratch_shapes=[pltpu.SemaphoreType.DMA(())],
        input_output_aliases={2: 0},
        compiler_params=_cp(("arbitrary",)), name="moe_dispatch")(dest_flat, h, jnp.zeros((nrows, D), f32))


def _ffn_kernel(be_ref, nu_ref, x_ref, w1_ref, w3_ref, w2_ref, o_ref):
    i = pl.program_id(0)

    @pl.when(i < nu_ref[0])
    def _():
        x = x_ref[...].astype(bf16)
        a = _dotf(x, w1_ref[...].astype(bf16))
        b = _dotf(x, w3_ref[...].astype(bf16))
        hid = ((a * _sigmoid(a)) * b).astype(bf16)
        o_ref[...] = _dotf(hid, w2_ref[...].astype(bf16))

    @pl.when(i >= nu_ref[0])
    def _():
        o_ref[...] = jnp.zeros_like(o_ref)


def _ffn(xs, blk_e, nused, w1, w3, w2, l):
    nblk = xs.shape[0] // MOE_ROWS
    return pl.pallas_call(
        _ffn_kernel,
        grid_spec=pltpu.PrefetchScalarGridSpec(
            num_scalar_prefetch=2, grid=(nblk,),
            in_specs=[pl.BlockSpec((MOE_ROWS, D), lambda i, be, nu: (i, 0)),
                      pl.BlockSpec((None, None, D, MOE_FF), lambda i, be, nu: (l, be[i], 0, 0)),
                      pl.BlockSpec((None, None, D, MOE_FF), lambda i, be, nu: (l, be[i], 0, 0)),
                      pl.BlockSpec((None, None, MOE_FF, D), lambda i, be, nu: (l, be[i], 0, 0))],
            out_specs=pl.BlockSpec((MOE_ROWS, D), lambda i, be, nu: (i, 0))),
        out_shape=jax.ShapeDtypeStruct(xs.shape, f32),
        compiler_params=_cp(("arbitrary",)), name="moe_ffn")(blk_e, nused, xs, w1, w3, w2)


def _combine_kernel(dest_ref, x_ref, gate_ref, route_ref, ys_ref, o_ref, buf_a, buf_b, sem, *, tm):
    def row_copy(d, buf, r):
        return pltpu.make_async_copy(ys_ref.at[pl.ds(d, 1), :], buf.at[pl.ds(r, 1), :], sem)

    def issue(r, _):
        row_copy(dest_ref[2 * r], buf_a, r).start()
        row_copy(dest_ref[2 * r + 1], buf_b, r).start()
        return 0

    lax.fori_loop(0, tm, issue, 0)

    def drain(r, _):
        row_copy(0, buf_a, 0).wait()
        row_copy(0, buf_b, 0).wait()
        return 0

    lax.fori_loop(0, tm, drain, 0)
    route = route_ref[...]
    o_ref[...] = x_ref[...] + gate_ref[...] * (route[:, 2:3] * buf_a[...] + route[:, 3:4] * buf_b[...])


def _combine(x, gate, route, ys, dest_flat):
    b, t, _ = x.shape
    tm = min(256, t)
    nt = t // tm
    return pl.pallas_call(
        functools.partial(_combine_kernel, tm=tm), grid=(b, nt),
        in_specs=[pl.BlockSpec((2 * tm,), lambda b, i: (b * nt + i,), memory_space=pltpu.SMEM),
                  pl.BlockSpec((None, tm, D), lambda b, i: (b, i, 0)), _mod_spec(gate, tm),
                  pl.BlockSpec((None, tm, LANES), lambda b, i: (b, i, 0)), pl.BlockSpec(memory_space=pl.ANY)],
        out_specs=pl.BlockSpec((None, tm, D), lambda b, i: (b, i, 0)),
        out_shape=jax.ShapeDtypeStruct((b, t, D), f32),
        scratch_shapes=[pltpu.VMEM((tm, D), f32), pltpu.VMEM((tm, D), f32), pltpu.SemaphoreType.DMA(())],
        compiler_params=_cp(("arbitrary", "arbitrary")), name="moe_combine")(dest_flat, x, gate, route, ys)


def _moe(x, sh, sc, gate, g, wr, br, w1, w3, w2, l):
    b, t, _ = x.shape
    n = b * t
    h, route, cnt = _router(x, sh, sc, g, wr, br)
    counts = cnt[0, :MOE_EXPERTS].astype(i32)
    pcounts = (counts + MOE_ROWS - 1) // MOE_ROWS * MOE_ROWS
    pends = jnp.cumsum(pcounts)
    pstarts = pends - pcounts
    nblk = -(-2 * n // MOE_ROWS) + MOE_EXPERTS
    blk_e = jnp.minimum(jnp.searchsorted(pends, jnp.arange(nblk, dtype=i32) * MOE_ROWS, side='right'),
                        MOE_EXPERTS - 1).astype(i32)
    nused = (pends[-1:] // MOE_ROWS).astype(i32)
    ps = jnp.zeros((8, LANES), f32).at[:, :MOE_EXPERTS].set(pstarts.astype(f32)[None])
    dest = _plan(route.reshape(n, LANES), ps)
    dest_flat = dest[:, :2].reshape(2 * n)
    xs = _dispatch(h.reshape(n, D), dest_flat, nblk * MOE_ROWS)
    ys = _ffn(xs, blk_e, nused, w1, w3, w2, l)
    return _combine(x, gate, route, ys, dest_flat)


def _rope_tables(pos):
    half = HD // 2
    inv = ROPE_THETA ** (-jnp.arange(half, dtype=f32) / half)
    ang = pos.astype(f32)[:, None] * inv[None, :]
    cos, sin = jnp.cos(ang), jnp.sin(ang)
    return jnp.tile(jnp.concatenate([cos, cos], axis=1), (1, 2)), jnp.tile(jnp.concatenate([-sin, sin], axis=1), (1, 2))


def _seg_mats(width):
    nseg = width // HD
    seg = (jnp.arange(width)[:, None] // HD == jnp.arange(LANES)[None, :]).astype(bf16)
    return seg, seg.T


def _even_params(i, ev_w_in, ev_w_out, ev_conv_w, ev_conv_b, ev_dt_bias, ev_a_log, ev_d_skip, ev_ssd_norm, ev_q_norm,
                 ev_k_norm, ev_cmp_w):
    w = ev_w_in[i]
    o = [0, 1024, 2560, 2576, 3600, 5136, 5184]
    small = jnp.concatenate([w[:, o[2]:o[3]], w[:, o[5]:o[6]], jnp.zeros((D, 64), f32)], axis=1)
    ws = [w[:, o[0]:o[1]], w[:, o[1]:o[2]], w[:, o[3]:o[4]], w[:, o[4]:o[5]], small]
    pad = lambda v: jnp.zeros((1, LANES), f32).at[0, :v.shape[0]].set(v)
    ssd = [ev_conv_w[i], ev_conv_b[i][None], pad(ev_dt_bias[i]), pad(-jnp.exp(ev_a_log[i])),
           jnp.repeat(ev_d_skip[i], SSD_HD)[None], ev_ssd_norm[i][None]]
    segq, expq = _seg_mats(D)
    segk, expk = _seg_mats(1536)
    kn = ev_k_norm[i]
    z4 = jnp.zeros((256,), f32)
    gk = jnp.concatenate([jnp.tile(kn[0], 4), z4, jnp.tile(kn[1], 4), z4, jnp.tile(kn[2], 4), z4])[None]
    vm = jnp.concatenate([z4, z4 + 1, z4, z4 + 1, z4, z4 + 1])[None]
    cw = jnp.tile(ev_cmp_w[i], (1, 1, 4))
    prep = [segq, expq, segk, expk, jnp.tile(ev_q_norm[i], 16)[None], gk, vm, cw]
    lane = jnp.arange(LANES)[None, None, :, None]
    col = jnp.arange(256)[None, None, None, :]
    gidx = jnp.arange(NSA_KVH)[:, None, None, None]
    jidx = jnp.arange(3)[None, :, None, None]
    gexp = (lane == 16 + ((gidx * NSA_REP + col // HD) * 3 + jidx)).astype(bf16)
    wo = ev_w_out[i]
    return dict(ws=[a.astype(bf16) for a in ws], ssd=ssd, prep=prep, gexp=gexp, cmp_w=ev_cmp_w[i],
                wo=[wo[:SSD_INNER].astype(bf16), wo[SSD_INNER:].astype(bf16)])


def _odd_params(i, od_w_in, od_w_out, a_re, a_im, log_dt, b_re, b_im, c_re, c_im, d, glu_w, glu_b, wa2, ba, gnorm):
    w = od_w_in[i]
    o = [0, 512, 768, 1024, 1536, 2048, 2064]
    small = jnp.concatenate([w[:, o[5]:o[6]], jnp.zeros((D, LANES - GLA_RANK), f32)], axis=1)
    ws = [w[:, o[k]:o[k + 1]] for k in range(5)] + [small]
    are, aim = a_re[i], a_im[i]
    dt = jnp.exp(log_dt[i])[:, None]
    lr, li = are * dt, aim * dt
    ab_re, ab_im = jnp.exp(lr) * jnp.cos(li), jnp.exp(lr) * jnp.sin(li)
    den = are * are + aim * aim
    nr = ab_re - 1.0
    f_re = (nr * are + ab_im * aim) / den
    f_im = (ab_im * are - nr * aim) / den
    bb_re = f_re[..., None] * b_re[i] - f_im[..., None] * b_im[i]
    bb_im = f_re[..., None] * b_im[i] + f_im[..., None] * b_re[i]
    eye = jnp.eye(S5_GROUPS, dtype=f32)
    bdiag = lambda m: jnp.einsum('gpc,gh->gchp', m, eye).reshape(S5_CH, S5_N)
    cdiag = lambda m: jnp.einsum('gcp,gh->gphc', m, eye).reshape(S5_N, S5_CH)
    bd = jnp.concatenate([bdiag(bb_re), bdiag(bb_im)], axis=1).astype(bf16)
    kk = jnp.arange(1, 9, dtype=f32)[:, None, None]
    mag, ang = jnp.exp(lr[None] * kk), li[None] * kk
    pwr, pwi = (mag * jnp.cos(ang)).reshape(8, S5_N), (mag * jnp.sin(ang)).reshape(8, S5_N)
    s5 = [bd, pwr, pwi, cdiag(c_re[i]).astype(bf16), cdiag(-c_im[i]).astype(bf16), d[i][None],
          glu_w[i].astype(bf16), glu_b[i][None]]
    wa = jnp.zeros((LANES, 256), f32).at[:GLA_RANK].set(wa2[i]).astype(bf16)
    gla = [wa, ba[i][None], gnorm[i][None]]
    wo = od_w_out[i]
    return dict(ws=[a.astype(bf16) for a in ws], s5=s5, gla=gla, wo=[wo[:S5_CH].astype(bf16), wo[S5_CH:].astype(bf16)])


def _pad_t(a, t):
    return jnp.pad(a, ((0, 0), (0, t - a.shape[1])) + ((0, 0),) * (a.ndim - 2))


def _even_prompt(x, sh, sc, gate, g, prm):
    b, t, _ = x.shape
    z, xbc, q, kv, sm = _mod_proj(x, sh, sc, g, prm['ws'], "even_in")
    ya, convn, ssdn = _ssd(xbc, sm, z, jnp.zeros((b, 8, SSD_CONV_DIM), f32),
                           jnp.zeros((b, SSD_HEADS, SSD_HD, SSD_STATE), f32), prm['ssd'], t)
    cos_t, sin_t = _rope_tables(jnp.arange(t))
    (qc, qr, kcmp, vcmp, ksel, vsel, kwin, vwin, gates, kaug, vsa, kwh, vwa, kch, vch) = _nsa_prep(
        q, kv, sm, cos_t, sin_t, prm['prep'], True)
    nb = t // NSA_BLOCK
    padb = lambda a: jnp.pad(a, ((0, 0), (0, 0), (0, LANES - nb), (0, 0)))
    ob = _nsa_prompt(qc, qr, padb(kch), padb(vch), kaug, vsa, kwh, vwa, gates, prm['gexp'])
    xn = _out_proj(x, gate, [ya, ob], prm['wo'], "even_out")
    r5 = lambda a: a.reshape(b, t, NSA_KVH, HD)
    keep = min(NSA_WINDOW, t)
    st = (r5(kcmp), r5(vcmp), r5(ksel), r5(vsel), r5(kwin)[:, t - keep:], r5(vwin)[:, t - keep:], ssdn, convn[:, 5:8])
    return xn, st


def _stack_q(qb, bs, ts):
    q = qb.reshape(bs, ts, NSA_KVH, NSA_REP, HD)
    q = jnp.pad(q, ((0, 0), (0, 8 - ts), (0, 0), (0, 0), (0, 0)))
    q = q.transpose(0, 2, 3, 1, 4).reshape(bs, NSA_KVH, NSA_REP * 8, HD)
    return jnp.pad(q, ((0, 0), (0, 0), (0, LANES - NSA_REP * 8), (0, 0)))


def _unstack_o(o, bs, ts):
    o = o[:, :, :NSA_REP * 8].reshape(bs, NSA_KVH, NSA_REP, 8, HD)[:, :, :, :ts]
    return o.transpose(0, 3, 1, 2, 4).reshape(1, bs * ts, D)


def _page_major(cache):
    l, p, r, h, d = cache.shape
    return jnp.transpose(cache, (0, 1, 3, 4, 2)).reshape(l, p, h * d, r)


def _even_sample(x, sh, sc, gate, g, prm, bs, ts, li, conv_state, ssd_state, page_table, cmp_k, cmp_v, sel_k, sel_v,
                 win_k, win_v):
    n = bs * ts
    npg = page_table.shape[1]
    past = npg * PAGE
    z, xbc, q, kv, sm = _mod_proj(x, sh, sc, g, prm['ws'], "even_in_s")
    seq = lambda a: _pad_t(a.reshape(bs, ts, a.shape[-1]), CHUNK)
    conv0 = jnp.pad(conv_state, ((0, 0), (5, 0), (0, 0)))
    ya, convn, ssdn = _ssd(seq(xbc), seq(sm), seq(z), conv0, ssd_state, prm['ssd'], ts)
    ya = ya[:, :ts].reshape(1, n, SSD_INNER)
    cos_t, sin_t = _rope_tables(past + jnp.arange(n) % ts)
    qc, qr, kcmp, vcmp, ksel, vsel, kwin, vwin, gates = _nsa_prep(q, kv, sm, cos_t, sin_t, prm['prep'], False)
    r8 = lambda a: _pad_t(a.reshape(bs, ts, 256), 8)
    qc_st = _stack_q(qc, bs, ts)
    nr = LANES // NSA_KVH
    qbd = jnp.einsum('bgid,gh->bgihd', _stack_q(qr, bs, ts)[:, :, :nr], jnp.eye(NSA_KVH, dtype=bf16)).reshape(bs, LANES, 256)
    rows_t = lambda a: jnp.pad(jnp.swapaxes(r8(a), 1, 2), ((0, 0), (0, 0), (0, LANES - 8)))
    cw = prm['cmp_w']
    w_t = lambda w: jnp.tile(w.T, (NSA_KVH, PAGE // NSA_BLOCK))
    kct, vct = _page_compress(page_table, _page_major(cmp_k), _page_major(cmp_v), li, w_t(cw[0]), w_t(cw[1]))
    nb_past = 2 * npg
    nbp = -(-(nb_past + 1) // LANES) * LANES
    hmaj = lambda a: jnp.pad(jnp.swapaxes(a.reshape(bs, NSA_KVH, HD, -1)[..., :nb_past], 2, 3),
                             ((0, 0), (0, 0), (0, nbp - nb_past), (0, 0)))
    cidx = jnp.arange(LANES)
    mrep = ((cidx[:, None] < NSA_REP * 8) & (cidx[None, :] < NSA_REP * 8)
            & (cidx[:, None] % 8 == cidx[None, :] % 8)).astype(bf16)
    oc, neg = _sample_select(qc_st, hmaj(kct), hmaj(vct), r8(kcmp), r8(vcmp), cw, mrep, past, ts)
    nch, bpc = npg // PCH, 2 * PCH
    slab = neg[:, :, :nb_past, :nr].reshape(bs, NSA_KVH, nch, bpc, nr).transpose(0, 2, 1, 4, 3).reshape(bs, nch, LANES, bpc)
    slab = jnp.pad(slab, ((0, 0), (0, 0), (0, 0), (0, LANES - bpc))).astype(bf16)
    expand = (jnp.arange(LANES)[:, None] == jnp.arange(PCH * PAGE)[None, :] // NSA_BLOCK).astype(bf16)
    o_s = _sample_selected(page_table, qbd, slab, expand, rows_t(ksel), rows_t(vsel), _page_major(sel_k),
                           _page_major(sel_v), li, ts)
    o_w, wkt, wvt = _sample_window(qbd, _page_major(win_k), _page_major(win_v), li, rows_t(kwin), rows_t(vwin), ts)
    unmajor = lambda a: a.reshape(bs, NSA_KVH, HD, -1).transpose(0, 3, 1, 2)
    wkn, wvn = unmajor(wkt), unmajor(wvt)
    gt = gates[0, :, 16:16 + 3 * NSA_HEADS].reshape(1, n, NSA_HEADS, 3)
    gx = lambda j: jnp.repeat(gt[..., j], HD, axis=-1)
    ob = gx(0) * _unstack_o(oc, bs, ts) + gx(1) * _unstack_o(o_s, bs, ts) + gx(2) * _unstack_o(o_w, bs, ts)
    xn = _out_proj(x, gate, [ya, ob], prm['wo'], "even_out_s")
    r5 = lambda a: a.reshape(bs, ts, NSA_KVH, HD)
    st = (r5(kcmp), r5(vcmp), r5(ksel), r5(vsel), wkn, wvn, ssdn, convn[:, 5:8])
    return xn, st


def _odd_layer(x, sh, sc, gate, g, prm, bs, ts, s5r0, s5i0, gla0):
    u, q, k, v, gg, sm = _mod_proj(x, sh, sc, g, prm['ws'], "odd_in")
    tp = -(-ts // CHUNK) * CHUNK
    seq = lambda a: _pad_t(a.reshape(bs, ts, a.shape[-1]), tp)
    yc, sr, si = _s5(seq(u), s5r0.reshape(bs, 1, S5_N), s5i0.reshape(bs, 1, S5_N), prm['s5'], ts)
    ridx = (ts - 1) % 8
    og, gn = _gla(seq(q), seq(k), seq(v), seq(gg), seq(sm), jnp.swapaxes(gla0, 2, 3), prm['gla'], ts)
    unseq = lambda a: a[:, :ts].reshape(x.shape[0], x.shape[1], a.shape[-1])
    xn = _out_proj(x, gate, [unseq(yc), unseq(og)], prm['wo'], "odd_out")
    st = (sr[:, ridx].reshape(bs, S5_GROUPS, S5_STATE), si[:, ridx].reshape(bs, S5_GROUPS, S5_STATE), jnp.swapaxes(gn, 2, 3))
    return xn, st


def kernel(x_prompt, x_sample, cache_cmp_k, cache_cmp_v, cache_sel_k, cache_sel_v, cache_win_k, cache_win_v, state_ssd, state_conv, state_s5_re, state_s5_im, state_gla, page_table, c_prompt, c_sample, ada_w, ada_b, norm_mix, norm_ffn, ev_w_in, ev_w_out, ev_conv_w, ev_conv_b, ev_dt_bias, ev_a_log, ev_d_skip, ev_ssd_norm, ev_q_norm, ev_k_norm, ev_cmp_w, od_w_in, od_w_out, od_s5_a_re, od_s5_a_im, od_s5_log_dt, od_s5_b_re, od_s5_b_im, od_s5_c_re, od_s5_c_im, od_s5_d, od_glu_w, od_glu_b, od_gla_wa2, od_gla_ba, od_gla_norm, moe_wc, moe_bc, moe_wf, moe_bf, moe_w1, moe_w3, moe_w2):
    bp, tp, _ = x_prompt.shape
    bs, ts, _ = x_sample.shape
    ns = bs * ts
    depth = ada_w.shape[0]
    bc = -(-(bp + bs) // 8) * 8
    c_all = jnp.zeros((bc, D), f32).at[:bp].set(c_prompt).at[bp:bp + bs].set(c_sample)
    mods = _ada(c_all, ada_w, ada_b)
    xp, xs = x_prompt, x_sample.reshape(1, ns, D)
    sp = {}
    ss = {}
    for l in range(depth):
        i = l // 2
        mp = [m[:, None, :] for m in jnp.split(mods[l, :bp], 6, axis=-1)]
        ms = [jnp.repeat(m, ts, axis=0)[None] for m in jnp.split(mods[l, bp:bp + bs], 6, axis=-1)]
        gm, gf = norm_mix[l][None], norm_ffn[l][None]
        if l % 2 == 0:
            prm = _even_params(i, ev_w_in, ev_w_out, ev_conv_w, ev_conv_b, ev_dt_bias, ev_a_log, ev_d_skip, ev_ssd_norm,
                               ev_q_norm, ev_k_norm, ev_cmp_w)
            xp, st_p = _even_prompt(xp, mp[0], mp[1], mp[2], gm, prm)
            xs, st_s = _even_sample(xs, ms[0], ms[1], ms[2], gm, prm, bs, ts, i, state_conv[i], state_ssd[i], page_table,
                                    cache_cmp_k, cache_cmp_v, cache_sel_k, cache_sel_v, cache_win_k, cache_win_v)
            names = ('cmp_k', 'cmp_v', 'sel_k', 'sel_v', 'win_k', 'win_v', 'ssd', 'conv')
        else:
            prm = _odd_params(i, od_w_in, od_w_out, od_s5_a_re, od_s5_a_im, od_s5_log_dt, od_s5_b_re, od_s5_b_im,
                              od_s5_c_re, od_s5_c_im, od_s5_d, od_glu_w, od_glu_b, od_gla_wa2, od_gla_ba, od_gla_norm)
            zs = jnp.zeros((bp, S5_GROUPS, S5_STATE), f32)
            xp, st_p = _odd_layer(xp, mp[0], mp[1], mp[2], gm, prm, bp, tp, zs, zs,
                                  jnp.zeros((bp, GLA_HEADS, GLA_DK, GLA_DV), f32))
            xs, st_s = _odd_layer(xs, ms[0], ms[1], ms[2], gm, prm, bs, ts, state_s5_re[i], state_s5_im[i], state_gla[i])
            names = ('s5_re', 's5_im', 'gla')
        for nm, a_p, a_s in zip(names, st_p, st_s):
            sp.setdefault(nm, []).append(a_p)
            ss.setdefault(nm, []).append(a_s)
        wr = jnp.zeros((D, LANES), f32).at[:, :MOE_GROUPS].set(moe_wc[l]).at[:, MOE_GROUPS:MOE_GROUPS + MOE_EXPERTS].set(moe_wf[l])
        br = jnp.zeros((1, LANES), f32).at[0, :MOE_GROUPS].set(moe_bc[l]).at[0, MOE_GROUPS:MOE_GROUPS + MOE_EXPERTS].set(moe_bf[l])
        xp = _moe(xp, mp[3], mp[4], mp[5], gf, wr, br, moe_w1, moe_w3, moe_w2, l)
        xs = _moe(xs, ms[3], ms[4], ms[5], gf, wr, br, moe_w1, moe_w3, moe_w2, l)
    order = ('cmp_k', 'cmp_v', 'sel_k', 'sel_v', 'win_k', 'win_v', 'ssd', 'conv', 's5_re', 's5_im', 'gla')
    outs = [xp, xs.reshape(bs, ts, D)]
    for nm in order:
        outs += [jnp.stack(sp[nm]), jnp.stack(ss[nm])]
    return tuple(outs)
```

```python
import functools
import math

import jax
import jax.numpy as jnp
from jax import lax
from jax.experimental import pallas as pl
from jax.experimental.pallas import tpu as pltpu

f32 = jnp.float32
bf16 = jnp.bfloat16
i32 = jnp.int32

D = 1024
PAGE = 128
SSD_HEADS, SSD_HD, SSD_INNER, SSD_GROUPS, SSD_STATE, SSD_CONV = 16, 64, 1024, 4, 64, 4
SSD_CONV_DIM = SSD_INNER + 2 * SSD_GROUPS * SSD_STATE
NSA_HEADS, NSA_KVH, NSA_REP, HD, NSA_BLOCK, NSA_TOPN, NSA_WINDOW = 16, 4, 4, 64, 64, 16, 512
ROPE_THETA = 10000.0
S5_CH, S5_GCH, S5_GROUPS, S5_STATE = 512, 16, 32, 64
S5_N = S5_GROUPS * S5_STATE
GLA_HEADS, GLA_DK, GLA_DV, GLA_RANK, GLA_TEMP = 4, 64, 128, 16, 16.0
MOE_GROUPS, MOE_PER_GROUP, MOE_EXPERTS, MOE_FF = 4, 8, 32, 256
MOE_ROWS = 256
EPS = 1e-6
NEG = -1e30
BIG = 1e30
GONE = -3e38
LANES = 128
CHUNK = 128
FLASH_SPLIT = 1
FIXED_STAB_MAX = 40.0
VMEM_LIMIT = 48 * 2**20


def _cp(sem, vmem=VMEM_LIMIT):
    return pltpu.CompilerParams(dimension_semantics=sem, vmem_limit_bytes=vmem)


def _sigmoid(x):
    return 1.0 / (1.0 + jnp.exp(-x))


def _softplus(x):
    return jnp.maximum(x, 0.0) + jnp.log1p(jnp.exp(-jnp.abs(x)))


def _dotf(a, b):
    return jnp.dot(a, b, preferred_element_type=f32)


def _dot_nt(a, b):
    return lax.dot_general(a, b, (((1,), (1,)), ((), ())), preferred_element_type=f32)


def _dot_tn(a, b):
    return lax.dot_general(a, b, (((0,), (0,)), ((), ())), preferred_element_type=f32)


def _split2(a):
    h = a.astype(bf16)
    return h, (a - h.astype(f32)).astype(bf16)


def _split3(a):
    h = a.astype(bf16)
    r = a - h.astype(f32)
    m = r.astype(bf16)
    return h, m, (r - m.astype(f32)).astype(bf16)


def _dot3_r(a, w):
    h, m, l = _split3(a)
    return _dotf(h, w) + _dotf(m, w) + _dotf(l, w)


def _dot3_l(w, a):
    h, m, l = _split3(a)
    return _dotf(w, h) + _dotf(w, m) + _dotf(w, l)


def _iota(shape, dim):
    return lax.broadcasted_iota(i32, shape, dim)


def _tri(n):
    return _iota((n, n), 0) >= _iota((n, n), 1)


def _modulate(x, sh, sc, g):
    ms = jnp.mean(x * x, axis=-1, keepdims=True)
    return (x * lax.rsqrt(ms + EPS) * g) * (1.0 + sc) + sh


def _mod_spec(mod, tm):
    if mod.shape[1] == 1:
        return pl.BlockSpec((None, 1, D), lambda b, i: (b, 0, 0))
    return pl.BlockSpec((None, tm, D), lambda b, i: (b, i, 0))


def _full(a):
    n = a.ndim
    return pl.BlockSpec(a.shape, lambda *_: (0,) * n)


def _ada_kernel(c_ref, w_ref, b_ref, o_ref):
    c = c_ref[...]
    a = c * _sigmoid(c)
    ah, al = _split2(a)
    wh, wl = _split2(w_ref[...])
    o_ref[...] = _dotf(ah, wh) + _dotf(al, wh) + _dotf(ah, wl) + b_ref[...]


def _ada(c_all, ada_w, ada_b):
    depth, bc, tn = ada_w.shape[0], c_all.shape[0], 1536
    return pl.pallas_call(
        _ada_kernel, grid=(depth, 6 * D // tn),
        in_specs=[pl.BlockSpec((bc, D), lambda l, j: (0, 0)),
                  pl.BlockSpec((None, D, tn), lambda l, j: (l, 0, j)),
                  pl.BlockSpec((None, 1, tn), lambda l, j: (l, 0, j))],
        out_specs=pl.BlockSpec((None, bc, tn), lambda l, j: (l, 0, j)),
        out_shape=jax.ShapeDtypeStruct((depth, bc, 6 * D), f32),
        compiler_params=_cp(("parallel", "parallel")), name="ada")(c_all, ada_w, ada_b.reshape(depth, 1, 6 * D))


def _proj_kernel(x_ref, sh_ref, sc_ref, g_ref, *refs, nseg):
    h = _modulate(x_ref[...], sh_ref[...], sc_ref[...], g_ref[...]).astype(bf16)
    for i in range(nseg):
        refs[nseg + i][...] = _dotf(h, refs[i][...])


def _mod_proj(x, sh, sc, g, ws, name):
    b, t, _ = x.shape
    tm = min(256, t)
    nseg = len(ws)
    return pl.pallas_call(
        functools.partial(_proj_kernel, nseg=nseg), grid=(b, t // tm),
        in_specs=[pl.BlockSpec((None, tm, D), lambda b, i: (b, i, 0)), _mod_spec(sh, tm), _mod_spec(sc, tm),
                  pl.BlockSpec((1, D), lambda b, i: (0, 0))] + [_full(w) for w in ws],
        out_specs=[pl.BlockSpec((None, tm, w.shape[1]), lambda b, i: (b, i, 0)) for w in ws],
        out_shape=[jax.ShapeDtypeStruct((b, t, w.shape[1]), f32) for w in ws],
        compiler_params=_cp(("parallel", "parallel")), name=name)(x, sh, sc, g, *ws)


def _outproj_kernel(x_ref, gate_ref, *refs, nseg):
    acc = _dotf(refs[0][...].astype(bf16), refs[nseg][...])
    for i in range(1, nseg):
        acc = acc + _dotf(refs[i][...].astype(bf16), refs[nseg + i][...])
    refs[2 * nseg][...] = x_ref[...] + gate_ref[...] * acc


def _out_proj(x, gate, acts, ws, name):
    b, t, _ = x.shape
    tm = min(512, t)
    nseg = len(acts)
    return pl.pallas_call(
        functools.partial(_outproj_kernel, nseg=nseg), grid=(b, t // tm),
        in_specs=[pl.BlockSpec((None, tm, D), lambda b, i: (b, i, 0)), _mod_spec(gate, tm)]
        + [pl.BlockSpec((None, tm, a.shape[2]), lambda b, i: (b, i, 0)) for a in acts] + [_full(w) for w in ws],
        out_specs=pl.BlockSpec((None, tm, D), lambda b, i: (b, i, 0)),
        out_shape=jax.ShapeDtypeStruct((b, t, D), f32),
        compiler_params=_cp(("parallel", "parallel")), name=name)(x, gate, *acts, *ws)


def _ssd_kernel(xbc_ref, dt_ref, z_ref, conv0_ref, h0_ref, cw_ref, cb_ref, dtb_ref, a_ref, dsk_ref, nrm_ref,
                y_ref, convn_ref, hn_ref, xpad, hst, ybuf, *, nc, t_valid):
    L = CHUNK
    c = pl.program_id(1)

    @pl.when(c == 0)
    def _():
        hst[...] = h0_ref[...]
        xpad[0:8, :] = conv0_ref[...]

    xpad[8:8 + L, :] = xbc_ref[...]
    acc = cb_ref[...] + cw_ref[0:1, :] * xpad[5:5 + L, :]
    for k in range(1, SSD_CONV):
        acc = acc + cw_ref[k:k + 1, :] * xpad[5 + k:5 + k + L, :]
    tv_last = t_valid - (nc - 1) * L
    convn_ref[...] = xpad[tv_last:tv_last + 8, :]
    xpad[0:8, :] = xpad[L:L + 8, :]

    xc = acc * _sigmoid(acc)
    xs = xc[:, :SSD_INNER]
    bm = xc[:, SSD_INNER:SSD_INNER + 256]
    cm = xc[:, SSD_INNER + 256:]
    dt = _softplus(dt_ref[...] + dtb_ref[...])
    if t_valid < nc * L:
        dt = jnp.where(c * L + _iota((L, LANES), 0) < t_valid, dt, 0.0)
    causal = _tri(L)
    trib = causal.astype(bf16)
    cs = _dot3_l(trib, dt * a_ref[...])
    cs_t, dt_t = cs.T, dt.T
    wend_t = jnp.exp(cs_t[:, L - 1:L] - cs_t) * dt_t
    ecs = jnp.exp(cs)
    xs_t = xs.T
    for g in range(SSD_GROUPS):
        bg = bm[:, g * 64:(g + 1) * 64].astype(bf16)
        cg = cm[:, g * 64:(g + 1) * 64].astype(bf16)
        gmat = _dot_nt(cg, bg)
        for r in range(SSD_HEADS // SSD_GROUPS):
            h = g * (SSD_HEADS // SSD_GROUPS) + r
            seg = cs[:, h:h + 1] - cs_t[h:h + 1, :]
            dec = jnp.where(causal, jnp.exp(jnp.where(causal, seg, 0.0)), 0.0)
            sc = (gmat * dec * dt_t[h:h + 1, :]).astype(bf16)
            hs = hst[h]
            yh = _dotf(sc, xs[:, h * 64:(h + 1) * 64].astype(bf16)) + _dot_nt(cg, hs.astype(bf16)) * ecs[:, h:h + 1]
            ybuf[:, h * 64:(h + 1) * 64] = yh
            xw = (xs_t[h * 64:(h + 1) * 64, :] * wend_t[h:h + 1, :]).astype(bf16)
            hst[h] = hs * ecs[L - 1:L, h:h + 1] + _dotf(xw, bg)
    zz = z_ref[...]
    y = (ybuf[...] + dsk_ref[...] * xs) * (zz * _sigmoid(zz))
    gw = SSD_INNER // SSD_GROUPS
    for g in range(SSD_GROUPS):
        s = y[:, g * gw:(g + 1) * gw]
        ms = jnp.mean(s * s, axis=-1, keepdims=True)
        y_ref[:, g * gw:(g + 1) * gw] = s * lax.rsqrt(ms + EPS) * nrm_ref[:, g * gw:(g + 1) * gw]

    @pl.when(c == nc - 1)
    def _():
        hn_ref[...] = hst[...]


def _ssd(xbc, sm, z, conv0, h0, prm, t_valid):
    b, t, _ = xbc.shape
    nc = t // CHUNK
    tok = lambda w: pl.BlockSpec((None, CHUNK, w), lambda b, c: (b, c, 0))
    return pl.pallas_call(
        functools.partial(_ssd_kernel, nc=nc, t_valid=t_valid), grid=(b, nc),
        in_specs=[tok(SSD_CONV_DIM), tok(LANES), tok(SSD_INNER),
                  pl.BlockSpec((None, 8, SSD_CONV_DIM), lambda b, c: (b, 0, 0)),
                  pl.BlockSpec((None, SSD_HEADS, SSD_HD, SSD_STATE), lambda b, c: (b, 0, 0, 0))]
        + [_full(p) for p in prm],
        out_specs=[tok(SSD_INNER), pl.BlockSpec((None, 8, SSD_CONV_DIM), lambda b, c: (b, 0, 0)),
                   pl.BlockSpec((None, SSD_HEADS, SSD_HD, SSD_STATE), lambda b, c: (b, 0, 0, 0))],
        out_shape=[jax.ShapeDtypeStruct((b, t, SSD_INNER), f32), jax.ShapeDtypeStruct((b, 8, SSD_CONV_DIM), f32),
                   jax.ShapeDtypeStruct((b, SSD_HEADS, SSD_HD, SSD_STATE), f32)],
        scratch_shapes=[pltpu.VMEM((CHUNK + 8, SSD_CONV_DIM), f32), pltpu.VMEM((SSD_HEADS, SSD_HD, SSD_STATE), f32),
                        pltpu.VMEM((CHUNK, SSD_INNER), f32)],
        compiler_params=_cp(("parallel", "arbitrary")), name="ssd")(xbc, sm, z, conv0, h0, *prm)


def _seg_rinv(x, seg_ref, exp_ref):
    x2 = x * x
    h, l = _split2(x2)
    ss = _dotf(h, seg_ref[...]) + _dotf(l, seg_ref[...])
    return _dot3_r(lax.rsqrt(ss * (1.0 / HD) + EPS), exp_ref[...])


def _rope(x, cosf, sinf):
    w = x.shape[1]
    first = (_iota(x.shape, 1) % HD) < (HD // 2)
    rot = jnp.where(first, pltpu.roll(x, w - HD // 2, 1), pltpu.roll(x, HD // 2, 1))
    return x * cosf + rot * sinf


def _nsaprep_kernel(q_ref, kv_ref, sm_ref, cos_ref, sin_ref, segq_ref, expq_ref, segk_ref, expk_ref, gq_ref, gk_ref,
                    vm_ref, cw_ref, *outs, tm, compress):
    qc_ref, qr_ref, kcmp_ref, vcmp_ref, ksel_ref, vsel_ref, kwin_ref, vwin_ref, gate_ref = outs[:9]
    i = pl.program_id(1)
    cos1, sin1 = cos_ref[...], sin_ref[...]
    q = q_ref[...]
    qn = q * _seg_rinv(q, segq_ref, expq_ref) * gq_ref[...]
    qc_ref[...] = (qn * 0.125).astype(bf16)
    qr_ref[...] = (_rope(qn, jnp.concatenate([cos1] * 8, axis=1), jnp.concatenate([sin1] * 8, axis=1)) * 0.125).astype(bf16)
    kv = kv_ref[...]
    kvn = kv * (_seg_rinv(kv, segk_ref, expk_ref) * gk_ref[...] + vm_ref[...])
    cos2, sin2 = jnp.concatenate([cos1] * 2, axis=1), jnp.concatenate([sin1] * 2, axis=1)
    kcmp, vcmp = kvn[:, 0:256], kvn[:, 256:512]
    ksel, vsel = _rope(kvn[:, 512:768], cos2, sin2), kvn[:, 768:1024]
    kwin, vwin = _rope(kvn[:, 1024:1280], cos2, sin2), kvn[:, 1280:1536]
    kcmp_ref[...] = kcmp
    vcmp_ref[...] = vcmp
    ksel_ref[...] = ksel
    vsel_ref[...] = vsel
    kwin_ref[...] = kwin
    vwin_ref[...] = vwin
    gate_ref[...] = _sigmoid(sm_ref[...])
    if compress:
        kaug_ref, vsa_ref, kwh_ref, vwa_ref, kch_ref, vch_ref, kn2_ref = outs[9:]
        sq = lambda a: jnp.square(a.astype(bf16).astype(f32))
        n2 = lambda a, seg: jnp.max(_dot3_r(sq(a), seg), axis=0, keepdims=True)
        kn2_ref[...] = jnp.broadcast_to(n2(ksel, segq_ref[0:256, :]) + n2(kwin, segq_ref[256:512, :]), (8, LANES))
        blk = (i * tm + _iota((tm, LANES), 0)) // NSA_BLOCK
        onehot = (_iota((tm, LANES), 1) == blk).astype(bf16)
        ones_t = (_iota((HD, tm), 0) == 0).astype(bf16)
        zero = jnp.zeros((tm, HD), bf16)
        kcw = (kcmp.reshape(tm // NSA_BLOCK, NSA_BLOCK, 256) * cw_ref[0][None]).sum(axis=1)
        vcw = (vcmp.reshape(tm // NSA_BLOCK, NSA_BLOCK, 256) * cw_ref[1][None]).sum(axis=1)
        vsel_t, vwin_t = vsel.T, vwin.T
        for g in range(NSA_KVH):
            sl = slice(g * HD, (g + 1) * HD)
            kaug_ref[g, :, 0:HD] = ksel[:, sl].astype(bf16)
            kaug_ref[g, :, HD:2 * HD] = zero
            kaug_ref[g, :, 2 * HD:] = onehot
            vsa_ref[g, 0:HD, :] = vsel_t[sl, :].astype(bf16)
            vsa_ref[g, HD:, :] = ones_t
            kwh_ref[g] = kwin[:, sl].astype(bf16)
            vwa_ref[g, 0:HD, :] = vwin_t[sl, :].astype(bf16)
            vwa_ref[g, HD:, :] = ones_t
            kch_ref[g] = kcw[:, sl]
            vch_ref[g] = vcw[:, sl]


def _nsa_prep(q, kv, sm, cos_t, sin_t, prm, compress):
    b, t, _ = q.shape
    tm = min(512, t)
    tok = lambda w: pl.BlockSpec((None, tm, w), lambda b, i: (b, i, 0))
    tab = pl.BlockSpec((tm, LANES), lambda b, i: (i, 0))
    hm = lambda w: pl.BlockSpec((None, NSA_KVH, tm, w), lambda b, i: (b, 0, i, 0))
    out_specs = [tok(D), tok(D)] + [tok(256)] * 6 + [tok(LANES)]
    out_shape = [jax.ShapeDtypeStruct((b, t, D), bf16)] * 2 + [jax.ShapeDtypeStruct((b, t, 256), f32)] * 6 \
        + [jax.ShapeDtypeStruct((b, t, LANES), f32)]
    if compress:
        nbt = tm // NSA_BLOCK
        cspec = pl.BlockSpec((None, NSA_KVH, nbt, HD), lambda b, i: (b, 0, i, 0))
        vt = pl.BlockSpec((None, NSA_KVH, LANES, tm), lambda b, i: (b, 0, 0, i))
        out_specs += [hm(256), vt, hm(HD), vt, cspec, cspec, pl.BlockSpec((None, None, 8, LANES), lambda b, i: (b, i, 0, 0))]
        out_shape += [jax.ShapeDtypeStruct((b, NSA_KVH, t, 256), bf16), jax.ShapeDtypeStruct((b, NSA_KVH, LANES, t), bf16),
                      jax.ShapeDtypeStruct((b, NSA_KVH, t, HD), bf16), jax.ShapeDtypeStruct((b, NSA_KVH, LANES, t), bf16),
                      jax.ShapeDtypeStruct((b, NSA_KVH, t // NSA_BLOCK, HD), f32),
                      jax.ShapeDtypeStruct((b, NSA_KVH, t // NSA_BLOCK, HD), f32),
                      jax.ShapeDtypeStruct((b, t // tm, 8, LANES), f32)]
    return pl.pallas_call(
        functools.partial(_nsaprep_kernel, tm=tm, compress=compress), grid=(b, t // tm),
        in_specs=[tok(D), tok(1536), tok(LANES), tab, tab] + [_full(p) for p in prm],
        out_specs=out_specs, out_shape=out_shape,
        compiler_params=_cp(("parallel", "parallel")), name="nsa_prep")(q, kv, sm, cos_t, sin_t, *prm)


def _cmp_branch(q_heads, kc, vc, tpos):
    nb, nq = kc.shape[0], tpos.shape[1]
    nrow = _iota((nb, nq), 0)
    ok = ((nrow + 1) * NSA_BLOCK - 1) <= tpos
    imp = jnp.zeros((nb, nq), f32)
    outs = []
    for q in q_heads:
        s = jnp.where(ok, _dot_nt(kc, q), NEG)
        m = jnp.max(s, axis=0, keepdims=True)
        e = jnp.where(ok, jnp.exp(s - m), 0.0)
        l = jnp.sum(e, axis=0, keepdims=True)
        p = e / jnp.where(l > 0.0, l, 1.0)
        imp = imp + p
        outs.append(_dot_tn(p.astype(bf16), vc))
    cur = tpos // NSA_BLOCK
    impm = jnp.where((nrow == cur) | (nrow == 0), BIG, jnp.where(nrow < cur, imp, NEG))

    def pick(_, carry):
        v, sel = carry
        mx = jnp.max(v, axis=0, keepdims=True)
        idx = jnp.min(jnp.where(v == mx, nrow, nb), axis=0, keepdims=True)
        hit = nrow == idx
        return jnp.where(hit, GONE, v), jnp.where(hit, 1.0, sel)

    _, sel = lax.fori_loop(0, NSA_TOPN, pick, (impm, jnp.zeros((nb, nq), f32)))
    neg = jnp.where((sel > 0.5) & (impm > 0.5 * NEG), 0.0, NEG)
    return outs, neg


def _flash_t(lhs, k_ref, vt_ref, lo, hi, trow, tk, m_ref, acc_ref, window):
    m_ref[...] = jnp.full(m_ref.shape, NEG, f32)
    acc_ref[...] = jnp.zeros(acc_ref.shape, f32)
    nq = lhs.shape[0]
    qs = nq // FLASH_SPLIT

    def step(kt, masked):
        ks = pl.multiple_of(kt * tk, tk)
        kt_tile = k_ref[pl.ds(ks, tk), :]
        vt_tile = vt_ref[:, pl.ds(ks, tk)]
        for h in range(FLASH_SPLIT):
            cols = slice(h * qs, (h + 1) * qs)
            s = _dot_nt(kt_tile, lhs[cols])
            if masked:
                kpos = ks + _iota((tk, 1), 0)
                keep = kpos <= trow[:, cols]
                if window:
                    keep = keep & (kpos > trow[:, cols] - NSA_WINDOW)
                s = jnp.where(keep, s, NEG)
            m_old = m_ref[:, cols]
            m_new = jnp.maximum(m_old, jnp.max(s, axis=0, keepdims=True))
            p = jnp.exp(s - m_new).astype(bf16)
            acc_ref[:, cols] = jnp.exp(m_old - m_new) * acc_ref[:, cols] + _dotf(vt_tile, p)
            m_ref[:, cols] = m_new

    def body(masked):
        def f(kt, _):
            step(kt, masked)
            return 0
        return f

    if window:
        lax.fori_loop(lo, hi, body(True), 0)
    else:
        lax.fori_loop(lo, hi - 1, body(False), 0)
        step(hi - 1, True)
    acc = acc_ref[...]
    return acc[0:HD, :] / acc[HD:HD + 1, :]


def _flash_fixed(lhs, k_ref, vt_ref, lo, hi, trow, tk, mrow, acc_ref, window):
    acc_ref[...] = jnp.zeros(acc_ref.shape, f32)

    def probs(kt, masked):
        ks = pl.multiple_of(kt * tk, tk)
        s = _dot_nt(k_ref[pl.ds(ks, tk), :], lhs)
        if masked:
            kpos = ks + _iota((tk, 1), 0)
            keep = kpos <= trow
            if window:
                keep = keep & (kpos > trow - NSA_WINDOW)
            s = jnp.where(keep, s, NEG)
        return jnp.exp(s - mrow).astype(bf16)

    def single(masked):
        def f(kt, _):
            ks = pl.multiple_of(kt * tk, tk)
            acc_ref[...] += _dotf(vt_ref[:, pl.ds(ks, tk)], probs(kt, masked))
            return 0
        return f

    def pair(j, _):
        ka = lo + 2 * j
        ks = pl.multiple_of(ka * tk, tk)
        pa, pb = probs(ka, False), probs(ka + 1, False)
        acc_ref[...] += _dotf(vt_ref[:, pl.ds(ks, tk)], pa) + _dotf(vt_ref[:, pl.ds(ks + tk, tk)], pb)
        return 0

    if window:
        lax.fori_loop(lo, hi, single(True), 0)
    else:
        npair = (hi - 1 - lo) // 2
        lax.fori_loop(0, npair, pair, 0)
        lax.fori_loop(lo + 2 * npair, hi - 1, single(False), 0)
        single(True)(hi - 1, 0)
    acc = acc_ref[...]
    return acc[0:HD, :] / acc[HD:HD + 1, :]


def _nsa_kernel(qc_ref, qr_ref, kc_ref, vc_ref, kaug_ref, vsa_ref, kw_ref, vwa_ref, gate_ref, gexp_ref, kb_ref, o_ref,
                lhs_ref, m_ref, acc_ref, os_ref, ow_ref, *, tq, tk):
    qi = pl.program_id(2)
    t0 = qi * tq
    tpos = t0 + _iota((1, tq), 1)
    qh = [qc_ref[:, r * HD:(r + 1) * HD] for r in range(NSA_REP)]
    oc, neg = _cmp_branch(qh, kc_ref[...].astype(bf16), vc_ref[...].astype(bf16), tpos)
    negq = neg.T.astype(bf16)
    for r in range(NSA_REP):
        rows = slice(r * tq, (r + 1) * tq)
        lhs_ref[rows, 0:HD] = qr_ref[:, r * HD:(r + 1) * HD]
        lhs_ref[rows, HD:2 * HD] = jnp.zeros((tq, HD), bf16)
        lhs_ref[rows, 2 * HD:] = negq
    trow = t0 + _iota((1, NSA_REP * tq), 1) % tq
    hi = (t0 + tq) // tk
    lo = jnp.maximum(t0 - (NSA_WINDOW - 1), 0) // tk
    qf = lhs_ref[:, 0:HD].astype(f32)
    qh, ql = _split2(qf * qf)
    ones = jnp.ones((8, HD), bf16)
    qn = jnp.sqrt((_dot_nt(ones, qh) + _dot_nt(ones, ql))[0:1, :]) * 1.02 + 1e-6
    kb = kb_ref[...]
    for k_ref, vt_ref, kmax, out_ref, lo_b, win in ((kaug_ref, vsa_ref, kb[0:1, 0:1], os_ref, 0, False),
                                                     (kw_ref, vwa_ref, kb[0:1, 1:2], ow_ref, lo, True)):
        lhs = lhs_ref[:, 0:HD] if win else lhs_ref[...]
        mrow = qn * kmax
        safe = jnp.max(mrow) <= FIXED_STAB_MAX

        @pl.when(safe)
        def _():
            out_ref[...] = _flash_fixed(lhs, k_ref, vt_ref, lo_b, hi, trow, tk, mrow, acc_ref, win)

        @pl.when(jnp.logical_not(safe))
        def _():
            out_ref[...] = _flash_t(lhs, k_ref, vt_ref, lo_b, hi, trow, tk, m_ref, acc_ref, win)

    os_t, ow_t = os_ref[...], ow_ref[...]
    gates = gate_ref[...]
    unstack = lambda a: jnp.concatenate([a[:, r * tq:(r + 1) * tq] for r in range(NSA_REP)], axis=0).T
    o = _dot3_r(gates, gexp_ref[0]) * jnp.concatenate(oc, axis=1) \
        + _dot3_r(gates, gexp_ref[1]) * unstack(os_t) + _dot3_r(gates, gexp_ref[2]) * unstack(ow_t)
    o_ref[...] = o.astype(bf16)


def _nsa_prompt(qc, qr, kch, vch, kaug, vsa, kwh, vwa, gates, gexp, kbound):
    b, t, _ = qc.shape
    tq = tk = min(256, t)
    nbp = kch.shape[2]
    qspec = pl.BlockSpec((None, tq, 256), lambda b, g, i: (b, i, g))
    kvspec = lambda n, w: pl.BlockSpec((None, None, n, w), lambda b, g, i: (b, g, 0, 0))
    return pl.pallas_call(
        functools.partial(_nsa_kernel, tq=tq, tk=tk), grid=(b, NSA_KVH, t // tq),
        in_specs=[qspec, qspec, kvspec(nbp, HD), kvspec(nbp, HD), kvspec(t, 256), kvspec(LANES, t), kvspec(t, HD),
                  kvspec(LANES, t), pl.BlockSpec((None, tq, LANES), lambda b, g, i: (b, i, 0)),
                  pl.BlockSpec((None, 3, LANES, 256), lambda b, g, i: (g, 0, 0, 0)), kvspec(8, LANES)],
        out_specs=qspec, out_shape=jax.ShapeDtypeStruct((b, t, D), bf16),
        scratch_shapes=[pltpu.VMEM((NSA_REP * tq, 256), bf16), pltpu.VMEM((1, NSA_REP * tq), f32),
                        pltpu.VMEM((LANES, NSA_REP * tq), f32), pltpu.VMEM((HD, NSA_REP * tq), f32),
                        pltpu.VMEM((HD, NSA_REP * tq), f32)],
        compiler_params=_cp(("parallel", "parallel", "arbitrary")), name="nsa_prompt")(
            qc, qr, kch, vch, kaug, vsa, kwh, vwa, gates, gexp, kbound)


PCH = 8


def _chunk_copies(cache_ref, li, pt_ref, b, c, buf, slot, sem):
    return [pltpu.make_async_copy(cache_ref.at[li, pt_ref[b, c * PCH + p]], buf.at[slot, p], sem.at[slot])
            for p in range(PCH)]


def _pagecmp_kernel(pt_ref, wk_ref, wv_ref, ck_ref, cv_ref, kc_ref, vc_ref, kbuf, vbuf, sem, *, li, npg):
    b = pl.program_id(0)
    nch = npg // PCH
    nbo = kc_ref.shape[1]

    def copies(c, slot):
        return _chunk_copies(ck_ref, li, pt_ref, b, c, kbuf, slot, sem) + _chunk_copies(cv_ref, li, pt_ref, b, c, vbuf, slot, sem)

    for cp in copies(0, 0):
        cp.start()
    kc_ref[...] = jnp.zeros(kc_ref.shape, f32)
    vc_ref[...] = jnp.zeros(vc_ref.shape, f32)
    rowblk = _iota((PAGE, nbo), 0) // NSA_BLOCK
    col = _iota((PAGE, nbo), 1)

    def body(c, _):
        slot = c % 2

        @pl.when(c + 1 < nch)
        def _():
            for cp in copies(c + 1, 1 - slot):
                cp.start()

        for cp in copies(c, slot):
            cp.wait()
        for p in range(PCH):
            place = (col == 2 * (c * PCH + p) + rowblk).astype(bf16)
            kc_ref[...] += _dotf((kbuf[slot, p] * wk_ref[...]).astype(bf16), place)
            vc_ref[...] += _dotf((vbuf[slot, p] * wv_ref[...]).astype(bf16), place)
        return 0

    lax.fori_loop(0, nch, body, 0)


def _page_compress(page_table, cache_k, cache_v, li, wk_t, wv_t):
    bs, npg = page_table.shape
    nbo = -(-2 * npg // LANES) * LANES
    out = pl.BlockSpec((None, 256, nbo), lambda b, pt: (b, 0, 0))
    shp = jax.ShapeDtypeStruct((bs, 256, nbo), f32)
    anyspec = pl.BlockSpec(memory_space=pl.ANY)
    return pl.pallas_call(
        functools.partial(_pagecmp_kernel, li=li, npg=npg),
        grid_spec=pltpu.PrefetchScalarGridSpec(
            num_scalar_prefetch=1, grid=(bs,),
            in_specs=[pl.BlockSpec((256, PAGE), lambda b, pt: (0, 0)), pl.BlockSpec((256, PAGE), lambda b, pt: (0, 0)),
                      anyspec, anyspec],
            out_specs=[out, out],
            scratch_shapes=[pltpu.VMEM((2, PCH, 256, PAGE), f32), pltpu.VMEM((2, PCH, 256, PAGE), f32),
                            pltpu.SemaphoreType.DMA((2,))]),
        out_shape=[shp, shp], compiler_params=_cp(("arbitrary",)), name="page_compress")(
            page_table, wk_t, wv_t, cache_k, cache_v)


def _ssel_kernel(qc_ref, kc_ref, vc_ref, kn_ref, vn_ref, w_ref, mrep_ref, oc_ref, neg_ref, *, past, ts):
    nbp = kc_ref.shape[1]
    nb_past = past // NSA_BLOCK
    col = _iota((1, LANES), 1)
    tpos = past + col % 8
    newrow = (_iota((nbp, 1), 0) == nb_past).astype(f32)
    tmask = (_iota((8, 1), 0) < ts).astype(f32)
    for g in range(NSA_KVH):
        sl = slice(g * HD, (g + 1) * HD)
        kc_new = jnp.sum(kn_ref[:, sl] * tmask * w_ref[0, 0:8, :], axis=0, keepdims=True)
        vc_new = jnp.sum(vn_ref[:, sl] * tmask * w_ref[1, 0:8, :], axis=0, keepdims=True)
        kc = (kc_ref[g] + newrow * kc_new).astype(bf16)
        vc = (vc_ref[g] + newrow * vc_new).astype(bf16)
        q = qc_ref[g]
        nrow = _iota((nbp, LANES), 0)
        ok = ((nrow + 1) * NSA_BLOCK - 1) <= tpos
        s = jnp.where(ok, _dot_nt(kc, q), NEG)
        m = jnp.max(s, axis=0, keepdims=True)
        e = jnp.where(ok, jnp.exp(s - m), 0.0)
        l = jnp.sum(e, axis=0, keepdims=True)
        p = e / jnp.where(l > 0.0, l, 1.0)
        oc_ref[g] = _dot_tn(p.astype(bf16), vc)
        imp = _dot3_r(p, mrep_ref[...])
        cur = tpos // NSA_BLOCK
        impm = jnp.where((nrow == cur) | (nrow == 0), BIG, jnp.where(nrow < cur, imp, NEG))

        def pick(_, carry):
            v, sel = carry
            mx = jnp.max(v, axis=0, keepdims=True)
            idx = jnp.min(jnp.where(v == mx, nrow, nbp), axis=0, keepdims=True)
            hit = nrow == idx
            return jnp.where(hit, GONE, v), jnp.where(hit, 1.0, sel)

        _, sel = lax.fori_loop(0, NSA_TOPN, pick, (impm, jnp.zeros((nbp, LANES), f32)))
        neg_ref[g] = jnp.where((sel > 0.5) & (impm > 0.5 * NEG), 0.0, NEG)


def _sample_select(qc_st, kc_h, vc_h, kn, vn, cmp_w, mrep, past, ts):
    bs, _, nbp, _ = kc_h.shape
    b4 = lambda n, w: pl.BlockSpec((None, NSA_KVH, n, w), lambda b: (b, 0, 0, 0))
    return pl.pallas_call(
        functools.partial(_ssel_kernel, past=past, ts=ts), grid=(bs,),
        in_specs=[b4(LANES, HD), b4(nbp, HD), b4(nbp, HD), pl.BlockSpec((None, 8, 256), lambda b: (b, 0, 0)),
                  pl.BlockSpec((None, 8, 256), lambda b: (b, 0, 0)), _full(cmp_w), _full(mrep)],
        out_specs=[b4(LANES, HD), b4(nbp, LANES)],
        out_shape=[jax.ShapeDtypeStruct((bs, NSA_KVH, LANES, HD), f32), jax.ShapeDtypeStruct((bs, NSA_KVH, nbp, LANES), f32)],
        compiler_params=_cp(("parallel",)), name="sample_select")(qc_st, kc_h, vc_h, kn, vn, cmp_w, mrep)


def _new_rows_mask(ts):
    tt = _iota((LANES, LANES), 0) % 8
    lane = _iota((LANES, LANES), 1)
    return (lane <= tt) & (lane < ts)


def _diag_blocks(o_ref, acc, l):
    nr = LANES // NSA_KVH
    for g in range(NSA_KVH):
        o_ref[g] = acc[g * nr:(g + 1) * nr, g * HD:(g + 1) * HD] / l[g * nr:(g + 1) * nr, :]


def _spage_kernel(pt_ref, q_ref, slab_ref, e_ref, kn_ref, vn_ref, ck_ref, cv_ref, o_ref, kbuf, vbuf, sem, m_s, l_s, acc_s,
                  *, li, npg, ts):
    b = pl.program_id(0)
    nch = npg // PCH

    def copies(c, slot):
        return _chunk_copies(ck_ref, li, pt_ref, b, c, kbuf, slot, sem) + _chunk_copies(cv_ref, li, pt_ref, b, c, vbuf, slot, sem)

    for cp in copies(0, 0):
        cp.start()
    q = q_ref[...]
    s = jnp.where(_new_rows_mask(ts), _dotf(q, kn_ref[...].astype(bf16)), NEG)
    m = jnp.max(s, axis=1, keepdims=True)
    p = jnp.exp(s - m)
    m_s[...] = m
    l_s[...] = jnp.sum(p, axis=1, keepdims=True)
    acc_s[...] = _dot_nt(p.astype(bf16), vn_ref[...].astype(bf16))

    def body(c, _):
        slot = c % 2

        @pl.when(c + 1 < nch)
        def _():
            for cp in copies(c + 1, 1 - slot):
                cp.start()

        for cp in copies(c, slot):
            cp.wait()
        bias = _dotf(slab_ref[c], e_ref[...])
        s = jnp.concatenate([_dotf(q, kbuf[slot, p].astype(bf16)) for p in range(PCH)], axis=1) + bias
        m_old = m_s[...]
        m_new = jnp.maximum(m_old, jnp.max(s, axis=1, keepdims=True))
        p = jnp.exp(s - m_new)
        alpha = jnp.exp(m_old - m_new)
        l_s[...] = alpha * l_s[...] + jnp.sum(p, axis=1, keepdims=True)
        pb = p.astype(bf16)
        pv = _dot_nt(pb[:, 0:PAGE], vbuf[slot, 0].astype(bf16))
        for j in range(1, PCH):
            pv = pv + _dot_nt(pb[:, j * PAGE:(j + 1) * PAGE], vbuf[slot, j].astype(bf16))
        acc_s[...] = alpha * acc_s[...] + pv
        m_s[...] = m_new
        return 0

    lax.fori_loop(0, nch, body, 0)
    _diag_blocks(o_ref, acc_s[...], l_s[...])


def _sample_selected(page_table, qbd, slab, expand, kn_t, vn_t, cache_k, cache_v, li, ts):
    bs, npg = page_table.shape
    nch = npg // PCH
    per_b = lambda shp: pl.BlockSpec((None,) + shp, lambda b, pt: (b,) + (0,) * len(shp))
    anyspec = pl.BlockSpec(memory_space=pl.ANY)
    return pl.pallas_call(
        functools.partial(_spage_kernel, li=li, npg=npg, ts=ts),
        grid_spec=pltpu.PrefetchScalarGridSpec(
            num_scalar_prefetch=1, grid=(bs,),
            in_specs=[per_b((LANES, 256)), per_b((nch, LANES, LANES)),
                      pl.BlockSpec((LANES, PCH * PAGE), lambda b, pt: (0, 0)), per_b((256, LANES)), per_b((256, LANES)),
                      anyspec, anyspec],
            out_specs=per_b((NSA_KVH, LANES // NSA_KVH, HD)),
            scratch_shapes=[pltpu.VMEM((2, PCH, 256, PAGE), f32), pltpu.VMEM((2, PCH, 256, PAGE), f32),
                            pltpu.SemaphoreType.DMA((2,)), pltpu.VMEM((LANES, 1), f32), pltpu.VMEM((LANES, 1), f32),
                            pltpu.VMEM((LANES, 256), f32)]),
        out_shape=jax.ShapeDtypeStruct((bs, NSA_KVH, LANES // NSA_KVH, HD), f32),
        compiler_params=_cp(("arbitrary",)), name="sample_selected")(
            page_table, qbd, slab, expand, kn_t, vn_t, cache_k, cache_v)


def _swin_kernel(q_ref, wk_ref, wv_ref, kn_ref, vn_ref, o_ref, wko_ref, wvo_ref, *, ts, wb):
    q = q_ref[...]
    wk, wv, kn, vn = wk_ref[...], wv_ref[...], kn_ref[...], vn_ref[...]
    tt = _iota((LANES, wb), 0) % 8
    s1 = jnp.where(_iota((LANES, wb), 1) > tt + (wb - NSA_WINDOW), _dotf(q, wk.astype(bf16)), NEG)
    s2 = jnp.where(_new_rows_mask(ts), _dotf(q, kn.astype(bf16)), NEG)
    m = jnp.maximum(jnp.max(s1, axis=1, keepdims=True), jnp.max(s2, axis=1, keepdims=True))
    p1, p2 = jnp.exp(s1 - m), jnp.exp(s2 - m)
    l = jnp.sum(p1, axis=1, keepdims=True) + jnp.sum(p2, axis=1, keepdims=True)
    acc = _dot_nt(p1.astype(bf16), wv.astype(bf16)) + _dot_nt(p2.astype(bf16), vn.astype(bf16))
    _diag_blocks(o_ref, acc, l)
    tail = _iota((256, wb), 1) >= wb - ts
    pad = jnp.zeros((256, wb - LANES), f32)
    wko_ref[...] = jnp.where(tail, jnp.concatenate([pad, pltpu.roll(kn, LANES - ts, 1)], axis=1), pltpu.roll(wk, wb - ts, 1))
    wvo_ref[...] = jnp.where(tail, jnp.concatenate([pad, pltpu.roll(vn, LANES - ts, 1)], axis=1), pltpu.roll(wv, wb - ts, 1))


def _sample_window(qbd, win_k, win_v, li, kn_t, vn_t, ts):
    bs, wb = win_k.shape[1], win_k.shape[3]
    wspec = pl.BlockSpec((None, None, 256, wb), lambda b: (li, b, 0, 0))
    per_b = lambda shp: pl.BlockSpec((None,) + shp, lambda b: (b,) + (0,) * len(shp))
    return pl.pallas_call(
        functools.partial(_swin_kernel, ts=ts, wb=wb), grid=(bs,),
        in_specs=[per_b((LANES, 256)), wspec, wspec, per_b((256, LANES)), per_b((256, LANES))],
        out_specs=[per_b((NSA_KVH, LANES // NSA_KVH, HD)), per_b((256, wb)), per_b((256, wb))],
        out_shape=[jax.ShapeDtypeStruct((bs, NSA_KVH, LANES // NSA_KVH, HD), f32), jax.ShapeDtypeStruct((bs, 256, wb), f32),
                   jax.ShapeDtypeStruct((bs, 256, wb), f32)],
        compiler_params=_cp(("parallel",)), name="sample_window")(qbd, win_k, win_v, kn_t, vn_t)


def _gelu_tanh(x):
    return 0.5 * x * (1.0 + jnp.tanh(math.sqrt(2.0 / math.pi) * (x + 0.044715 * (x * x * x))))


def _s5_kernel(u_ref, h0r_ref, h0i_ref, perm_ref, permt_ref, bd_ref, pwr_ref, pwi_ref, cdr_ref, cdi_ref, d_ref, gw_ref,
               gb_ref, y_ref, sr_ref, si_ref, xr_s, xi_s, cr_s, ci_s, *, tc, srow):
    c = pl.program_id(1)

    @pl.when(c == 0)
    def _():
        cr_s[...] = h0r_ref[...]
        ci_s[...] = h0i_ref[...]

    ns = tc // 8
    u = u_ref[...]
    ub = _dotf(perm_ref[...], u.astype(bf16)).astype(bf16)
    hc, hn = S5_CH // 2, S5_N // 2
    for h in range(2):
        uh = ub[:, h * hc:(h + 1) * hc]
        xr_s[:, h * hn:(h + 1) * hn] = _dotf(uh, bd_ref[h * hc:(h + 1) * hc, h * hn:(h + 1) * hn])
        xi_s[:, h * hn:(h + 1) * hn] = _dotf(uh, bd_ref[h * hc:(h + 1) * hc, S5_N + h * hn:S5_N + (h + 1) * hn])
    ar, ai = pwr_ref[0:1, :], pwi_ref[0:1, :]

    def scan(t, carry):
        xr, xi = carry
        rs = pl.multiple_of(t * 8, 8)
        nr = ar * xr - ai * xi + xr_s[pl.ds(rs, 8), :]
        ni = ar * xi + ai * xr + xi_s[pl.ds(rs, 8), :]
        xr_s[pl.ds(rs, 8), :] = nr
        xi_s[pl.ds(rs, 8), :] = ni
        return nr, ni

    zero = jnp.zeros((8, S5_N), f32)
    fr, fi = lax.fori_loop(0, ns, scan, (zero, zero))
    asr, asi = pwr_ref[ns - 1:ns, :], pwi_ref[ns - 1:ns, :]
    cr, ci = cr_s[...], ci_s[...]
    er, ei = [], []
    for s in range(8):
        er.append(cr)
        ei.append(ci)
        cr, ci = asr * cr - asi * ci + fr[s:s + 1, :], asr * ci + asi * cr + fi[s:s + 1, :]
    cr_s[...] = cr
    ci_s[...] = ci
    er, ei = jnp.concatenate(er, axis=0), jnp.concatenate(ei, axis=0)

    def fix(t, _):
        rs = pl.multiple_of(t * 8, 8)
        pr, pi = pwr_ref[pl.ds(t, 1), :], pwi_ref[pl.ds(t, 1), :]
        xr_s[pl.ds(rs, 8), :] += pr * er - pi * ei
        xi_s[pl.ds(rs, 8), :] += pr * ei + pi * er
        return 0

    lax.fori_loop(0, ns, fix, 0)
    sr_ref[...] = xr_s[srow:srow + 8, :]
    si_ref[...] = xi_s[srow:srow + 8, :]
    yh = []
    for h in range(2):
        rows, cols = slice(h * hn, (h + 1) * hn), slice(h * hc, (h + 1) * hc)
        yh.append(_dotf(xr_s[:, rows].astype(bf16), cdr_ref[rows, cols]) + _dotf(xi_s[:, rows].astype(bf16), cdi_ref[rows, cols]))
    y = _dot3_l(permt_ref[...], jnp.concatenate(yh, axis=1)) + d_ref[...] * u
    y = _gelu_tanh(y)
    y_ref[...] = y * _sigmoid(_dotf(y.astype(bf16), gw_ref[...]) + gb_ref[...])


def _s5(u, h0r, h0i, prm, t_valid):
    b, t, _ = u.shape
    tc = min(256, t)
    ns = tc // 8
    assert (t_valid - 1) // tc == t // tc - 1
    r = (t_valid - 1) % tc
    prow = (r % ns) * 8 + r // ns
    srow = prow // 8 * 8
    rid = jnp.arange(tc)
    perm = (rid[None, :] == (rid[:, None] % 8) * ns + rid[:, None] // 8).astype(bf16)
    st = pl.BlockSpec((None, 8, S5_N), lambda b, c: (b, 0, 0))
    h0 = pl.BlockSpec((None, 1, S5_N), lambda b, c: (b, 0, 0))
    y, sr, si = pl.pallas_call(
        functools.partial(_s5_kernel, tc=tc, srow=srow), grid=(b, t // tc),
        in_specs=[pl.BlockSpec((None, tc, S5_CH), lambda b, c: (b, c, 0)), h0, h0, _full(perm), _full(perm)]
        + [_full(p) for p in prm],
        out_specs=[pl.BlockSpec((None, tc, S5_CH), lambda b, c: (b, c, 0)), st, st],
        out_shape=[jax.ShapeDtypeStruct((b, t, S5_CH), f32), jax.ShapeDtypeStruct((b, 8, S5_N), f32),
                   jax.ShapeDtypeStruct((b, 8, S5_N), f32)],
        scratch_shapes=[pltpu.VMEM((tc, S5_N), f32), pltpu.VMEM((tc, S5_N), f32), pltpu.VMEM((1, S5_N), f32),
                        pltpu.VMEM((1, S5_N), f32)],
        compiler_params=_cp(("parallel", "arbitrary")), name="s5")(u, h0r, h0i, perm, perm.T, *prm)
    return y, sr[:, prow % 8], si[:, prow % 8]


def _gla_kernel(q_ref, k_ref, v_ref, g_ref, sm_ref, s0_ref, wa_ref, ba_ref, gn_ref, o_ref, sn_ref, st, *, nc, t_valid):
    L = CHUNK
    c = pl.program_id(1)

    @pl.when(c == 0)
    def _():
        st[...] = s0_ref[...]

    x = _dotf(sm_ref[...].astype(bf16), wa_ref[...]) + ba_ref[...]
    la = (jnp.minimum(x, 0.0) - jnp.log1p(jnp.exp(-jnp.abs(x)))) * (1.0 / GLA_TEMP)
    k = k_ref[...]
    if t_valid < nc * L:
        valid = c * L + _iota((L, 1), 0) < t_valid
        la = jnp.where(valid, la, 0.0)
        k = jnp.where(valid, k, 0.0)
    causal = _tri(L)
    b = _dot3_l(causal.astype(bf16), la)
    bl = b[L - 1:L, :]
    qe = (q_ref[...] * (GLA_DK ** -0.5) * jnp.exp(b)).astype(bf16)
    ke = (k * jnp.exp(-b)).astype(bf16)
    kl = (k * jnp.exp(bl - b)).astype(bf16)
    ebl = jnp.exp(bl)
    v, gg, gn = v_ref[...], g_ref[...], gn_ref[...]
    for h in range(GLA_HEADS):
        sl = slice(h * GLA_DK, (h + 1) * GLA_DK)
        vl = slice(h * GLA_DV, (h + 1) * GLA_DV)
        att = jnp.where(causal, _dot_nt(qe[:, sl], ke[:, sl]), 0.0)
        vh = v[:, vl].astype(bf16)
        s_t = st[h]
        o = _dotf(att.astype(bf16), vh) + _dot_nt(qe[:, sl], s_t.astype(bf16))
        st[h] = s_t * ebl[:, sl] + _dot_tn(vh, kl[:, sl])
        ms = jnp.mean(o * o, axis=-1, keepdims=True)
        gh = gg[:, vl]
        o_ref[:, vl] = o * lax.rsqrt(ms + EPS) * gn * (gh * _sigmoid(gh))

    @pl.when(c == nc - 1)
    def _():
        sn_ref[...] = st[...]


def _gla(q, k, v, g, sm, s0t, prm, t_valid):
    b, t, _ = q.shape
    nc = t // CHUNK
    tok = lambda w: pl.BlockSpec((None, CHUNK, w), lambda b, c: (b, c, 0))
    st = pl.BlockSpec((None, GLA_HEADS, GLA_DV, GLA_DK), lambda b, c: (b, 0, 0, 0))
    return pl.pallas_call(
        functools.partial(_gla_kernel, nc=nc, t_valid=t_valid), grid=(b, nc),
        in_specs=[tok(256), tok(256), tok(512), tok(512), tok(LANES), st] + [_full(p) for p in prm],
        out_specs=[tok(512), st],
        out_shape=[jax.ShapeDtypeStruct((b, t, 512), f32), jax.ShapeDtypeStruct((b, GLA_HEADS, GLA_DV, GLA_DK), f32)],
        scratch_shapes=[pltpu.VMEM((GLA_HEADS, GLA_DV, GLA_DK), f32)],
        compiler_params=_cp(("parallel", "arbitrary")), name="gla")(q, k, v, g, sm, s0t, *prm)


def _router_kernel(x_ref, sh_ref, sc_ref, g_ref, w_ref, b_ref, h_ref, route_ref, cnt_ref):
    first = (pl.program_id(0) == 0) & (pl.program_id(1) == 0)
    h = _modulate(x_ref[...], sh_ref[...], sc_ref[...], g_ref[...])
    h_ref[...] = h
    hh, hl = _split2(h)
    wh, wl = _split2(w_ref[...])
    lg = _dotf(hh, wh) + _dotf(hl, wh) + _dotf(hh, wl) + b_ref[...]
    lane = _iota(lg.shape, 1)
    big = 4 * LANES
    coarse = lane < MOE_GROUPS
    lc = jnp.where(coarse, lg, GONE)
    mx = jnp.max(lc, axis=1, keepdims=True)
    gsel = jnp.min(jnp.where(lc == mx, lane, big), axis=1, keepdims=True)
    gc = 1.0 / jnp.sum(jnp.where(coarse, jnp.exp(lg - mx), 0.0), axis=1, keepdims=True)
    base = MOE_GROUPS + gsel * MOE_PER_GROUP
    fine = (lane >= base) & (lane < base + MOE_PER_GROUP)
    mf = jnp.max(jnp.where(fine, lg, GONE), axis=1, keepdims=True)
    ef = jnp.where(fine, jnp.exp(lg - mf), 0.0)
    pf = ef / jnp.sum(ef, axis=1, keepdims=True)
    cand = jnp.where(fine, pf, -1.0)
    v1 = jnp.max(cand, axis=1, keepdims=True)
    i1 = jnp.min(jnp.where(cand == v1, lane, big), axis=1, keepdims=True)
    cand = jnp.where(lane == i1, -1.0, cand)
    v2 = jnp.max(cand, axis=1, keepdims=True)
    i2 = jnp.min(jnp.where(cand == v2, lane, big), axis=1, keepdims=True)
    e1, e2 = i1 - MOE_GROUPS, i2 - MOE_GROUPS
    w1, w2 = gc * v1 / (v1 + v2), gc * v2 / (v1 + v2)
    route_ref[...] = jnp.where(lane == 0, e1.astype(f32), jnp.where(lane == 1, e2.astype(f32),
                               jnp.where(lane == 2, w1, jnp.where(lane == 3, w2, 0.0))))
    cnt = jnp.sum((lane == e1).astype(f32) + (lane == e2).astype(f32), axis=0, keepdims=True)

    @pl.when(first)
    def _():
        cnt_ref[...] = jnp.zeros_like(cnt_ref)

    cnt_ref[...] += jnp.broadcast_to(cnt, cnt_ref.shape)


def _router(x, sh, sc, g, wr, br):
    b, t, _ = x.shape
    tm = min(256, t)
    return pl.pallas_call(
        _router_kernel, grid=(b, t // tm),
        in_specs=[pl.BlockSpec((None, tm, D), lambda b, i: (b, i, 0)), _mod_spec(sh, tm), _mod_spec(sc, tm),
                  pl.BlockSpec((1, D), lambda b, i: (0, 0)), _full(wr), _full(br)],
        out_specs=[pl.BlockSpec((None, tm, D), lambda b, i: (b, i, 0)), pl.BlockSpec((None, tm, LANES), lambda b, i: (b, i, 0)),
                   pl.BlockSpec((8, LANES), lambda b, i: (0, 0))],
        out_shape=[jax.ShapeDtypeStruct((b, t, D), f32), jax.ShapeDtypeStruct((b, t, LANES), f32),
                   jax.ShapeDtypeStruct((8, LANES), f32)],
        compiler_params=_cp(("arbitrary", "arbitrary")), name="moe_router")(x, sh, sc, g, wr, br)


def _plan_kernel(route_ref, ps_ref, dest_ref, run_s, *, tm):
    @pl.when(pl.program_id(0) == 0)
    def _():
        run_s[...] = jnp.zeros_like(run_s)

    route = route_ref[...]
    lane = _iota((tm, LANES), 1).astype(f32)
    oh1 = (lane == route[:, 0:1]).astype(f32)
    oh2 = (lane == route[:, 1:2]).astype(f32)
    tot = oh1 + oh2
    strict = (_iota((tm, tm), 0) > _iota((tm, tm), 1)).astype(bf16)
    before = _dotf(strict, tot.astype(bf16)) + run_s[0:1, :] + ps_ref[0:1, :]
    d1 = jnp.sum(oh1 * before, axis=1, keepdims=True)
    d2 = jnp.sum(oh2 * before, axis=1, keepdims=True)
    lane_i = _iota((tm, LANES), 1)
    dest_ref[...] = jnp.where(lane_i == 0, d1, jnp.where(lane_i == 1, d2, 0.0)).astype(i32)
    run_s[...] += jnp.broadcast_to(jnp.sum(tot, axis=0, keepdims=True), run_s.shape)


def _plan(route, pstart):
    n = route.shape[0]
    tm = min(256, n)
    return pl.pallas_call(
        functools.partial(_plan_kernel, tm=tm), grid=(n // tm,),
        in_specs=[pl.BlockSpec((tm, LANES), lambda i: (i, 0)), pl.BlockSpec((8, LANES), lambda i: (0, 0))],
        out_specs=pl.BlockSpec((tm, LANES), lambda i: (i, 0)),
        out_shape=jax.ShapeDtypeStruct((n, LANES), i32),
        scratch_shapes=[pltpu.VMEM((8, LANES), f32)],
        compiler_params=_cp(("arbitrary",)), name="moe_plan")(route, pstart)


def _dispatch_kernel(dest_ref, h_ref, xs_in, xs_out, sem, *, tm):
    del xs_in

    def row_copy(r, d):
        return pltpu.make_async_copy(h_ref.at[pl.ds(r, 1), :], xs_out.at[pl.ds(d, 1), :], sem)

    def issue(r, _):
        row_copy(r, dest_ref[2 * r]).start()
        row_copy(r, dest_ref[2 * r + 1]).start()
        return 0

    lax.fori_loop(0, tm, issue, 0, unroll=8)
    for _ in range(2):
        pltpu.make_async_copy(h_ref, xs_out.at[pl.ds(0, tm), :], sem).wait()


def _dispatch(h, dest_flat, nrows):
    n = h.shape[0]
    tm = min(256, n)
    return pl.pallas_call(
        functools.partial(_dispatch_kernel, tm=tm), grid=(n // tm,),
        in_specs=[pl.BlockSpec((2 * tm,), lambda i: (i,), memory_space=pltpu.SMEM),
                  pl.BlockSpec((tm, D), lambda i: (i, 0)), pl.BlockSpec(memory_space=pl.ANY)],
        out_specs=pl.BlockSpec(memory_space=pl.ANY),
        out_shape=jax.ShapeDtypeStruct((nrows, D), f32),
        scratch_shapes=[pltpu.SemaphoreType.DMA(())],
        input_output_aliases={2: 0},
        compiler_params=_cp(("arbitrary",)), name="moe_dispatch")(dest_flat, h, jnp.zeros((nrows, D), f32))


def _ffn_kernel(be_ref, nu_ref, x_ref, w1_ref, w3_ref, w2_ref, o_ref):
    i = pl.program_id(0)

    @pl.when(i < nu_ref[0])
    def _():
        x = x_ref[...].astype(bf16)
        a = _dotf(x, w1_ref[...].astype(bf16))
        b = _dotf(x, w3_ref[...].astype(bf16))
        hid = ((a * _sigmoid(a)) * b).astype(bf16)
        o_ref[...] = _dotf(hid, w2_ref[...].astype(bf16))

    @pl.when(i >= nu_ref[0])
    def _():
        o_ref[...] = jnp.zeros_like(o_ref)


def _ffn(xs, blk_e, nused, w1, w3, w2, l):
    nblk = xs.shape[0] // MOE_ROWS
    return pl.pallas_call(
        _ffn_kernel,
        grid_spec=pltpu.PrefetchScalarGridSpec(
            num_scalar_prefetch=2, grid=(nblk,),
            in_specs=[pl.BlockSpec((MOE_ROWS, D), lambda i, be, nu: (i, 0)),
                      pl.BlockSpec((None, None, D, MOE_FF), lambda i, be, nu: (l, be[i], 0, 0)),
                      pl.BlockSpec((None, None, D, MOE_FF), lambda i, be, nu: (l, be[i], 0, 0)),
                      pl.BlockSpec((None, None, MOE_FF, D), lambda i, be, nu: (l, be[i], 0, 0))],
            out_specs=pl.BlockSpec((MOE_ROWS, D), lambda i, be, nu: (i, 0))),
        out_shape=jax.ShapeDtypeStruct(xs.shape, f32),
        compiler_params=_cp(("arbitrary",)), name="moe_ffn")(blk_e, nused, xs, w1, w3, w2)


def _combine_kernel(dest_ref, x_ref, gate_ref, route_ref, ys_ref, o_ref, buf_a, buf_b, sem, *, tm):
    def row_copy(d, buf, r):
        return pltpu.make_async_copy(ys_ref.at[pl.ds(d, 1), :], buf.at[pl.ds(r, 1), :], sem)

    def issue(r, _):
        row_copy(dest_ref[2 * r], buf_a, r).start()
        row_copy(dest_ref[2 * r + 1], buf_b, r).start()
        return 0

    lax.fori_loop(0, tm, issue, 0, unroll=8)
    for buf in (buf_a, buf_b):
        pltpu.make_async_copy(ys_ref.at[pl.ds(0, tm), :], buf, sem).wait()
    route = route_ref[...]
    o_ref[...] = x_ref[...] + gate_ref[...] * (route[:, 2:3] * buf_a[...] + route[:, 3:4] * buf_b[...])


def _combine(x, gate, route, ys, dest_flat):
    b, t, _ = x.shape
    tm = min(256, t)
    nt = t // tm
    return pl.pallas_call(
        functools.partial(_combine_kernel, tm=tm), grid=(b, nt),
        in_specs=[pl.BlockSpec((2 * tm,), lambda b, i: (b * nt + i,), memory_space=pltpu.SMEM),
                  pl.BlockSpec((None, tm, D), lambda b, i: (b, i, 0)), _mod_spec(gate, tm),
                  pl.BlockSpec((None, tm, LANES), lambda b, i: (b, i, 0)), pl.BlockSpec(memory_space=pl.ANY)],
        out_specs=pl.BlockSpec((None, tm, D), lambda b, i: (b, i, 0)),
        out_shape=jax.ShapeDtypeStruct((b, t, D), f32),
        scratch_shapes=[pltpu.VMEM((tm, D), f32), pltpu.VMEM((tm, D), f32), pltpu.SemaphoreType.DMA(())],
        compiler_params=_cp(("arbitrary", "arbitrary")), name="moe_combine")(dest_flat, x, gate, route, ys)


def _moe(x, sh, sc, gate, g, wr, br, w1, w3, w2, l):
    b, t, _ = x.shape
    n = b * t
    h, route, cnt = _router(x, sh, sc, g, wr, br)
    counts = cnt[0, :MOE_EXPERTS].astype(i32)
    pcounts = (counts + MOE_ROWS - 1) // MOE_ROWS * MOE_ROWS
    pends = jnp.cumsum(pcounts)
    pstarts = pends - pcounts
    nblk = -(-2 * n // MOE_ROWS) + MOE_EXPERTS
    blk_start = jnp.arange(nblk, dtype=i32) * MOE_ROWS
    blk_e = jnp.minimum(jnp.sum((pends[None, :] <= blk_start[:, None]).astype(i32), axis=1), MOE_EXPERTS - 1)
    nused = (pends[-1:] // MOE_ROWS).astype(i32)
    ps = jnp.zeros((8, LANES), f32).at[:, :MOE_EXPERTS].set(pstarts.astype(f32)[None])
    dest = _plan(route.reshape(n, LANES), ps)
    dest_flat = dest[:, :2].reshape(2 * n)
    xs = _dispatch(h.reshape(n, D), dest_flat, nblk * MOE_ROWS)
    ys = _ffn(xs, blk_e, nused, w1, w3, w2, l)
    return _combine(x, gate, route, ys, dest_flat)


def _rope_tables(pos):
    half = HD // 2
    inv = ROPE_THETA ** (-jnp.arange(half, dtype=f32) / half)
    ang = pos.astype(f32)[:, None] * inv[None, :]
    cos, sin = jnp.cos(ang), jnp.sin(ang)
    return jnp.tile(jnp.concatenate([cos, cos], axis=1), (1, 2)), jnp.tile(jnp.concatenate([-sin, sin], axis=1), (1, 2))


def _seg_mats(width):
    nseg = width // HD
    seg = (jnp.arange(width)[:, None] // HD == jnp.arange(LANES)[None, :]).astype(bf16)
    return seg, seg.T


def _even_params(i, ev_w_in, ev_w_out, ev_conv_w, ev_conv_b, ev_dt_bias, ev_a_log, ev_d_skip, ev_ssd_norm, ev_q_norm,
                 ev_k_norm, ev_cmp_w):
    w = ev_w_in[i]
    o = [0, 1024, 2560, 2576, 3600, 5136, 5184]
    small = jnp.concatenate([w[:, o[2]:o[3]], w[:, o[5]:o[6]], jnp.zeros((D, 64), f32)], axis=1)
    ws = [w[:, o[0]:o[1]], w[:, o[1]:o[2]], w[:, o[3]:o[4]], w[:, o[4]:o[5]], small]
    pad = lambda v: jnp.zeros((1, LANES), f32).at[0, :v.shape[0]].set(v)
    ssd = [ev_conv_w[i], ev_conv_b[i][None], pad(ev_dt_bias[i]), pad(-jnp.exp(ev_a_log[i])),
           jnp.repeat(ev_d_skip[i], SSD_HD)[None], ev_ssd_norm[i][None]]
    segq, expq = _seg_mats(D)
    segk, expk = _seg_mats(1536)
    kn = ev_k_norm[i]
    z4 = jnp.zeros((256,), f32)
    gk = jnp.concatenate([jnp.tile(kn[0], 4), z4, jnp.tile(kn[1], 4), z4, jnp.tile(kn[2], 4), z4])[None]
    vm = jnp.concatenate([z4, z4 + 1, z4, z4 + 1, z4, z4 + 1])[None]
    cw = jnp.tile(ev_cmp_w[i], (1, 1, 4))
    prep = [segq, expq, segk, expk, jnp.tile(ev_q_norm[i], 16)[None], gk, vm, cw]
    lane = jnp.arange(LANES)[None, None, :, None]
    col = jnp.arange(256)[None, None, None, :]
    gidx = jnp.arange(NSA_KVH)[:, None, None, None]
    jidx = jnp.arange(3)[None, :, None, None]
    gexp = (lane == 16 + ((gidx * NSA_REP + col // HD) * 3 + jidx)).astype(bf16)
    wo = ev_w_out[i]
    return dict(ws=[a.astype(bf16) for a in ws], ssd=ssd, prep=prep, gexp=gexp, cmp_w=ev_cmp_w[i],
                wo=[wo[:SSD_INNER].astype(bf16), wo[SSD_INNER:].astype(bf16)])


def _odd_params(i, od_w_in, od_w_out, a_re, a_im, log_dt, b_re, b_im, c_re, c_im, d, glu_w, glu_b, wa2, ba, gnorm):
    w = od_w_in[i]
    o = [0, 512, 768, 1024, 1536, 2048, 2064]
    small = jnp.concatenate([w[:, o[5]:o[6]], jnp.zeros((D, LANES - GLA_RANK), f32)], axis=1)
    ws = [w[:, o[k]:o[k + 1]] for k in range(5)] + [small]
    are, aim = a_re[i], a_im[i]
    dt = jnp.exp(log_dt[i])[:, None]
    lr, li = are * dt, aim * dt
    ab_re, ab_im = jnp.exp(lr) * jnp.cos(li), jnp.exp(lr) * jnp.sin(li)
    den = are * are + aim * aim
    nr = ab_re - 1.0
    f_re = (nr * are + ab_im * aim) / den
    f_im = (ab_im * are - nr * aim) / den
    bb_re = f_re[..., None] * b_re[i] - f_im[..., None] * b_im[i]
    bb_im = f_re[..., None] * b_im[i] + f_im[..., None] * b_re[i]
    eye = jnp.eye(S5_GROUPS, dtype=f32)
    bdiag = lambda m: jnp.einsum('gpc,gh->gchp', m, eye).reshape(S5_CH, S5_N)
    cdiag = lambda m: jnp.einsum('gcp,gh->gphc', m, eye).reshape(S5_N, S5_CH)
    bd = jnp.concatenate([bdiag(bb_re), bdiag(bb_im)], axis=1).astype(bf16)
    npow = 32
    kk = jnp.arange(1, npow + 1, dtype=f32)[:, None, None]
    mag, ang = jnp.exp(lr[None] * kk), li[None] * kk
    pwr, pwi = (mag * jnp.cos(ang)).reshape(npow, S5_N), (mag * jnp.sin(ang)).reshape(npow, S5_N)
    s5 = [bd, pwr, pwi, cdiag(c_re[i]).astype(bf16), cdiag(-c_im[i]).astype(bf16), d[i][None],
          glu_w[i].astype(bf16), glu_b[i][None]]
    wa = jnp.zeros((LANES, 256), f32).at[:GLA_RANK].set(wa2[i]).astype(bf16)
    gla = [wa, ba[i][None], gnorm[i][None]]
    wo = od_w_out[i]
    return dict(ws=[a.astype(bf16) for a in ws], s5=s5, gla=gla, wo=[wo[:S5_CH].astype(bf16), wo[S5_CH:].astype(bf16)])


def _pad_t(a, t):
    return jnp.pad(a, ((0, 0), (0, t - a.shape[1])) + ((0, 0),) * (a.ndim - 2))


def _even_prompt(x, sh, sc, gate, g, prm):
    b, t, _ = x.shape
    z, xbc, q, kv, sm = _mod_proj(x, sh, sc, g, prm['ws'], "even_in")
    ya, convn, ssdn = _ssd(xbc, sm, z, jnp.zeros((b, 8, SSD_CONV_DIM), f32),
                           jnp.zeros((b, SSD_HEADS, SSD_HD, SSD_STATE), f32), prm['ssd'], t)
    cos_t, sin_t = _rope_tables(jnp.arange(t))
    (qc, qr, kcmp, vcmp, ksel, vsel, kwin, vwin, gates, kaug, vsa, kwh, vwa, kch, vch, kn2) = _nsa_prep(
        q, kv, sm, cos_t, sin_t, prm['prep'], True)
    nb = t // NSA_BLOCK
    padb = lambda a: jnp.pad(a, ((0, 0), (0, 0), (0, LANES - nb), (0, 0)))
    knorm = jnp.sqrt(jnp.max(kn2[:, :, 0, :2 * NSA_KVH], axis=1)).reshape(b, 2, NSA_KVH)
    kbound = jnp.zeros((b, NSA_KVH, 8, LANES), f32).at[:, :, :, 0:2].set(jnp.swapaxes(knorm, 1, 2)[:, :, None, :])
    ob = _nsa_prompt(qc, qr, padb(kch), padb(vch), kaug, vsa, kwh, vwa, gates, prm['gexp'], kbound)
    xn = _out_proj(x, gate, [ya, ob], prm['wo'], "even_out")
    r5 = lambda a: a.reshape(b, t, NSA_KVH, HD)
    keep = min(NSA_WINDOW, t)
    st = (r5(kcmp), r5(vcmp), r5(ksel), r5(vsel), r5(kwin)[:, t - keep:], r5(vwin)[:, t - keep:], ssdn, convn[:, 5:8])
    return xn, st


def _stack_q(qb, bs, ts):
    q = qb.reshape(bs, ts, NSA_KVH, NSA_REP, HD)
    q = jnp.pad(q, ((0, 0), (0, 8 - ts), (0, 0), (0, 0), (0, 0)))
    q = q.transpose(0, 2, 3, 1, 4).reshape(bs, NSA_KVH, NSA_REP * 8, HD)
    return jnp.pad(q, ((0, 0), (0, 0), (0, LANES - NSA_REP * 8), (0, 0)))


def _unstack_o(o, bs, ts):
    o = o[:, :, :NSA_REP * 8].reshape(bs, NSA_KVH, NSA_REP, 8, HD)[:, :, :, :ts]
    return o.transpose(0, 3, 1, 2, 4).reshape(1, bs * ts, D)


def _page_major(cache):
    l, p, r, h, d = cache.shape
    return jnp.transpose(cache, (0, 1, 3, 4, 2)).reshape(l, p, h * d, r)


def _even_sample(x, sh, sc, gate, g, prm, bs, ts, li, conv_state, ssd_state, page_table, cmp_k, cmp_v, sel_k, sel_v,
                 win_k, win_v):
    n = bs * ts
    npg = page_table.shape[1]
    past = npg * PAGE
    z, xbc, q, kv, sm = _mod_proj(x, sh, sc, g, prm['ws'], "even_in_s")
    seq = lambda a: _pad_t(a.reshape(bs, ts, a.shape[-1]), CHUNK)
    conv0 = jnp.pad(conv_state, ((0, 0), (5, 0), (0, 0)))
    ya, convn, ssdn = _ssd(seq(xbc), seq(sm), seq(z), conv0, ssd_state, prm['ssd'], ts)
    ya = ya[:, :ts].reshape(1, n, SSD_INNER)
    cos_t, sin_t = _rope_tables(past + jnp.arange(n) % ts)
    qc, qr, kcmp, vcmp, ksel, vsel, kwin, vwin, gates = _nsa_prep(q, kv, sm, cos_t, sin_t, prm['prep'], False)
    r8 = lambda a: _pad_t(a.reshape(bs, ts, 256), 8)
    qc_st = _stack_q(qc, bs, ts)
    nr = LANES // NSA_KVH
    qbd = jnp.einsum('bgid,gh->bgihd', _stack_q(qr, bs, ts)[:, :, :nr], jnp.eye(NSA_KVH, dtype=bf16)).reshape(bs, LANES, 256)
    rows_t = lambda a: jnp.pad(jnp.swapaxes(r8(a), 1, 2), ((0, 0), (0, 0), (0, LANES - 8)))
    cw = prm['cmp_w']
    w_t = lambda w: jnp.tile(w.T, (NSA_KVH, PAGE // NSA_BLOCK))
    kct, vct = _page_compress(page_table, _page_major(cmp_k), _page_major(cmp_v), li, w_t(cw[0]), w_t(cw[1]))
    nb_past = 2 * npg
    nbp = -(-(nb_past + 1) // LANES) * LANES
    hmaj = lambda a: jnp.pad(jnp.swapaxes(a.reshape(bs, NSA_KVH, HD, -1)[..., :nb_past], 2, 3),
                             ((0, 0), (0, 0), (0, nbp - nb_past), (0, 0)))
    cidx = jnp.arange(LANES)
    mrep = ((cidx[:, None] < NSA_REP * 8) & (cidx[None, :] < NSA_REP * 8)
            & (cidx[:, None] % 8 == cidx[None, :] % 8)).astype(bf16)
    oc, neg = _sample_select(qc_st, hmaj(kct), hmaj(vct), r8(kcmp), r8(vcmp), cw, mrep, past, ts)
    nch, bpc = npg // PCH, 2 * PCH
    slab = neg[:, :, :nb_past, :nr].reshape(bs, NSA_KVH, nch, bpc, nr).transpose(0, 2, 1, 4, 3).reshape(bs, nch, LANES, bpc)
    slab = jnp.pad(slab, ((0, 0), (0, 0), (0, 0), (0, LANES - bpc))).astype(bf16)
    expand = (jnp.arange(LANES)[:, None] == jnp.arange(PCH * PAGE)[None, :] // NSA_BLOCK).astype(bf16)
    o_s = _sample_selected(page_table, qbd, slab, expand, rows_t(ksel), rows_t(vsel), _page_major(sel_k),
                           _page_major(sel_v), li, ts)
    o_w, wkt, wvt = _sample_window(qbd, _page_major(win_k), _page_major(win_v), li, rows_t(kwin), rows_t(vwin), ts)
    unmajor = lambda a: a.reshape(bs, NSA_KVH, HD, -1).transpose(0, 3, 1, 2)
    wkn, wvn = unmajor(wkt), unmajor(wvt)
    gt = gates[0, :, 16:16 + 3 * NSA_HEADS].reshape(1, n, NSA_HEADS, 3)
    gx = lambda j: jnp.repeat(gt[..., j], HD, axis=-1)
    ob = gx(0) * _unstack_o(oc, bs, ts) + gx(1) * _unstack_o(o_s, bs, ts) + gx(2) * _unstack_o(o_w, bs, ts)
    xn = _out_proj(x, gate, [ya, ob], prm['wo'], "even_out_s")
    r5 = lambda a: a.reshape(bs, ts, NSA_KVH, HD)
    st = (r5(kcmp), r5(vcmp), r5(ksel), r5(vsel), wkn, wvn, ssdn, convn[:, 5:8])
    return xn, st


def _odd_layer(x, sh, sc, gate, g, prm, bs, ts, s5r0, s5i0, gla0):
    u, q, k, v, gg, sm = _mod_proj(x, sh, sc, g, prm['ws'], "odd_in")
    tp = -(-ts // CHUNK) * CHUNK
    seq = lambda a: _pad_t(a.reshape(bs, ts, a.shape[-1]), tp)
    yc, sr, si = _s5(seq(u), s5r0.reshape(bs, 1, S5_N), s5i0.reshape(bs, 1, S5_N), prm['s5'], ts)
    og, gn = _gla(seq(q), seq(k), seq(v), seq(gg), seq(sm), jnp.swapaxes(gla0, 2, 3), prm['gla'], ts)
    unseq = lambda a: a[:, :ts].reshape(x.shape[0], x.shape[1], a.shape[-1])
    xn = _out_proj(x, gate, [unseq(yc), unseq(og)], prm['wo'], "odd_out")
    st = (sr.reshape(bs, S5_GROUPS, S5_STATE), si.reshape(bs, S5_GROUPS, S5_STATE), jnp.swapaxes(gn, 2, 3))
    return xn, st


def kernel(x_prompt, x_sample, cache_cmp_k, cache_cmp_v, cache_sel_k, cache_sel_v, cache_win_k, cache_win_v, state_ssd, state_conv, state_s5_re, state_s5_im, state_gla, page_table, c_prompt, c_sample, ada_w, ada_b, norm_mix, norm_ffn, ev_w_in, ev_w_out, ev_conv_w, ev_conv_b, ev_dt_bias, ev_a_log, ev_d_skip, ev_ssd_norm, ev_q_norm, ev_k_norm, ev_cmp_w, od_w_in, od_w_out, od_s5_a_re, od_s5_a_im, od_s5_log_dt, od_s5_b_re, od_s5_b_im, od_s5_c_re, od_s5_c_im, od_s5_d, od_glu_w, od_glu_b, od_gla_wa2, od_gla_ba, od_gla_norm, moe_wc, moe_bc, moe_wf, moe_bf, moe_w1, moe_w3, moe_w2):
    bp, tp, _ = x_prompt.shape
    bs, ts, _ = x_sample.shape
    ns = bs * ts
    depth = ada_w.shape[0]
    bc = -(-(bp + bs) // 8) * 8
    c_all = jnp.zeros((bc, D), f32).at[:bp].set(c_prompt).at[bp:bp + bs].set(c_sample)
    mods = _ada(c_all, ada_w, ada_b)
    xp, xs = x_prompt, x_sample.reshape(1, ns, D)
    sp = {}
    ss = {}
    for l in range(depth):
        i = l // 2
        mp = [m[:, None, :] for m in jnp.split(mods[l, :bp], 6, axis=-1)]
        ms = [jnp.repeat(m, ts, axis=0)[None] for m in jnp.split(mods[l, bp:bp + bs], 6, axis=-1)]
        gm, gf = norm_mix[l][None], norm_ffn[l][None]
        if l % 2 == 0:
            prm = _even_params(i, ev_w_in, ev_w_out, ev_conv_w, ev_conv_b, ev_dt_bias, ev_a_log, ev_d_skip, ev_ssd_norm,
                               ev_q_norm, ev_k_norm, ev_cmp_w)
            xp, st_p = _even_prompt(xp, mp[0], mp[1], mp[2], gm, prm)
            xs, st_s = _even_sample(xs, ms[0], ms[1], ms[2], gm, prm, bs, ts, i, state_conv[i], state_ssd[i], page_table,
                                    cache_cmp_k, cache_cmp_v, cache_sel_k, cache_sel_v, cache_win_k, cache_win_v)
            names = ('cmp_k', 'cmp_v', 'sel_k', 'sel_v', 'win_k', 'win_v', 'ssd', 'conv')
        else:
            prm = _odd_params(i, od_w_in, od_w_out, od_s5_a_re, od_s5_a_im, od_s5_log_dt, od_s5_b_re, od_s5_b_im,
                              od_s5_c_re, od_s5_c_im, od_s5_d, od_glu_w, od_glu_b, od_gla_wa2, od_gla_ba, od_gla_norm)
            zs = jnp.zeros((bp, S5_GROUPS, S5_STATE), f32)
            xp, st_p = _odd_layer(xp, mp[0], mp[1], mp[2], gm, prm, bp, tp, zs, zs,
                                  jnp.zeros((bp, GLA_HEADS, GLA_DK, GLA_DV), f32))
            xs, st_s = _odd_layer(xs, ms[0], ms[1], ms[2], gm, prm, bs, ts, state_s5_re[i], state_s5_im[i], state_gla[i])
            names = ('s5_re', 's5_im', 'gla')
        for nm, a_p, a_s in zip(names, st_p, st_s):
            sp.setdefault(nm, []).append(a_p)
            ss.setdefault(nm, []).append(a_s)
        wr = jnp.zeros((D, LANES), f32).at[:, :MOE_GROUPS].set(moe_wc[l]).at[:, MOE_GROUPS:MOE_GROUPS + MOE_EXPERTS].set(moe_wf[l])
        br = jnp.zeros((1, LANES), f32).at[0, :MOE_GROUPS].set(moe_bc[l]).at[0, MOE_GROUPS:MOE_GROUPS + MOE_EXPERTS].set(moe_bf[l])
        xp = _moe(xp, mp[3], mp[4], mp[5], gf, wr, br, moe_w1, moe_w3, moe_w2, l)
        xs = _moe(xs, ms[3], ms[4], ms[5], gf, wr, br, moe_w1, moe_w3, moe_w2, l)
    order = ('cmp_k', 'cmp_v', 'sel_k', 'sel_v', 'win_k', 'win_v', 'ssd', 'conv', 's5_re', 's5_im', 'gla')
    outs = [xp, xs.reshape(bs, ts, D)]
    for nm in order:
        outs += [jnp.stack(sp[nm]), jnp.stack(ss[nm])]
    return tuple(outs)
```

```python
import functools
import math

import jax
import jax.numpy as jnp
from jax import lax
from jax.experimental import pallas as pl
from jax.experimental.pallas import tpu as pltpu

f32 = jnp.float32
bf16 = jnp.bfloat16
i32 = jnp.int32

D = 1024
PAGE = 128
SSD_HEADS, SSD_HD, SSD_INNER, SSD_GROUPS, SSD_STATE, SSD_CONV = 16, 64, 1024, 4, 64, 4
SSD_CONV_DIM = SSD_INNER + 2 * SSD_GROUPS * SSD_STATE
NSA_HEADS, NSA_KVH, NSA_REP, HD, NSA_BLOCK, NSA_TOPN, NSA_WINDOW = 16, 4, 4, 64, 64, 16, 512
ROPE_THETA = 10000.0
S5_CH, S5_GCH, S5_GROUPS, S5_STATE = 512, 16, 32, 64
S5_N = S5_GROUPS * S5_STATE
GLA_HEADS, GLA_DK, GLA_DV, GLA_RANK, GLA_TEMP = 4, 64, 128, 16, 16.0
MOE_GROUPS, MOE_PER_GROUP, MOE_EXPERTS, MOE_FF = 4, 8, 32, 256
MOE_ROWS = 256
EPS = 1e-6
NEG = -1e30
BIG = 1e30
GONE = -3e38
LANES = 128
CHUNK = 128
FLASH_SPLIT = 1
VT_ROWS = HD + 16
FLASH_GROUP = 2
FIXED_STAB_MAX = 40.0
VMEM_LIMIT = 48 * 2**20


def _cp(sem, vmem=VMEM_LIMIT):
    return pltpu.CompilerParams(dimension_semantics=sem, vmem_limit_bytes=vmem)


def _sigmoid(x):
    return 1.0 / (1.0 + jnp.exp(-x))


def _softplus(x):
    return jnp.maximum(x, 0.0) + jnp.log1p(jnp.exp(-jnp.abs(x)))


def _dotf(a, b):
    return jnp.dot(a, b, preferred_element_type=f32)


def _dot_nt(a, b):
    return lax.dot_general(a, b, (((1,), (1,)), ((), ())), preferred_element_type=f32)


def _dot_tn(a, b):
    return lax.dot_general(a, b, (((0,), (0,)), ((), ())), preferred_element_type=f32)


def _split2(a):
    h = a.astype(bf16)
    return h, (a - h.astype(f32)).astype(bf16)


def _split3(a):
    h = a.astype(bf16)
    r = a - h.astype(f32)
    m = r.astype(bf16)
    return h, m, (r - m.astype(f32)).astype(bf16)


def _dot3_r(a, w):
    h, m, l = _split3(a)
    return _dotf(h, w) + _dotf(m, w) + _dotf(l, w)


def _dot3_l(w, a):
    h, m, l = _split3(a)
    return _dotf(w, h) + _dotf(w, m) + _dotf(w, l)


def _iota(shape, dim):
    return lax.broadcasted_iota(i32, shape, dim)


def _tri(n):
    return _iota((n, n), 0) >= _iota((n, n), 1)


def _modulate(x, sh, sc, g):
    ms = jnp.mean(x * x, axis=-1, keepdims=True)
    return (x * lax.rsqrt(ms + EPS) * g) * (1.0 + sc) + sh


def _mod_spec(mod, tm):
    if mod.shape[1] == 1:
        return pl.BlockSpec((None, 1, D), lambda b, i: (b, 0, 0))
    return pl.BlockSpec((None, tm, D), lambda b, i: (b, i, 0))


def _full(a):
    n = a.ndim
    return pl.BlockSpec(a.shape, lambda *_: (0,) * n)


def _ada_kernel(c_ref, w_ref, b_ref, o_ref):
    c = c_ref[...]
    a = c * _sigmoid(c)
    ah, al = _split2(a)
    wh, wl = _split2(w_ref[...])
    o_ref[...] = _dotf(ah, wh) + _dotf(al, wh) + _dotf(ah, wl) + b_ref[...]


def _ada(c_all, ada_w, ada_b):
    depth, bc, tn = ada_w.shape[0], c_all.shape[0], 1536
    return pl.pallas_call(
        _ada_kernel, grid=(depth, 6 * D // tn),
        in_specs=[pl.BlockSpec((bc, D), lambda l, j: (0, 0)),
                  pl.BlockSpec((None, D, tn), lambda l, j: (l, 0, j)),
                  pl.BlockSpec((None, 1, tn), lambda l, j: (l, 0, j))],
        out_specs=pl.BlockSpec((None, bc, tn), lambda l, j: (l, 0, j)),
        out_shape=jax.ShapeDtypeStruct((depth, bc, 6 * D), f32),
        compiler_params=_cp(("parallel", "parallel")), name="ada")(c_all, ada_w, ada_b.reshape(depth, 1, 6 * D))


def _proj_kernel(x_ref, sh_ref, sc_ref, g_ref, *refs, nseg):
    h = _modulate(x_ref[...], sh_ref[...], sc_ref[...], g_ref[...]).astype(bf16)
    for i in range(nseg):
        refs[nseg + i][...] = _dotf(h, refs[i][...])


def _mod_proj(x, sh, sc, g, ws, name):
    b, t, _ = x.shape
    tm = min(256, t)
    nseg = len(ws)
    return pl.pallas_call(
        functools.partial(_proj_kernel, nseg=nseg), grid=(b, t // tm),
        in_specs=[pl.BlockSpec((None, tm, D), lambda b, i: (b, i, 0)), _mod_spec(sh, tm), _mod_spec(sc, tm),
                  pl.BlockSpec((1, D), lambda b, i: (0, 0))] + [_full(w) for w in ws],
        out_specs=[pl.BlockSpec((None, tm, w.shape[1]), lambda b, i: (b, i, 0)) for w in ws],
        out_shape=[jax.ShapeDtypeStruct((b, t, w.shape[1]), f32) for w in ws],
        compiler_params=_cp(("parallel", "parallel")), name=name)(x, sh, sc, g, *ws)


def _outproj_kernel(x_ref, gate_ref, *refs, nseg):
    acc = _dotf(refs[0][...].astype(bf16), refs[nseg][...])
    for i in range(1, nseg):
        acc = acc + _dotf(refs[i][...].astype(bf16), refs[nseg + i][...])
    refs[2 * nseg][...] = x_ref[...] + gate_ref[...] * acc


def _out_proj(x, gate, acts, ws, name):
    b, t, _ = x.shape
    tm = min(512, t)
    nseg = len(acts)
    return pl.pallas_call(
        functools.partial(_outproj_kernel, nseg=nseg), grid=(b, t // tm),
        in_specs=[pl.BlockSpec((None, tm, D), lambda b, i: (b, i, 0)), _mod_spec(gate, tm)]
        + [pl.BlockSpec((None, tm, a.shape[2]), lambda b, i: (b, i, 0)) for a in acts] + [_full(w) for w in ws],
        out_specs=pl.BlockSpec((None, tm, D), lambda b, i: (b, i, 0)),
        out_shape=jax.ShapeDtypeStruct((b, t, D), f32),
        compiler_params=_cp(("parallel", "parallel")), name=name)(x, gate, *acts, *ws)


def _ssd_kernel(xbc_ref, dt_ref, z_ref, conv0_ref, h0_ref, cw_ref, cb_ref, dtb_ref, a_ref, dsk_ref, nrm_ref,
                y_ref, convn_ref, hn_ref, xpad, hst, ybuf, *, nc, t_valid):
    L = CHUNK
    c = pl.program_id(1)

    @pl.when(c == 0)
    def _():
        hst[...] = h0_ref[...]
        xpad[0:8, :] = conv0_ref[...]

    xpad[8:8 + L, :] = xbc_ref[...]
    acc = cb_ref[...] + cw_ref[0:1, :] * xpad[5:5 + L, :]
    for k in range(1, SSD_CONV):
        acc = acc + cw_ref[k:k + 1, :] * xpad[5 + k:5 + k + L, :]
    tv_last = t_valid - (nc - 1) * L
    convn_ref[...] = xpad[tv_last:tv_last + 8, :]
    xpad[0:8, :] = xpad[L:L + 8, :]

    xc = acc * _sigmoid(acc)
    xs = xc[:, :SSD_INNER]
    bm = xc[:, SSD_INNER:SSD_INNER + 256]
    cm = xc[:, SSD_INNER + 256:]
    dt = _softplus(dt_ref[...] + dtb_ref[...])
    if t_valid < nc * L:
        dt = jnp.where(c * L + _iota((L, LANES), 0) < t_valid, dt, 0.0)
    causal = _tri(L)
    trib = causal.astype(bf16)
    cs = _dot3_l(trib, dt * a_ref[...])
    cs_t, dt_t = cs.T, dt.T
    wend_t = jnp.exp(cs_t[:, L - 1:L] - cs_t) * dt_t
    ecs = jnp.exp(cs)
    xs_t = xs.T
    for g in range(SSD_GROUPS):
        bg = bm[:, g * 64:(g + 1) * 64].astype(bf16)
        cg = cm[:, g * 64:(g + 1) * 64].astype(bf16)
        gmat = _dot_nt(cg, bg)
        for r in range(SSD_HEADS // SSD_GROUPS):
            h = g * (SSD_HEADS // SSD_GROUPS) + r
            seg = cs[:, h:h + 1] - cs_t[h:h + 1, :]
            dec = jnp.where(causal, jnp.exp(jnp.where(causal, seg, 0.0)), 0.0)
            sc = (gmat * dec * dt_t[h:h + 1, :]).astype(bf16)
            hs = hst[h]
            yh = _dotf(sc, xs[:, h * 64:(h + 1) * 64].astype(bf16)) + _dot_nt(cg, hs.astype(bf16)) * ecs[:, h:h + 1]
            ybuf[:, h * 64:(h + 1) * 64] = yh
            xw = (xs_t[h * 64:(h + 1) * 64, :] * wend_t[h:h + 1, :]).astype(bf16)
            hst[h] = hs * ecs[L - 1:L, h:h + 1] + _dotf(xw, bg)
    zz = z_ref[...]
    y = (ybuf[...] + dsk_ref[...] * xs) * (zz * _sigmoid(zz))
    gw = SSD_INNER // SSD_GROUPS
    for g in range(SSD_GROUPS):
        s = y[:, g * gw:(g + 1) * gw]
        ms = jnp.mean(s * s, axis=-1, keepdims=True)
        y_ref[:, g * gw:(g + 1) * gw] = s * lax.rsqrt(ms + EPS) * nrm_ref[:, g * gw:(g + 1) * gw]

    @pl.when(c == nc - 1)
    def _():
        hn_ref[...] = hst[...]


def _ssd(xbc, sm, z, conv0, h0, prm, t_valid):
    b, t, _ = xbc.shape
    nc = t // CHUNK
    tok = lambda w: pl.BlockSpec((None, CHUNK, w), lambda b, c: (b, c, 0))
    return pl.pallas_call(
        functools.partial(_ssd_kernel, nc=nc, t_valid=t_valid), grid=(b, nc),
        in_specs=[tok(SSD_CONV_DIM), tok(LANES), tok(SSD_INNER),
                  pl.BlockSpec((None, 8, SSD_CONV_DIM), lambda b, c: (b, 0, 0)),
                  pl.BlockSpec((None, SSD_HEADS, SSD_HD, SSD_STATE), lambda b, c: (b, 0, 0, 0))]
        + [_full(p) for p in prm],
        out_specs=[tok(SSD_INNER), pl.BlockSpec((None, 8, SSD_CONV_DIM), lambda b, c: (b, 0, 0)),
                   pl.BlockSpec((None, SSD_HEADS, SSD_HD, SSD_STATE), lambda b, c: (b, 0, 0, 0))],
        out_shape=[jax.ShapeDtypeStruct((b, t, SSD_INNER), f32), jax.ShapeDtypeStruct((b, 8, SSD_CONV_DIM), f32),
                   jax.ShapeDtypeStruct((b, SSD_HEADS, SSD_HD, SSD_STATE), f32)],
        scratch_shapes=[pltpu.VMEM((CHUNK + 8, SSD_CONV_DIM), f32), pltpu.VMEM((SSD_HEADS, SSD_HD, SSD_STATE), f32),
                        pltpu.VMEM((CHUNK, SSD_INNER), f32)],
        compiler_params=_cp(("parallel", "arbitrary")), name="ssd")(xbc, sm, z, conv0, h0, *prm)


def _seg_rinv(x, seg_ref, exp_ref):
    x2 = x * x
    h, l = _split2(x2)
    ss = _dotf(h, seg_ref[...]) + _dotf(l, seg_ref[...])
    return _dot3_r(lax.rsqrt(ss * (1.0 / HD) + EPS), exp_ref[...])


def _rope(x, cosf, sinf):
    w = x.shape[1]
    first = (_iota(x.shape, 1) % HD) < (HD // 2)
    rot = jnp.where(first, pltpu.roll(x, w - HD // 2, 1), pltpu.roll(x, HD // 2, 1))
    return x * cosf + rot * sinf


def _nsaprep_kernel(q_ref, kv_ref, sm_ref, cos_ref, sin_ref, segq_ref, expq_ref, segk_ref, expk_ref, gq_ref, gk_ref,
                    vm_ref, cw_ref, *outs, tm, compress):
    qc_ref, qr_ref, kcmp_ref, vcmp_ref, ksel_ref, vsel_ref, kwin_ref, vwin_ref, gate_ref = outs[:9]
    i = pl.program_id(1)
    cos1, sin1 = cos_ref[...], sin_ref[...]
    q = q_ref[...]
    qn = q * _seg_rinv(q, segq_ref, expq_ref) * gq_ref[...]
    qc_ref[...] = (qn * 0.125).astype(bf16)
    qr_ref[...] = (_rope(qn, jnp.concatenate([cos1] * 8, axis=1), jnp.concatenate([sin1] * 8, axis=1)) * 0.125).astype(bf16)
    kv = kv_ref[...]
    kvn = kv * (_seg_rinv(kv, segk_ref, expk_ref) * gk_ref[...] + vm_ref[...])
    cos2, sin2 = jnp.concatenate([cos1] * 2, axis=1), jnp.concatenate([sin1] * 2, axis=1)
    kcmp, vcmp = kvn[:, 0:256], kvn[:, 256:512]
    ksel, vsel = _rope(kvn[:, 512:768], cos2, sin2), kvn[:, 768:1024]
    kwin, vwin = _rope(kvn[:, 1024:1280], cos2, sin2), kvn[:, 1280:1536]
    kcmp_ref[...] = kcmp
    vcmp_ref[...] = vcmp
    ksel_ref[...] = ksel
    vsel_ref[...] = vsel
    kwin_ref[...] = kwin
    vwin_ref[...] = vwin
    gate_ref[...] = _sigmoid(sm_ref[...])
    if compress:
        kaug_ref, vsa_ref, kwh_ref, vwa_ref, kch_ref, vch_ref, kn2_ref = outs[9:]
        sq = lambda a: jnp.square(a.astype(bf16).astype(f32))
        n2 = lambda a, seg: jnp.max(_dot3_r(sq(a), seg), axis=0, keepdims=True)
        kn2_ref[...] = jnp.broadcast_to(n2(ksel, segq_ref[0:256, :]) + n2(kwin, segq_ref[256:512, :]), (8, LANES))
        blk = (i * tm + _iota((tm, LANES), 0)) // NSA_BLOCK
        onehot = (_iota((tm, LANES), 1) == blk).astype(bf16)
        ones_t = (_iota((VT_ROWS - HD, tm), 0) == 0).astype(bf16)
        zero = jnp.zeros((tm, HD), bf16)
        kcw = (kcmp.reshape(tm // NSA_BLOCK, NSA_BLOCK, 256) * cw_ref[0][None]).sum(axis=1)
        vcw = (vcmp.reshape(tm // NSA_BLOCK, NSA_BLOCK, 256) * cw_ref[1][None]).sum(axis=1)
        vsel_t, vwin_t = vsel.T, vwin.T
        for g in range(NSA_KVH):
            sl = slice(g * HD, (g + 1) * HD)
            kaug_ref[g, :, 0:HD] = ksel[:, sl].astype(bf16)
            kaug_ref[g, :, HD:2 * HD] = zero
            kaug_ref[g, :, 2 * HD:] = onehot
            vsa_ref[g, 0:HD, :] = vsel_t[sl, :].astype(bf16)
            vsa_ref[g, HD:, :] = ones_t
            kwh_ref[g] = kwin[:, sl].astype(bf16)
            vwa_ref[g, 0:HD, :] = vwin_t[sl, :].astype(bf16)
            vwa_ref[g, HD:, :] = ones_t
            kch_ref[g] = kcw[:, sl]
            vch_ref[g] = vcw[:, sl]


def _nsa_prep(q, kv, sm, cos_t, sin_t, prm, compress):
    b, t, _ = q.shape
    tm = min(512, t)
    tok = lambda w: pl.BlockSpec((None, tm, w), lambda b, i: (b, i, 0))
    tab = pl.BlockSpec((tm, LANES), lambda b, i: (i, 0))
    hm = lambda w: pl.BlockSpec((None, NSA_KVH, tm, w), lambda b, i: (b, 0, i, 0))
    out_specs = [tok(D), tok(D)] + [tok(256)] * 6 + [tok(LANES)]
    out_shape = [jax.ShapeDtypeStruct((b, t, D), bf16)] * 2 + [jax.ShapeDtypeStruct((b, t, 256), f32)] * 6 \
        + [jax.ShapeDtypeStruct((b, t, LANES), f32)]
    if compress:
        nbt = tm // NSA_BLOCK
        cspec = pl.BlockSpec((None, NSA_KVH, nbt, HD), lambda b, i: (b, 0, i, 0))
        vt = pl.BlockSpec((None, NSA_KVH, VT_ROWS, tm), lambda b, i: (b, 0, 0, i))
        out_specs += [hm(256), vt, hm(HD), vt, cspec, cspec, pl.BlockSpec((None, None, 8, LANES), lambda b, i: (b, i, 0, 0))]
        out_shape += [jax.ShapeDtypeStruct((b, NSA_KVH, t, 256), bf16), jax.ShapeDtypeStruct((b, NSA_KVH, VT_ROWS, t), bf16),
                      jax.ShapeDtypeStruct((b, NSA_KVH, t, HD), bf16), jax.ShapeDtypeStruct((b, NSA_KVH, VT_ROWS, t), bf16),
                      jax.ShapeDtypeStruct((b, NSA_KVH, t // NSA_BLOCK, HD), f32),
                      jax.ShapeDtypeStruct((b, NSA_KVH, t // NSA_BLOCK, HD), f32),
                      jax.ShapeDtypeStruct((b, t // tm, 8, LANES), f32)]
    return pl.pallas_call(
        functools.partial(_nsaprep_kernel, tm=tm, compress=compress), grid=(b, t // tm),
        in_specs=[tok(D), tok(1536), tok(LANES), tab, tab] + [_full(p) for p in prm],
        out_specs=out_specs, out_shape=out_shape,
        compiler_params=_cp(("parallel", "parallel")), name="nsa_prep")(q, kv, sm, cos_t, sin_t, *prm)


def _cmp_branch(q_heads, kc, vc, tpos):
    nb, nq = kc.shape[0], tpos.shape[1]
    nrow = _iota((nb, nq), 0)
    ok = ((nrow + 1) * NSA_BLOCK - 1) <= tpos
    imp = jnp.zeros((nb, nq), f32)
    outs = []
    for q in q_heads:
        s = jnp.where(ok, _dot_nt(kc, q), NEG)
        m = jnp.max(s, axis=0, keepdims=True)
        e = jnp.where(ok, jnp.exp(s - m), 0.0)
        l = jnp.sum(e, axis=0, keepdims=True)
        p = e / jnp.where(l > 0.0, l, 1.0)
        imp = imp + p
        outs.append(_dot_tn(p.astype(bf16), vc))
    cur = tpos // NSA_BLOCK
    impm = jnp.where((nrow == cur) | (nrow == 0), BIG, jnp.where(nrow < cur, imp, NEG))

    def pick(_, carry):
        v, sel = carry
        mx = jnp.max(v, axis=0, keepdims=True)
        idx = jnp.min(jnp.where(v == mx, nrow, nb), axis=0, keepdims=True)
        hit = nrow == idx
        return jnp.where(hit, GONE, v), jnp.where(hit, 1.0, sel)

    _, sel = lax.fori_loop(0, NSA_TOPN, pick, (impm, jnp.zeros((nb, nq), f32)))
    neg = jnp.where((sel > 0.5) & (impm > 0.5 * NEG), 0.0, NEG)
    return outs, neg


def _flash_t(lhs, k_ref, vt_ref, lo, hi, trow, tk, m_ref, acc_ref, window):
    m_ref[...] = jnp.full(m_ref.shape, NEG, f32)
    acc_ref[...] = jnp.zeros(acc_ref.shape, f32)
    nq = lhs.shape[0]
    qs = nq // FLASH_SPLIT

    def step(kt, masked):
        ks = pl.multiple_of(kt * tk, tk)
        kt_tile = k_ref[pl.ds(ks, tk), :]
        vt_tile = vt_ref[:, pl.ds(ks, tk)]
        for h in range(FLASH_SPLIT):
            cols = slice(h * qs, (h + 1) * qs)
            s = _dot_nt(kt_tile, lhs[cols])
            if masked:
                kpos = ks + _iota((tk, 1), 0)
                keep = kpos <= trow[:, cols]
                if window:
                    keep = keep & (kpos > trow[:, cols] - NSA_WINDOW)
                s = jnp.where(keep, s, NEG)
            m_old = m_ref[:, cols]
            m_new = jnp.maximum(m_old, jnp.max(s, axis=0, keepdims=True))
            p = jnp.exp(s - m_new).astype(bf16)
            acc_ref[:, cols] = jnp.exp(m_old - m_new) * acc_ref[:, cols] + _dotf(vt_tile, p)
            m_ref[:, cols] = m_new

    def body(masked):
        def f(kt, _):
            step(kt, masked)
            return 0
        return f

    if window:
        lax.fori_loop(lo, hi, body(True), 0)
    else:
        lax.fori_loop(lo, hi - 1, body(False), 0)
        step(hi - 1, True)
    acc = acc_ref[...]
    return acc[0:HD, :] / acc[HD:HD + 1, :]


def _flash_fixed(lhs, k_ref, vt_ref, lo, hi, trow, tk, mrow, acc_ref, window):
    acc_ref[...] = jnp.zeros(acc_ref.shape, f32)

    def probs(kt, masked):
        ks = pl.multiple_of(kt * tk, tk)
        s = _dot_nt(k_ref[pl.ds(ks, tk), :], lhs)
        if masked:
            kpos = ks + _iota((tk, 1), 0)
            keep = kpos <= trow
            if window:
                keep = keep & (kpos > trow - NSA_WINDOW)
            s = jnp.where(keep, s, NEG)
        return jnp.exp(s - mrow).astype(bf16)

    def group(first, masks):
        pv = None
        for j, masked in enumerate(masks):
            ks = pl.multiple_of((first + j) * tk, tk)
            d = _dotf(vt_ref[:, pl.ds(ks, tk)], probs(first + j, masked))
            pv = d if pv is None else pv + d
        acc_ref[...] += pv

    def single(masked):
        def f(kt, _):
            group(kt, (masked,))
            return 0
        return f

    if window:
        full = hi - lo == 3

        @pl.when(full)
        def _():
            group(lo, (True, NSA_WINDOW != 2 * tk or lhs.shape[0] != NSA_REP * tk, True))

        @pl.when(jnp.logical_not(full))
        def _():
            lax.fori_loop(lo, hi, single(True), 0)
    else:
        ngrp = (hi - 1 - lo) // FLASH_GROUP

        def grp(j, _):
            group(lo + FLASH_GROUP * j, (False,) * FLASH_GROUP)
            return 0

        lax.fori_loop(0, ngrp, grp, 0)
        lax.fori_loop(lo + FLASH_GROUP * ngrp, hi - 1, single(False), 0)
        group(hi - 1, (True,))
    acc = acc_ref[...]
    return acc[0:HD, :] / acc[HD:HD + 1, :]


def _nsa_kernel(qc_ref, qr_ref, kc_ref, vc_ref, kaug_ref, vsa_ref, kw_ref, vwa_ref, gate_ref, gexp_ref, kb_ref, o_ref,
                lhs_ref, m_ref, acc_ref, os_ref, ow_ref, *, tq, tk):
    qi = pl.program_id(2)
    t0 = qi * tq
    tpos = t0 + _iota((1, tq), 1)
    qh = [qc_ref[:, r * HD:(r + 1) * HD] for r in range(NSA_REP)]
    oc, neg = _cmp_branch(qh, kc_ref[...].astype(bf16), vc_ref[...].astype(bf16), tpos)
    negq = neg.T.astype(bf16)
    for r in range(NSA_REP):
        rows = slice(r * tq, (r + 1) * tq)
        lhs_ref[rows, 0:HD] = qr_ref[:, r * HD:(r + 1) * HD]
        lhs_ref[rows, HD:2 * HD] = jnp.zeros((tq, HD), bf16)
        lhs_ref[rows, 2 * HD:] = negq
    trow = t0 + _iota((1, NSA_REP * tq), 1) % tq
    hi = (t0 + tq) // tk
    lo = jnp.maximum(t0 - (NSA_WINDOW - 1), 0) // tk
    qf = lhs_ref[:, 0:HD].astype(f32)
    qh, ql = _split2(qf * qf)
    ones = jnp.ones((8, HD), bf16)
    qn = jnp.sqrt((_dot_nt(ones, qh) + _dot_nt(ones, ql))[0:1, :]) * 1.02 + 1e-6
    kb = kb_ref[...]
    for k_ref, vt_ref, kmax, out_ref, lo_b, win in ((kaug_ref, vsa_ref, kb[0:1, 0:1], os_ref, 0, False),
                                                     (kw_ref, vwa_ref, kb[0:1, 1:2], ow_ref, lo, True)):
        lhs = lhs_ref[:, 0:HD] if win else lhs_ref[...]
        mrow = qn * kmax
        safe = jnp.max(mrow) <= FIXED_STAB_MAX

        @pl.when(safe)
        def _():
            out_ref[...] = _flash_fixed(lhs, k_ref, vt_ref, lo_b, hi, trow, tk, mrow, acc_ref, win)

        @pl.when(jnp.logical_not(safe))
        def _():
            out_ref[...] = _flash_t(lhs, k_ref, vt_ref, lo_b, hi, trow, tk, m_ref, acc_ref, win)

    os_t, ow_t = os_ref[...], ow_ref[...]
    gates = gate_ref[...]
    unstack = lambda a: jnp.concatenate([a[:, r * tq:(r + 1) * tq] for r in range(NSA_REP)], axis=0).T
    o = _dot3_r(gates, gexp_ref[0]) * jnp.concatenate(oc, axis=1) \
        + _dot3_r(gates, gexp_ref[1]) * unstack(os_t) + _dot3_r(gates, gexp_ref[2]) * unstack(ow_t)
    o_ref[...] = o.astype(bf16)


def _nsa_prompt(qc, qr, kch, vch, kaug, vsa, kwh, vwa, gates, gexp, kbound):
    b, t, _ = qc.shape
    tq = tk = min(256, t)
    nbp = kch.shape[2]
    qspec = pl.BlockSpec((None, tq, 256), lambda b, g, i: (b, i, g))
    kvspec = lambda n, w: pl.BlockSpec((None, None, n, w), lambda b, g, i: (b, g, 0, 0))
    return pl.pallas_call(
        functools.partial(_nsa_kernel, tq=tq, tk=tk), grid=(b, NSA_KVH, t // tq),
        in_specs=[qspec, qspec, kvspec(nbp, HD), kvspec(nbp, HD), kvspec(t, 256), kvspec(VT_ROWS, t), kvspec(t, HD),
                  kvspec(VT_ROWS, t), pl.BlockSpec((None, tq, LANES), lambda b, g, i: (b, i, 0)),
                  pl.BlockSpec((None, 3, LANES, 256), lambda b, g, i: (g, 0, 0, 0)), kvspec(8, LANES)],
        out_specs=qspec, out_shape=jax.ShapeDtypeStruct((b, t, D), bf16),
        scratch_shapes=[pltpu.VMEM((NSA_REP * tq, 256), bf16), pltpu.VMEM((1, NSA_REP * tq), f32),
                        pltpu.VMEM((VT_ROWS, NSA_REP * tq), f32), pltpu.VMEM((HD, NSA_REP * tq), f32),
                        pltpu.VMEM((HD, NSA_REP * tq), f32)],
        compiler_params=_cp(("parallel", "parallel", "arbitrary")), name="nsa_prompt")(
            qc, qr, kch, vch, kaug, vsa, kwh, vwa, gates, gexp, kbound)


PCH = 8


def _chunk_copies(cache_ref, li, pt_ref, b, c, buf, slot, sem):
    return [pltpu.make_async_copy(cache_ref.at[li, pt_ref[b, c * PCH + p]], buf.at[slot, p], sem.at[slot])
            for p in range(PCH)]


def _pagecmp_kernel(pt_ref, wk_ref, wv_ref, ck_ref, cv_ref, kc_ref, vc_ref, kbuf, vbuf, sem, *, li, npg):
    b = pl.program_id(0)
    nch = npg // PCH
    nbo = kc_ref.shape[1]

    def copies(c, slot):
        return _chunk_copies(ck_ref, li, pt_ref, b, c, kbuf, slot, sem) + _chunk_copies(cv_ref, li, pt_ref, b, c, vbuf, slot, sem)

    for cp in copies(0, 0):
        cp.start()
    kc_ref[...] = jnp.zeros(kc_ref.shape, f32)
    vc_ref[...] = jnp.zeros(vc_ref.shape, f32)
    rowblk = _iota((PAGE, nbo), 0) // NSA_BLOCK
    col = _iota((PAGE, nbo), 1)

    def body(c, _):
        slot = c % 2

        @pl.when(c + 1 < nch)
        def _():
            for cp in copies(c + 1, 1 - slot):
                cp.start()

        for cp in copies(c, slot):
            cp.wait()
        for p in range(PCH):
            place = (col == 2 * (c * PCH + p) + rowblk).astype(bf16)
            kc_ref[...] += _dotf((kbuf[slot, p] * wk_ref[...]).astype(bf16), place)
            vc_ref[...] += _dotf((vbuf[slot, p] * wv_ref[...]).astype(bf16), place)
        return 0

    lax.fori_loop(0, nch, body, 0)


def _page_compress(page_table, cache_k, cache_v, li, wk_t, wv_t):
    bs, npg = page_table.shape
    nbo = -(-2 * npg // LANES) * LANES
    out = pl.BlockSpec((None, 256, nbo), lambda b, pt: (b, 0, 0))
    shp = jax.ShapeDtypeStruct((bs, 256, nbo), f32)
    anyspec = pl.BlockSpec(memory_space=pl.ANY)
    return pl.pallas_call(
        functools.partial(_pagecmp_kernel, li=li, npg=npg),
        grid_spec=pltpu.PrefetchScalarGridSpec(
            num_scalar_prefetch=1, grid=(bs,),
            in_specs=[pl.BlockSpec((256, PAGE), lambda b, pt: (0, 0)), pl.BlockSpec((256, PAGE), lambda b, pt: (0, 0)),
                      anyspec, anyspec],
            out_specs=[out, out],
            scratch_shapes=[pltpu.VMEM((2, PCH, 256, PAGE), f32), pltpu.VMEM((2, PCH, 256, PAGE), f32),
                            pltpu.SemaphoreType.DMA((2,))]),
        out_shape=[shp, shp], compiler_params=_cp(("arbitrary",)), name="page_compress")(
            page_table, wk_t, wv_t, cache_k, cache_v)


def _ssel_kernel(qc_ref, kc_ref, vc_ref, kn_ref, vn_ref, w_ref, mrep_ref, oc_ref, neg_ref, *, past, ts):
    nbp = kc_ref.shape[1]
    nb_past = past // NSA_BLOCK
    col = _iota((1, LANES), 1)
    tpos = past + col % 8
    newrow = (_iota((nbp, 1), 0) == nb_past).astype(f32)
    tmask = (_iota((8, 1), 0) < ts).astype(f32)
    for g in range(NSA_KVH):
        sl = slice(g * HD, (g + 1) * HD)
        kc_new = jnp.sum(kn_ref[:, sl] * tmask * w_ref[0, 0:8, :], axis=0, keepdims=True)
        vc_new = jnp.sum(vn_ref[:, sl] * tmask * w_ref[1, 0:8, :], axis=0, keepdims=True)
        kc = (kc_ref[g] + newrow * kc_new).astype(bf16)
        vc = (vc_ref[g] + newrow * vc_new).astype(bf16)
        q = qc_ref[g]
        nrow = _iota((nbp, LANES), 0)
        ok = ((nrow + 1) * NSA_BLOCK - 1) <= tpos
        s = jnp.where(ok, _dot_nt(kc, q), NEG)
        m = jnp.max(s, axis=0, keepdims=True)
        e = jnp.where(ok, jnp.exp(s - m), 0.0)
        l = jnp.sum(e, axis=0, keepdims=True)
        p = e / jnp.where(l > 0.0, l, 1.0)
        oc_ref[g] = _dot_tn(p.astype(bf16), vc)
        imp = _dot3_r(p, mrep_ref[...])
        cur = tpos // NSA_BLOCK
        impm = jnp.where((nrow == cur) | (nrow == 0), BIG, jnp.where(nrow < cur, imp, NEG))

        def pick(_, carry):
            v, sel = carry
            mx = jnp.max(v, axis=0, keepdims=True)
            idx = jnp.min(jnp.where(v == mx, nrow, nbp), axis=0, keepdims=True)
            hit = nrow == idx
            return jnp.where(hit, GONE, v), jnp.where(hit, 1.0, sel)

        _, sel = lax.fori_loop(0, NSA_TOPN, pick, (impm, jnp.zeros((nbp, LANES), f32)))
        neg_ref[g] = jnp.where((sel > 0.5) & (impm > 0.5 * NEG), 0.0, NEG)


def _sample_select(qc_st, kc_h, vc_h, kn, vn, cmp_w, mrep, past, ts):
    bs, _, nbp, _ = kc_h.shape
    b4 = lambda n, w: pl.BlockSpec((None, NSA_KVH, n, w), lambda b: (b, 0, 0, 0))
    return pl.pallas_call(
        functools.partial(_ssel_kernel, past=past, ts=ts), grid=(bs,),
        in_specs=[b4(LANES, HD), b4(nbp, HD), b4(nbp, HD), pl.BlockSpec((None, 8, 256), lambda b: (b, 0, 0)),
                  pl.BlockSpec((None, 8, 256), lambda b: (b, 0, 0)), _full(cmp_w), _full(mrep)],
        out_specs=[b4(LANES, HD), b4(nbp, LANES)],
        out_shape=[jax.ShapeDtypeStruct((bs, NSA_KVH, LANES, HD), f32), jax.ShapeDtypeStruct((bs, NSA_KVH, nbp, LANES), f32)],
        compiler_params=_cp(("parallel",)), name="sample_select")(qc_st, kc_h, vc_h, kn, vn, cmp_w, mrep)


def _new_rows_mask(ts):
    tt = _iota((LANES, LANES), 0) % 8
    lane = _iota((LANES, LANES), 1)
    return (lane <= tt) & (lane < ts)


def _diag_blocks(o_ref, acc, l):
    nr = LANES // NSA_KVH
    for g in range(NSA_KVH):
        o_ref[g] = acc[g * nr:(g + 1) * nr, g * HD:(g + 1) * HD] / l[g * nr:(g + 1) * nr, :]


def _spage_kernel(pt_ref, q_ref, slab_ref, e_ref, kn_ref, vn_ref, ck_ref, cv_ref, o_ref, kbuf, vbuf, sem, m_s, l_s, acc_s,
                  *, li, npg, ts):
    b = pl.program_id(0)
    nch = npg // PCH

    def copies(c, slot):
        return _chunk_copies(ck_ref, li, pt_ref, b, c, kbuf, slot, sem) + _chunk_copies(cv_ref, li, pt_ref, b, c, vbuf, slot, sem)

    for cp in copies(0, 0):
        cp.start()
    q = q_ref[...]
    s = jnp.where(_new_rows_mask(ts), _dotf(q, kn_ref[...].astype(bf16)), NEG)
    m = jnp.max(s, axis=1, keepdims=True)
    p = jnp.exp(s - m)
    m_s[...] = m
    l_s[...] = jnp.sum(p, axis=1, keepdims=True)
    acc_s[...] = _dot_nt(p.astype(bf16), vn_ref[...].astype(bf16))

    def body(c, _):
        slot = c % 2

        @pl.when(c + 1 < nch)
        def _():
            for cp in copies(c + 1, 1 - slot):
                cp.start()

        for cp in copies(c, slot):
            cp.wait()
        bias = _dotf(slab_ref[c], e_ref[...])
        s = jnp.concatenate([_dotf(q, kbuf[slot, p].astype(bf16)) for p in range(PCH)], axis=1) + bias
        m_old = m_s[...]
        m_new = jnp.maximum(m_old, jnp.max(s, axis=1, keepdims=True))
        p = jnp.exp(s - m_new)
        alpha = jnp.exp(m_old - m_new)
        l_s[...] = alpha * l_s[...] + jnp.sum(p, axis=1, keepdims=True)
        pb = p.astype(bf16)
        pv = _dot_nt(pb[:, 0:PAGE], vbuf[slot, 0].astype(bf16))
        for j in range(1, PCH):
            pv = pv + _dot_nt(pb[:, j * PAGE:(j + 1) * PAGE], vbuf[slot, j].astype(bf16))
        acc_s[...] = alpha * acc_s[...] + pv
        m_s[...] = m_new
        return 0

    lax.fori_loop(0, nch, body, 0)
    _diag_blocks(o_ref, acc_s[...], l_s[...])


def _sample_selected(page_table, qbd, slab, expand, kn_t, vn_t, cache_k, cache_v, li, ts):
    bs, npg = page_table.shape
    nch = npg // PCH
    per_b = lambda shp: pl.BlockSpec((None,) + shp, lambda b, pt: (b,) + (0,) * len(shp))
    anyspec = pl.BlockSpec(memory_space=pl.ANY)
    return pl.pallas_call(
        functools.partial(_spage_kernel, li=li, npg=npg, ts=ts),
        grid_spec=pltpu.PrefetchScalarGridSpec(
            num_scalar_prefetch=1, grid=(bs,),
            in_specs=[per_b((LANES, 256)), per_b((nch, LANES, LANES)),
                      pl.BlockSpec((LANES, PCH * PAGE), lambda b, pt: (0, 0)), per_b((256, LANES)), per_b((256, LANES)),
                      anyspec, anyspec],
            out_specs=per_b((NSA_KVH, LANES // NSA_KVH, HD)),
            scratch_shapes=[pltpu.VMEM((2, PCH, 256, PAGE), f32), pltpu.VMEM((2, PCH, 256, PAGE), f32),
                            pltpu.SemaphoreType.DMA((2,)), pltpu.VMEM((LANES, 1), f32), pltpu.VMEM((LANES, 1), f32),
                            pltpu.VMEM((LANES, 256), f32)]),
        out_shape=jax.ShapeDtypeStruct((bs, NSA_KVH, LANES // NSA_KVH, HD), f32),
        compiler_params=_cp(("arbitrary",)), name="sample_selected")(
            page_table, qbd, slab, expand, kn_t, vn_t, cache_k, cache_v)


def _swin_kernel(q_ref, wk_ref, wv_ref, kn_ref, vn_ref, o_ref, wko_ref, wvo_ref, *, ts, wb):
    q = q_ref[...]
    wk, wv, kn, vn = wk_ref[...], wv_ref[...], kn_ref[...], vn_ref[...]
    tt = _iota((LANES, wb), 0) % 8
    s1 = jnp.where(_iota((LANES, wb), 1) > tt + (wb - NSA_WINDOW), _dotf(q, wk.astype(bf16)), NEG)
    s2 = jnp.where(_new_rows_mask(ts), _dotf(q, kn.astype(bf16)), NEG)
    m = jnp.maximum(jnp.max(s1, axis=1, keepdims=True), jnp.max(s2, axis=1, keepdims=True))
    p1, p2 = jnp.exp(s1 - m), jnp.exp(s2 - m)
    l = jnp.sum(p1, axis=1, keepdims=True) + jnp.sum(p2, axis=1, keepdims=True)
    acc = _dot_nt(p1.astype(bf16), wv.astype(bf16)) + _dot_nt(p2.astype(bf16), vn.astype(bf16))
    _diag_blocks(o_ref, acc, l)
    tail = _iota((256, wb), 1) >= wb - ts
    pad = jnp.zeros((256, wb - LANES), f32)
    wko_ref[...] = jnp.where(tail, jnp.concatenate([pad, pltpu.roll(kn, LANES - ts, 1)], axis=1), pltpu.roll(wk, wb - ts, 1))
    wvo_ref[...] = jnp.where(tail, jnp.concatenate([pad, pltpu.roll(vn, LANES - ts, 1)], axis=1), pltpu.roll(wv, wb - ts, 1))


def _sample_window(qbd, win_k, win_v, li, kn_t, vn_t, ts):
    bs, wb = win_k.shape[1], win_k.shape[3]
    wspec = pl.BlockSpec((None, None, 256, wb), lambda b: (li, b, 0, 0))
    per_b = lambda shp: pl.BlockSpec((None,) + shp, lambda b: (b,) + (0,) * len(shp))
    return pl.pallas_call(
        functools.partial(_swin_kernel, ts=ts, wb=wb), grid=(bs,),
        in_specs=[per_b((LANES, 256)), wspec, wspec, per_b((256, LANES)), per_b((256, LANES))],
        out_specs=[per_b((NSA_KVH, LANES // NSA_KVH, HD)), per_b((256, wb)), per_b((256, wb))],
        out_shape=[jax.ShapeDtypeStruct((bs, NSA_KVH, LANES // NSA_KVH, HD), f32), jax.ShapeDtypeStruct((bs, 256, wb), f32),
                   jax.ShapeDtypeStruct((bs, 256, wb), f32)],
        compiler_params=_cp(("parallel",)), name="sample_window")(qbd, win_k, win_v, kn_t, vn_t)


def _gelu_tanh(x):
    return 0.5 * x * (1.0 + jnp.tanh(math.sqrt(2.0 / math.pi) * (x + 0.044715 * (x * x * x))))


def _s5_kernel(u_ref, h0r_ref, h0i_ref, perm_ref, permt_ref, bd_ref, pwr_ref, pwi_ref, cdr_ref, cdi_ref, d_ref, gw_ref,
               gb_ref, y_ref, sr_ref, si_ref, xr_s, xi_s, cr_s, ci_s, *, tc, srow):
    c = pl.program_id(1)

    @pl.when(c == 0)
    def _():
        cr_s[...] = h0r_ref[...]
        ci_s[...] = h0i_ref[...]

    ns = tc // 8
    u = u_ref[...]
    ub = _dotf(perm_ref[...], u.astype(bf16)).astype(bf16)
    hc, hn = S5_CH // 2, S5_N // 2
    for h in range(2):
        uh = ub[:, h * hc:(h + 1) * hc]
        xr_s[:, h * hn:(h + 1) * hn] = _dotf(uh, bd_ref[h * hc:(h + 1) * hc, h * hn:(h + 1) * hn])
        xi_s[:, h * hn:(h + 1) * hn] = _dotf(uh, bd_ref[h * hc:(h + 1) * hc, S5_N + h * hn:S5_N + (h + 1) * hn])
    ar, ai = pwr_ref[0:1, :], pwi_ref[0:1, :]

    def scan(t, carry):
        xr, xi = carry
        rs = pl.multiple_of(t * 8, 8)
        nr = ar * xr - ai * xi + xr_s[pl.ds(rs, 8), :]
        ni = ar * xi + ai * xr + xi_s[pl.ds(rs, 8), :]
        xr_s[pl.ds(rs, 8), :] = nr
        xi_s[pl.ds(rs, 8), :] = ni
        return nr, ni

    zero = jnp.zeros((8, S5_N), f32)
    fr, fi = lax.fori_loop(0, ns, scan, (zero, zero))
    asr, asi = pwr_ref[ns - 1:ns, :], pwi_ref[ns - 1:ns, :]
    cr, ci = cr_s[...], ci_s[...]
    er, ei = [], []
    for s in range(8):
        er.append(cr)
        ei.append(ci)
        cr, ci = asr * cr - asi * ci + fr[s:s + 1, :], asr * ci + asi * cr + fi[s:s + 1, :]
    cr_s[...] = cr
    ci_s[...] = ci
    er, ei = jnp.concatenate(er, axis=0), jnp.concatenate(ei, axis=0)

    def fix(t, _):
        rs = pl.multiple_of(t * 8, 8)
        pr, pi = pwr_ref[pl.ds(t, 1), :], pwi_ref[pl.ds(t, 1), :]
        xr_s[pl.ds(rs, 8), :] += pr * er - pi * ei
        xi_s[pl.ds(rs, 8), :] += pr * ei + pi * er
        return 0

    lax.fori_loop(0, ns, fix, 0)
    sr_ref[...] = xr_s[srow:srow + 8, :]
    si_ref[...] = xi_s[srow:srow + 8, :]
    yh = []
    for h in range(2):
        rows, cols = slice(h * hn, (h + 1) * hn), slice(h * hc, (h + 1) * hc)
        yh.append(_dotf(xr_s[:, rows].astype(bf16), cdr_ref[rows, cols]) + _dotf(xi_s[:, rows].astype(bf16), cdi_ref[rows, cols]))
    y = _dot3_l(permt_ref[...], jnp.concatenate(yh, axis=1)) + d_ref[...] * u
    y = _gelu_tanh(y)
    y_ref[...] = y * _sigmoid(_dotf(y.astype(bf16), gw_ref[...]) + gb_ref[...])


def _s5(u, h0r, h0i, prm, t_valid):
    b, t, _ = u.shape
    tc = min(256, t)
    ns = tc // 8
    assert (t_valid - 1) // tc == t // tc - 1
    r = (t_valid - 1) % tc
    prow = (r % ns) * 8 + r // ns
    srow = prow // 8 * 8
    rid = jnp.arange(tc)
    perm = (rid[None, :] == (rid[:, None] % 8) * ns + rid[:, None] // 8).astype(bf16)
    st = pl.BlockSpec((None, 8, S5_N), lambda b, c: (b, 0, 0))
    h0 = pl.BlockSpec((None, 1, S5_N), lambda b, c: (b, 0, 0))
    y, sr, si = pl.pallas_call(
        functools.partial(_s5_kernel, tc=tc, srow=srow), grid=(b, t // tc),
        in_specs=[pl.BlockSpec((None, tc, S5_CH), lambda b, c: (b, c, 0)), h0, h0, _full(perm), _full(perm)]
        + [_full(p) for p in prm],
        out_specs=[pl.BlockSpec((None, tc, S5_CH), lambda b, c: (b, c, 0)), st, st],
        out_shape=[jax.ShapeDtypeStruct((b, t, S5_CH), f32), jax.ShapeDtypeStruct((b, 8, S5_N), f32),
                   jax.ShapeDtypeStruct((b, 8, S5_N), f32)],
        scratch_shapes=[pltpu.VMEM((tc, S5_N), f32), pltpu.VMEM((tc, S5_N), f32), pltpu.VMEM((1, S5_N), f32),
                        pltpu.VMEM((1, S5_N), f32)],
        compiler_params=_cp(("parallel", "arbitrary")), name="s5")(u, h0r, h0i, perm, perm.T, *prm)
    return y, sr[:, prow % 8], si[:, prow % 8]


def _gla_kernel(q_ref, k_ref, v_ref, g_ref, sm_ref, s0_ref, wa_ref, ba_ref, gn_ref, o_ref, sn_ref, st, *, nc, t_valid):
    L = CHUNK
    c = pl.program_id(1)

    @pl.when(c == 0)
    def _():
        st[...] = s0_ref[...]

    x = _dotf(sm_ref[...].astype(bf16), wa_ref[...]) + ba_ref[...]
    la = (jnp.minimum(x, 0.0) - jnp.log1p(jnp.exp(-jnp.abs(x)))) * (1.0 / GLA_TEMP)
    k = k_ref[...]
    if t_valid < nc * L:
        valid = c * L + _iota((L, 1), 0) < t_valid
        la = jnp.where(valid, la, 0.0)
        k = jnp.where(valid, k, 0.0)
    causal = _tri(L)
    b = _dot3_l(causal.astype(bf16), la)
    bl = b[L - 1:L, :]
    qe = (q_ref[...] * (GLA_DK ** -0.5) * jnp.exp(b)).astype(bf16)
    ke = (k * jnp.exp(-b)).astype(bf16)
    kl = (k * jnp.exp(bl - b)).astype(bf16)
    ebl = jnp.exp(bl)
    v, gg, gn = v_ref[...], g_ref[...], gn_ref[...]
    for h in range(GLA_HEADS):
        sl = slice(h * GLA_DK, (h + 1) * GLA_DK)
        vl = slice(h * GLA_DV, (h + 1) * GLA_DV)
        att = jnp.where(causal, _dot_nt(qe[:, sl], ke[:, sl]), 0.0)
        vh = v[:, vl].astype(bf16)
        s_t = st[h]
        o = _dotf(att.astype(bf16), vh) + _dot_nt(qe[:, sl], s_t.astype(bf16))
        st[h] = s_t * ebl[:, sl] + _dot_tn(vh, kl[:, sl])
        ms = jnp.mean(o * o, axis=-1, keepdims=True)
        gh = gg[:, vl]
        o_ref[:, vl] = o * lax.rsqrt(ms + EPS) * gn * (gh * _sigmoid(gh))

    @pl.when(c == nc - 1)
    def _():
        sn_ref[...] = st[...]


def _gla(q, k, v, g, sm, s0t, prm, t_valid):
    b, t, _ = q.shape
    nc = t // CHUNK
    tok = lambda w: pl.BlockSpec((None, CHUNK, w), lambda b, c: (b, c, 0))
    st = pl.BlockSpec((None, GLA_HEADS, GLA_DV, GLA_DK), lambda b, c: (b, 0, 0, 0))
    return pl.pallas_call(
        functools.partial(_gla_kernel, nc=nc, t_valid=t_valid), grid=(b, nc),
        in_specs=[tok(256), tok(256), tok(512), tok(512), tok(LANES), st] + [_full(p) for p in prm],
        out_specs=[tok(512), st],
        out_shape=[jax.ShapeDtypeStruct((b, t, 512), f32), jax.ShapeDtypeStruct((b, GLA_HEADS, GLA_DV, GLA_DK), f32)],
        scratch_shapes=[pltpu.VMEM((GLA_HEADS, GLA_DV, GLA_DK), f32)],
        compiler_params=_cp(("parallel", "arbitrary")), name="gla")(q, k, v, g, sm, s0t, *prm)


def _router_kernel(x_ref, sh_ref, sc_ref, g_ref, w_ref, b_ref, h_ref, route_ref, cnt_ref):
    first = (pl.program_id(0) == 0) & (pl.program_id(1) == 0)
    h = _modulate(x_ref[...], sh_ref[...], sc_ref[...], g_ref[...])
    hb = lax.bitcast_convert_type(h.astype(bf16).astype(f32), jnp.uint32)
    h_ref[...] = (hb[:, :D // 2] >> 16) | hb[:, D // 2:]
    hh, hl = _split2(h)
    wh, wl = _split2(w_ref[...])
    lg = _dotf(hh, wh) + _dotf(hl, wh) + _dotf(hh, wl) + b_ref[...]
    lane = _iota(lg.shape, 1)
    big = 4 * LANES
    coarse = lane < MOE_GROUPS
    lc = jnp.where(coarse, lg, GONE)
    mx = jnp.max(lc, axis=1, keepdims=True)
    gsel = jnp.min(jnp.where(lc == mx, lane, big), axis=1, keepdims=True)
    gc = 1.0 / jnp.sum(jnp.where(coarse, jnp.exp(lg - mx), 0.0), axis=1, keepdims=True)
    base = MOE_GROUPS + gsel * MOE_PER_GROUP
    fine = (lane >= base) & (lane < base + MOE_PER_GROUP)
    mf = jnp.max(jnp.where(fine, lg, GONE), axis=1, keepdims=True)
    ef = jnp.where(fine, jnp.exp(lg - mf), 0.0)
    pf = ef / jnp.sum(ef, axis=1, keepdims=True)
    cand = jnp.where(fine, pf, -1.0)
    v1 = jnp.max(cand, axis=1, keepdims=True)
    i1 = jnp.min(jnp.where(cand == v1, lane, big), axis=1, keepdims=True)
    cand = jnp.where(lane == i1, -1.0, cand)
    v2 = jnp.max(cand, axis=1, keepdims=True)
    i2 = jnp.min(jnp.where(cand == v2, lane, big), axis=1, keepdims=True)
    e1, e2 = i1 - MOE_GROUPS, i2 - MOE_GROUPS
    w1, w2 = gc * v1 / (v1 + v2), gc * v2 / (v1 + v2)
    route_ref[...] = jnp.where(lane == 0, e1.astype(f32), jnp.where(lane == 1, e2.astype(f32),
                               jnp.where(lane == 2, w1, jnp.where(lane == 3, w2, 0.0))))
    cnt = jnp.sum((lane == e1).astype(f32) + (lane == e2).astype(f32), axis=0, keepdims=True)

    @pl.when(first)
    def _():
        cnt_ref[...] = jnp.zeros_like(cnt_ref)

    cnt_ref[...] += jnp.broadcast_to(cnt, cnt_ref.shape)


def _router(x, sh, sc, g, wr, br):
    b, t, _ = x.shape
    tm = min(256, t)
    return pl.pallas_call(
        _router_kernel, grid=(b, t // tm),
        in_specs=[pl.BlockSpec((None, tm, D), lambda b, i: (b, i, 0)), _mod_spec(sh, tm), _mod_spec(sc, tm),
                  pl.BlockSpec((1, D), lambda b, i: (0, 0)), _full(wr), _full(br)],
        out_specs=[pl.BlockSpec((None, tm, D // 2), lambda b, i: (b, i, 0)), pl.BlockSpec((None, tm, LANES), lambda b, i: (b, i, 0)),
                   pl.BlockSpec((8, LANES), lambda b, i: (0, 0))],
        out_shape=[jax.ShapeDtypeStruct((b, t, D // 2), jnp.uint32), jax.ShapeDtypeStruct((b, t, LANES), f32),
                   jax.ShapeDtypeStruct((8, LANES), f32)],
        compiler_params=_cp(("arbitrary", "arbitrary")), name="moe_router")(x, sh, sc, g, wr, br)


def _plan_kernel(route_ref, ps_ref, dest_ref, run_s, *, tm):
    @pl.when(pl.program_id(0) == 0)
    def _():
        run_s[...] = jnp.zeros_like(run_s)

    route = route_ref[...]
    lane = _iota((tm, LANES), 1).astype(f32)
    oh1 = (lane == route[:, 0:1]).astype(f32)
    oh2 = (lane == route[:, 1:2]).astype(f32)
    tot = oh1 + oh2
    strict = (_iota((tm, tm), 0) > _iota((tm, tm), 1)).astype(bf16)
    before = _dotf(strict, tot.astype(bf16)) + run_s[0:1, :] + ps_ref[0:1, :]
    d1 = jnp.sum(oh1 * before, axis=1, keepdims=True)
    d2 = jnp.sum(oh2 * before, axis=1, keepdims=True)
    lane_i = _iota((tm, LANES), 1)
    dest_ref[...] = jnp.where(lane_i == 0, d1, jnp.where(lane_i == 1, d2, 0.0)).astype(i32)
    run_s[...] += jnp.broadcast_to(jnp.sum(tot, axis=0, keepdims=True), run_s.shape)


def _plan(route, pstart):
    n = route.shape[0]
    tm = min(256, n)
    return pl.pallas_call(
        functools.partial(_plan_kernel, tm=tm), grid=(n // tm,),
        in_specs=[pl.BlockSpec((tm, LANES), lambda i: (i, 0)), pl.BlockSpec((8, LANES), lambda i: (0, 0))],
        out_specs=pl.BlockSpec((tm, LANES), lambda i: (i, 0)),
        out_shape=jax.ShapeDtypeStruct((n, LANES), i32),
        scratch_shapes=[pltpu.VMEM((8, LANES), f32)],
        compiler_params=_cp(("arbitrary",)), name="moe_plan")(route, pstart)


def _dispatch_kernel(dest_ref, h_ref, xs_in, xs_out, sem, *, tm):
    del xs_in

    def row_copy(r, d):
        return pltpu.make_async_copy(h_ref.at[pl.ds(r, 1), :], xs_out.at[pl.ds(d, 1), :], sem)

    def issue(r, _):
        row_copy(r, dest_ref[2 * r]).start()
        row_copy(r, dest_ref[2 * r + 1]).start()
        return 0

    lax.fori_loop(0, tm, issue, 0, unroll=8)
    for _ in range(2):
        pltpu.make_async_copy(h_ref, xs_out.at[pl.ds(0, tm), :], sem).wait()


def _dispatch(h, dest_flat, nrows):
    n, w = h.shape
    tm = min(256, n)
    return pl.pallas_call(
        functools.partial(_dispatch_kernel, tm=tm), grid=(n // tm,),
        in_specs=[pl.BlockSpec((2 * tm,), lambda i: (i,), memory_space=pltpu.SMEM),
                  pl.BlockSpec((tm, w), lambda i: (i, 0)), pl.BlockSpec(memory_space=pl.ANY)],
        out_specs=pl.BlockSpec(memory_space=pl.ANY),
        out_shape=jax.ShapeDtypeStruct((nrows, w), h.dtype),
        scratch_shapes=[pltpu.SemaphoreType.DMA(())],
        input_output_aliases={2: 0},
        compiler_params=_cp(("arbitrary",)), name="moe_dispatch")(dest_flat, h, jnp.zeros((nrows, w), h.dtype))


def _ffn_kernel(be_ref, nu_ref, x_ref, w1_ref, w3_ref, w2_ref, o_ref):
    i = pl.program_id(0)

    @pl.when(i < nu_ref[0])
    def _():
        xp = x_ref[...]
        x_lo = lax.bitcast_convert_type(xp << 16, f32).astype(bf16)
        x_hi = lax.bitcast_convert_type(xp & jnp.uint32(0xFFFF0000), f32).astype(bf16)
        half = D // 2
        a = _dotf(x_lo, w1_ref[0:half, :].astype(bf16)) + _dotf(x_hi, w1_ref[half:, :].astype(bf16))
        b = _dotf(x_lo, w3_ref[0:half, :].astype(bf16)) + _dotf(x_hi, w3_ref[half:, :].astype(bf16))
        hid = ((a * _sigmoid(a)) * b).astype(bf16)
        o_ref[...] = _dotf(hid, w2_ref[...].astype(bf16))

    @pl.when(i >= nu_ref[0])
    def _():
        o_ref[...] = jnp.zeros_like(o_ref)


def _ffn(xs, blk_e, nused, w1, w3, w2, l):
    nblk = xs.shape[0] // MOE_ROWS
    return pl.pallas_call(
        _ffn_kernel,
        grid_spec=pltpu.PrefetchScalarGridSpec(
            num_scalar_prefetch=2, grid=(nblk,),
            in_specs=[pl.BlockSpec((MOE_ROWS, D // 2), lambda i, be, nu: (i, 0)),
                      pl.BlockSpec((None, None, D, MOE_FF), lambda i, be, nu: (l, be[i], 0, 0)),
                      pl.BlockSpec((None, None, D, MOE_FF), lambda i, be, nu: (l, be[i], 0, 0)),
                      pl.BlockSpec((None, None, MOE_FF, D), lambda i, be, nu: (l, be[i], 0, 0))],
            out_specs=pl.BlockSpec((MOE_ROWS, D), lambda i, be, nu: (i, 0))),
        out_shape=jax.ShapeDtypeStruct((xs.shape[0], D), f32),
        compiler_params=_cp(("arbitrary",)), name="moe_ffn")(blk_e, nused, xs, w1, w3, w2)


def _combine_kernel(dest_ref, x_ref, gate_ref, route_ref, ys_ref, o_ref, buf_a, buf_b, sem, *, tm):
    def row_copy(d, buf, r):
        return pltpu.make_async_copy(ys_ref.at[pl.ds(d, 1), :], buf.at[pl.ds(r, 1), :], sem)

    def issue(r, _):
        row_copy(dest_ref[2 * r], buf_a, r).start()
        row_copy(dest_ref[2 * r + 1], buf_b, r).start()
        return 0

    lax.fori_loop(0, tm, issue, 0, unroll=8)
    for buf in (buf_a, buf_b):
        pltpu.make_async_copy(ys_ref.at[pl.ds(0, tm), :], buf, sem).wait()
    route = route_ref[...]
    o_ref[...] = x_ref[...] + gate_ref[...] * (route[:, 2:3] * buf_a[...] + route[:, 3:4] * buf_b[...])


def _combine(x, gate, route, ys, dest_flat):
    b, t, _ = x.shape
    tm = min(256, t)
    nt = t // tm
    return pl.pallas_call(
        functools.partial(_combine_kernel, tm=tm), grid=(b, nt),
        in_specs=[pl.BlockSpec((2 * tm,), lambda b, i: (b * nt + i,), memory_space=pltpu.SMEM),
                  pl.BlockSpec((None, tm, D), lambda b, i: (b, i, 0)), _mod_spec(gate, tm),
                  pl.BlockSpec((None, tm, LANES), lambda b, i: (b, i, 0)), pl.BlockSpec(memory_space=pl.ANY)],
        out_specs=pl.BlockSpec((None, tm, D), lambda b, i: (b, i, 0)),
        out_shape=jax.ShapeDtypeStruct((b, t, D), f32),
        scratch_shapes=[pltpu.VMEM((tm, D), f32), pltpu.VMEM((tm, D), f32), pltpu.SemaphoreType.DMA(())],
        compiler_params=_cp(("arbitrary", "arbitrary")), name="moe_combine")(dest_flat, x, gate, route, ys)


def _moe(x, sh, sc, gate, g, wr, br, w1, w3, w2, l):
    b, t, _ = x.shape
    n = b * t
    h, route, cnt = _router(x, sh, sc, g, wr, br)
    counts = cnt[0, :MOE_EXPERTS].astype(i32)
    pcounts = (counts + MOE_ROWS - 1) // MOE_ROWS * MOE_ROWS
    pends = jnp.cumsum(pcounts)
    pstarts = pends - pcounts
    nblk = -(-2 * n // MOE_ROWS) + MOE_EXPERTS
    blk_start = jnp.arange(nblk, dtype=i32) * MOE_ROWS
    blk_e = jnp.minimum(jnp.sum((pends[None, :] <= blk_start[:, None]).astype(i32), axis=1), MOE_EXPERTS - 1)
    nused = (pends[-1:] // MOE_ROWS).astype(i32)
    ps = jnp.zeros((8, LANES), f32).at[:, :MOE_EXPERTS].set(pstarts.astype(f32)[None])
    dest = _plan(route.reshape(n, LANES), ps)
    dest_flat = dest[:, :2].reshape(2 * n)
    xs = _dispatch(h.reshape(n, D // 2), dest_flat, nblk * MOE_ROWS)
    ys = _ffn(xs, blk_e, nused, w1, w3, w2, l)
    return _combine(x, gate, route, ys, dest_flat)


def _rope_tables(pos):
    half = HD // 2
    inv = ROPE_THETA ** (-jnp.arange(half, dtype=f32) / half)
    ang = pos.astype(f32)[:, None] * inv[None, :]
    cos, sin = jnp.cos(ang), jnp.sin(ang)
    return jnp.tile(jnp.concatenate([cos, cos], axis=1), (1, 2)), jnp.tile(jnp.concatenate([-sin, sin], axis=1), (1, 2))


def _seg_mats(width):
    nseg = width // HD
    seg = (jnp.arange(width)[:, None] // HD == jnp.arange(LANES)[None, :]).astype(bf16)
    return seg, seg.T


def _even_params(i, ev_w_in, ev_w_out, ev_conv_w, ev_conv_b, ev_dt_bias, ev_a_log, ev_d_skip, ev_ssd_norm, ev_q_norm,
                 ev_k_norm, ev_cmp_w):
    w = ev_w_in[i]
    o = [0, 1024, 2560, 2576, 3600, 5136, 5184]
    small = jnp.concatenate([w[:, o[2]:o[3]], w[:, o[5]:o[6]], jnp.zeros((D, 64), f32)], axis=1)
    ws = [w[:, o[0]:o[1]], w[:, o[1]:o[2]], w[:, o[3]:o[4]], w[:, o[4]:o[5]], small]
    pad = lambda v: jnp.zeros((1, LANES), f32).at[0, :v.shape[0]].set(v)
    ssd = [ev_conv_w[i], ev_conv_b[i][None], pad(ev_dt_bias[i]), pad(-jnp.exp(ev_a_log[i])),
           jnp.repeat(ev_d_skip[i], SSD_HD)[None], ev_ssd_norm[i][None]]
    segq, expq = _seg_mats(D)
    segk, expk = _seg_mats(1536)
    kn = ev_k_norm[i]
    z4 = jnp.zeros((256,), f32)
    gk = jnp.concatenate([jnp.tile(kn[0], 4), z4, jnp.tile(kn[1], 4), z4, jnp.tile(kn[2], 4), z4])[None]
    vm = jnp.concatenate([z4, z4 + 1, z4, z4 + 1, z4, z4 + 1])[None]
    cw = jnp.tile(ev_cmp_w[i], (1, 1, 4))
    prep = [segq, expq, segk, expk, jnp.tile(ev_q_norm[i], 16)[None], gk, vm, cw]
    lane = jnp.arange(LANES)[None, None, :, None]
    col = jnp.arange(256)[None, None, None, :]
    gidx = jnp.arange(NSA_KVH)[:, None, None, None]
    jidx = jnp.arange(3)[None, :, None, None]
    gexp = (lane == 16 + ((gidx * NSA_REP + col // HD) * 3 + jidx)).astype(bf16)
    wo = ev_w_out[i]
    return dict(ws=[a.astype(bf16) for a in ws], ssd=ssd, prep=prep, gexp=gexp, cmp_w=ev_cmp_w[i],
                wo=[wo[:SSD_INNER].astype(bf16), wo[SSD_INNER:].astype(bf16)])


def _odd_params(i, od_w_in, od_w_out, a_re, a_im, log_dt, b_re, b_im, c_re, c_im, d, glu_w, glu_b, wa2, ba, gnorm):
    w = od_w_in[i]
    o = [0, 512, 768, 1024, 1536, 2048, 2064]
    small = jnp.concatenate([w[:, o[5]:o[6]], jnp.zeros((D, LANES - GLA_RANK), f32)], axis=1)
    ws = [w[:, o[k]:o[k + 1]] for k in range(5)] + [small]
    are, aim = a_re[i], a_im[i]
    dt = jnp.exp(log_dt[i])[:, None]
    lr, li = are * dt, aim * dt
    ab_re, ab_im = jnp.exp(lr) * jnp.cos(li), jnp.exp(lr) * jnp.sin(li)
    den = are * are + aim * aim
    nr = ab_re - 1.0
    f_re = (nr * are + ab_im * aim) / den
    f_im = (ab_im * are - nr * aim) / den
    bb_re = f_re[..., None] * b_re[i] - f_im[..., None] * b_im[i]
    bb_im = f_re[..., None] * b_im[i] + f_im[..., None] * b_re[i]
    eye = jnp.eye(S5_GROUPS, dtype=f32)
    bdiag = lambda m: jnp.einsum('gpc,gh->gchp', m, eye).reshape(S5_CH, S5_N)
    cdiag = lambda m: jnp.einsum('gcp,gh->gphc', m, eye).reshape(S5_N, S5_CH)
    bd = jnp.concatenate([bdiag(bb_re), bdiag(bb_im)], axis=1).astype(bf16)
    npow = 32
    kk = jnp.arange(1, npow + 1, dtype=f32)[:, None, None]
    mag, ang = jnp.exp(lr[None] * kk), li[None] * kk
    pwr, pwi = (mag * jnp.cos(ang)).reshape(npow, S5_N), (mag * jnp.sin(ang)).reshape(npow, S5_N)
    s5 = [bd, pwr, pwi, cdiag(c_re[i]).astype(bf16), cdiag(-c_im[i]).astype(bf16), d[i][None],
          glu_w[i].astype(bf16), glu_b[i][None]]
    wa = jnp.zeros((LANES, 256), f32).at[:GLA_RANK].set(wa2[i]).astype(bf16)
    gla = [wa, ba[i][None], gnorm[i][None]]
    wo = od_w_out[i]
    return dict(ws=[a.astype(bf16) for a in ws], s5=s5, gla=gla, wo=[wo[:S5_CH].astype(bf16), wo[S5_CH:].astype(bf16)])


def _pad_t(a, t):
    return jnp.pad(a, ((0, 0), (0, t - a.shape[1])) + ((0, 0),) * (a.ndim - 2))


def _even_prompt(x, sh, sc, gate, g, prm):
    b, t, _ = x.shape
    z, xbc, q, kv, sm = _mod_proj(x, sh, sc, g, prm['ws'], "even_in")
    ya, convn, ssdn = _ssd(xbc, sm, z, jnp.zeros((b, 8, SSD_CONV_DIM), f32),
                           jnp.zeros((b, SSD_HEADS, SSD_HD, SSD_STATE), f32), prm['ssd'], t)
    cos_t, sin_t = _rope_tables(jnp.arange(t))
    (qc, qr, kcmp, vcmp, ksel, vsel, kwin, vwin, gates, kaug, vsa, kwh, vwa, kch, vch, kn2) = _nsa_prep(
        q, kv, sm, cos_t, sin_t, prm['prep'], True)
    nb = t // NSA_BLOCK
    padb = lambda a: jnp.pad(a, ((0, 0), (0, 0), (0, LANES - nb), (0, 0)))
    knorm = jnp.sqrt(jnp.max(kn2[:, :, 0, :2 * NSA_KVH], axis=1)).reshape(b, 2, NSA_KVH)
    kbound = jnp.zeros((b, NSA_KVH, 8, LANES), f32).at[:, :, :, 0:2].set(jnp.swapaxes(knorm, 1, 2)[:, :, None, :])
    ob = _nsa_prompt(qc, qr, padb(kch), padb(vch), kaug, vsa, kwh, vwa, gates, prm['gexp'], kbound)
    xn = _out_proj(x, gate, [ya, ob], prm['wo'], "even_out")
    r5 = lambda a: a.reshape(b, t, NSA_KVH, HD)
    keep = min(NSA_WINDOW, t)
    st = (r5(kcmp), r5(vcmp), r5(ksel), r5(vsel), r5(kwin)[:, t - keep:], r5(vwin)[:, t - keep:], ssdn, convn[:, 5:8])
    return xn, st


def _stack_q(qb, bs, ts):
    q = qb.reshape(bs, ts, NSA_KVH, NSA_REP, HD)
    q = jnp.pad(q, ((0, 0), (0, 8 - ts), (0, 0), (0, 0), (0, 0)))
    q = q.transpose(0, 2, 3, 1, 4).reshape(bs, NSA_KVH, NSA_REP * 8, HD)
    return jnp.pad(q, ((0, 0), (0, 0), (0, LANES - NSA_REP * 8), (0, 0)))


def _unstack_o(o, bs, ts):
    o = o[:, :, :NSA_REP * 8].reshape(bs, NSA_KVH, NSA_REP, 8, HD)[:, :, :, :ts]
    return o.transpose(0, 3, 1, 2, 4).reshape(1, bs * ts, D)


def _page_major(cache):
    l, p, r, h, d = cache.shape
    return jnp.transpose(cache, (0, 1, 3, 4, 2)).reshape(l, p, h * d, r)


def _even_sample(x, sh, sc, gate, g, prm, bs, ts, li, conv_state, ssd_state, page_table, cmp_k, cmp_v, sel_k, sel_v,
                 win_k, win_v):
    n = bs * ts
    npg = page_table.shape[1]
    past = npg * PAGE
    z, xbc, q, kv, sm = _mod_proj(x, sh, sc, g, prm['ws'], "even_in_s")
    seq = lambda a: _pad_t(a.reshape(bs, ts, a.shape[-1]), CHUNK)
    conv0 = jnp.pad(conv_state, ((0, 0), (5, 0), (0, 0)))
    ya, convn, ssdn = _ssd(seq(xbc), seq(sm), seq(z), conv0, ssd_state, prm['ssd'], ts)
    ya = ya[:, :ts].reshape(1, n, SSD_INNER)
    cos_t, sin_t = _rope_tables(past + jnp.arange(n) % ts)
    qc, qr, kcmp, vcmp, ksel, vsel, kwin, vwin, gates = _nsa_prep(q, kv, sm, cos_t, sin_t, prm['prep'], False)
    r8 = lambda a: _pad_t(a.reshape(bs, ts, 256), 8)
    qc_st = _stack_q(qc, bs, ts)
    nr = LANES // NSA_KVH
    qbd = jnp.einsum('bgid,gh->bgihd', _stack_q(qr, bs, ts)[:, :, :nr], jnp.eye(NSA_KVH, dtype=bf16)).reshape(bs, LANES, 256)
    rows_t = lambda a: jnp.pad(jnp.swapaxes(r8(a), 1, 2), ((0, 0), (0, 0), (0, LANES - 8)))
    cw = prm['cmp_w']
    w_t = lambda w: jnp.tile(w.T, (NSA_KVH, PAGE // NSA_BLOCK))
    kct, vct = _page_compress(page_table, _page_major(cmp_k), _page_major(cmp_v), li, w_t(cw[0]), w_t(cw[1]))
    nb_past = 2 * npg
    nbp = -(-(nb_past + 1) // 8) * 8
    hmaj = lambda a: jnp.pad(jnp.swapaxes(a.reshape(bs, NSA_KVH, HD, -1)[..., :nb_past], 2, 3),
                             ((0, 0), (0, 0), (0, nbp - nb_past), (0, 0)))
    cidx = jnp.arange(LANES)
    mrep = ((cidx[:, None] < NSA_REP * 8) & (cidx[None, :] < NSA_REP * 8)
            & (cidx[:, None] % 8 == cidx[None, :] % 8)).astype(bf16)
    oc, neg = _sample_select(qc_st, hmaj(kct), hmaj(vct), r8(kcmp), r8(vcmp), cw, mrep, past, ts)
    nch, bpc = npg // PCH, 2 * PCH
    slab = neg[:, :, :nb_past, :nr].reshape(bs, NSA_KVH, nch, bpc, nr).transpose(0, 2, 1, 4, 3).reshape(bs, nch, LANES, bpc)
    slab = jnp.pad(slab, ((0, 0), (0, 0), (0, 0), (0, LANES - bpc))).astype(bf16)
    expand = (jnp.arange(LANES)[:, None] == jnp.arange(PCH * PAGE)[None, :] // NSA_BLOCK).astype(bf16)
    o_s = _sample_selected(page_table, qbd, slab, expand, rows_t(ksel), rows_t(vsel), _page_major(sel_k),
                           _page_major(sel_v), li, ts)
    o_w, wkt, wvt = _sample_window(qbd, _page_major(win_k), _page_major(win_v), li, rows_t(kwin), rows_t(vwin), ts)
    unmajor = lambda a: a.reshape(bs, NSA_KVH, HD, -1).transpose(0, 3, 1, 2)
    wkn, wvn = unmajor(wkt), unmajor(wvt)
    gt = gates[0, :, 16:16 + 3 * NSA_HEADS].reshape(1, n, NSA_HEADS, 3)
    gx = lambda j: jnp.repeat(gt[..., j], HD, axis=-1)
    ob = gx(0) * _unstack_o(oc, bs, ts) + gx(1) * _unstack_o(o_s, bs, ts) + gx(2) * _unstack_o(o_w, bs, ts)
    xn = _out_proj(x, gate, [ya, ob], prm['wo'], "even_out_s")
    r5 = lambda a: a.reshape(bs, ts, NSA_KVH, HD)
    st = (r5(kcmp), r5(vcmp), r5(ksel), r5(vsel), wkn, wvn, ssdn, convn[:, 5:8])
    return xn, st


def _odd_layer(x, sh, sc, gate, g, prm, bs, ts, s5r0, s5i0, gla0):
    u, q, k, v, gg, sm = _mod_proj(x, sh, sc, g, prm['ws'], "odd_in")
    tp = -(-ts // CHUNK) * CHUNK
    seq = lambda a: _pad_t(a.reshape(bs, ts, a.shape[-1]), tp)
    yc, sr, si = _s5(seq(u), s5r0.reshape(bs, 1, S5_N), s5i0.reshape(bs, 1, S5_N), prm['s5'], ts)
    og, gn = _gla(seq(q), seq(k), seq(v), seq(gg), seq(sm), jnp.swapaxes(gla0, 2, 3), prm['gla'], ts)
    unseq = lambda a: a[:, :ts].reshape(x.shape[0], x.shape[1], a.shape[-1])
    xn = _out_proj(x, gate, [unseq(yc), unseq(og)], prm['wo'], "odd_out")
    st = (sr.reshape(bs, S5_GROUPS, S5_STATE), si.reshape(bs, S5_GROUPS, S5_STATE), jnp.swapaxes(gn, 2, 3))
    return xn, st


def kernel(x_prompt, x_sample, cache_cmp_k, cache_cmp_v, cache_sel_k, cache_sel_v, cache_win_k, cache_win_v, state_ssd, state_conv, state_s5_re, state_s5_im, state_gla, page_table, c_prompt, c_sample, ada_w, ada_b, norm_mix, norm_ffn, ev_w_in, ev_w_out, ev_conv_w, ev_conv_b, ev_dt_bias, ev_a_log, ev_d_skip, ev_ssd_norm, ev_q_norm, ev_k_norm, ev_cmp_w, od_w_in, od_w_out, od_s5_a_re, od_s5_a_im, od_s5_log_dt, od_s5_b_re, od_s5_b_im, od_s5_c_re, od_s5_c_im, od_s5_d, od_glu_w, od_glu_b, od_gla_wa2, od_gla_ba, od_gla_norm, moe_wc, moe_bc, moe_wf, moe_bf, moe_w1, moe_w3, moe_w2):
    bp, tp, _ = x_prompt.shape
    bs, ts, _ = x_sample.shape
    ns = bs * ts
    depth = ada_w.shape[0]
    bc = -(-(bp + bs) // 8) * 8
    c_all = jnp.zeros((bc, D), f32).at[:bp].set(c_prompt).at[bp:bp + bs].set(c_sample)
    mods = _ada(c_all, ada_w, ada_b)
    xp, xs = x_prompt, x_sample.reshape(1, ns, D)
    sp = {}
    ss = {}
    for l in range(depth):
        i = l // 2
        mp = [m[:, None, :] for m in jnp.split(mods[l, :bp], 6, axis=-1)]
        ms = [jnp.repeat(m, ts, axis=0)[None] for m in jnp.split(mods[l, bp:bp + bs], 6, axis=-1)]
        gm, gf = norm_mix[l][None], norm_ffn[l][None]
        if l % 2 == 0:
            prm = _even_params(i, ev_w_in, ev_w_out, ev_conv_w, ev_conv_b, ev_dt_bias, ev_a_log, ev_d_skip, ev_ssd_norm,
                               ev_q_norm, ev_k_norm, ev_cmp_w)
            xp, st_p = _even_prompt(xp, mp[0], mp[1], mp[2], gm, prm)
            xs, st_s = _even_sample(xs, ms[0], ms[1], ms[2], gm, prm, bs, ts, i, state_conv[i], state_ssd[i], page_table,
                                    cache_cmp_k, cache_cmp_v, cache_sel_k, cache_sel_v, cache_win_k, cache_win_v)
            names = ('cmp_k', 'cmp_v', 'sel_k', 'sel_v', 'win_k', 'win_v', 'ssd', 'conv')
        else:
            prm = _odd_params(i, od_w_in, od_w_out, od_s5_a_re, od_s5_a_im, od_s5_log_dt, od_s5_b_re, od_s5_b_im,
                              od_s5_c_re, od_s5_c_im, od_s5_d, od_glu_w, od_glu_b, od_gla_wa2, od_gla_ba, od_gla_norm)
            zs = jnp.zeros((bp, S5_GROUPS, S5_STATE), f32)
            xp, st_p = _odd_layer(xp, mp[0], mp[1], mp[2], gm, prm, bp, tp, zs, zs,
                                  jnp.zeros((bp, GLA_HEADS, GLA_DK, GLA_DV), f32))
            xs, st_s = _odd_layer(xs, ms[0], ms[1], ms[2], gm, prm, bs, ts, state_s5_re[i], state_s5_im[i], state_gla[i])
            names = ('s5_re', 's5_im', 'gla')
        for nm, a_p, a_s in zip(names, st_p, st_s):
            sp.setdefault(nm, []).append(a_p)
            ss.setdefault(nm, []).append(a_s)
        wr = jnp.zeros((D, LANES), f32).at[:, :MOE_GROUPS].set(moe_wc[l]).at[:, MOE_GROUPS:MOE_GROUPS + MOE_EXPERTS].set(moe_wf[l])
        br = jnp.zeros((1, LANES), f32).at[0, :MOE_GROUPS].set(moe_bc[l]).at[0, MOE_GROUPS:MOE_GROUPS + MOE_EXPERTS].set(moe_bf[l])
        xp = _moe(xp, mp[3], mp[4], mp[5], gf, wr, br, moe_w1, moe_w3, moe_w2, l)
        xs = _moe(xs, ms[3], ms[4], ms[5], gf, wr, br, moe_w1, moe_w3, moe_w2, l)
    order = ('cmp_k', 'cmp_v', 'sel_k', 'sel_v', 'win_k', 'win_v', 'ssd', 'conv', 's5_re', 's5_im', 'gla')
    outs = [xp, xs.reshape(bs, ts, D)]
    for nm in order:
        outs += [jnp.stack(sp[nm]), jnp.stack(ss[nm])]
    return tuple(outs)
```

```python
import functools
import math

import jax
import jax.numpy as jnp
from jax import lax
from jax.experimental import pallas as pl
from jax.experimental.pallas import tpu as pltpu

f32 = jnp.float32
bf16 = jnp.bfloat16
i32 = jnp.int32

D = 1024
PAGE = 128
SSD_HEADS, SSD_HD, SSD_INNER, SSD_GROUPS, SSD_STATE, SSD_CONV = 16, 64, 1024, 4, 64, 4
SSD_CONV_DIM = SSD_INNER + 2 * SSD_GROUPS * SSD_STATE
NSA_HEADS, NSA_KVH, NSA_REP, HD, NSA_BLOCK, NSA_TOPN, NSA_WINDOW = 16, 4, 4, 64, 64, 16, 512
ROPE_THETA = 10000.0
S5_CH, S5_GCH, S5_GROUPS, S5_STATE = 512, 16, 32, 64
S5_N = S5_GROUPS * S5_STATE
GLA_HEADS, GLA_DK, GLA_DV, GLA_RANK, GLA_TEMP = 4, 64, 128, 16, 16.0
MOE_GROUPS, MOE_PER_GROUP, MOE_EXPERTS, MOE_FF = 4, 8, 32, 256
MOE_ROWS = 256
EPS = 1e-6
NEG = -1e30
BIG = 1e30
GONE = -3e38
LANES = 128
CHUNK = 128
VT_ROWS = HD + 16
FLASH_GROUP = 2
FIXED_STAB_MAX = 40.0
VMEM_LIMIT = 48 * 2**20


def _cp(sem, vmem=VMEM_LIMIT):
    return pltpu.CompilerParams(dimension_semantics=sem, vmem_limit_bytes=vmem)


def _sigmoid(x):
    return 1.0 / (1.0 + jnp.exp(-x))


def _softplus(x):
    return jnp.maximum(x, 0.0) + jnp.log1p(jnp.exp(-jnp.abs(x)))


def _dotf(a, b):
    return jnp.dot(a, b, preferred_element_type=f32)


def _dot_nt(a, b):
    return lax.dot_general(a, b, (((1,), (1,)), ((), ())), preferred_element_type=f32)


def _dot_tn(a, b):
    return lax.dot_general(a, b, (((0,), (0,)), ((), ())), preferred_element_type=f32)


def _split2(a):
    h = a.astype(bf16)
    return h, (a - h.astype(f32)).astype(bf16)


def _split3(a):
    h = a.astype(bf16)
    r = a - h.astype(f32)
    m = r.astype(bf16)
    return h, m, (r - m.astype(f32)).astype(bf16)


def _dot3_r(a, w):
    h, m, l = _split3(a)
    return _dotf(h, w) + _dotf(m, w) + _dotf(l, w)


def _dot3_l(w, a):
    h, m, l = _split3(a)
    return _dotf(w, h) + _dotf(w, m) + _dotf(w, l)


def _iota(shape, dim):
    return lax.broadcasted_iota(i32, shape, dim)


def _tri(n):
    return _iota((n, n), 0) >= _iota((n, n), 1)


def _modulate(x, sh, sc, g):
    ms = jnp.mean(x * x, axis=-1, keepdims=True)
    return (x * lax.rsqrt(ms + EPS) * g) * (1.0 + sc) + sh


def _mod_spec(mod, tm):
    if mod.shape[1] == 1:
        return pl.BlockSpec((None, 1, D), lambda b, i: (b, 0, 0))
    return pl.BlockSpec((None, tm, D), lambda b, i: (b, i, 0))


def _full(a):
    n = a.ndim
    return pl.BlockSpec(a.shape, lambda *_: (0,) * n)


def _ada_kernel(c_ref, w_ref, b_ref, o_ref):
    c = c_ref[...]
    a = c * _sigmoid(c)
    ah, al = _split2(a)
    wh, wl = _split2(w_ref[...])
    o_ref[...] = _dotf(ah, wh) + _dotf(al, wh) + _dotf(ah, wl) + b_ref[...]


def _ada(c_all, ada_w, ada_b):
    depth, bc, tn = ada_w.shape[0], c_all.shape[0], 1536
    return pl.pallas_call(
        _ada_kernel, grid=(depth, 6 * D // tn),
        in_specs=[pl.BlockSpec((bc, D), lambda l, j: (0, 0)),
                  pl.BlockSpec((None, D, tn), lambda l, j: (l, 0, j)),
                  pl.BlockSpec((None, 1, tn), lambda l, j: (l, 0, j))],
        out_specs=pl.BlockSpec((None, bc, tn), lambda l, j: (l, 0, j)),
        out_shape=jax.ShapeDtypeStruct((depth, bc, 6 * D), f32),
        compiler_params=_cp(("parallel", "parallel")), name="ada")(c_all, ada_w, ada_b.reshape(depth, 1, 6 * D))


def _proj_kernel(x_ref, sh_ref, sc_ref, g_ref, *refs, nseg):
    h = _modulate(x_ref[...], sh_ref[...], sc_ref[...], g_ref[...]).astype(bf16)
    for i in range(nseg):
        refs[nseg + i][...] = _dotf(h, refs[i][...])


def _mod_proj(x, sh, sc, g, ws, name):
    b, t, _ = x.shape
    tm = min(256, t)
    nseg = len(ws)
    return pl.pallas_call(
        functools.partial(_proj_kernel, nseg=nseg), grid=(b, t // tm),
        in_specs=[pl.BlockSpec((None, tm, D), lambda b, i: (b, i, 0)), _mod_spec(sh, tm), _mod_spec(sc, tm),
                  pl.BlockSpec((1, D), lambda b, i: (0, 0))] + [_full(w) for w in ws],
        out_specs=[pl.BlockSpec((None, tm, w.shape[1]), lambda b, i: (b, i, 0)) for w in ws],
        out_shape=[jax.ShapeDtypeStruct((b, t, w.shape[1]), f32) for w in ws],
        compiler_params=_cp(("parallel", "parallel")), name=name)(x, sh, sc, g, *ws)


def _outproj_kernel(x_ref, gate_ref, *refs, nseg):
    acc = _dotf(refs[0][...].astype(bf16), refs[nseg][...])
    for i in range(1, nseg):
        acc = acc + _dotf(refs[i][...].astype(bf16), refs[nseg + i][...])
    refs[2 * nseg][...] = x_ref[...] + gate_ref[...] * acc


def _out_proj(x, gate, acts, ws, name):
    b, t, _ = x.shape
    tm = min(512, t)
    nseg = len(acts)
    return pl.pallas_call(
        functools.partial(_outproj_kernel, nseg=nseg), grid=(b, t // tm),
        in_specs=[pl.BlockSpec((None, tm, D), lambda b, i: (b, i, 0)), _mod_spec(gate, tm)]
        + [pl.BlockSpec((None, tm, a.shape[2]), lambda b, i: (b, i, 0)) for a in acts] + [_full(w) for w in ws],
        out_specs=pl.BlockSpec((None, tm, D), lambda b, i: (b, i, 0)),
        out_shape=jax.ShapeDtypeStruct((b, t, D), f32),
        compiler_params=_cp(("parallel", "parallel")), name=name)(x, gate, *acts, *ws)


def _ssd_kernel(xbc_ref, dt_ref, z_ref, conv0_ref, h0_ref, cw_ref, cb_ref, dtb_ref, a_ref, dsk_ref, nrm_ref,
                y_ref, convn_ref, hn_ref, xpad, hst, ybuf, *, nc, t_valid):
    L = CHUNK
    c = pl.program_id(1)

    @pl.when(c == 0)
    def _():
        hst[...] = h0_ref[...]
        xpad[0:8, :] = conv0_ref[...]

    xpad[8:8 + L, :] = xbc_ref[...]
    acc = cb_ref[...] + cw_ref[0:1, :] * xpad[5:5 + L, :]
    for k in range(1, SSD_CONV):
        acc = acc + cw_ref[k:k + 1, :] * xpad[5 + k:5 + k + L, :]
    tv_last = t_valid - (nc - 1) * L
    convn_ref[...] = xpad[tv_last:tv_last + 8, :]
    xpad[0:8, :] = xpad[L:L + 8, :]

    xc = acc * _sigmoid(acc)
    xs = xc[:, :SSD_INNER]
    bm = xc[:, SSD_INNER:SSD_INNER + 256]
    cm = xc[:, SSD_INNER + 256:]
    dt = _softplus(dt_ref[...] + dtb_ref[...])
    if t_valid < nc * L:
        dt = jnp.where(c * L + _iota((L, LANES), 0) < t_valid, dt, 0.0)
    causal = _tri(L)
    trib = causal.astype(bf16)
    cs = _dot3_l(trib, dt * a_ref[...])
    cs_t, dt_t = cs.T, dt.T
    wend_t = jnp.exp(cs_t[:, L - 1:L] - cs_t) * dt_t
    ecs = jnp.exp(cs)
    xs_t = xs.T
    for g in range(SSD_GROUPS):
        bg = bm[:, g * 64:(g + 1) * 64].astype(bf16)
        cg = cm[:, g * 64:(g + 1) * 64].astype(bf16)
        gmat = _dot_nt(cg, bg)
        for r in range(SSD_HEADS // SSD_GROUPS):
            h = g * (SSD_HEADS // SSD_GROUPS) + r
            seg = cs[:, h:h + 1] - cs_t[h:h + 1, :]
            dec = jnp.where(causal, jnp.exp(jnp.where(causal, seg, 0.0)), 0.0)
            sc = (gmat * dec * dt_t[h:h + 1, :]).astype(bf16)
            hs = hst[h]
            yh = _dotf(sc, xs[:, h * 64:(h + 1) * 64].astype(bf16)) + _dot_nt(cg, hs.astype(bf16)) * ecs[:, h:h + 1]
            ybuf[:, h * 64:(h + 1) * 64] = yh
            xw = (xs_t[h * 64:(h + 1) * 64, :] * wend_t[h:h + 1, :]).astype(bf16)
            hst[h] = hs * ecs[L - 1:L, h:h + 1] + _dotf(xw, bg)
    zz = z_ref[...]
    y = (ybuf[...] + dsk_ref[...] * xs) * (zz * _sigmoid(zz))
    gw = SSD_INNER // SSD_GROUPS
    for g in range(SSD_GROUPS):
        s = y[:, g * gw:(g + 1) * gw]
        ms = jnp.mean(s * s, axis=-1, keepdims=True)
        y_ref[:, g * gw:(g + 1) * gw] = s * lax.rsqrt(ms + EPS) * nrm_ref[:, g * gw:(g + 1) * gw]

    @pl.when(c == nc - 1)
    def _():
        hn_ref[...] = hst[...]


def _ssd(xbc, sm, z, conv0, h0, prm, t_valid):
    b, t, _ = xbc.shape
    nc = t // CHUNK
    tok = lambda w: pl.BlockSpec((None, CHUNK, w), lambda b, c: (b, c, 0))
    return pl.pallas_call(
        functools.partial(_ssd_kernel, nc=nc, t_valid=t_valid), grid=(b, nc),
        in_specs=[tok(SSD_CONV_DIM), tok(LANES), tok(SSD_INNER),
                  pl.BlockSpec((None, 8, SSD_CONV_DIM), lambda b, c: (b, 0, 0)),
                  pl.BlockSpec((None, SSD_HEADS, SSD_HD, SSD_STATE), lambda b, c: (b, 0, 0, 0))]
        + [_full(p) for p in prm],
        out_specs=[tok(SSD_INNER), pl.BlockSpec((None, 8, SSD_CONV_DIM), lambda b, c: (b, 0, 0)),
                   pl.BlockSpec((None, SSD_HEADS, SSD_HD, SSD_STATE), lambda b, c: (b, 0, 0, 0))],
        out_shape=[jax.ShapeDtypeStruct((b, t, SSD_INNER), f32), jax.ShapeDtypeStruct((b, 8, SSD_CONV_DIM), f32),
                   jax.ShapeDtypeStruct((b, SSD_HEADS, SSD_HD, SSD_STATE), f32)],
        scratch_shapes=[pltpu.VMEM((CHUNK + 8, SSD_CONV_DIM), f32), pltpu.VMEM((SSD_HEADS, SSD_HD, SSD_STATE), f32),
                        pltpu.VMEM((CHUNK, SSD_INNER), f32)],
        compiler_params=_cp(("parallel", "arbitrary")), name="ssd")(xbc, sm, z, conv0, h0, *prm)


def _seg_rinv(x, seg_ref, exp_ref):
    x2 = x * x
    h, l = _split2(x2)
    ss = _dotf(h, seg_ref[...]) + _dotf(l, seg_ref[...])
    return _dot3_r(lax.rsqrt(ss * (1.0 / HD) + EPS), exp_ref[...])


def _rope(x, cosf, sinf):
    w = x.shape[1]
    first = (_iota(x.shape, 1) % HD) < (HD // 2)
    rot = jnp.where(first, pltpu.roll(x, w - HD // 2, 1), pltpu.roll(x, HD // 2, 1))
    return x * cosf + rot * sinf


def _nsaprep_kernel(q_ref, kv_ref, sm_ref, cos_ref, sin_ref, segq_ref, expq_ref, segk_ref, expk_ref, gq_ref, gk_ref,
                    vm_ref, cw_ref, *outs, tm, compress):
    qc_ref, qr_ref, kcmp_ref, vcmp_ref, ksel_ref, vsel_ref, kwin_ref, vwin_ref, gate_ref = outs[:9]
    i = pl.program_id(1)
    cos1, sin1 = cos_ref[...], sin_ref[...]
    q = q_ref[...]
    qn = q * _seg_rinv(q, segq_ref, expq_ref) * gq_ref[...]
    qc_ref[...] = (qn * 0.125).astype(bf16)
    qr_ref[...] = (_rope(qn, jnp.concatenate([cos1] * 8, axis=1), jnp.concatenate([sin1] * 8, axis=1)) * 0.125).astype(bf16)
    kv = kv_ref[...]
    kvn = kv * (_seg_rinv(kv, segk_ref, expk_ref) * gk_ref[...] + vm_ref[...])
    cos2, sin2 = jnp.concatenate([cos1] * 2, axis=1), jnp.concatenate([sin1] * 2, axis=1)
    kcmp, vcmp = kvn[:, 0:256], kvn[:, 256:512]
    ksel, vsel = _rope(kvn[:, 512:768], cos2, sin2), kvn[:, 768:1024]
    kwin, vwin = _rope(kvn[:, 1024:1280], cos2, sin2), kvn[:, 1280:1536]
    kcmp_ref[...] = kcmp
    vcmp_ref[...] = vcmp
    ksel_ref[...] = ksel
    vsel_ref[...] = vsel
    kwin_ref[...] = kwin
    vwin_ref[...] = vwin
    gate_ref[...] = _sigmoid(sm_ref[...])
    if compress:
        kaug_ref, vsa_ref, kwh_ref, vwa_ref, kch_ref, vch_ref, kn2_ref = outs[9:]
        sq = lambda a: jnp.square(a.astype(bf16).astype(f32))
        n2 = lambda a, seg: jnp.max(_dot3_r(sq(a), seg), axis=0, keepdims=True)
        kn2_ref[...] = jnp.broadcast_to(n2(ksel, segq_ref[0:256, :]) + n2(kwin, segq_ref[256:512, :]), (8, LANES))
        blk = (i * tm + _iota((tm, LANES), 0)) // NSA_BLOCK
        onehot = (_iota((tm, LANES), 1) == blk).astype(bf16)
        ones_t = (_iota((VT_ROWS - HD, tm), 0) == 0).astype(bf16)
        zero = jnp.zeros((tm, HD), bf16)
        kcw = (kcmp.reshape(tm // NSA_BLOCK, NSA_BLOCK, 256) * cw_ref[0][None]).sum(axis=1)
        vcw = (vcmp.reshape(tm // NSA_BLOCK, NSA_BLOCK, 256) * cw_ref[1][None]).sum(axis=1)
        vsel_t, vwin_t = vsel.T, vwin.T
        for g in range(NSA_KVH):
            sl = slice(g * HD, (g + 1) * HD)
            kaug_ref[g, :, 0:HD] = ksel[:, sl].astype(bf16)
            kaug_ref[g, :, HD:2 * HD] = zero
            kaug_ref[g, :, 2 * HD:] = onehot
            vsa_ref[g, 0:HD, :] = vsel_t[sl, :].astype(bf16)
            vsa_ref[g, HD:, :] = ones_t
            kwh_ref[g] = kwin[:, sl].astype(bf16)
            vwa_ref[g, 0:HD, :] = vwin_t[sl, :].astype(bf16)
            vwa_ref[g, HD:, :] = ones_t
            kch_ref[g] = kcw[:, sl]
            vch_ref[g] = vcw[:, sl]


def _nsa_prep(q, kv, sm, cos_t, sin_t, prm, compress):
    b, t, _ = q.shape
    tm = min(512, t)
    tok = lambda w: pl.BlockSpec((None, tm, w), lambda b, i: (b, i, 0))
    tab = pl.BlockSpec((tm, LANES), lambda b, i: (i, 0))
    hm = lambda w: pl.BlockSpec((None, NSA_KVH, tm, w), lambda b, i: (b, 0, i, 0))
    out_specs = [tok(D), tok(D)] + [tok(256)] * 6 + [tok(LANES)]
    out_shape = [jax.ShapeDtypeStruct((b, t, D), bf16)] * 2 + [jax.ShapeDtypeStruct((b, t, 256), f32)] * 6 \
        + [jax.ShapeDtypeStruct((b, t, LANES), f32)]
    if compress:
        nbt = tm // NSA_BLOCK
        cspec = pl.BlockSpec((None, NSA_KVH, nbt, HD), lambda b, i: (b, 0, i, 0))
        vt = pl.BlockSpec((None, NSA_KVH, VT_ROWS, tm), lambda b, i: (b, 0, 0, i))
        out_specs += [hm(256), vt, hm(HD), vt, cspec, cspec, pl.BlockSpec((None, None, 8, LANES), lambda b, i: (b, i, 0, 0))]
        out_shape += [jax.ShapeDtypeStruct((b, NSA_KVH, t, 256), bf16), jax.ShapeDtypeStruct((b, NSA_KVH, VT_ROWS, t), bf16),
                      jax.ShapeDtypeStruct((b, NSA_KVH, t, HD), bf16), jax.ShapeDtypeStruct((b, NSA_KVH, VT_ROWS, t), bf16),
                      jax.ShapeDtypeStruct((b, NSA_KVH, t // NSA_BLOCK, HD), f32),
                      jax.ShapeDtypeStruct((b, NSA_KVH, t // NSA_BLOCK, HD), f32),
                      jax.ShapeDtypeStruct((b, t // tm, 8, LANES), f32)]
    return pl.pallas_call(
        functools.partial(_nsaprep_kernel, tm=tm, compress=compress), grid=(b, t // tm),
        in_specs=[tok(D), tok(1536), tok(LANES), tab, tab] + [_full(p) for p in prm],
        out_specs=out_specs, out_shape=out_shape,
        compiler_params=_cp(("parallel", "parallel")), name="nsa_prep")(q, kv, sm, cos_t, sin_t, *prm)


def _cmp_branch(q_heads, kc, vc, tpos):
    nb, nq = kc.shape[0], tpos.shape[1]
    nrow = _iota((nb, nq), 0)
    ok = ((nrow + 1) * NSA_BLOCK - 1) <= tpos
    imp = jnp.zeros((nb, nq), f32)
    outs = []
    for q in q_heads:
        s = jnp.where(ok, _dot_nt(kc, q), NEG)
        m = jnp.max(s, axis=0, keepdims=True)
        e = jnp.where(ok, jnp.exp(s - m), 0.0)
        l = jnp.sum(e, axis=0, keepdims=True)
        p = e / jnp.where(l > 0.0, l, 1.0)
        imp = imp + p
        outs.append(_dot_tn(p.astype(bf16), vc))
    cur = tpos // NSA_BLOCK
    impm = jnp.where((nrow == cur) | (nrow == 0), BIG, jnp.where(nrow < cur, imp, NEG))

    def pick(_, v):
        mx = jnp.max(v, axis=0, keepdims=True)
        idx = jnp.min(jnp.where(v == mx, nrow, nb), axis=0, keepdims=True)
        return jnp.where(nrow == idx, GONE, v)

    taken = lax.fori_loop(0, NSA_TOPN, pick, impm) == GONE
    neg = jnp.where(taken & (impm > 0.5 * NEG), 0.0, NEG)
    return outs, neg


def _flash_t(lhs, k_ref, vt_ref, lo, hi, trow, tk, m_ref, acc_ref, window):
    m_ref[...] = jnp.full(m_ref.shape, NEG, f32)
    acc_ref[...] = jnp.zeros(acc_ref.shape, f32)

    def step(kt, masked):
        ks = pl.multiple_of(kt * tk, tk)
        s = _dot_nt(k_ref[pl.ds(ks, tk), :], lhs)
        if masked:
            kpos = ks + _iota((tk, 1), 0)
            keep = kpos <= trow
            if window:
                keep = keep & (kpos > trow - NSA_WINDOW)
            s = jnp.where(keep, s, NEG)
        m_old = m_ref[...]
        m_new = jnp.maximum(m_old, jnp.max(s, axis=0, keepdims=True))
        p = jnp.exp(s - m_new).astype(bf16)
        acc_ref[...] = jnp.exp(m_old - m_new) * acc_ref[...] + _dotf(vt_ref[:, pl.ds(ks, tk)], p)
        m_ref[...] = m_new

    def body(masked):
        def f(kt, _):
            step(kt, masked)
            return 0
        return f

    if window:
        lax.fori_loop(lo, hi, body(True), 0)
    else:
        lax.fori_loop(lo, hi - 1, body(False), 0)
        step(hi - 1, True)
    acc = acc_ref[...]
    return acc[0:HD, :] / acc[HD:HD + 1, :]


def _flash_fixed(lhs, k_ref, vt_ref, lo, hi, trow, tk, mrow, acc_ref, window):
    acc_ref[...] = jnp.zeros(acc_ref.shape, f32)

    def probs(kt, masked):
        ks = pl.multiple_of(kt * tk, tk)
        s = _dot_nt(k_ref[pl.ds(ks, tk), :], lhs)
        if masked:
            kpos = ks + _iota((tk, 1), 0)
            keep = kpos <= trow
            if window:
                keep = keep & (kpos > trow - NSA_WINDOW)
            s = jnp.where(keep, s, NEG)
        return jnp.exp(s - mrow).astype(bf16)

    def group(first, masks):
        pv = None
        for j, masked in enumerate(masks):
            ks = pl.multiple_of((first + j) * tk, tk)
            d = _dotf(vt_ref[:, pl.ds(ks, tk)], probs(first + j, masked))
            pv = d if pv is None else pv + d
        acc_ref[...] += pv

    def single(masked):
        def f(kt, _):
            group(kt, (masked,))
            return 0
        return f

    if window:
        full = hi - lo == 3

        @pl.when(full)
        def _():
            group(lo, (True, NSA_WINDOW != 2 * tk or lhs.shape[0] != NSA_REP * tk, True))

        @pl.when(jnp.logical_not(full))
        def _():
            lax.fori_loop(lo, hi, single(True), 0)
    else:
        ngrp = (hi - 1 - lo) // FLASH_GROUP

        def grp(j, _):
            group(lo + FLASH_GROUP * j, (False,) * FLASH_GROUP)
            return 0

        lax.fori_loop(0, ngrp, grp, 0)
        lax.fori_loop(lo + FLASH_GROUP * ngrp, hi - 1, single(False), 0)
        group(hi - 1, (True,))
    acc = acc_ref[...]
    return acc[0:HD, :] / acc[HD:HD + 1, :]


def _nsa_kernel(qc_ref, qr_ref, kc_ref, vc_ref, kaug_ref, vsa_ref, kw_ref, vwa_ref, gate_ref, gexp_ref, kb_ref, o_ref,
                lhs_ref, m_ref, acc_ref, os_ref, ow_ref, oc_ref, *, tq, tk):
    qi = pl.program_id(2)
    t0 = qi * tq
    tpos = t0 + _iota((1, tq), 1)
    qh = [qc_ref[:, r * HD:(r + 1) * HD] for r in range(NSA_REP)]
    nbp = kc_ref.shape[0]
    need = (t0 + tq) // NSA_BLOCK
    sizes = [r for r in (32, 64) if r < nbp] + [nbp]
    for j, nr in enumerate(sizes):
        fits = need <= nr if j == 0 else (need > sizes[j - 1]) if nr == nbp else (need > sizes[j - 1]) & (need <= nr)

        @pl.when(fits)
        def _():
            oc, neg = _cmp_branch(qh, kc_ref[0:nr, :].astype(bf16), vc_ref[0:nr, :].astype(bf16), tpos)
            if nr < nbp:
                neg = jnp.concatenate([neg, jnp.full((nbp - nr, tq), NEG, f32)], axis=0)
            negq = neg.T.astype(bf16)
            oc_ref[...] = jnp.concatenate(oc, axis=1)
            for r in range(NSA_REP):
                lhs_ref[r * tq:(r + 1) * tq, 2 * HD:] = negq

    for r in range(NSA_REP):
        rows = slice(r * tq, (r + 1) * tq)
        lhs_ref[rows, 0:HD] = qr_ref[:, r * HD:(r + 1) * HD]
        lhs_ref[rows, HD:2 * HD] = jnp.zeros((tq, HD), bf16)
    trow = t0 + _iota((1, NSA_REP * tq), 1) % tq
    hi = (t0 + tq) // tk
    lo = jnp.maximum(t0 - (NSA_WINDOW - 1), 0) // tk
    qf = lhs_ref[:, 0:HD].astype(f32)
    qh, ql = _split2(qf * qf)
    ones = jnp.ones((8, HD), bf16)
    qn = jnp.sqrt((_dot_nt(ones, qh) + _dot_nt(ones, ql))[0:1, :]) * 1.02 + 1e-6
    kb = kb_ref[...]
    for k_ref, vt_ref, kmax, out_ref, lo_b, win in ((kaug_ref, vsa_ref, kb[0:1, 0:1], os_ref, 0, False),
                                                     (kw_ref, vwa_ref, kb[0:1, 1:2], ow_ref, lo, True)):
        lhs = lhs_ref[:, 0:HD] if win else lhs_ref[...]
        mrow = qn * kmax
        safe = jnp.max(mrow) <= FIXED_STAB_MAX

        @pl.when(safe)
        def _():
            out_ref[...] = _flash_fixed(lhs, k_ref, vt_ref, lo_b, hi, trow, tk, mrow, acc_ref, win)

        @pl.when(jnp.logical_not(safe))
        def _():
            out_ref[...] = _flash_t(lhs, k_ref, vt_ref, lo_b, hi, trow, tk, m_ref, acc_ref, win)

    os_t, ow_t = os_ref[...], ow_ref[...]
    gates = gate_ref[...]
    unstack = lambda a: jnp.concatenate([a[:, r * tq:(r + 1) * tq] for r in range(NSA_REP)], axis=0).T
    o = _dot3_r(gates, gexp_ref[0]) * oc_ref[...] \
        + _dot3_r(gates, gexp_ref[1]) * unstack(os_t) + _dot3_r(gates, gexp_ref[2]) * unstack(ow_t)
    o_ref[...] = o.astype(bf16)


def _nsa_prompt(qc, qr, kch, vch, kaug, vsa, kwh, vwa, gates, gexp, kbound):
    b, t, _ = qc.shape
    tq = tk = min(256, t)
    nbp = kch.shape[2]
    qspec = pl.BlockSpec((None, tq, 256), lambda b, g, i: (b, i, g))
    kvspec = lambda n, w: pl.BlockSpec((None, None, n, w), lambda b, g, i: (b, g, 0, 0))
    return pl.pallas_call(
        functools.partial(_nsa_kernel, tq=tq, tk=tk), grid=(b, NSA_KVH, t // tq),
        in_specs=[qspec, qspec, kvspec(nbp, HD), kvspec(nbp, HD), kvspec(t, 256), kvspec(VT_ROWS, t), kvspec(t, HD),
                  kvspec(VT_ROWS, t), pl.BlockSpec((None, tq, LANES), lambda b, g, i: (b, i, 0)),
                  pl.BlockSpec((None, 3, LANES, 256), lambda b, g, i: (g, 0, 0, 0)), kvspec(8, LANES)],
        out_specs=qspec, out_shape=jax.ShapeDtypeStruct((b, t, D), bf16),
        scratch_shapes=[pltpu.VMEM((NSA_REP * tq, 256), bf16), pltpu.VMEM((1, NSA_REP * tq), f32),
                        pltpu.VMEM((VT_ROWS, NSA_REP * tq), f32), pltpu.VMEM((HD, NSA_REP * tq), f32),
                        pltpu.VMEM((HD, NSA_REP * tq), f32), pltpu.VMEM((tq, 256), f32)],
        compiler_params=_cp(("parallel", "parallel", "arbitrary")), name="nsa_prompt")(
            qc, qr, kch, vch, kaug, vsa, kwh, vwa, gates, gexp, kbound)


PCH = 8


def _chunk_copies(cache_ref, li, pt_ref, b, c, buf, slot, sem):
    return [pltpu.make_async_copy(cache_ref.at[li, pt_ref[b, c * PCH + p]], buf.at[slot, p], sem.at[slot])
            for p in range(PCH)]


def _pagecmp_kernel(pt_ref, wk_ref, wv_ref, place_ref, ck_ref, cv_ref, kc_ref, vc_ref, kbuf, vbuf, sem, *, li, npg):
    b = pl.program_id(0)
    nch = npg // PCH

    def copies(c, slot):
        return _chunk_copies(ck_ref, li, pt_ref, b, c, kbuf, slot, sem) + _chunk_copies(cv_ref, li, pt_ref, b, c, vbuf, slot, sem)

    for cp in copies(0, 0):
        cp.start()

    def body(c, _):
        slot = c % 2

        @pl.when(c + 1 < nch)
        def _():
            for cp in copies(c + 1, 1 - slot):
                cp.start()

        for cp in copies(c, slot):
            cp.wait()
        ka = _dotf((kbuf[slot, 0] * wk_ref[...]).astype(bf16), place_ref[0])
        va = _dotf((vbuf[slot, 0] * wv_ref[...]).astype(bf16), place_ref[0])
        for p in range(1, PCH):
            ka = ka + _dotf((kbuf[slot, p] * wk_ref[...]).astype(bf16), place_ref[p])
            va = va + _dotf((vbuf[slot, p] * wv_ref[...]).astype(bf16), place_ref[p])
        kc_ref[c] = ka
        vc_ref[c] = va
        return 0

    lax.fori_loop(0, nch, body, 0)


def _page_compress(page_table, cache_k, cache_v, li, wk_t, wv_t):
    bs, npg = page_table.shape
    nch = npg // PCH
    place = (jnp.arange(LANES)[None, None, :]
             == 2 * jnp.arange(PCH)[:, None, None] + jnp.arange(PAGE)[None, :, None] // NSA_BLOCK).astype(bf16)
    out = pl.BlockSpec((None, nch, 256, LANES), lambda b, pt: (b, 0, 0, 0))
    shp = jax.ShapeDtypeStruct((bs, nch, 256, LANES), f32)
    anyspec = pl.BlockSpec(memory_space=pl.ANY)
    return pl.pallas_call(
        functools.partial(_pagecmp_kernel, li=li, npg=npg),
        grid_spec=pltpu.PrefetchScalarGridSpec(
            num_scalar_prefetch=1, grid=(bs,),
            in_specs=[pl.BlockSpec((256, PAGE), lambda b, pt: (0, 0)), pl.BlockSpec((256, PAGE), lambda b, pt: (0, 0)),
                      pl.BlockSpec((PCH, PAGE, LANES), lambda b, pt: (0, 0, 0)), anyspec, anyspec],
            out_specs=[out, out],
            scratch_shapes=[pltpu.VMEM((2, PCH, 256, PAGE), f32), pltpu.VMEM((2, PCH, 256, PAGE), f32),
                            pltpu.SemaphoreType.DMA((2,))]),
        out_shape=[shp, shp], compiler_params=_cp(("arbitrary",)), name="page_compress")(
            page_table, wk_t, wv_t, place, cache_k, cache_v)


def _ssel_kernel(qc_ref, kc_ref, vc_ref, kn_ref, vn_ref, w_ref, mrep_ref, oc_ref, neg_ref, *, past, ts):
    nbp = kc_ref.shape[1]
    nb_past = past // NSA_BLOCK
    col = _iota((1, LANES), 1)
    tpos = past + col % 8
    newrow = (_iota((nbp, 1), 0) == nb_past).astype(f32)
    tmask = (_iota((8, 1), 0) < ts).astype(f32)
    for g in range(NSA_KVH):
        sl = slice(g * HD, (g + 1) * HD)
        kc_new = jnp.sum(kn_ref[:, sl] * tmask * w_ref[0, 0:8, :], axis=0, keepdims=True)
        vc_new = jnp.sum(vn_ref[:, sl] * tmask * w_ref[1, 0:8, :], axis=0, keepdims=True)
        kc = (kc_ref[g] + newrow * kc_new).astype(bf16)
        vc = (vc_ref[g] + newrow * vc_new).astype(bf16)
        q = qc_ref[g]
        nrow = _iota((nbp, LANES), 0)
        ok = ((nrow + 1) * NSA_BLOCK - 1) <= tpos
        s = jnp.where(ok, _dot_nt(kc, q), NEG)
        m = jnp.max(s, axis=0, keepdims=True)
        e = jnp.where(ok, jnp.exp(s - m), 0.0)
        l = jnp.sum(e, axis=0, keepdims=True)
        p = e / jnp.where(l > 0.0, l, 1.0)
        oc_ref[g] = _dot_tn(p.astype(bf16), vc)
        imp = _dot3_r(p, mrep_ref[...])
        cur = tpos // NSA_BLOCK
        impm = jnp.where((nrow == cur) | (nrow == 0), BIG, jnp.where(nrow < cur, imp, NEG))

        def pick(_, v):
            mx = jnp.max(v, axis=0, keepdims=True)
            idx = jnp.min(jnp.where(v == mx, nrow, nbp), axis=0, keepdims=True)
            return jnp.where(nrow == idx, GONE, v)

        taken = lax.fori_loop(0, NSA_TOPN, pick, impm) == GONE
        neg_ref[g] = jnp.where(taken & (impm > 0.5 * NEG), 0.0, NEG)


def _sample_select(qc_st, kc_h, vc_h, kn, vn, cmp_w, mrep, past, ts):
    bs, _, nbp, _ = kc_h.shape
    b4 = lambda n, w: pl.BlockSpec((None, NSA_KVH, n, w), lambda b: (b, 0, 0, 0))
    return pl.pallas_call(
        functools.partial(_ssel_kernel, past=past, ts=ts), grid=(bs,),
        in_specs=[b4(LANES, HD), b4(nbp, HD), b4(nbp, HD), pl.BlockSpec((None, 8, 256), lambda b: (b, 0, 0)),
                  pl.BlockSpec((None, 8, 256), lambda b: (b, 0, 0)), _full(cmp_w), _full(mrep)],
        out_specs=[b4(LANES, HD), b4(nbp, LANES)],
        out_shape=[jax.ShapeDtypeStruct((bs, NSA_KVH, LANES, HD), f32), jax.ShapeDtypeStruct((bs, NSA_KVH, nbp, LANES), f32)],
        compiler_params=_cp(("parallel",)), name="sample_select")(qc_st, kc_h, vc_h, kn, vn, cmp_w, mrep)


def _new_rows_mask(ts):
    tt = _iota((LANES, LANES), 0) % 8
    lane = _iota((LANES, LANES), 1)
    return (lane <= tt) & (lane < ts)


def _diag_blocks(o_ref, acc, l):
    nr = LANES // NSA_KVH
    for g in range(NSA_KVH):
        o_ref[g] = acc[g * nr:(g + 1) * nr, g * HD:(g + 1) * HD] / l[g * nr:(g + 1) * nr, :]


def _spage_kernel(pt_ref, q_ref, slab_ref, e_ref, kn_ref, vn_ref, ck_ref, cv_ref, o_ref, kbuf, vbuf, sem, m_s, l_s, acc_s,
                  *, li, npg, ts):
    b = pl.program_id(0)
    nch = npg // PCH

    def copies(c, slot):
        return _chunk_copies(ck_ref, li, pt_ref, b, c, kbuf, slot, sem) + _chunk_copies(cv_ref, li, pt_ref, b, c, vbuf, slot, sem)

    for cp in copies(0, 0):
        cp.start()
    q = q_ref[...]
    s = jnp.where(_new_rows_mask(ts), _dotf(q, kn_ref[...].astype(bf16)), NEG)
    m = jnp.max(s, axis=1, keepdims=True)
    p = jnp.exp(s - m)
    m_s[...] = m
    l_s[...] = jnp.sum(p, axis=1, keepdims=True)
    acc_s[...] = _dot_nt(p.astype(bf16), vn_ref[...].astype(bf16))

    def body(c, _):
        slot = c % 2

        @pl.when(c + 1 < nch)
        def _():
            for cp in copies(c + 1, 1 - slot):
                cp.start()

        for cp in copies(c, slot):
            cp.wait()
        bias = _dotf(slab_ref[c], e_ref[...])
        s = jnp.concatenate([_dotf(q, kbuf[slot, p].astype(bf16)) for p in range(PCH)], axis=1) + bias
        m_old = m_s[...]
        m_new = jnp.maximum(m_old, jnp.max(s, axis=1, keepdims=True))
        p = jnp.exp(s - m_new)
        alpha = jnp.exp(m_old - m_new)
        l_s[...] = alpha * l_s[...] + jnp.sum(p, axis=1, keepdims=True)
        pb = p.astype(bf16)
        pv = _dot_nt(pb[:, 0:PAGE], vbuf[slot, 0].astype(bf16))
        for j in range(1, PCH):
            pv = pv + _dot_nt(pb[:, j * PAGE:(j + 1) * PAGE], vbuf[slot, j].astype(bf16))
        acc_s[...] = alpha * acc_s[...] + pv
        m_s[...] = m_new
        return 0

    lax.fori_loop(0, nch, body, 0)
    _diag_blocks(o_ref, acc_s[...], l_s[...])


def _sample_selected(page_table, qbd, slab, expand, kn_t, vn_t, cache_k, cache_v, li, ts):
    bs, npg = page_table.shape
    nch = npg // PCH
    per_b = lambda shp: pl.BlockSpec((None,) + shp, lambda b, pt: (b,) + (0,) * len(shp))
    anyspec = pl.BlockSpec(memory_space=pl.ANY)
    return pl.pallas_call(
        functools.partial(_spage_kernel, li=li, npg=npg, ts=ts),
        grid_spec=pltpu.PrefetchScalarGridSpec(
            num_scalar_prefetch=1, grid=(bs,),
            in_specs=[per_b((LANES, 256)), per_b((nch, LANES, LANES)),
                      pl.BlockSpec((LANES, PCH * PAGE), lambda b, pt: (0, 0)), per_b((256, LANES)), per_b((256, LANES)),
                      anyspec, anyspec],
            out_specs=per_b((NSA_KVH, LANES // NSA_KVH, HD)),
            scratch_shapes=[pltpu.VMEM((2, PCH, 256, PAGE), f32), pltpu.VMEM((2, PCH, 256, PAGE), f32),
                            pltpu.SemaphoreType.DMA((2,)), pltpu.VMEM((LANES, 1), f32), pltpu.VMEM((LANES, 1), f32),
                            pltpu.VMEM((LANES, 256), f32)]),
        out_shape=jax.ShapeDtypeStruct((bs, NSA_KVH, LANES // NSA_KVH, HD), f32),
        compiler_params=_cp(("arbitrary",)), name="sample_selected")(
            page_table, qbd, slab, expand, kn_t, vn_t, cache_k, cache_v)


def _swin_kernel(q_ref, wk_ref, wv_ref, kn_ref, vn_ref, o_ref, wko_ref, wvo_ref, *, ts, wb):
    q = q_ref[...]
    wk, wv, kn, vn = wk_ref[...], wv_ref[...], kn_ref[...], vn_ref[...]
    tt = _iota((LANES, wb), 0) % 8
    s1 = jnp.where(_iota((LANES, wb), 1) > tt + (wb - NSA_WINDOW), _dotf(q, wk.astype(bf16)), NEG)
    s2 = jnp.where(_new_rows_mask(ts), _dotf(q, kn.astype(bf16)), NEG)
    m = jnp.maximum(jnp.max(s1, axis=1, keepdims=True), jnp.max(s2, axis=1, keepdims=True))
    p1, p2 = jnp.exp(s1 - m), jnp.exp(s2 - m)
    l = jnp.sum(p1, axis=1, keepdims=True) + jnp.sum(p2, axis=1, keepdims=True)
    acc = _dot_nt(p1.astype(bf16), wv.astype(bf16)) + _dot_nt(p2.astype(bf16), vn.astype(bf16))
    _diag_blocks(o_ref, acc, l)
    tail = _iota((256, wb), 1) >= wb - ts
    pad = jnp.zeros((256, wb - LANES), f32)
    wko_ref[...] = jnp.where(tail, jnp.concatenate([pad, pltpu.roll(kn, LANES - ts, 1)], axis=1), pltpu.roll(wk, wb - ts, 1))
    wvo_ref[...] = jnp.where(tail, jnp.concatenate([pad, pltpu.roll(vn, LANES - ts, 1)], axis=1), pltpu.roll(wv, wb - ts, 1))


def _sample_window(qbd, win_k, win_v, li, kn_t, vn_t, ts):
    bs, wb = win_k.shape[1], win_k.shape[3]
    wspec = pl.BlockSpec((None, None, 256, wb), lambda b: (li, b, 0, 0))
    per_b = lambda shp: pl.BlockSpec((None,) + shp, lambda b: (b,) + (0,) * len(shp))
    return pl.pallas_call(
        functools.partial(_swin_kernel, ts=ts, wb=wb), grid=(bs,),
        in_specs=[per_b((LANES, 256)), wspec, wspec, per_b((256, LANES)), per_b((256, LANES))],
        out_specs=[per_b((NSA_KVH, LANES // NSA_KVH, HD)), per_b((256, wb)), per_b((256, wb))],
        out_shape=[jax.ShapeDtypeStruct((bs, NSA_KVH, LANES // NSA_KVH, HD), f32), jax.ShapeDtypeStruct((bs, 256, wb), f32),
                   jax.ShapeDtypeStruct((bs, 256, wb), f32)],
        compiler_params=_cp(("parallel",)), name="sample_window")(qbd, win_k, win_v, kn_t, vn_t)


def _gelu_tanh(x):
    return 0.5 * x * (1.0 + jnp.tanh(math.sqrt(2.0 / math.pi) * (x + 0.044715 * (x * x * x))))


def _s5_kernel(u_ref, h0r_ref, h0i_ref, perm_ref, permt_ref, bd_ref, pwr_ref, pwi_ref, cdr_ref, cdi_ref, d_ref, gw_ref,
               gb_ref, y_ref, sr_ref, si_ref, xr_s, xi_s, cr_s, ci_s, *, tc, srow):
    c = pl.program_id(1)

    @pl.when(c == 0)
    def _():
        cr_s[...] = h0r_ref[...]
        ci_s[...] = h0i_ref[...]

    ns = tc // 8
    u = u_ref[...]
    ub = _dotf(perm_ref[...], u.astype(bf16)).astype(bf16)
    hc, hn = S5_CH // 2, S5_N // 2
    for h in range(2):
        uh = ub[:, h * hc:(h + 1) * hc]
        xr_s[:, h * hn:(h + 1) * hn] = _dotf(uh, bd_ref[h * hc:(h + 1) * hc, h * hn:(h + 1) * hn])
        xi_s[:, h * hn:(h + 1) * hn] = _dotf(uh, bd_ref[h * hc:(h + 1) * hc, S5_N + h * hn:S5_N + (h + 1) * hn])
    ar, ai = pwr_ref[0:1, :], pwi_ref[0:1, :]

    def scan(t, carry):
        xr, xi = carry
        rs = pl.multiple_of(t * 8, 8)
        nr = ar * xr - ai * xi + xr_s[pl.ds(rs, 8), :]
        ni = ar * xi + ai * xr + xi_s[pl.ds(rs, 8), :]
        xr_s[pl.ds(rs, 8), :] = nr
        xi_s[pl.ds(rs, 8), :] = ni
        return nr, ni

    zero = jnp.zeros((8, S5_N), f32)
    fr, fi = lax.fori_loop(0, ns, scan, (zero, zero))
    asr, asi = pwr_ref[ns - 1:ns, :], pwi_ref[ns - 1:ns, :]
    cr, ci = cr_s[...], ci_s[...]
    er, ei = [], []
    for s in range(8):
        er.append(cr)
        ei.append(ci)
        cr, ci = asr * cr - asi * ci + fr[s:s + 1, :], asr * ci + asi * cr + fi[s:s + 1, :]
    cr_s[...] = cr
    ci_s[...] = ci
    er, ei = jnp.concatenate(er, axis=0), jnp.concatenate(ei, axis=0)

    def fix(t, _):
        rs = pl.multiple_of(t * 8, 8)
        pr, pi = pwr_ref[pl.ds(t, 1), :], pwi_ref[pl.ds(t, 1), :]
        xr_s[pl.ds(rs, 8), :] += pr * er - pi * ei
        xi_s[pl.ds(rs, 8), :] += pr * ei + pi * er
        return 0

    lax.fori_loop(0, ns, fix, 0)
    sr_ref[...] = xr_s[srow:srow + 8, :]
    si_ref[...] = xi_s[srow:srow + 8, :]
    yh = []
    for h in range(2):
        rows, cols = slice(h * hn, (h + 1) * hn), slice(h * hc, (h + 1) * hc)
        yh.append(_dotf(xr_s[:, rows].astype(bf16), cdr_ref[rows, cols]) + _dotf(xi_s[:, rows].astype(bf16), cdi_ref[rows, cols]))
    y = _dot3_l(permt_ref[...], jnp.concatenate(yh, axis=1)) + d_ref[...] * u
    y = _gelu_tanh(y)
    y_ref[...] = y * _sigmoid(_dotf(y.astype(bf16), gw_ref[...]) + gb_ref[...])


def _s5(u, h0r, h0i, prm, t_valid):
    b, t, _ = u.shape
    tc = min(256, t)
    ns = tc // 8
    assert (t_valid - 1) // tc == t // tc - 1
    r = (t_valid - 1) % tc
    prow = (r % ns) * 8 + r // ns
    srow = prow // 8 * 8
    rid = jnp.arange(tc)
    perm = (rid[None, :] == (rid[:, None] % 8) * ns + rid[:, None] // 8).astype(bf16)
    st = pl.BlockSpec((None, 8, S5_N), lambda b, c: (b, 0, 0))
    h0 = pl.BlockSpec((None, 1, S5_N), lambda b, c: (b, 0, 0))
    y, sr, si = pl.pallas_call(
        functools.partial(_s5_kernel, tc=tc, srow=srow), grid=(b, t // tc),
        in_specs=[pl.BlockSpec((None, tc, S5_CH), lambda b, c: (b, c, 0)), h0, h0, _full(perm), _full(perm)]
        + [_full(p) for p in prm],
        out_specs=[pl.BlockSpec((None, tc, S5_CH), lambda b, c: (b, c, 0)), st, st],
        out_shape=[jax.ShapeDtypeStruct((b, t, S5_CH), f32), jax.ShapeDtypeStruct((b, 8, S5_N), f32),
                   jax.ShapeDtypeStruct((b, 8, S5_N), f32)],
        scratch_shapes=[pltpu.VMEM((tc, S5_N), f32), pltpu.VMEM((tc, S5_N), f32), pltpu.VMEM((1, S5_N), f32),
                        pltpu.VMEM((1, S5_N), f32)],
        compiler_params=_cp(("parallel", "arbitrary")), name="s5")(u, h0r, h0i, perm, perm.T, *prm)
    return y, sr[:, prow % 8], si[:, prow % 8]


def _gla_kernel(q_ref, k_ref, v_ref, g_ref, sm_ref, s0_ref, wa_ref, ba_ref, gn_ref, o_ref, sn_ref, st, *, nc, t_valid):
    L = CHUNK
    c = pl.program_id(1)

    @pl.when(c == 0)
    def _():
        st[...] = s0_ref[...]

    x = _dotf(sm_ref[...].astype(bf16), wa_ref[...]) + ba_ref[...]
    la = (jnp.minimum(x, 0.0) - jnp.log1p(jnp.exp(-jnp.abs(x)))) * (1.0 / GLA_TEMP)
    k = k_ref[...]
    if t_valid < nc * L:
        valid = c * L + _iota((L, 1), 0) < t_valid
        la = jnp.where(valid, la, 0.0)
        k = jnp.where(valid, k, 0.0)
    causal = _tri(L)
    b = _dot3_l(causal.astype(bf16), la)
    bl = b[L - 1:L, :]
    qe = (q_ref[...] * (GLA_DK ** -0.5) * jnp.exp(b)).astype(bf16)
    ke = (k * jnp.exp(-b)).astype(bf16)
    kl = (k * jnp.exp(bl - b)).astype(bf16)
    ebl = jnp.exp(bl)
    v, gg, gn = v_ref[...], g_ref[...], gn_ref[...]
    for h in range(GLA_HEADS):
        sl = slice(h * GLA_DK, (h + 1) * GLA_DK)
        vl = slice(h * GLA_DV, (h + 1) * GLA_DV)
        att = jnp.where(causal, _dot_nt(qe[:, sl], ke[:, sl]), 0.0)
        vh = v[:, vl].astype(bf16)
        s_t = st[h]
        o = _dotf(att.astype(bf16), vh) + _dot_nt(qe[:, sl], s_t.astype(bf16))
        st[h] = s_t * ebl[:, sl] + _dot_tn(vh, kl[:, sl])
        ms = jnp.mean(o * o, axis=-1, keepdims=True)
        gh = gg[:, vl]
        o_ref[:, vl] = o * lax.rsqrt(ms + EPS) * gn * (gh * _sigmoid(gh))

    @pl.when(c == nc - 1)
    def _():
        sn_ref[...] = st[...]


def _gla(q, k, v, g, sm, s0t, prm, t_valid):
    b, t, _ = q.shape
    nc = t // CHUNK
    tok = lambda w: pl.BlockSpec((None, CHUNK, w), lambda b, c: (b, c, 0))
    st = pl.BlockSpec((None, GLA_HEADS, GLA_DV, GLA_DK), lambda b, c: (b, 0, 0, 0))
    return pl.pallas_call(
        functools.partial(_gla_kernel, nc=nc, t_valid=t_valid), grid=(b, nc),
        in_specs=[tok(256), tok(256), tok(512), tok(512), tok(LANES), st] + [_full(p) for p in prm],
        out_specs=[tok(512), st],
        out_shape=[jax.ShapeDtypeStruct((b, t, 512), f32), jax.ShapeDtypeStruct((b, GLA_HEADS, GLA_DV, GLA_DK), f32)],
        scratch_shapes=[pltpu.VMEM((GLA_HEADS, GLA_DV, GLA_DK), f32)],
        compiler_params=_cp(("parallel", "arbitrary")), name="gla")(q, k, v, g, sm, s0t, *prm)


def _router_kernel(x_ref, sh_ref, sc_ref, g_ref, w_ref, b_ref, h_ref, route_ref, cnt_ref):
    first = (pl.program_id(0) == 0) & (pl.program_id(1) == 0)
    h = _modulate(x_ref[...], sh_ref[...], sc_ref[...], g_ref[...])
    hb = lax.bitcast_convert_type(h.astype(bf16).astype(f32), jnp.uint32)
    h_ref[...] = (hb[:, :D // 2] >> 16) | hb[:, D // 2:]
    hh, hl = _split2(h)
    wh, wl = _split2(w_ref[...])
    lg = _dotf(hh, wh) + _dotf(hl, wh) + _dotf(hh, wl) + b_ref[...]
    lane = _iota(lg.shape, 1)
    big = 4 * LANES
    coarse = lane < MOE_GROUPS
    lc = jnp.where(coarse, lg, GONE)
    mx = jnp.max(lc, axis=1, keepdims=True)
    gsel = jnp.min(jnp.where(lc == mx, lane, big), axis=1, keepdims=True)
    gc = 1.0 / jnp.sum(jnp.where(coarse, jnp.exp(lg - mx), 0.0), axis=1, keepdims=True)
    base = MOE_GROUPS + gsel * MOE_PER_GROUP
    fine = (lane >= base) & (lane < base + MOE_PER_GROUP)
    mf = jnp.max(jnp.where(fine, lg, GONE), axis=1, keepdims=True)
    ef = jnp.where(fine, jnp.exp(lg - mf), 0.0)
    pf = ef / jnp.sum(ef, axis=1, keepdims=True)
    cand = jnp.where(fine, pf, -1.0)
    v1 = jnp.max(cand, axis=1, keepdims=True)
    i1 = jnp.min(jnp.where(cand == v1, lane, big), axis=1, keepdims=True)
    cand = jnp.where(lane == i1, -1.0, cand)
    v2 = jnp.max(cand, axis=1, keepdims=True)
    i2 = jnp.min(jnp.where(cand == v2, lane, big), axis=1, keepdims=True)
    e1, e2 = i1 - MOE_GROUPS, i2 - MOE_GROUPS
    w1, w2 = gc * v1 / (v1 + v2), gc * v2 / (v1 + v2)
    route_ref[...] = jnp.where(lane == 0, e1.astype(f32), jnp.where(lane == 1, e2.astype(f32),
                               jnp.where(lane == 2, w1, jnp.where(lane == 3, w2, 0.0))))
    cnt = jnp.sum((lane == e1).astype(f32) + (lane == e2).astype(f32), axis=0, keepdims=True)

    @pl.when(first)
    def _():
        cnt_ref[...] = jnp.zeros_like(cnt_ref)

    cnt_ref[...] += jnp.broadcast_to(cnt, cnt_ref.shape)


def _router(x, sh, sc, g, wr, br):
    b, t, _ = x.shape
    tm = min(256, t)
    return pl.pallas_call(
        _router_kernel, grid=(b, t // tm),
        in_specs=[pl.BlockSpec((None, tm, D), lambda b, i: (b, i, 0)), _mod_spec(sh, tm), _mod_spec(sc, tm),
                  pl.BlockSpec((1, D), lambda b, i: (0, 0)), _full(wr), _full(br)],
        out_specs=[pl.BlockSpec((None, tm, D // 2), lambda b, i: (b, i, 0)), pl.BlockSpec((None, tm, LANES), lambda b, i: (b, i, 0)),
                   pl.BlockSpec((8, LANES), lambda b, i: (0, 0))],
        out_shape=[jax.ShapeDtypeStruct((b, t, D // 2), jnp.uint32), jax.ShapeDtypeStruct((b, t, LANES), f32),
                   jax.ShapeDtypeStruct((8, LANES), f32)],
        compiler_params=_cp(("arbitrary", "arbitrary")), name="moe_router")(x, sh, sc, g, wr, br)


def _plan_kernel(route_ref, ps_ref, dest_ref, run_s, *, tm):
    @pl.when(pl.program_id(0) == 0)
    def _():
        run_s[...] = jnp.zeros_like(run_s)

    route = route_ref[...]
    lane = _iota((tm, LANES), 1).astype(f32)
    oh1 = (lane == route[:, 0:1]).astype(f32)
    oh2 = (lane == route[:, 1:2]).astype(f32)
    tot = oh1 + oh2
    strict = (_iota((tm, tm), 0) > _iota((tm, tm), 1)).astype(bf16)
    before = _dotf(strict, tot.astype(bf16)) + run_s[0:1, :] + ps_ref[0:1, :]
    d1 = jnp.sum(oh1 * before, axis=1, keepdims=True)
    d2 = jnp.sum(oh2 * before, axis=1, keepdims=True)
    lane_i = _iota((tm, LANES), 1)
    dest_ref[...] = jnp.where(lane_i == 0, d1, jnp.where(lane_i == 1, d2, 0.0)).astype(i32)
    run_s[...] += jnp.broadcast_to(jnp.sum(tot, axis=0, keepdims=True), run_s.shape)


def _plan(route, pstart):
    n = route.shape[0]
    tm = min(256, n)
    return pl.pallas_call(
        functools.partial(_plan_kernel, tm=tm), grid=(n // tm,),
        in_specs=[pl.BlockSpec((tm, LANES), lambda i: (i, 0)), pl.BlockSpec((8, LANES), lambda i: (0, 0))],
        out_specs=pl.BlockSpec((tm, LANES), lambda i: (i, 0)),
        out_shape=jax.ShapeDtypeStruct((n, LANES), i32),
        scratch_shapes=[pltpu.VMEM((8, LANES), f32)],
        compiler_params=_cp(("arbitrary",)), name="moe_plan")(route, pstart)


def _dispatch_kernel(dest_ref, h_ref, xs_in, xs_out, sem, *, tm):
    del xs_in

    def row_copy(r, d):
        return pltpu.make_async_copy(h_ref.at[pl.ds(r, 1), :], xs_out.at[pl.ds(d, 1), :], sem)

    def issue(r, _):
        row_copy(r, dest_ref[2 * r]).start()
        row_copy(r, dest_ref[2 * r + 1]).start()
        return 0

    lax.fori_loop(0, tm, issue, 0, unroll=8)
    for _ in range(2):
        pltpu.make_async_copy(h_ref, xs_out.at[pl.ds(0, tm), :], sem).wait()


def _dispatch(h, dest_flat, nrows):
    n, w = h.shape
    tm = min(256, n)
    return pl.pallas_call(
        functools.partial(_dispatch_kernel, tm=tm), grid=(n // tm,),
        in_specs=[pl.BlockSpec((2 * tm,), lambda i: (i,), memory_space=pltpu.SMEM),
                  pl.BlockSpec((tm, w), lambda i: (i, 0)), pl.BlockSpec(memory_space=pl.ANY)],
        out_specs=pl.BlockSpec(memory_space=pl.ANY),
        out_shape=jax.ShapeDtypeStruct((nrows, w), h.dtype),
        scratch_shapes=[pltpu.SemaphoreType.DMA(())],
        input_output_aliases={2: 0},
        compiler_params=_cp(("arbitrary",)), name="moe_dispatch")(dest_flat, h, jnp.zeros((nrows, w), h.dtype))


def _ffn_kernel(be_ref, nu_ref, x_ref, w1_ref, w3_ref, w2_ref, o_ref):
    i = pl.program_id(0)

    @pl.when(i < nu_ref[0])
    def _():
        xp = x_ref[...]
        x_lo = lax.bitcast_convert_type(xp << 16, f32).astype(bf16)
        x_hi = lax.bitcast_convert_type(xp & jnp.uint32(0xFFFF0000), f32).astype(bf16)
        half = D // 2
        a = _dotf(x_lo, w1_ref[0:half, :].astype(bf16)) + _dotf(x_hi, w1_ref[half:, :].astype(bf16))
        b = _dotf(x_lo, w3_ref[0:half, :].astype(bf16)) + _dotf(x_hi, w3_ref[half:, :].astype(bf16))
        hid = ((a * _sigmoid(a)) * b).astype(bf16)
        o_ref[...] = _dotf(hid, w2_ref[...].astype(bf16))

    @pl.when(i >= nu_ref[0])
    def _():
        o_ref[...] = jnp.zeros_like(o_ref)


def _ffn(xs, blk_e, nused, w1, w3, w2, l):
    nblk = xs.shape[0] // MOE_ROWS
    return pl.pallas_call(
        _ffn_kernel,
        grid_spec=pltpu.PrefetchScalarGridSpec(
            num_scalar_prefetch=2, grid=(nblk,),
            in_specs=[pl.BlockSpec((MOE_ROWS, D // 2), lambda i, be, nu: (i, 0)),
                      pl.BlockSpec((None, None, D, MOE_FF), lambda i, be, nu: (l, be[i], 0, 0)),
                      pl.BlockSpec((None, None, D, MOE_FF), lambda i, be, nu: (l, be[i], 0, 0)),
                      pl.BlockSpec((None, None, MOE_FF, D), lambda i, be, nu: (l, be[i], 0, 0))],
            out_specs=pl.BlockSpec((MOE_ROWS, D), lambda i, be, nu: (i, 0))),
        out_shape=jax.ShapeDtypeStruct((xs.shape[0], D), f32),
        compiler_params=_cp(("arbitrary",)), name="moe_ffn")(blk_e, nused, xs, w1, w3, w2)


def _combine_kernel(dest_ref, x_ref, gate_ref, route_ref, ys_ref, o_ref, buf_a, buf_b, sem, *, tm):
    def row_copy(d, buf, r):
        return pltpu.make_async_copy(ys_ref.at[pl.ds(d, 1), :], buf.at[pl.ds(r, 1), :], sem)

    def issue(r, _):
        row_copy(dest_ref[2 * r], buf_a, r).start()
        row_copy(dest_ref[2 * r + 1], buf_b, r).start()
        return 0

    lax.fori_loop(0, tm, issue, 0, unroll=8)
    for buf in (buf_a, buf_b):
        pltpu.make_async_copy(ys_ref.at[pl.ds(0, tm), :], buf, sem).wait()
    route = route_ref[...]
    o_ref[...] = x_ref[...] + gate_ref[...] * (route[:, 2:3] * buf_a[...] + route[:, 3:4] * buf_b[...])


def _combine(x, gate, route, ys, dest_flat):
    b, t, _ = x.shape
    tm = min(256, t)
    nt = t // tm
    return pl.pallas_call(
        functools.partial(_combine_kernel, tm=tm), grid=(b, nt),
        in_specs=[pl.BlockSpec((2 * tm,), lambda b, i: (b * nt + i,), memory_space=pltpu.SMEM),
                  pl.BlockSpec((None, tm, D), lambda b, i: (b, i, 0)), _mod_spec(gate, tm),
                  pl.BlockSpec((None, tm, LANES), lambda b, i: (b, i, 0)), pl.BlockSpec(memory_space=pl.ANY)],
        out_specs=pl.BlockSpec((None, tm, D), lambda b, i: (b, i, 0)),
        out_shape=jax.ShapeDtypeStruct((b, t, D), f32),
        scratch_shapes=[pltpu.VMEM((tm, D), f32), pltpu.VMEM((tm, D), f32), pltpu.SemaphoreType.DMA(())],
        compiler_params=_cp(("arbitrary", "arbitrary")), name="moe_combine")(dest_flat, x, gate, route, ys)


def _moe(x, sh, sc, gate, g, wr, br, w1, w3, w2, l):
    b, t, _ = x.shape
    n = b * t
    h, route, cnt = _router(x, sh, sc, g, wr, br)
    counts = cnt[0, :MOE_EXPERTS].astype(i32)
    pcounts = (counts + MOE_ROWS - 1) // MOE_ROWS * MOE_ROWS
    pends = jnp.cumsum(pcounts)
    pstarts = pends - pcounts
    nblk = -(-2 * n // MOE_ROWS) + MOE_EXPERTS
    blk_start = jnp.arange(nblk, dtype=i32) * MOE_ROWS
    blk_e = jnp.minimum(jnp.sum((pends[None, :] <= blk_start[:, None]).astype(i32), axis=1), MOE_EXPERTS - 1)
    nused = (pends[-1:] // MOE_ROWS).astype(i32)
    ps = jnp.zeros((8, LANES), f32).at[:, :MOE_EXPERTS].set(pstarts.astype(f32)[None])
    dest = _plan(route.reshape(n, LANES), ps)
    dest_flat = dest[:, :2].reshape(2 * n)
    xs = _dispatch(h.reshape(n, D // 2), dest_flat, nblk * MOE_ROWS)
    ys = _ffn(xs, blk_e, nused, w1, w3, w2, l)
    return _combine(x, gate, route, ys, dest_flat)


def _rope_tables(pos):
    half = HD // 2
    inv = ROPE_THETA ** (-jnp.arange(half, dtype=f32) / half)
    ang = pos.astype(f32)[:, None] * inv[None, :]
    cos, sin = jnp.cos(ang), jnp.sin(ang)
    return jnp.tile(jnp.concatenate([cos, cos], axis=1), (1, 2)), jnp.tile(jnp.concatenate([-sin, sin], axis=1), (1, 2))


def _seg_mats(width):
    nseg = width // HD
    seg = (jnp.arange(width)[:, None] // HD == jnp.arange(LANES)[None, :]).astype(bf16)
    return seg, seg.T


def _even_params(i, ev_w_in, ev_w_out, ev_conv_w, ev_conv_b, ev_dt_bias, ev_a_log, ev_d_skip, ev_ssd_norm, ev_q_norm,
                 ev_k_norm, ev_cmp_w):
    w = ev_w_in[i]
    o = [0, 1024, 2560, 2576, 3600, 5136, 5184]
    small = jnp.concatenate([w[:, o[2]:o[3]], w[:, o[5]:o[6]], jnp.zeros((D, 64), f32)], axis=1)
    ws = [w[:, o[0]:o[1]], w[:, o[1]:o[2]], w[:, o[3]:o[4]], w[:, o[4]:o[5]], small]
    pad = lambda v: jnp.zeros((1, LANES), f32).at[0, :v.shape[0]].set(v)
    ssd = [ev_conv_w[i], ev_conv_b[i][None], pad(ev_dt_bias[i]), pad(-jnp.exp(ev_a_log[i])),
           jnp.repeat(ev_d_skip[i], SSD_HD)[None], ev_ssd_norm[i][None]]
    segq, expq = _seg_mats(D)
    segk, expk = _seg_mats(1536)
    kn = ev_k_norm[i]
    z4 = jnp.zeros((256,), f32)
    gk = jnp.concatenate([jnp.tile(kn[0], 4), z4, jnp.tile(kn[1], 4), z4, jnp.tile(kn[2], 4), z4])[None]
    vm = jnp.concatenate([z4, z4 + 1, z4, z4 + 1, z4, z4 + 1])[None]
    cw = jnp.tile(ev_cmp_w[i], (1, 1, 4))
    prep = [segq, expq, segk, expk, jnp.tile(ev_q_norm[i], 16)[None], gk, vm, cw]
    lane = jnp.arange(LANES)[None, None, :, None]
    col = jnp.arange(256)[None, None, None, :]
    gidx = jnp.arange(NSA_KVH)[:, None, None, None]
    jidx = jnp.arange(3)[None, :, None, None]
    gexp = (lane == 16 + ((gidx * NSA_REP + col // HD) * 3 + jidx)).astype(bf16)
    wo = ev_w_out[i]
    return dict(ws=[a.astype(bf16) for a in ws], ssd=ssd, prep=prep, gexp=gexp, cmp_w=ev_cmp_w[i],
                wo=[wo[:SSD_INNER].astype(bf16), wo[SSD_INNER:].astype(bf16)])


def _odd_params(i, od_w_in, od_w_out, a_re, a_im, log_dt, b_re, b_im, c_re, c_im, d, glu_w, glu_b, wa2, ba, gnorm):
    w = od_w_in[i]
    o = [0, 512, 768, 1024, 1536, 2048, 2064]
    small = jnp.concatenate([w[:, o[5]:o[6]], jnp.zeros((D, LANES - GLA_RANK), f32)], axis=1)
    ws = [w[:, o[k]:o[k + 1]] for k in range(5)] + [small]
    are, aim = a_re[i], a_im[i]
    dt = jnp.exp(log_dt[i])[:, None]
    lr, li = are * dt, aim * dt
    ab_re, ab_im = jnp.exp(lr) * jnp.cos(li), jnp.exp(lr) * jnp.sin(li)
    den = are * are + aim * aim
    nr = ab_re - 1.0
    f_re = (nr * are + ab_im * aim) / den
    f_im = (ab_im * are - nr * aim) / den
    bb_re = f_re[..., None] * b_re[i] - f_im[..., None] * b_im[i]
    bb_im = f_re[..., None] * b_im[i] + f_im[..., None] * b_re[i]
    eye = jnp.eye(S5_GROUPS, dtype=f32)
    bdiag = lambda m: jnp.einsum('gpc,gh->gchp', m, eye).reshape(S5_CH, S5_N)
    cdiag = lambda m: jnp.einsum('gcp,gh->gphc', m, eye).reshape(S5_N, S5_CH)
    bd = jnp.concatenate([bdiag(bb_re), bdiag(bb_im)], axis=1).astype(bf16)
    npow = 32
    kk = jnp.arange(1, npow + 1, dtype=f32)[:, None, None]
    mag, ang = jnp.exp(lr[None] * kk), li[None] * kk
    pwr, pwi = (mag * jnp.cos(ang)).reshape(npow, S5_N), (mag * jnp.sin(ang)).reshape(npow, S5_N)
    s5 = [bd, pwr, pwi, cdiag(c_re[i]).astype(bf16), cdiag(-c_im[i]).astype(bf16), d[i][None],
          glu_w[i].astype(bf16), glu_b[i][None]]
    wa = jnp.zeros((LANES, 256), f32).at[:GLA_RANK].set(wa2[i]).astype(bf16)
    gla = [wa, ba[i][None], gnorm[i][None]]
    wo = od_w_out[i]
    return dict(ws=[a.astype(bf16) for a in ws], s5=s5, gla=gla, wo=[wo[:S5_CH].astype(bf16), wo[S5_CH:].astype(bf16)])


def _pad_t(a, t):
    return jnp.pad(a, ((0, 0), (0, t - a.shape[1])) + ((0, 0),) * (a.ndim - 2))


def _even_prompt(x, sh, sc, gate, g, prm):
    b, t, _ = x.shape
    z, xbc, q, kv, sm = _mod_proj(x, sh, sc, g, prm['ws'], "even_in")
    ya, convn, ssdn = _ssd(xbc, sm, z, jnp.zeros((b, 8, SSD_CONV_DIM), f32),
                           jnp.zeros((b, SSD_HEADS, SSD_HD, SSD_STATE), f32), prm['ssd'], t)
    cos_t, sin_t = _rope_tables(jnp.arange(t))
    (qc, qr, kcmp, vcmp, ksel, vsel, kwin, vwin, gates, kaug, vsa, kwh, vwa, kch, vch, kn2) = _nsa_prep(
        q, kv, sm, cos_t, sin_t, prm['prep'], True)
    nb = t // NSA_BLOCK
    padb = lambda a: jnp.pad(a, ((0, 0), (0, 0), (0, LANES - nb), (0, 0)))
    knorm = jnp.sqrt(jnp.max(kn2[:, :, 0, :2 * NSA_KVH], axis=1)).reshape(b, 2, NSA_KVH)
    kbound = jnp.zeros((b, NSA_KVH, 8, LANES), f32).at[:, :, :, 0:2].set(jnp.swapaxes(knorm, 1, 2)[:, :, None, :])
    ob = _nsa_prompt(qc, qr, padb(kch), padb(vch), kaug, vsa, kwh, vwa, gates, prm['gexp'], kbound)
    xn = _out_proj(x, gate, [ya, ob], prm['wo'], "even_out")
    r5 = lambda a: a.reshape(b, t, NSA_KVH, HD)
    keep = min(NSA_WINDOW, t)
    st = (r5(kcmp), r5(vcmp), r5(ksel), r5(vsel), r5(kwin)[:, t - keep:], r5(vwin)[:, t - keep:], ssdn, convn[:, 5:8])
    return xn, st


def _stack_q(qb, bs, ts):
    q = qb.reshape(bs, ts, NSA_KVH, NSA_REP, HD)
    q = jnp.pad(q, ((0, 0), (0, 8 - ts), (0, 0), (0, 0), (0, 0)))
    q = q.transpose(0, 2, 3, 1, 4).reshape(bs, NSA_KVH, NSA_REP * 8, HD)
    return jnp.pad(q, ((0, 0), (0, 0), (0, LANES - NSA_REP * 8), (0, 0)))


def _unstack_o(o, bs, ts):
    o = o[:, :, :NSA_REP * 8].reshape(bs, NSA_KVH, NSA_REP, 8, HD)[:, :, :, :ts]
    return o.transpose(0, 3, 1, 2, 4).reshape(1, bs * ts, D)


def _page_major(cache):
    l, p, r, h, d = cache.shape
    return jnp.transpose(cache, (0, 1, 3, 4, 2)).reshape(l, p, h * d, r)


def _even_sample(x, sh, sc, gate, g, prm, bs, ts, li, conv_state, ssd_state, page_table, cmp_k, cmp_v, sel_k, sel_v,
                 win_k, win_v):
    n = bs * ts
    npg = page_table.shape[1]
    past = npg * PAGE
    z, xbc, q, kv, sm = _mod_proj(x, sh, sc, g, prm['ws'], "even_in_s")
    seq = lambda a: _pad_t(a.reshape(bs, ts, a.shape[-1]), CHUNK)
    conv0 = jnp.pad(conv_state, ((0, 0), (5, 0), (0, 0)))
    ya, convn, ssdn = _ssd(seq(xbc), seq(sm), seq(z), conv0, ssd_state, prm['ssd'], ts)
    ya = ya[:, :ts].reshape(1, n, SSD_INNER)
    cos_t, sin_t = _rope_tables(past + jnp.arange(n) % ts)
    qc, qr, kcmp, vcmp, ksel, vsel, kwin, vwin, gates = _nsa_prep(q, kv, sm, cos_t, sin_t, prm['prep'], False)
    r8 = lambda a: _pad_t(a.reshape(bs, ts, 256), 8)
    qc_st = _stack_q(qc, bs, ts)
    nr = LANES // NSA_KVH
    qbd = jnp.einsum('bgid,gh->bgihd', _stack_q(qr, bs, ts)[:, :, :nr], jnp.eye(NSA_KVH, dtype=bf16)).reshape(bs, LANES, 256)
    rows_t = lambda a: jnp.pad(jnp.swapaxes(r8(a), 1, 2), ((0, 0), (0, 0), (0, LANES - 8)))
    cw = prm['cmp_w']
    w_t = lambda w: jnp.tile(w.T, (NSA_KVH, PAGE // NSA_BLOCK))
    kct, vct = _page_compress(page_table, _page_major(cmp_k), _page_major(cmp_v), li, w_t(cw[0]), w_t(cw[1]))
    nb_past = 2 * npg
    nbp = -(-(nb_past + 1) // 8) * 8
    hmaj = lambda a: jnp.pad(
        a[..., :2 * PCH].reshape(bs, -1, NSA_KVH, HD, 2 * PCH).transpose(0, 2, 1, 4, 3).reshape(bs, NSA_KVH, nb_past, HD),
        ((0, 0), (0, 0), (0, nbp - nb_past), (0, 0)))
    cidx = jnp.arange(LANES)
    mrep = ((cidx[:, None] < NSA_REP * 8) & (cidx[None, :] < NSA_REP * 8)
            & (cidx[:, None] % 8 == cidx[None, :] % 8)).astype(bf16)
    oc, neg = _sample_select(qc_st, hmaj(kct), hmaj(vct), r8(kcmp), r8(vcmp), cw, mrep, past, ts)
    nch, bpc = npg // PCH, 2 * PCH
    slab = neg[:, :, :nb_past, :nr].reshape(bs, NSA_KVH, nch, bpc, nr).transpose(0, 2, 1, 4, 3).reshape(bs, nch, LANES, bpc)
    slab = jnp.pad(slab, ((0, 0), (0, 0), (0, 0), (0, LANES - bpc))).astype(bf16)
    expand = (jnp.arange(LANES)[:, None] == jnp.arange(PCH * PAGE)[None, :] // NSA_BLOCK).astype(bf16)
    o_s = _sample_selected(page_table, qbd, slab, expand, rows_t(ksel), rows_t(vsel), _page_major(sel_k),
                           _page_major(sel_v), li, ts)
    o_w, wkt, wvt = _sample_window(qbd, _page_major(win_k), _page_major(win_v), li, rows_t(kwin), rows_t(vwin), ts)
    unmajor = lambda a: a.reshape(bs, NSA_KVH, HD, -1).transpose(0, 3, 1, 2)
    wkn, wvn = unmajor(wkt), unmajor(wvt)
    gt = gates[0, :, 16:16 + 3 * NSA_HEADS].reshape(1, n, NSA_HEADS, 3)
    gx = lambda j: jnp.repeat(gt[..., j], HD, axis=-1)
    ob = gx(0) * _unstack_o(oc, bs, ts) + gx(1) * _unstack_o(o_s, bs, ts) + gx(2) * _unstack_o(o_w, bs, ts)
    xn = _out_proj(x, gate, [ya, ob], prm['wo'], "even_out_s")
    r5 = lambda a: a.reshape(bs, ts, NSA_KVH, HD)
    st = (r5(kcmp), r5(vcmp), r5(ksel), r5(vsel), wkn, wvn, ssdn, convn[:, 5:8])
    return xn, st


def _odd_layer(x, sh, sc, gate, g, prm, bs, ts, s5r0, s5i0, gla0):
    u, q, k, v, gg, sm = _mod_proj(x, sh, sc, g, prm['ws'], "odd_in")
    tp = -(-ts // CHUNK) * CHUNK
    seq = lambda a: _pad_t(a.reshape(bs, ts, a.shape[-1]), tp)
    yc, sr, si = _s5(seq(u), s5r0.reshape(bs, 1, S5_N), s5i0.reshape(bs, 1, S5_N), prm['s5'], ts)
    og, gn = _gla(seq(q), seq(k), seq(v), seq(gg), seq(sm), jnp.swapaxes(gla0, 2, 3), prm['gla'], ts)
    unseq = lambda a: a[:, :ts].reshape(x.shape[0], x.shape[1], a.shape[-1])
    xn = _out_proj(x, gate, [unseq(yc), unseq(og)], prm['wo'], "odd_out")
    st = (sr.reshape(bs, S5_GROUPS, S5_STATE), si.reshape(bs, S5_GROUPS, S5_STATE), jnp.swapaxes(gn, 2, 3))
    return xn, st


def kernel(x_prompt, x_sample, cache_cmp_k, cache_cmp_v, cache_sel_k, cache_sel_v, cache_win_k, cache_win_v, state_ssd, state_conv, state_s5_re, state_s5_im, state_gla, page_table, c_prompt, c_sample, ada_w, ada_b, norm_mix, norm_ffn, ev_w_in, ev_w_out, ev_conv_w, ev_conv_b, ev_dt_bias, ev_a_log, ev_d_skip, ev_ssd_norm, ev_q_norm, ev_k_norm, ev_cmp_w, od_w_in, od_w_out, od_s5_a_re, od_s5_a_im, od_s5_log_dt, od_s5_b_re, od_s5_b_im, od_s5_c_re, od_s5_c_im, od_s5_d, od_glu_w, od_glu_b, od_gla_wa2, od_gla_ba, od_gla_norm, moe_wc, moe_bc, moe_wf, moe_bf, moe_w1, moe_w3, moe_w2):
    bp, tp, _ = x_prompt.shape
    bs, ts, _ = x_sample.shape
    ns = bs * ts
    depth = ada_w.shape[0]
    bc = -(-(bp + bs) // 8) * 8
    c_all = jnp.zeros((bc, D), f32).at[:bp].set(c_prompt).at[bp:bp + bs].set(c_sample)
    mods = _ada(c_all, ada_w, ada_b)
    xp, xs = x_prompt, x_sample.reshape(1, ns, D)
    sp = {}
    ss = {}
    for l in range(depth):
        i = l // 2
        mp = [m[:, None, :] for m in jnp.split(mods[l, :bp], 6, axis=-1)]
        ms = [jnp.repeat(m, ts, axis=0)[None] for m in jnp.split(mods[l, bp:bp + bs], 6, axis=-1)]
        gm, gf = norm_mix[l][None], norm_ffn[l][None]
        if l % 2 == 0:
            prm = _even_params(i, ev_w_in, ev_w_out, ev_conv_w, ev_conv_b, ev_dt_bias, ev_a_log, ev_d_skip, ev_ssd_norm,
                               ev_q_norm, ev_k_norm, ev_cmp_w)
            xp, st_p = _even_prompt(xp, mp[0], mp[1], mp[2], gm, prm)
            xs, st_s = _even_sample(xs, ms[0], ms[1], ms[2], gm, prm, bs, ts, i, state_conv[i], state_ssd[i], page_table,
                                    cache_cmp_k, cache_cmp_v, cache_sel_k, cache_sel_v, cache_win_k, cache_win_v)
            names = ('cmp_k', 'cmp_v', 'sel_k', 'sel_v', 'win_k', 'win_v', 'ssd', 'conv')
        else:
            prm = _odd_params(i, od_w_in, od_w_out, od_s5_a_re, od_s5_a_im, od_s5_log_dt, od_s5_b_re, od_s5_b_im,
                              od_s5_c_re, od_s5_c_im, od_s5_d, od_glu_w, od_glu_b, od_gla_wa2, od_gla_ba, od_gla_norm)
            zs = jnp.zeros((bp, S5_GROUPS, S5_STATE), f32)
            xp, st_p = _odd_layer(xp, mp[0], mp[1], mp[2], gm, prm, bp, tp, zs, zs,
                                  jnp.zeros((bp, GLA_HEADS, GLA_DK, GLA_DV), f32))
            xs, st_s = _odd_layer(xs, ms[0], ms[1], ms[2], gm, prm, bs, ts, state_s5_re[i], state_s5_im[i], state_gla[i])
            names = ('s5_re', 's5_im', 'gla')
        for nm, a_p, a_s in zip(names, st_p, st_s):
            sp.setdefault(nm, []).append(a_p)
            ss.setdefault(nm, []).append(a_s)
        wr = jnp.zeros((D, LANES), f32).at[:, :MOE_GROUPS].set(moe_wc[l]).at[:, MOE_GROUPS:MOE_GROUPS + MOE_EXPERTS].set(moe_wf[l])
        br = jnp.zeros((1, LANES), f32).at[0, :MOE_GROUPS].set(moe_bc[l]).at[0, MOE_GROUPS:MOE_GROUPS + MOE_EXPERTS].set(moe_bf[l])
        xp = _moe(xp, mp[3], mp[4], mp[5], gf, wr, br, moe_w1, moe_w3, moe_w2, l)
        xs = _moe(xs, ms[3], ms[4], ms[5], gf, wr, br, moe_w1, moe_w3, moe_w2, l)
    order = ('cmp_k', 'cmp_v', 'sel_k', 'sel_v', 'win_k', 'win_v', 'ssd', 'conv', 's5_re', 's5_im', 'gla')
    outs = [xp, xs.reshape(bs, ts, D)]
    for nm in order:
        outs += [jnp.stack(sp[nm]), jnp.stack(ss[nm])]
    return tuple(outs)
```

```python
import functools
import math

import jax
import jax.numpy as jnp
from jax import lax
from jax.experimental import pallas as pl
from jax.experimental.pallas import tpu as pltpu

f32 = jnp.float32
bf16 = jnp.bfloat16
i32 = jnp.int32

D = 1024
PAGE = 128
SSD_HEADS, SSD_HD, SSD_INNER, SSD_GROUPS, SSD_STATE, SSD_CONV = 16, 64, 1024, 4, 64, 4
SSD_CONV_DIM = SSD_INNER + 2 * SSD_GROUPS * SSD_STATE
NSA_HEADS, NSA_KVH, NSA_REP, HD, NSA_BLOCK, NSA_TOPN, NSA_WINDOW = 16, 4, 4, 64, 64, 16, 512
ROPE_THETA = 10000.0
S5_CH, S5_GCH, S5_GROUPS, S5_STATE = 512, 16, 32, 64
S5_N = S5_GROUPS * S5_STATE
GLA_HEADS, GLA_DK, GLA_DV, GLA_RANK, GLA_TEMP = 4, 64, 128, 16, 16.0
MOE_GROUPS, MOE_PER_GROUP, MOE_EXPERTS, MOE_FF = 4, 8, 32, 256
MOE_ROWS = 256
EPS = 1e-6
NEG = -1e30
BIG = 1e30
GONE = -3e38
LANES = 128
CHUNK = 128
VT_ROWS = HD + 16
FLASH_GROUP = 2
FIXED_STAB_MAX = 40.0
VMEM_LIMIT = 48 * 2**20


def _cp(sem, vmem=VMEM_LIMIT):
    return pltpu.CompilerParams(dimension_semantics=sem, vmem_limit_bytes=vmem)


def _sigmoid(x):
    return 1.0 / (1.0 + jnp.exp(-x))


def _softplus(x):
    return jnp.maximum(x, 0.0) + jnp.log1p(jnp.exp(-jnp.abs(x)))


def _dotf(a, b):
    return jnp.dot(a, b, preferred_element_type=f32)


def _dot_nt(a, b):
    return lax.dot_general(a, b, (((1,), (1,)), ((), ())), preferred_element_type=f32)


def _dot_tn(a, b):
    return lax.dot_general(a, b, (((0,), (0,)), ((), ())), preferred_element_type=f32)


def _split2(a):
    h = a.astype(bf16)
    return h, (a - h.astype(f32)).astype(bf16)


def _split3(a):
    h = a.astype(bf16)
    r = a - h.astype(f32)
    m = r.astype(bf16)
    return h, m, (r - m.astype(f32)).astype(bf16)


def _dot3_r(a, w):
    h, m, l = _split3(a)
    return _dotf(h, w) + _dotf(m, w) + _dotf(l, w)


def _dot3_l(w, a):
    h, m, l = _split3(a)
    return _dotf(w, h) + _dotf(w, m) + _dotf(w, l)


def _iota(shape, dim):
    return lax.broadcasted_iota(i32, shape, dim)


def _tri(n):
    return _iota((n, n), 0) >= _iota((n, n), 1)


def _modulate(x, sh, sc, g):
    ms = jnp.mean(x * x, axis=-1, keepdims=True)
    return (x * lax.rsqrt(ms + EPS) * g) * (1.0 + sc) + sh


def _mod_spec(mod, tm):
    if mod.shape[1] == 1:
        return pl.BlockSpec((None, 1, D), lambda b, i: (b, 0, 0))
    return pl.BlockSpec((None, tm, D), lambda b, i: (b, i, 0))


def _full(a):
    n = a.ndim
    return pl.BlockSpec(a.shape, lambda *_: (0,) * n)


def _ada_kernel(c_ref, w_ref, b_ref, o_ref):
    c = c_ref[...]
    a = c * _sigmoid(c)
    ah, al = _split2(a)
    wh, wl = _split2(w_ref[...])
    o_ref[...] = _dotf(ah, wh) + _dotf(al, wh) + _dotf(ah, wl) + b_ref[...]


def _ada(c_all, ada_w, ada_b):
    depth, bc, tn = ada_w.shape[0], c_all.shape[0], 1536
    return pl.pallas_call(
        _ada_kernel, grid=(depth, 6 * D // tn),
        in_specs=[pl.BlockSpec((bc, D), lambda l, j: (0, 0)),
                  pl.BlockSpec((None, D, tn), lambda l, j: (l, 0, j)),
                  pl.BlockSpec((None, 1, tn), lambda l, j: (l, 0, j))],
        out_specs=pl.BlockSpec((None, bc, tn), lambda l, j: (l, 0, j)),
        out_shape=jax.ShapeDtypeStruct((depth, bc, 6 * D), f32),
        compiler_params=_cp(("parallel", "parallel")), name="ada")(c_all, ada_w, ada_b.reshape(depth, 1, 6 * D))


def _proj_kernel(x_ref, sh_ref, sc_ref, g_ref, *refs, nseg):
    h = _modulate(x_ref[...], sh_ref[...], sc_ref[...], g_ref[...]).astype(bf16)
    for i in range(nseg):
        refs[nseg + i][...] = _dotf(h, refs[i][...])


def _mod_proj(x, sh, sc, g, ws, name):
    b, t, _ = x.shape
    tm = min(256, t)
    nseg = len(ws)
    return pl.pallas_call(
        functools.partial(_proj_kernel, nseg=nseg), grid=(b, t // tm),
        in_specs=[pl.BlockSpec((None, tm, D), lambda b, i: (b, i, 0)), _mod_spec(sh, tm), _mod_spec(sc, tm),
                  pl.BlockSpec((1, D), lambda b, i: (0, 0))] + [_full(w) for w in ws],
        out_specs=[pl.BlockSpec((None, tm, w.shape[1]), lambda b, i: (b, i, 0)) for w in ws],
        out_shape=[jax.ShapeDtypeStruct((b, t, w.shape[1]), f32) for w in ws],
        compiler_params=_cp(("parallel", "parallel")), name=name)(x, sh, sc, g, *ws)


def _outproj_kernel(x_ref, gate_ref, *refs, nseg):
    acc = _dotf(refs[0][...].astype(bf16), refs[nseg][...])
    for i in range(1, nseg):
        acc = acc + _dotf(refs[i][...].astype(bf16), refs[nseg + i][...])
    refs[2 * nseg][...] = x_ref[...] + gate_ref[...] * acc


def _out_proj(x, gate, acts, ws, name):
    b, t, _ = x.shape
    tm = min(512, t)
    nseg = len(acts)
    return pl.pallas_call(
        functools.partial(_outproj_kernel, nseg=nseg), grid=(b, t // tm),
        in_specs=[pl.BlockSpec((None, tm, D), lambda b, i: (b, i, 0)), _mod_spec(gate, tm)]
        + [pl.BlockSpec((None, tm, a.shape[2]), lambda b, i: (b, i, 0)) for a in acts] + [_full(w) for w in ws],
        out_specs=pl.BlockSpec((None, tm, D), lambda b, i: (b, i, 0)),
        out_shape=jax.ShapeDtypeStruct((b, t, D), f32),
        compiler_params=_cp(("parallel", "parallel")), name=name)(x, gate, *acts, *ws)


def _ssd_kernel(xbc_ref, dt_ref, z_ref, conv0_ref, h0_ref, cw_ref, cb_ref, dtb_ref, a_ref, dsk_ref, nrm_ref,
                y_ref, convn_ref, hn_ref, xpad, hst, ybuf, *, nc, t_valid):
    L = CHUNK
    c = pl.program_id(1)

    @pl.when(c == 0)
    def _():
        hst[...] = h0_ref[...]
        xpad[0:8, :] = conv0_ref[...]

    xpad[8:8 + L, :] = xbc_ref[...]
    acc = cb_ref[...] + cw_ref[0:1, :] * xpad[5:5 + L, :]
    for k in range(1, SSD_CONV):
        acc = acc + cw_ref[k:k + 1, :] * xpad[5 + k:5 + k + L, :]
    tv_last = t_valid - (nc - 1) * L
    convn_ref[...] = xpad[tv_last:tv_last + 8, :]
    xpad[0:8, :] = xpad[L:L + 8, :]

    xc = acc * _sigmoid(acc)
    xs = xc[:, :SSD_INNER]
    bm = xc[:, SSD_INNER:SSD_INNER + 256]
    cm = xc[:, SSD_INNER + 256:]
    dt = _softplus(dt_ref[...] + dtb_ref[...])
    if t_valid < nc * L:
        dt = jnp.where(c * L + _iota((L, LANES), 0) < t_valid, dt, 0.0)
    causal = _tri(L)
    trib = causal.astype(bf16)
    cs = _dot3_l(trib, dt * a_ref[...])
    cs_t, dt_t = cs.T, dt.T
    wend_t = jnp.exp(cs_t[:, L - 1:L] - cs_t) * dt_t
    ecs = jnp.exp(cs)
    xs_t = xs.T
    for g in range(SSD_GROUPS):
        bg = bm[:, g * 64:(g + 1) * 64].astype(bf16)
        cg = cm[:, g * 64:(g + 1) * 64].astype(bf16)
        gmat = _dot_nt(cg, bg)
        for r in range(SSD_HEADS // SSD_GROUPS):
            h = g * (SSD_HEADS // SSD_GROUPS) + r
            seg = cs[:, h:h + 1] - cs_t[h:h + 1, :]
            dec = jnp.where(causal, jnp.exp(jnp.where(causal, seg, 0.0)), 0.0)
            sc = (gmat * dec * dt_t[h:h + 1, :]).astype(bf16)
            hs = hst[h]
            yh = _dotf(sc, xs[:, h * 64:(h + 1) * 64].astype(bf16)) + _dot_nt(cg, hs.astype(bf16)) * ecs[:, h:h + 1]
            ybuf[:, h * 64:(h + 1) * 64] = yh
            xw = (xs_t[h * 64:(h + 1) * 64, :] * wend_t[h:h + 1, :]).astype(bf16)
            hst[h] = hs * ecs[L - 1:L, h:h + 1] + _dotf(xw, bg)
    zz = z_ref[...]
    y = (ybuf[...] + dsk_ref[...] * xs) * (zz * _sigmoid(zz))
    gw = SSD_INNER // SSD_GROUPS
    for g in range(SSD_GROUPS):
        s = y[:, g * gw:(g + 1) * gw]
        ms = jnp.mean(s * s, axis=-1, keepdims=True)
        y_ref[:, g * gw:(g + 1) * gw] = s * lax.rsqrt(ms + EPS) * nrm_ref[:, g * gw:(g + 1) * gw]

    @pl.when(c == nc - 1)
    def _():
        hn_ref[...] = hst[...]


def _ssd(xbc, sm, z, conv0, h0, prm, t_valid):
    b, t, _ = xbc.shape
    nc = t // CHUNK
    tok = lambda w: pl.BlockSpec((None, CHUNK, w), lambda b, c: (b, c, 0))
    return pl.pallas_call(
        functools.partial(_ssd_kernel, nc=nc, t_valid=t_valid), grid=(b, nc),
        in_specs=[tok(SSD_CONV_DIM), tok(LANES), tok(SSD_INNER),
                  pl.BlockSpec((None, 8, SSD_CONV_DIM), lambda b, c: (b, 0, 0)),
                  pl.BlockSpec((None, SSD_HEADS, SSD_HD, SSD_STATE), lambda b, c: (b, 0, 0, 0))]
        + [_full(p) for p in prm],
        out_specs=[tok(SSD_INNER), pl.BlockSpec((None, 8, SSD_CONV_DIM), lambda b, c: (b, 0, 0)),
                   pl.BlockSpec((None, SSD_HEADS, SSD_HD, SSD_STATE), lambda b, c: (b, 0, 0, 0))],
        out_shape=[jax.ShapeDtypeStruct((b, t, SSD_INNER), f32), jax.ShapeDtypeStruct((b, 8, SSD_CONV_DIM), f32),
                   jax.ShapeDtypeStruct((b, SSD_HEADS, SSD_HD, SSD_STATE), f32)],
        scratch_shapes=[pltpu.VMEM((CHUNK + 8, SSD_CONV_DIM), f32), pltpu.VMEM((SSD_HEADS, SSD_HD, SSD_STATE), f32),
                        pltpu.VMEM((CHUNK, SSD_INNER), f32)],
        compiler_params=_cp(("parallel", "arbitrary")), name="ssd")(xbc, sm, z, conv0, h0, *prm)


def _seg_rinv(x, seg_ref, exp_ref):
    x2 = x * x
    h, l = _split2(x2)
    ss = _dotf(h, seg_ref[...]) + _dotf(l, seg_ref[...])
    return _dot3_r(lax.rsqrt(ss * (1.0 / HD) + EPS), exp_ref[...])


def _rope(x, cosf, sinf):
    w = x.shape[1]
    first = (_iota(x.shape, 1) % HD) < (HD // 2)
    rot = jnp.where(first, pltpu.roll(x, w - HD // 2, 1), pltpu.roll(x, HD // 2, 1))
    return x * cosf + rot * sinf


def _nsaprep_kernel(q_ref, kv_ref, sm_ref, cos_ref, sin_ref, segq_ref, expq_ref, segk_ref, expk_ref, gq_ref, gk_ref,
                    vm_ref, cw_ref, *outs, tm, compress):
    qc_ref, qr_ref, kcmp_ref, vcmp_ref, ksel_ref, vsel_ref, kwin_ref, vwin_ref, gate_ref = outs[:9]
    i = pl.program_id(1)
    cos1, sin1 = cos_ref[...], sin_ref[...]
    q = q_ref[...]
    qn = q * _seg_rinv(q, segq_ref, expq_ref) * gq_ref[...]
    qc_ref[...] = (qn * 0.125).astype(bf16)
    qr_ref[...] = (_rope(qn, jnp.concatenate([cos1] * 8, axis=1), jnp.concatenate([sin1] * 8, axis=1)) * 0.125).astype(bf16)
    kv = kv_ref[...]
    kvn = kv * (_seg_rinv(kv, segk_ref, expk_ref) * gk_ref[...] + vm_ref[...])
    cos2, sin2 = jnp.concatenate([cos1] * 2, axis=1), jnp.concatenate([sin1] * 2, axis=1)
    kcmp, vcmp = kvn[:, 0:256], kvn[:, 256:512]
    ksel, vsel = _rope(kvn[:, 512:768], cos2, sin2), kvn[:, 768:1024]
    kwin, vwin = _rope(kvn[:, 1024:1280], cos2, sin2), kvn[:, 1280:1536]
    kcmp_ref[...] = kcmp
    vcmp_ref[...] = vcmp
    ksel_ref[...] = ksel
    vsel_ref[...] = vsel
    kwin_ref[...] = kwin
    vwin_ref[...] = vwin
    gate_ref[...] = _sigmoid(sm_ref[...])
    if compress:
        kaug_ref, vsa_ref, kwh_ref, vwa_ref, kch_ref, vch_ref, kn2_ref = outs[9:]
        sq = lambda a: jnp.square(a.astype(bf16).astype(f32))
        n2 = lambda a, seg: jnp.max(_dot3_r(sq(a), seg), axis=0, keepdims=True)
        kn2_ref[...] = jnp.broadcast_to(n2(ksel, segq_ref[0:256, :]) + n2(kwin, segq_ref[256:512, :]), (8, LANES))
        blk = (i * tm + _iota((tm, LANES), 0)) // NSA_BLOCK
        onehot = (_iota((tm, LANES), 1) == blk).astype(bf16)
        ones_t = (_iota((VT_ROWS - HD, tm), 0) == 0).astype(bf16)
        zero = jnp.zeros((tm, HD), bf16)
        kcw = (kcmp.reshape(tm // NSA_BLOCK, NSA_BLOCK, 256) * cw_ref[0][None]).sum(axis=1)
        vcw = (vcmp.reshape(tm // NSA_BLOCK, NSA_BLOCK, 256) * cw_ref[1][None]).sum(axis=1)
        vsel_t, vwin_t = vsel.T, vwin.T
        for g in range(NSA_KVH):
            sl = slice(g * HD, (g + 1) * HD)
            kaug_ref[g, :, 0:HD] = ksel[:, sl].astype(bf16)
            kaug_ref[g, :, HD:2 * HD] = zero
            kaug_ref[g, :, 2 * HD:] = onehot
            vsa_ref[g, 0:HD, :] = vsel_t[sl, :].astype(bf16)
            vsa_ref[g, HD:, :] = ones_t
            kwh_ref[g] = kwin[:, sl].astype(bf16)
            vwa_ref[g, 0:HD, :] = vwin_t[sl, :].astype(bf16)
            vwa_ref[g, HD:, :] = ones_t
            kch_ref[g] = kcw[:, sl]
            vch_ref[g] = vcw[:, sl]


def _nsa_prep(q, kv, sm, cos_t, sin_t, prm, compress):
    b, t, _ = q.shape
    tm = min(512, t)
    tok = lambda w: pl.BlockSpec((None, tm, w), lambda b, i: (b, i, 0))
    tab = pl.BlockSpec((tm, LANES), lambda b, i: (i, 0))
    hm = lambda w: pl.BlockSpec((None, NSA_KVH, tm, w), lambda b, i: (b, 0, i, 0))
    out_specs = [tok(D), tok(D)] + [tok(256)] * 6 + [tok(LANES)]
    out_shape = [jax.ShapeDtypeStruct((b, t, D), bf16)] * 2 + [jax.ShapeDtypeStruct((b, t, 256), f32)] * 6 \
        + [jax.ShapeDtypeStruct((b, t, LANES), f32)]
    if compress:
        nbt = tm // NSA_BLOCK
        cspec = pl.BlockSpec((None, NSA_KVH, nbt, HD), lambda b, i: (b, 0, i, 0))
        vt = pl.BlockSpec((None, NSA_KVH, VT_ROWS, tm), lambda b, i: (b, 0, 0, i))
        out_specs += [hm(256), vt, hm(HD), vt, cspec, cspec, pl.BlockSpec((None, None, 8, LANES), lambda b, i: (b, i, 0, 0))]
        out_shape += [jax.ShapeDtypeStruct((b, NSA_KVH, t, 256), bf16), jax.ShapeDtypeStruct((b, NSA_KVH, VT_ROWS, t), bf16),
                      jax.ShapeDtypeStruct((b, NSA_KVH, t, HD), bf16), jax.ShapeDtypeStruct((b, NSA_KVH, VT_ROWS, t), bf16),
                      jax.ShapeDtypeStruct((b, NSA_KVH, t // NSA_BLOCK, HD), f32),
                      jax.ShapeDtypeStruct((b, NSA_KVH, t // NSA_BLOCK, HD), f32),
                      jax.ShapeDtypeStruct((b, t // tm, 8, LANES), f32)]
    return pl.pallas_call(
        functools.partial(_nsaprep_kernel, tm=tm, compress=compress), grid=(b, t // tm),
        in_specs=[tok(D), tok(1536), tok(LANES), tab, tab] + [_full(p) for p in prm],
        out_specs=out_specs, out_shape=out_shape,
        compiler_params=_cp(("parallel", "parallel")), name="nsa_prep")(q, kv, sm, cos_t, sin_t, *prm)


def _cmp_branch(q_heads, kc, vc, tpos):
    nb, nq = kc.shape[0], tpos.shape[1]
    nrow = _iota((nb, nq), 0)
    ok = ((nrow + 1) * NSA_BLOCK - 1) <= tpos
    imp = jnp.zeros((nb, nq), f32)
    outs = []
    for q in q_heads:
        s = jnp.where(ok, _dot_nt(kc, q), NEG)
        m = jnp.max(s, axis=0, keepdims=True)
        e = jnp.where(ok, jnp.exp(s - m), 0.0)
        l = jnp.sum(e, axis=0, keepdims=True)
        p = e / jnp.where(l > 0.0, l, 1.0)
        imp = imp + p
        outs.append(_dot_tn(p.astype(bf16), vc))
    cur = tpos // NSA_BLOCK
    impm = jnp.where((nrow == cur) | (nrow == 0), BIG, jnp.where(nrow < cur, imp, NEG))

    def pick(_, v):
        mx = jnp.max(v, axis=0, keepdims=True)
        idx = jnp.min(jnp.where(v == mx, nrow, nb), axis=0, keepdims=True)
        return jnp.where(nrow == idx, GONE, v)

    taken = lax.fori_loop(0, NSA_TOPN, pick, impm) == GONE
    neg = jnp.where(taken & (impm > 0.5 * NEG), 0.0, NEG)
    return outs, neg


def _flash_t(lhs, k_ref, vt_ref, lo, hi, trow, tk, m_ref, acc_ref, window):
    m_ref[...] = jnp.full(m_ref.shape, NEG, f32)
    acc_ref[...] = jnp.zeros(acc_ref.shape, f32)

    def step(kt, masked):
        ks = pl.multiple_of(kt * tk, tk)
        s = _dot_nt(k_ref[pl.ds(ks, tk), :], lhs)
        if masked:
            kpos = ks + _iota((tk, 1), 0)
            keep = kpos <= trow
            if window:
                keep = keep & (kpos > trow - NSA_WINDOW)
            s = jnp.where(keep, s, NEG)
        m_old = m_ref[...]
        m_new = jnp.maximum(m_old, jnp.max(s, axis=0, keepdims=True))
        p = jnp.exp(s - m_new).astype(bf16)
        acc_ref[...] = jnp.exp(m_old - m_new) * acc_ref[...] + _dotf(vt_ref[:, pl.ds(ks, tk)], p)
        m_ref[...] = m_new

    def body(masked):
        def f(kt, _):
            step(kt, masked)
            return 0
        return f

    if window:
        lax.fori_loop(lo, hi, body(True), 0)
    else:
        lax.fori_loop(lo, hi - 1, body(False), 0)
        step(hi - 1, True)
    acc = acc_ref[...]
    return acc[0:HD, :] / acc[HD:HD + 1, :]


def _flash_fixed(lhs, k_ref, vt_ref, lo, hi, trow, tk, mrow, acc_ref, window):
    acc_ref[...] = jnp.zeros(acc_ref.shape, f32)

    def probs(kt, masked):
        ks = pl.multiple_of(kt * tk, tk)
        s = _dot_nt(k_ref[pl.ds(ks, tk), :], lhs)
        if masked:
            kpos = ks + _iota((tk, 1), 0)
            keep = kpos <= trow
            if window:
                keep = keep & (kpos > trow - NSA_WINDOW)
            s = jnp.where(keep, s, NEG)
        return jnp.exp(s - mrow).astype(bf16)

    def group(first, masks):
        pv = None
        for j, masked in enumerate(masks):
            ks = pl.multiple_of((first + j) * tk, tk)
            d = _dotf(vt_ref[:, pl.ds(ks, tk)], probs(first + j, masked))
            pv = d if pv is None else pv + d
        acc_ref[...] += pv

    def single(masked):
        def f(kt, _):
            group(kt, (masked,))
            return 0
        return f

    if window:
        full = hi - lo == 3

        @pl.when(full)
        def _():
            group(lo, (True, NSA_WINDOW != 2 * tk or lhs.shape[0] != NSA_REP * tk, True))

        @pl.when(jnp.logical_not(full))
        def _():
            lax.fori_loop(lo, hi, single(True), 0)
    else:
        ngrp = (hi - 1 - lo) // FLASH_GROUP

        def grp(j, _):
            group(lo + FLASH_GROUP * j, (False,) * FLASH_GROUP)
            return 0

        lax.fori_loop(0, ngrp, grp, 0)
        lax.fori_loop(lo + FLASH_GROUP * ngrp, hi - 1, single(False), 0)
        group(hi - 1, (True,))
    acc = acc_ref[...]
    return acc[0:HD, :] / acc[HD:HD + 1, :]


def _nsa_kernel(qc_ref, qr_ref, kc_ref, vc_ref, kaug_ref, vsa_ref, kw_ref, vwa_ref, gate_ref, gexp_ref, kb_ref, o_ref,
                lhs_ref, m_ref, acc_ref, os_ref, ow_ref, oc_ref, *, tq, tk):
    qi = pl.program_id(2)
    t0 = qi * tq
    tpos = t0 + _iota((1, tq), 1)
    qh = [qc_ref[:, r * HD:(r + 1) * HD] for r in range(NSA_REP)]
    nbp = kc_ref.shape[0]
    need = (t0 + tq) // NSA_BLOCK
    sizes = [r for r in (32, 64) if r < nbp] + [nbp]
    for j, nr in enumerate(sizes):
        fits = need <= nr if j == 0 else (need > sizes[j - 1]) if nr == nbp else (need > sizes[j - 1]) & (need <= nr)

        @pl.when(fits)
        def _():
            oc, neg = _cmp_branch(qh, kc_ref[0:nr, :].astype(bf16), vc_ref[0:nr, :].astype(bf16), tpos)
            if nr < nbp:
                neg = jnp.concatenate([neg, jnp.full((nbp - nr, tq), NEG, f32)], axis=0)
            negq = neg.T.astype(bf16)
            oc_ref[...] = jnp.concatenate(oc, axis=1)
            for r in range(NSA_REP):
                lhs_ref[r * tq:(r + 1) * tq, 2 * HD:] = negq

    for r in range(NSA_REP):
        rows = slice(r * tq, (r + 1) * tq)
        lhs_ref[rows, 0:HD] = qr_ref[:, r * HD:(r + 1) * HD]
        lhs_ref[rows, HD:2 * HD] = jnp.zeros((tq, HD), bf16)
    trow = t0 + _iota((1, NSA_REP * tq), 1) % tq
    hi = (t0 + tq) // tk
    lo = jnp.maximum(t0 - (NSA_WINDOW - 1), 0) // tk
    qf = lhs_ref[:, 0:HD].astype(f32)
    qh, ql = _split2(qf * qf)
    ones = jnp.ones((8, HD), bf16)
    qn = jnp.sqrt((_dot_nt(ones, qh) + _dot_nt(ones, ql))[0:1, :]) * 1.02 + 1e-6
    kb = kb_ref[...]
    for k_ref, vt_ref, kmax, out_ref, lo_b, win in ((kaug_ref, vsa_ref, kb[0:1, 0:1], os_ref, 0, False),
                                                     (kw_ref, vwa_ref, kb[0:1, 1:2], ow_ref, lo, True)):
        lhs = lhs_ref[:, 0:HD] if win else lhs_ref[...]
        mrow = qn * kmax
        safe = jnp.max(mrow) <= FIXED_STAB_MAX

        @pl.when(safe)
        def _():
            out_ref[...] = _flash_fixed(lhs, k_ref, vt_ref, lo_b, hi, trow, tk, mrow, acc_ref, win)

        @pl.when(jnp.logical_not(safe))
        def _():
            out_ref[...] = _flash_t(lhs, k_ref, vt_ref, lo_b, hi, trow, tk, m_ref, acc_ref, win)

    os_t, ow_t = os_ref[...], ow_ref[...]
    gates = gate_ref[...]
    unstack = lambda a: jnp.concatenate([a[:, r * tq:(r + 1) * tq] for r in range(NSA_REP)], axis=0).T
    o = _dot3_r(gates, gexp_ref[0]) * oc_ref[...] \
        + _dot3_r(gates, gexp_ref[1]) * unstack(os_t) + _dot3_r(gates, gexp_ref[2]) * unstack(ow_t)
    o_ref[...] = o.astype(bf16)


def _nsa_prompt(qc, qr, kch, vch, kaug, vsa, kwh, vwa, gates, gexp, kbound):
    b, t, _ = qc.shape
    tq = tk = min(256, t)
    nbp = kch.shape[2]
    qspec = pl.BlockSpec((None, tq, 256), lambda b, g, i: (b, i, g))
    kvspec = lambda n, w: pl.BlockSpec((None, None, n, w), lambda b, g, i: (b, g, 0, 0))
    return pl.pallas_call(
        functools.partial(_nsa_kernel, tq=tq, tk=tk), grid=(b, NSA_KVH, t // tq),
        in_specs=[qspec, qspec, kvspec(nbp, HD), kvspec(nbp, HD), kvspec(t, 256), kvspec(VT_ROWS, t), kvspec(t, HD),
                  kvspec(VT_ROWS, t), pl.BlockSpec((None, tq, LANES), lambda b, g, i: (b, i, 0)),
                  pl.BlockSpec((None, 3, LANES, 256), lambda b, g, i: (g, 0, 0, 0)), kvspec(8, LANES)],
        out_specs=qspec, out_shape=jax.ShapeDtypeStruct((b, t, D), bf16),
        scratch_shapes=[pltpu.VMEM((NSA_REP * tq, 256), bf16), pltpu.VMEM((1, NSA_REP * tq), f32),
                        pltpu.VMEM((VT_ROWS, NSA_REP * tq), f32), pltpu.VMEM((HD, NSA_REP * tq), f32),
                        pltpu.VMEM((HD, NSA_REP * tq), f32), pltpu.VMEM((tq, 256), f32)],
        compiler_params=_cp(("parallel", "parallel", "arbitrary")), name="nsa_prompt")(
            qc, qr, kch, vch, kaug, vsa, kwh, vwa, gates, gexp, kbound)


PCH = 8
NSLOT = 4


def _chunk_copies(cache_ref, li, pt_ref, b, c, buf, slot, sem):
    return [pltpu.make_async_copy(cache_ref.at[li, pt_ref[b, c * PCH + p]], buf.at[slot, p], sem.at[slot])
            for p in range(PCH)]


def _pagecmp_kernel(pt_ref, wk_ref, wv_ref, place_ref, ck_ref, cv_ref, kc_ref, vc_ref, kbuf, vbuf, sem, *, li, npg):
    b = pl.program_id(0)
    nch = npg // PCH

    def copies(c, slot):
        return _chunk_copies(ck_ref, li, pt_ref, b, c, kbuf, slot, sem) + _chunk_copies(cv_ref, li, pt_ref, b, c, vbuf, slot, sem)

    for c0 in range(min(NSLOT - 1, nch)):
        for cp in copies(c0, c0):
            cp.start()

    def body(c, _):
        slot = c % NSLOT
        ahead = c + NSLOT - 1

        @pl.when(ahead < nch)
        def _():
            for cp in copies(ahead, ahead % NSLOT):
                cp.start()

        for cp in copies(c, slot):
            cp.wait()
        ka = _dotf((kbuf[slot, 0] * wk_ref[...]).astype(bf16), place_ref[0])
        va = _dotf((vbuf[slot, 0] * wv_ref[...]).astype(bf16), place_ref[0])
        for p in range(1, PCH):
            ka = ka + _dotf((kbuf[slot, p] * wk_ref[...]).astype(bf16), place_ref[p])
            va = va + _dotf((vbuf[slot, p] * wv_ref[...]).astype(bf16), place_ref[p])
        kc_ref[c] = ka
        vc_ref[c] = va
        return 0

    lax.fori_loop(0, nch, body, 0)


def _page_compress(page_table, cache_k, cache_v, li, wk_t, wv_t):
    bs, npg = page_table.shape
    nch = npg // PCH
    place = (jnp.arange(LANES)[None, None, :]
             == 2 * jnp.arange(PCH)[:, None, None] + jnp.arange(PAGE)[None, :, None] // NSA_BLOCK).astype(bf16)
    out = pl.BlockSpec((None, nch, 256, LANES), lambda b, pt: (b, 0, 0, 0))
    shp = jax.ShapeDtypeStruct((bs, nch, 256, LANES), f32)
    anyspec = pl.BlockSpec(memory_space=pl.ANY)
    return pl.pallas_call(
        functools.partial(_pagecmp_kernel, li=li, npg=npg),
        grid_spec=pltpu.PrefetchScalarGridSpec(
            num_scalar_prefetch=1, grid=(bs,),
            in_specs=[pl.BlockSpec((256, PAGE), lambda b, pt: (0, 0)), pl.BlockSpec((256, PAGE), lambda b, pt: (0, 0)),
                      pl.BlockSpec((PCH, PAGE, LANES), lambda b, pt: (0, 0, 0)), anyspec, anyspec],
            out_specs=[out, out],
            scratch_shapes=[pltpu.VMEM((NSLOT, PCH, 256, PAGE), f32), pltpu.VMEM((NSLOT, PCH, 256, PAGE), f32),
                            pltpu.SemaphoreType.DMA((NSLOT,))]),
        out_shape=[shp, shp], compiler_params=_cp(("arbitrary",)), name="page_compress")(
            page_table, wk_t, wv_t, place, cache_k, cache_v)


def _ssel_kernel(qc_ref, kc_ref, vc_ref, kn_ref, vn_ref, w_ref, mrep_ref, oc_ref, neg_ref, *, past, ts):
    nbp = kc_ref.shape[1]
    nb_past = past // NSA_BLOCK
    col = _iota((1, LANES), 1)
    tpos = past + col % 8
    newrow = (_iota((nbp, 1), 0) == nb_past).astype(f32)
    tmask = (_iota((8, 1), 0) < ts).astype(f32)
    for g in range(NSA_KVH):
        sl = slice(g * HD, (g + 1) * HD)
        kc_new = jnp.sum(kn_ref[:, sl] * tmask * w_ref[0, 0:8, :], axis=0, keepdims=True)
        vc_new = jnp.sum(vn_ref[:, sl] * tmask * w_ref[1, 0:8, :], axis=0, keepdims=True)
        kc = (kc_ref[g] + newrow * kc_new).astype(bf16)
        vc = (vc_ref[g] + newrow * vc_new).astype(bf16)
        q = qc_ref[g]
        nrow = _iota((nbp, LANES), 0)
        ok = ((nrow + 1) * NSA_BLOCK - 1) <= tpos
        s = jnp.where(ok, _dot_nt(kc, q), NEG)
        m = jnp.max(s, axis=0, keepdims=True)
        e = jnp.where(ok, jnp.exp(s - m), 0.0)
        l = jnp.sum(e, axis=0, keepdims=True)
        p = e / jnp.where(l > 0.0, l, 1.0)
        oc_ref[g] = _dot_tn(p.astype(bf16), vc)
        imp = _dot3_r(p, mrep_ref[...])
        cur = tpos // NSA_BLOCK
        impm = jnp.where((nrow == cur) | (nrow == 0), BIG, jnp.where(nrow < cur, imp, NEG))

        def pick(_, v):
            mx = jnp.max(v, axis=0, keepdims=True)
            idx = jnp.min(jnp.where(v == mx, nrow, nbp), axis=0, keepdims=True)
            return jnp.where(nrow == idx, GONE, v)

        taken = lax.fori_loop(0, NSA_TOPN, pick, impm) == GONE
        neg_ref[g] = jnp.where(taken & (impm > 0.5 * NEG), 0.0, NEG)


def _sample_select(qc_st, kc_h, vc_h, kn, vn, cmp_w, mrep, past, ts):
    bs, _, nbp, _ = kc_h.shape
    b4 = lambda n, w: pl.BlockSpec((None, NSA_KVH, n, w), lambda b: (b, 0, 0, 0))
    return pl.pallas_call(
        functools.partial(_ssel_kernel, past=past, ts=ts), grid=(bs,),
        in_specs=[b4(LANES, HD), b4(nbp, HD), b4(nbp, HD), pl.BlockSpec((None, 8, 256), lambda b: (b, 0, 0)),
                  pl.BlockSpec((None, 8, 256), lambda b: (b, 0, 0)), _full(cmp_w), _full(mrep)],
        out_specs=[b4(LANES, HD), b4(nbp, LANES)],
        out_shape=[jax.ShapeDtypeStruct((bs, NSA_KVH, LANES, HD), f32), jax.ShapeDtypeStruct((bs, NSA_KVH, nbp, LANES), f32)],
        compiler_params=_cp(("parallel",)), name="sample_select")(qc_st, kc_h, vc_h, kn, vn, cmp_w, mrep)


def _new_rows_mask(ts):
    tt = _iota((LANES, LANES), 0) % 8
    lane = _iota((LANES, LANES), 1)
    return (lane <= tt) & (lane < ts)


def _diag_blocks(o_ref, acc, l):
    nr = LANES // NSA_KVH
    for g in range(NSA_KVH):
        o_ref[g] = acc[g * nr:(g + 1) * nr, g * HD:(g + 1) * HD] / l[g * nr:(g + 1) * nr, :]


def _spage_kernel(pt_ref, q_ref, slab_ref, e_ref, kn_ref, vn_ref, ck_ref, cv_ref, o_ref, kbuf, vbuf, sem, m_s, l_s, acc_s,
                  *, li, npg, ts):
    b = pl.program_id(0)
    nch = npg // PCH

    def copies(c, slot):
        return _chunk_copies(ck_ref, li, pt_ref, b, c, kbuf, slot, sem) + _chunk_copies(cv_ref, li, pt_ref, b, c, vbuf, slot, sem)

    for c0 in range(min(NSLOT - 1, nch)):
        for cp in copies(c0, c0):
            cp.start()
    q = q_ref[...]
    s = jnp.where(_new_rows_mask(ts), _dotf(q, kn_ref[...].astype(bf16)), NEG)
    m = jnp.max(s, axis=1, keepdims=True)
    p = jnp.exp(s - m)
    m_s[...] = m
    l_s[...] = jnp.sum(p, axis=1, keepdims=True)
    acc_s[...] = _dot_nt(p.astype(bf16), vn_ref[...].astype(bf16))

    def body(c, _):
        slot = c % NSLOT
        ahead = c + NSLOT - 1

        @pl.when(ahead < nch)
        def _():
            for cp in copies(ahead, ahead % NSLOT):
                cp.start()

        for cp in copies(c, slot):
            cp.wait()
        bias = _dotf(slab_ref[c], e_ref[...])
        s = jnp.concatenate([_dotf(q, kbuf[slot, p].astype(bf16)) for p in range(PCH)], axis=1) + bias
        m_old = m_s[...]
        m_new = jnp.maximum(m_old, jnp.max(s, axis=1, keepdims=True))
        p = jnp.exp(s - m_new)
        alpha = jnp.exp(m_old - m_new)
        l_s[...] = alpha * l_s[...] + jnp.sum(p, axis=1, keepdims=True)
        pb = p.astype(bf16)
        pv = _dot_nt(pb[:, 0:PAGE], vbuf[slot, 0].astype(bf16))
        for j in range(1, PCH):
            pv = pv + _dot_nt(pb[:, j * PAGE:(j + 1) * PAGE], vbuf[slot, j].astype(bf16))
        acc_s[...] = alpha * acc_s[...] + pv
        m_s[...] = m_new
        return 0

    lax.fori_loop(0, nch, body, 0)
    _diag_blocks(o_ref, acc_s[...], l_s[...])


def _sample_selected(page_table, qbd, slab, expand, kn_t, vn_t, cache_k, cache_v, li, ts):
    bs, npg = page_table.shape
    nch = npg // PCH
    per_b = lambda shp: pl.BlockSpec((None,) + shp, lambda b, pt: (b,) + (0,) * len(shp))
    anyspec = pl.BlockSpec(memory_space=pl.ANY)
    return pl.pallas_call(
        functools.partial(_spage_kernel, li=li, npg=npg, ts=ts),
        grid_spec=pltpu.PrefetchScalarGridSpec(
            num_scalar_prefetch=1, grid=(bs,),
            in_specs=[per_b((LANES, 256)), per_b((nch, LANES, LANES)),
                      pl.BlockSpec((LANES, PCH * PAGE), lambda b, pt: (0, 0)), per_b((256, LANES)), per_b((256, LANES)),
                      anyspec, anyspec],
            out_specs=per_b((NSA_KVH, LANES // NSA_KVH, HD)),
            scratch_shapes=[pltpu.VMEM((NSLOT, PCH, 256, PAGE), f32), pltpu.VMEM((NSLOT, PCH, 256, PAGE), f32),
                            pltpu.SemaphoreType.DMA((NSLOT,)), pltpu.VMEM((LANES, 1), f32), pltpu.VMEM((LANES, 1), f32),
                            pltpu.VMEM((LANES, 256), f32)]),
        out_shape=jax.ShapeDtypeStruct((bs, NSA_KVH, LANES // NSA_KVH, HD), f32),
        compiler_params=_cp(("arbitrary",)), name="sample_selected")(
            page_table, qbd, slab, expand, kn_t, vn_t, cache_k, cache_v)


def _swin_kernel(q_ref, wk_ref, wv_ref, kn_ref, vn_ref, o_ref, wko_ref, wvo_ref, *, ts, wb):
    q = q_ref[...]
    wk, wv, kn, vn = wk_ref[...], wv_ref[...], kn_ref[...], vn_ref[...]
    tt = _iota((LANES, wb), 0) % 8
    s1 = jnp.where(_iota((LANES, wb), 1) > tt + (wb - NSA_WINDOW), _dotf(q, wk.astype(bf16)), NEG)
    s2 = jnp.where(_new_rows_mask(ts), _dotf(q, kn.astype(bf16)), NEG)
    m = jnp.maximum(jnp.max(s1, axis=1, keepdims=True), jnp.max(s2, axis=1, keepdims=True))
    p1, p2 = jnp.exp(s1 - m), jnp.exp(s2 - m)
    l = jnp.sum(p1, axis=1, keepdims=True) + jnp.sum(p2, axis=1, keepdims=True)
    acc = _dot_nt(p1.astype(bf16), wv.astype(bf16)) + _dot_nt(p2.astype(bf16), vn.astype(bf16))
    _diag_blocks(o_ref, acc, l)
    tail = _iota((256, wb), 1) >= wb - ts
    pad = jnp.zeros((256, wb - LANES), f32)
    wko_ref[...] = jnp.where(tail, jnp.concatenate([pad, pltpu.roll(kn, LANES - ts, 1)], axis=1), pltpu.roll(wk, wb - ts, 1))
    wvo_ref[...] = jnp.where(tail, jnp.concatenate([pad, pltpu.roll(vn, LANES - ts, 1)], axis=1), pltpu.roll(wv, wb - ts, 1))


def _sample_window(qbd, win_k, win_v, li, kn_t, vn_t, ts):
    bs, wb = win_k.shape[1], win_k.shape[3]
    wspec = pl.BlockSpec((None, None, 256, wb), lambda b: (li, b, 0, 0))
    per_b = lambda shp: pl.BlockSpec((None,) + shp, lambda b: (b,) + (0,) * len(shp))
    return pl.pallas_call(
        functools.partial(_swin_kernel, ts=ts, wb=wb), grid=(bs,),
        in_specs=[per_b((LANES, 256)), wspec, wspec, per_b((256, LANES)), per_b((256, LANES))],
        out_specs=[per_b((NSA_KVH, LANES // NSA_KVH, HD)), per_b((256, wb)), per_b((256, wb))],
        out_shape=[jax.ShapeDtypeStruct((bs, NSA_KVH, LANES // NSA_KVH, HD), f32), jax.ShapeDtypeStruct((bs, 256, wb), f32),
                   jax.ShapeDtypeStruct((bs, 256, wb), f32)],
        compiler_params=_cp(("parallel",)), name="sample_window")(qbd, win_k, win_v, kn_t, vn_t)


def _gelu_tanh(x):
    return 0.5 * x * (1.0 + jnp.tanh(math.sqrt(2.0 / math.pi) * (x + 0.044715 * (x * x * x))))


def _s5_kernel(u_ref, h0r_ref, h0i_ref, perm_ref, permt_ref, bd_ref, pwr_ref, pwi_ref, cdr_ref, cdi_ref, d_ref, gw_ref,
               gb_ref, y_ref, sr_ref, si_ref, xr_s, xi_s, cr_s, ci_s, *, tc, srow):
    c = pl.program_id(1)

    @pl.when(c == 0)
    def _():
        cr_s[...] = h0r_ref[...]
        ci_s[...] = h0i_ref[...]

    ns = tc // 8
    u = u_ref[...]
    ub = _dotf(perm_ref[...], u.astype(bf16)).astype(bf16)
    hc, hn = S5_CH // 2, S5_N // 2
    for h in range(2):
        uh = ub[:, h * hc:(h + 1) * hc]
        xr_s[:, h * hn:(h + 1) * hn] = _dotf(uh, bd_ref[h * hc:(h + 1) * hc, h * hn:(h + 1) * hn])
        xi_s[:, h * hn:(h + 1) * hn] = _dotf(uh, bd_ref[h * hc:(h + 1) * hc, S5_N + h * hn:S5_N + (h + 1) * hn])
    ar, ai = pwr_ref[0:1, :], pwi_ref[0:1, :]

    def scan(t, carry):
        xr, xi = carry
        rs = pl.multiple_of(t * 8, 8)
        nr = ar * xr - ai * xi + xr_s[pl.ds(rs, 8), :]
        ni = ar * xi + ai * xr + xi_s[pl.ds(rs, 8), :]
        xr_s[pl.ds(rs, 8), :] = nr
        xi_s[pl.ds(rs, 8), :] = ni
        return nr, ni

    zero = jnp.zeros((8, S5_N), f32)
    fr, fi = lax.fori_loop(0, ns, scan, (zero, zero))
    asr, asi = pwr_ref[ns - 1:ns, :], pwi_ref[ns - 1:ns, :]
    cr, ci = cr_s[...], ci_s[...]
    er, ei = [], []
    for s in range(8):
        er.append(cr)
        ei.append(ci)
        cr, ci = asr * cr - asi * ci + fr[s:s + 1, :], asr * ci + asi * cr + fi[s:s + 1, :]
    cr_s[...] = cr
    ci_s[...] = ci
    er, ei = jnp.concatenate(er, axis=0), jnp.concatenate(ei, axis=0)

    def fix(t, _):
        rs = pl.multiple_of(t * 8, 8)
        pr, pi = pwr_ref[pl.ds(t, 1), :], pwi_ref[pl.ds(t, 1), :]
        xr_s[pl.ds(rs, 8), :] += pr * er - pi * ei
        xi_s[pl.ds(rs, 8), :] += pr * ei + pi * er
        return 0

    lax.fori_loop(0, ns, fix, 0)
    sr_ref[...] = xr_s[srow:srow + 8, :]
    si_ref[...] = xi_s[srow:srow + 8, :]
    yh = []
    for h in range(2):
        rows, cols = slice(h * hn, (h + 1) * hn), slice(h * hc, (h + 1) * hc)
        yh.append(_dotf(xr_s[:, rows].astype(bf16), cdr_ref[rows, cols]) + _dotf(xi_s[:, rows].astype(bf16), cdi_ref[rows, cols]))
    y = _dot3_l(permt_ref[...], jnp.concatenate(yh, axis=1)) + d_ref[...] * u
    y = _gelu_tanh(y)
    y_ref[...] = y * _sigmoid(_dotf(y.astype(bf16), gw_ref[...]) + gb_ref[...])


def _s5(u, h0r, h0i, prm, t_valid):
    b, t, _ = u.shape
    tc = min(256, t)
    ns = tc // 8
    assert (t_valid - 1) // tc == t // tc - 1
    r = (t_valid - 1) % tc
    prow = (r % ns) * 8 + r // ns
    srow = prow // 8 * 8
    rid = jnp.arange(tc)
    perm = (rid[None, :] == (rid[:, None] % 8) * ns + rid[:, None] // 8).astype(bf16)
    st = pl.BlockSpec((None, 8, S5_N), lambda b, c: (b, 0, 0))
    h0 = pl.BlockSpec((None, 1, S5_N), lambda b, c: (b, 0, 0))
    y, sr, si = pl.pallas_call(
        functools.partial(_s5_kernel, tc=tc, srow=srow), grid=(b, t // tc),
        in_specs=[pl.BlockSpec((None, tc, S5_CH), lambda b, c: (b, c, 0)), h0, h0, _full(perm), _full(perm)]
        + [_full(p) for p in prm],
        out_specs=[pl.BlockSpec((None, tc, S5_CH), lambda b, c: (b, c, 0)), st, st],
        out_shape=[jax.ShapeDtypeStruct((b, t, S5_CH), f32), jax.ShapeDtypeStruct((b, 8, S5_N), f32),
                   jax.ShapeDtypeStruct((b, 8, S5_N), f32)],
        scratch_shapes=[pltpu.VMEM((tc, S5_N), f32), pltpu.VMEM((tc, S5_N), f32), pltpu.VMEM((1, S5_N), f32),
                        pltpu.VMEM((1, S5_N), f32)],
        compiler_params=_cp(("parallel", "arbitrary")), name="s5")(u, h0r, h0i, perm, perm.T, *prm)
    return y, sr[:, prow % 8], si[:, prow % 8]


def _gla_kernel(q_ref, k_ref, v_ref, g_ref, sm_ref, s0_ref, wa_ref, ba_ref, gn_ref, o_ref, sn_ref, st, *, nc, t_valid):
    L = CHUNK
    c = pl.program_id(1)

    @pl.when(c == 0)
    def _():
        st[...] = s0_ref[...]

    x = _dotf(sm_ref[...].astype(bf16), wa_ref[...]) + ba_ref[...]
    la = (jnp.minimum(x, 0.0) - jnp.log1p(jnp.exp(-jnp.abs(x)))) * (1.0 / GLA_TEMP)
    k = k_ref[...]
    if t_valid < nc * L:
        valid = c * L + _iota((L, 1), 0) < t_valid
        la = jnp.where(valid, la, 0.0)
        k = jnp.where(valid, k, 0.0)
    causal = _tri(L)
    b = _dot3_l(causal.astype(bf16), la)
    bl = b[L - 1:L, :]
    qe = (q_ref[...] * (GLA_DK ** -0.5) * jnp.exp(b)).astype(bf16)
    ke = (k * jnp.exp(-b)).astype(bf16)
    kl = (k * jnp.exp(bl - b)).astype(bf16)
    ebl = jnp.exp(bl)
    v, gg, gn = v_ref[...], g_ref[...], gn_ref[...]
    for h in range(GLA_HEADS):
        sl = slice(h * GLA_DK, (h + 1) * GLA_DK)
        vl = slice(h * GLA_DV, (h + 1) * GLA_DV)
        att = jnp.where(causal, _dot_nt(qe[:, sl], ke[:, sl]), 0.0)
        vh = v[:, vl].astype(bf16)
        s_t = st[h]
        o = _dotf(att.astype(bf16), vh) + _dot_nt(qe[:, sl], s_t.astype(bf16))
        st[h] = s_t * ebl[:, sl] + _dot_tn(vh, kl[:, sl])
        ms = jnp.mean(o * o, axis=-1, keepdims=True)
        gh = gg[:, vl]
        o_ref[:, vl] = o * lax.rsqrt(ms + EPS) * gn * (gh * _sigmoid(gh))

    @pl.when(c == nc - 1)
    def _():
        sn_ref[...] = st[...]


def _gla(q, k, v, g, sm, s0t, prm, t_valid):
    b, t, _ = q.shape
    nc = t // CHUNK
    tok = lambda w: pl.BlockSpec((None, CHUNK, w), lambda b, c: (b, c, 0))
    st = pl.BlockSpec((None, GLA_HEADS, GLA_DV, GLA_DK), lambda b, c: (b, 0, 0, 0))
    return pl.pallas_call(
        functools.partial(_gla_kernel, nc=nc, t_valid=t_valid), grid=(b, nc),
        in_specs=[tok(256), tok(256), tok(512), tok(512), tok(LANES), st] + [_full(p) for p in prm],
        out_specs=[tok(512), st],
        out_shape=[jax.ShapeDtypeStruct((b, t, 512), f32), jax.ShapeDtypeStruct((b, GLA_HEADS, GLA_DV, GLA_DK), f32)],
        scratch_shapes=[pltpu.VMEM((GLA_HEADS, GLA_DV, GLA_DK), f32)],
        compiler_params=_cp(("parallel", "arbitrary")), name="gla")(q, k, v, g, sm, s0t, *prm)


def _router_kernel(x_ref, sh_ref, sc_ref, g_ref, w_ref, b_ref, h_ref, route_ref, cnt_ref):
    first = (pl.program_id(0) == 0) & (pl.program_id(1) == 0)
    h = _modulate(x_ref[...], sh_ref[...], sc_ref[...], g_ref[...])
    hb = lax.bitcast_convert_type(h.astype(bf16).astype(f32), jnp.uint32)
    h_ref[...] = (hb[:, :D // 2] >> 16) | hb[:, D // 2:]
    hh, hl = _split2(h)
    wh, wl = _split2(w_ref[...])
    lg = _dotf(hh, wh) + _dotf(hl, wh) + _dotf(hh, wl) + b_ref[...]
    lane = _iota(lg.shape, 1)
    big = 4 * LANES
    coarse = lane < MOE_GROUPS
    lc = jnp.where(coarse, lg, GONE)
    mx = jnp.max(lc, axis=1, keepdims=True)
    gsel = jnp.min(jnp.where(lc == mx, lane, big), axis=1, keepdims=True)
    gc = 1.0 / jnp.sum(jnp.where(coarse, jnp.exp(lg - mx), 0.0), axis=1, keepdims=True)
    base = MOE_GROUPS + gsel * MOE_PER_GROUP
    fine = (lane >= base) & (lane < base + MOE_PER_GROUP)
    mf = jnp.max(jnp.where(fine, lg, GONE), axis=1, keepdims=True)
    ef = jnp.where(fine, jnp.exp(lg - mf), 0.0)
    pf = ef / jnp.sum(ef, axis=1, keepdims=True)
    cand = jnp.where(fine, pf, -1.0)
    v1 = jnp.max(cand, axis=1, keepdims=True)
    i1 = jnp.min(jnp.where(cand == v1, lane, big), axis=1, keepdims=True)
    cand = jnp.where(lane == i1, -1.0, cand)
    v2 = jnp.max(cand, axis=1, keepdims=True)
    i2 = jnp.min(jnp.where(cand == v2, lane, big), axis=1, keepdims=True)
    e1, e2 = i1 - MOE_GROUPS, i2 - MOE_GROUPS
    w1, w2 = gc * v1 / (v1 + v2), gc * v2 / (v1 + v2)
    route_ref[...] = jnp.where(lane == 0, e1.astype(f32), jnp.where(lane == 1, e2.astype(f32),
                               jnp.where(lane == 2, w1, jnp.where(lane == 3, w2, 0.0))))
    cnt = jnp.sum((lane == e1).astype(f32) + (lane == e2).astype(f32), axis=0, keepdims=True)

    @pl.when(first)
    def _():
        cnt_ref[...] = jnp.zeros_like(cnt_ref)

    cnt_ref[...] += jnp.broadcast_to(cnt, cnt_ref.shape)


def _router(x, sh, sc, g, wr, br):
    b, t, _ = x.shape
    tm = min(256, t)
    return pl.pallas_call(
        _router_kernel, grid=(b, t // tm),
        in_specs=[pl.BlockSpec((None, tm, D), lambda b, i: (b, i, 0)), _mod_spec(sh, tm), _mod_spec(sc, tm),
                  pl.BlockSpec((1, D), lambda b, i: (0, 0)), _full(wr), _full(br)],
        out_specs=[pl.BlockSpec((None, tm, D // 2), lambda b, i: (b, i, 0)), pl.BlockSpec((None, tm, LANES), lambda b, i: (b, i, 0)),
                   pl.BlockSpec((8, LANES), lambda b, i: (0, 0))],
        out_shape=[jax.ShapeDtypeStruct((b, t, D // 2), jnp.uint32), jax.ShapeDtypeStruct((b, t, LANES), f32),
                   jax.ShapeDtypeStruct((8, LANES), f32)],
        compiler_params=_cp(("arbitrary", "arbitrary")), name="moe_router")(x, sh, sc, g, wr, br)


def _plan_kernel(route_ref, ps_ref, dest_ref, run_s, *, tm):
    @pl.when(pl.program_id(0) == 0)
    def _():
        run_s[...] = jnp.zeros_like(run_s)

    route = route_ref[...]
    lane = _iota((tm, LANES), 1).astype(f32)
    oh1 = (lane == route[:, 0:1]).astype(f32)
    oh2 = (lane == route[:, 1:2]).astype(f32)
    tot = oh1 + oh2
    strict = (_iota((tm, tm), 0) > _iota((tm, tm), 1)).astype(bf16)
    before = _dotf(strict, tot.astype(bf16)) + run_s[0:1, :] + ps_ref[0:1, :]
    d1 = jnp.sum(oh1 * before, axis=1, keepdims=True)
    d2 = jnp.sum(oh2 * before, axis=1, keepdims=True)
    lane_i = _iota((tm, LANES), 1)
    dest_ref[...] = jnp.where(lane_i == 0, d1, jnp.where(lane_i == 1, d2, 0.0)).astype(i32)
    run_s[...] += jnp.broadcast_to(jnp.sum(tot, axis=0, keepdims=True), run_s.shape)


def _plan(route, pstart):
    n = route.shape[0]
    tm = min(256, n)
    return pl.pallas_call(
        functools.partial(_plan_kernel, tm=tm), grid=(n // tm,),
        in_specs=[pl.BlockSpec((tm, LANES), lambda i: (i, 0)), pl.BlockSpec((8, LANES), lambda i: (0, 0))],
        out_specs=pl.BlockSpec((tm, LANES), lambda i: (i, 0)),
        out_shape=jax.ShapeDtypeStruct((n, LANES), i32),
        scratch_shapes=[pltpu.VMEM((8, LANES), f32)],
        compiler_params=_cp(("arbitrary",)), name="moe_plan")(route, pstart)


def _dispatch_kernel(dest_ref, h_ref, xs_in, xs_out, sem, *, tm):
    del xs_in

    def row_copy(r, d):
        return pltpu.make_async_copy(h_ref.at[pl.ds(r, 1), :], xs_out.at[pl.ds(d, 1), :], sem)

    def issue(r, _):
        row_copy(r, dest_ref[2 * r]).start()
        row_copy(r, dest_ref[2 * r + 1]).start()
        return 0

    lax.fori_loop(0, tm, issue, 0, unroll=8)
    for _ in range(2):
        pltpu.make_async_copy(h_ref, xs_out.at[pl.ds(0, tm), :], sem).wait()


def _dispatch(h, dest_flat, nrows):
    n, w = h.shape
    tm = min(256, n)
    return pl.pallas_call(
        functools.partial(_dispatch_kernel, tm=tm), grid=(n // tm,),
        in_specs=[pl.BlockSpec((2 * tm,), lambda i: (i,), memory_space=pltpu.SMEM),
                  pl.BlockSpec((tm, w), lambda i: (i, 0)), pl.BlockSpec(memory_space=pl.ANY)],
        out_specs=pl.BlockSpec(memory_space=pl.ANY),
        out_shape=jax.ShapeDtypeStruct((nrows, w), h.dtype),
        scratch_shapes=[pltpu.SemaphoreType.DMA(())],
        input_output_aliases={2: 0},
        compiler_params=_cp(("arbitrary",)), name="moe_dispatch")(dest_flat, h, jnp.zeros((nrows, w), h.dtype))


def _ffn_kernel(be_ref, nu_ref, x_ref, w1_ref, w3_ref, w2_ref, o_ref):
    i = pl.program_id(0)

    @pl.when(i < nu_ref[0])
    def _():
        xp = x_ref[...]
        x_lo = lax.bitcast_convert_type(xp << 16, f32).astype(bf16)
        x_hi = lax.bitcast_convert_type(xp & jnp.uint32(0xFFFF0000), f32).astype(bf16)
        half = D // 2
        a = _dotf(x_lo, w1_ref[0:half, :].astype(bf16)) + _dotf(x_hi, w1_ref[half:, :].astype(bf16))
        b = _dotf(x_lo, w3_ref[0:half, :].astype(bf16)) + _dotf(x_hi, w3_ref[half:, :].astype(bf16))
        hid = ((a * _sigmoid(a)) * b).astype(bf16)
        o_ref[...] = _dotf(hid, w2_ref[...].astype(bf16))

    @pl.when(i >= nu_ref[0])
    def _():
        o_ref[...] = jnp.zeros_like(o_ref)


def _ffn(xs, blk_e, nused, w1, w3, w2, l):
    nblk = xs.shape[0] // MOE_ROWS
    return pl.pallas_call(
        _ffn_kernel,
        grid_spec=pltpu.PrefetchScalarGridSpec(
            num_scalar_prefetch=2, grid=(nblk,),
            in_specs=[pl.BlockSpec((MOE_ROWS, D // 2), lambda i, be, nu: (i, 0)),
                      pl.BlockSpec((None, None, D, MOE_FF), lambda i, be, nu: (l, be[i], 0, 0)),
                      pl.BlockSpec((None, None, D, MOE_FF), lambda i, be, nu: (l, be[i], 0, 0)),
                      pl.BlockSpec((None, None, MOE_FF, D), lambda i, be, nu: (l, be[i], 0, 0))],
            out_specs=pl.BlockSpec((MOE_ROWS, D), lambda i, be, nu: (i, 0))),
        out_shape=jax.ShapeDtypeStruct((xs.shape[0], D), f32),
        compiler_params=_cp(("arbitrary",)), name="moe_ffn")(blk_e, nused, xs, w1, w3, w2)


def _combine_kernel(dest_ref, x_ref, gate_ref, route_ref, ys_ref, o_ref, buf_a, buf_b, sem, *, tm):
    def row_copy(d, buf, r):
        return pltpu.make_async_copy(ys_ref.at[pl.ds(d, 1), :], buf.at[pl.ds(r, 1), :], sem)

    def issue(r, _):
        row_copy(dest_ref[2 * r], buf_a, r).start()
        row_copy(dest_ref[2 * r + 1], buf_b, r).start()
        return 0

    lax.fori_loop(0, tm, issue, 0, unroll=8)
    for buf in (buf_a, buf_b):
        pltpu.make_async_copy(ys_ref.at[pl.ds(0, tm), :], buf, sem).wait()
    route = route_ref[...]
    o_ref[...] = x_ref[...] + gate_ref[...] * (route[:, 2:3] * buf_a[...] + route[:, 3:4] * buf_b[...])


def _combine(x, gate, route, ys, dest_flat):
    b, t, _ = x.shape
    tm = min(256, t)
    nt = t // tm
    return pl.pallas_call(
        functools.partial(_combine_kernel, tm=tm), grid=(b, nt),
        in_specs=[pl.BlockSpec((2 * tm,), lambda b, i: (b * nt + i,), memory_space=pltpu.SMEM),
                  pl.BlockSpec((None, tm, D), lambda b, i: (b, i, 0)), _mod_spec(gate, tm),
                  pl.BlockSpec((None, tm, LANES), lambda b, i: (b, i, 0)), pl.BlockSpec(memory_space=pl.ANY)],
        out_specs=pl.BlockSpec((None, tm, D), lambda b, i: (b, i, 0)),
        out_shape=jax.ShapeDtypeStruct((b, t, D), f32),
        scratch_shapes=[pltpu.VMEM((tm, D), f32), pltpu.VMEM((tm, D), f32), pltpu.SemaphoreType.DMA(())],
        compiler_params=_cp(("arbitrary", "arbitrary")), name="moe_combine")(dest_flat, x, gate, route, ys)


def _moe(x, sh, sc, gate, g, wr, br, w1, w3, w2, l):
    b, t, _ = x.shape
    n = b * t
    h, route, cnt = _router(x, sh, sc, g, wr, br)
    counts = cnt[0, :MOE_EXPERTS].astype(i32)
    pcounts = (counts + MOE_ROWS - 1) // MOE_ROWS * MOE_ROWS
    pends = jnp.cumsum(pcounts)
    pstarts = pends - pcounts
    nblk = -(-2 * n // MOE_ROWS) + MOE_EXPERTS
    blk_start = jnp.arange(nblk, dtype=i32) * MOE_ROWS
    blk_e = jnp.minimum(jnp.sum((pends[None, :] <= blk_start[:, None]).astype(i32), axis=1), MOE_EXPERTS - 1)
    nused = (pends[-1:] // MOE_ROWS).astype(i32)
    ps = jnp.zeros((8, LANES), f32).at[:, :MOE_EXPERTS].set(pstarts.astype(f32)[None])
    dest = _plan(route.reshape(n, LANES), ps)
    dest_flat = dest[:, :2].reshape(2 * n)
    xs = _dispatch(h.reshape(n, D // 2), dest_flat, nblk * MOE_ROWS)
    ys = _ffn(xs, blk_e, nused, w1, w3, w2, l)
    return _combine(x, gate, route, ys, dest_flat)


def _rope_tables(pos):
    half = HD // 2
    inv = ROPE_THETA ** (-jnp.arange(half, dtype=f32) / half)
    ang = pos.astype(f32)[:, None] * inv[None, :]
    cos, sin = jnp.cos(ang), jnp.sin(ang)
    return jnp.tile(jnp.concatenate([cos, cos], axis=1), (1, 2)), jnp.tile(jnp.concatenate([-sin, sin], axis=1), (1, 2))


def _seg_mats(width):
    nseg = width // HD
    seg = (jnp.arange(width)[:, None] // HD == jnp.arange(LANES)[None, :]).astype(bf16)
    return seg, seg.T


def _even_params(i, ev_w_in, ev_w_out, ev_conv_w, ev_conv_b, ev_dt_bias, ev_a_log, ev_d_skip, ev_ssd_norm, ev_q_norm,
                 ev_k_norm, ev_cmp_w):
    w = ev_w_in[i]
    o = [0, 1024, 2560, 2576, 3600, 5136, 5184]
    small = jnp.concatenate([w[:, o[2]:o[3]], w[:, o[5]:o[6]], jnp.zeros((D, 64), f32)], axis=1)
    ws = [w[:, o[0]:o[1]], w[:, o[1]:o[2]], w[:, o[3]:o[4]], w[:, o[4]:o[5]], small]
    pad = lambda v: jnp.zeros((1, LANES), f32).at[0, :v.shape[0]].set(v)
    ssd = [ev_conv_w[i], ev_conv_b[i][None], pad(ev_dt_bias[i]), pad(-jnp.exp(ev_a_log[i])),
           jnp.repeat(ev_d_skip[i], SSD_HD)[None], ev_ssd_norm[i][None]]
    segq, expq = _seg_mats(D)
    segk, expk = _seg_mats(1536)
    kn = ev_k_norm[i]
    z4 = jnp.zeros((256,), f32)
    gk = jnp.concatenate([jnp.tile(kn[0], 4), z4, jnp.tile(kn[1], 4), z4, jnp.tile(kn[2], 4), z4])[None]
    vm = jnp.concatenate([z4, z4 + 1, z4, z4 + 1, z4, z4 + 1])[None]
    cw = jnp.tile(ev_cmp_w[i], (1, 1, 4))
    prep = [segq, expq, segk, expk, jnp.tile(ev_q_norm[i], 16)[None], gk, vm, cw]
    lane = jnp.arange(LANES)[None, None, :, None]
    col = jnp.arange(256)[None, None, None, :]
    gidx = jnp.arange(NSA_KVH)[:, None, None, None]
    jidx = jnp.arange(3)[None, :, None, None]
    gexp = (lane == 16 + ((gidx * NSA_REP + col // HD) * 3 + jidx)).astype(bf16)
    wo = ev_w_out[i]
    return dict(ws=[a.astype(bf16) for a in ws], ssd=ssd, prep=prep, gexp=gexp, cmp_w=ev_cmp_w[i],
                wo=[wo[:SSD_INNER].astype(bf16), wo[SSD_INNER:].astype(bf16)])


def _odd_params(i, od_w_in, od_w_out, a_re, a_im, log_dt, b_re, b_im, c_re, c_im, d, glu_w, glu_b, wa2, ba, gnorm):
    w = od_w_in[i]
    o = [0, 512, 768, 1024, 1536, 2048, 2064]
    small = jnp.concatenate([w[:, o[5]:o[6]], jnp.zeros((D, LANES - GLA_RANK), f32)], axis=1)
    ws = [w[:, o[k]:o[k + 1]] for k in range(5)] + [small]
    are, aim = a_re[i], a_im[i]
    dt = jnp.exp(log_dt[i])[:, None]
    lr, li = are * dt, aim * dt
    ab_re, ab_im = jnp.exp(lr) * jnp.cos(li), jnp.exp(lr) * jnp.sin(li)
    den = are * are + aim * aim
    nr = ab_re - 1.0
    f_re = (nr * are + ab_im * aim) / den
    f_im = (ab_im * are - nr * aim) / den
    bb_re = f_re[..., None] * b_re[i] - f_im[..., None] * b_im[i]
    bb_im = f_re[..., None] * b_im[i] + f_im[..., None] * b_re[i]
    eye = jnp.eye(S5_GROUPS, dtype=f32)
    bdiag = lambda m: jnp.einsum('gpc,gh->gchp', m, eye).reshape(S5_CH, S5_N)
    cdiag = lambda m: jnp.einsum('gcp,gh->gphc', m, eye).reshape(S5_N, S5_CH)
    bd = jnp.concatenate([bdiag(bb_re), bdiag(bb_im)], axis=1).astype(bf16)
    npow = 32
    kk = jnp.arange(1, npow + 1, dtype=f32)[:, None, None]
    mag, ang = jnp.exp(lr[None] * kk), li[None] * kk
    pwr, pwi = (mag * jnp.cos(ang)).reshape(npow, S5_N), (mag * jnp.sin(ang)).reshape(npow, S5_N)
    s5 = [bd, pwr, pwi, cdiag(c_re[i]).astype(bf16), cdiag(-c_im[i]).astype(bf16), d[i][None],
          glu_w[i].astype(bf16), glu_b[i][None]]
    wa = jnp.zeros((LANES, 256), f32).at[:GLA_RANK].set(wa2[i]).astype(bf16)
    gla = [wa, ba[i][None], gnorm[i][None]]
    wo = od_w_out[i]
    return dict(ws=[a.astype(bf16) for a in ws], s5=s5, gla=gla, wo=[wo[:S5_CH].astype(bf16), wo[S5_CH:].astype(bf16)])


def _pad_t(a, t):
    return jnp.pad(a, ((0, 0), (0, t - a.shape[1])) + ((0, 0),) * (a.ndim - 2))


def _even_prompt(x, sh, sc, gate, g, prm):
    b, t, _ = x.shape
    z, xbc, q, kv, sm = _mod_proj(x, sh, sc, g, prm['ws'], "even_in")
    ya, convn, ssdn = _ssd(xbc, sm, z, jnp.zeros((b, 8, SSD_CONV_DIM), f32),
                           jnp.zeros((b, SSD_HEADS, SSD_HD, SSD_STATE), f32), prm['ssd'], t)
    cos_t, sin_t = _rope_tables(jnp.arange(t))
    (qc, qr, kcmp, vcmp, ksel, vsel, kwin, vwin, gates, kaug, vsa, kwh, vwa, kch, vch, kn2) = _nsa_prep(
        q, kv, sm, cos_t, sin_t, prm['prep'], True)
    nb = t // NSA_BLOCK
    padb = lambda a: jnp.pad(a, ((0, 0), (0, 0), (0, LANES - nb), (0, 0)))
    knorm = jnp.sqrt(jnp.max(kn2[:, :, 0, :2 * NSA_KVH], axis=1)).reshape(b, 2, NSA_KVH)
    kbound = jnp.zeros((b, NSA_KVH, 8, LANES), f32).at[:, :, :, 0:2].set(jnp.swapaxes(knorm, 1, 2)[:, :, None, :])
    ob = _nsa_prompt(qc, qr, padb(kch), padb(vch), kaug, vsa, kwh, vwa, gates, prm['gexp'], kbound)
    xn = _out_proj(x, gate, [ya, ob], prm['wo'], "even_out")
    r5 = lambda a: a.reshape(b, t, NSA_KVH, HD)
    keep = min(NSA_WINDOW, t)
    st = (r5(kcmp), r5(vcmp), r5(ksel), r5(vsel), r5(kwin)[:, t - keep:], r5(vwin)[:, t - keep:], ssdn, convn[:, 5:8])
    return xn, st


def _stack_q(qb, bs, ts):
    q = qb.reshape(bs, ts, NSA_KVH, NSA_REP, HD)
    q = jnp.pad(q, ((0, 0), (0, 8 - ts), (0, 0), (0, 0), (0, 0)))
    q = q.transpose(0, 2, 3, 1, 4).reshape(bs, NSA_KVH, NSA_REP * 8, HD)
    return jnp.pad(q, ((0, 0), (0, 0), (0, LANES - NSA_REP * 8), (0, 0)))


def _unstack_o(o, bs, ts):
    o = o[:, :, :NSA_REP * 8].reshape(bs, NSA_KVH, NSA_REP, 8, HD)[:, :, :, :ts]
    return o.transpose(0, 3, 1, 2, 4).reshape(1, bs * ts, D)


def _page_major(cache):
    l, p, r, h, d = cache.shape
    return jnp.transpose(cache, (0, 1, 3, 4, 2)).reshape(l, p, h * d, r)


def _even_sample(x, sh, sc, gate, g, prm, bs, ts, li, conv_state, ssd_state, page_table, cmp_k, cmp_v, sel_k, sel_v,
                 win_k, win_v):
    n = bs * ts
    npg = page_table.shape[1]
    past = npg * PAGE
    z, xbc, q, kv, sm = _mod_proj(x, sh, sc, g, prm['ws'], "even_in_s")
    seq = lambda a: _pad_t(a.reshape(bs, ts, a.shape[-1]), CHUNK)
    conv0 = jnp.pad(conv_state, ((0, 0), (5, 0), (0, 0)))
    ya, convn, ssdn = _ssd(seq(xbc), seq(sm), seq(z), conv0, ssd_state, prm['ssd'], ts)
    ya = ya[:, :ts].reshape(1, n, SSD_INNER)
    cos_t, sin_t = _rope_tables(past + jnp.arange(n) % ts)
    qc, qr, kcmp, vcmp, ksel, vsel, kwin, vwin, gates = _nsa_prep(q, kv, sm, cos_t, sin_t, prm['prep'], False)
    r8 = lambda a: _pad_t(a.reshape(bs, ts, 256), 8)
    qc_st = _stack_q(qc, bs, ts)
    nr = LANES // NSA_KVH
    qbd = jnp.einsum('bgid,gh->bgihd', _stack_q(qr, bs, ts)[:, :, :nr], jnp.eye(NSA_KVH, dtype=bf16)).reshape(bs, LANES, 256)
    rows_t = lambda a: jnp.pad(jnp.swapaxes(r8(a), 1, 2), ((0, 0), (0, 0), (0, LANES - 8)))
    cw = prm['cmp_w']
    w_t = lambda w: jnp.tile(w.T, (NSA_KVH, PAGE // NSA_BLOCK))
    kct, vct = _page_compress(page_table, _page_major(cmp_k), _page_major(cmp_v), li, w_t(cw[0]), w_t(cw[1]))
    nb_past = 2 * npg
    nbp = -(-(nb_past + 1) // 8) * 8
    hmaj = lambda a: jnp.pad(
        a[..., :2 * PCH].reshape(bs, -1, NSA_KVH, HD, 2 * PCH).transpose(0, 2, 1, 4, 3).reshape(bs, NSA_KVH, nb_past, HD),
        ((0, 0), (0, 0), (0, nbp - nb_past), (0, 0)))
    cidx = jnp.arange(LANES)
    mrep = ((cidx[:, None] < NSA_REP * 8) & (cidx[None, :] < NSA_REP * 8)
            & (cidx[:, None] % 8 == cidx[None, :] % 8)).astype(bf16)
    oc, neg = _sample_select(qc_st, hmaj(kct), hmaj(vct), r8(kcmp), r8(vcmp), cw, mrep, past, ts)
    nch, bpc = npg // PCH, 2 * PCH
    slab = neg[:, :, :nb_past, :nr].reshape(bs, NSA_KVH, nch, bpc, nr).transpose(0, 2, 1, 4, 3).reshape(bs, nch, LANES, bpc)
    slab = jnp.pad(slab, ((0, 0), (0, 0), (0, 0), (0, LANES - bpc))).astype(bf16)
    expand = (jnp.arange(LANES)[:, None] == jnp.arange(PCH * PAGE)[None, :] // NSA_BLOCK).astype(bf16)
    o_s = _sample_selected(page_table, qbd, slab, expand, rows_t(ksel), rows_t(vsel), _page_major(sel_k),
                           _page_major(sel_v), li, ts)
    o_w, wkt, wvt = _sample_window(qbd, _page_major(win_k), _page_major(win_v), li, rows_t(kwin), rows_t(vwin), ts)
    unmajor = lambda a: a.reshape(bs, NSA_KVH, HD, -1).transpose(0, 3, 1, 2)
    wkn, wvn = unmajor(wkt), unmajor(wvt)
    gt = gates[0, :, 16:16 + 3 * NSA_HEADS].reshape(1, n, NSA_HEADS, 3)
    gx = lambda j: jnp.repeat(gt[..., j], HD, axis=-1)
    ob = gx(0) * _unstack_o(oc, bs, ts) + gx(1) * _unstack_o(o_s, bs, ts) + gx(2) * _unstack_o(o_w, bs, ts)
    xn = _out_proj(x, gate, [ya, ob], prm['wo'], "even_out_s")
    r5 = lambda a: a.reshape(bs, ts, NSA_KVH, HD)
    st = (r5(kcmp), r5(vcmp), r5(ksel), r5(vsel), wkn, wvn, ssdn, convn[:, 5:8])
    return xn, st


def _odd_layer(x, sh, sc, gate, g, prm, bs, ts, s5r0, s5i0, gla0):
    u, q, k, v, gg, sm = _mod_proj(x, sh, sc, g, prm['ws'], "odd_in")
    tp = -(-ts // CHUNK) * CHUNK
    seq = lambda a: _pad_t(a.reshape(bs, ts, a.shape[-1]), tp)
    yc, sr, si = _s5(seq(u), s5r0.reshape(bs, 1, S5_N), s5i0.reshape(bs, 1, S5_N), prm['s5'], ts)
    og, gn = _gla(seq(q), seq(k), seq(v), seq(gg), seq(sm), jnp.swapaxes(gla0, 2, 3), prm['gla'], ts)
    unseq = lambda a: a[:, :ts].reshape(x.shape[0], x.shape[1], a.shape[-1])
    xn = _out_proj(x, gate, [unseq(yc), unseq(og)], prm['wo'], "odd_out")
    st = (sr.reshape(bs, S5_GROUPS, S5_STATE), si.reshape(bs, S5_GROUPS, S5_STATE), jnp.swapaxes(gn, 2, 3))
    return xn, st


def kernel(x_prompt, x_sample, cache_cmp_k, cache_cmp_v, cache_sel_k, cache_sel_v, cache_win_k, cache_win_v, state_ssd, state_conv, state_s5_re, state_s5_im, state_gla, page_table, c_prompt, c_sample, ada_w, ada_b, norm_mix, norm_ffn, ev_w_in, ev_w_out, ev_conv_w, ev_conv_b, ev_dt_bias, ev_a_log, ev_d_skip, ev_ssd_norm, ev_q_norm, ev_k_norm, ev_cmp_w, od_w_in, od_w_out, od_s5_a_re, od_s5_a_im, od_s5_log_dt, od_s5_b_re, od_s5_b_im, od_s5_c_re, od_s5_c_im, od_s5_d, od_glu_w, od_glu_b, od_gla_wa2, od_gla_ba, od_gla_norm, moe_wc, moe_bc, moe_wf, moe_bf, moe_w1, moe_w3, moe_w2):
    bp, tp, _ = x_prompt.shape
    bs, ts, _ = x_sample.shape
    ns = bs * ts
    depth = ada_w.shape[0]
    bc = -(-(bp + bs) // 8) * 8
    c_all = jnp.zeros((bc, D), f32).at[:bp].set(c_prompt).at[bp:bp + bs].set(c_sample)
    mods = _ada(c_all, ada_w, ada_b)
    xp, xs = x_prompt, x_sample.reshape(1, ns, D)
    sp = {}
    ss = {}
    for l in range(depth):
        i = l // 2
        mp = [m[:, None, :] for m in jnp.split(mods[l, :bp], 6, axis=-1)]
        ms = [jnp.repeat(m, ts, axis=0)[None] for m in jnp.split(mods[l, bp:bp + bs], 6, axis=-1)]
        gm, gf = norm_mix[l][None], norm_ffn[l][None]
        if l % 2 == 0:
            prm = _even_params(i, ev_w_in, ev_w_out, ev_conv_w, ev_conv_b, ev_dt_bias, ev_a_log, ev_d_skip, ev_ssd_norm,
                               ev_q_norm, ev_k_norm, ev_cmp_w)
            xp, st_p = _even_prompt(xp, mp[0], mp[1], mp[2], gm, prm)
            xs, st_s = _even_sample(xs, ms[0], ms[1], ms[2], gm, prm, bs, ts, i, state_conv[i], state_ssd[i], page_table,
                                    cache_cmp_k, cache_cmp_v, cache_sel_k, cache_sel_v, cache_win_k, cache_win_v)
            names = ('cmp_k', 'cmp_v', 'sel_k', 'sel_v', 'win_k', 'win_v', 'ssd', 'conv')
        else:
            prm = _odd_params(i, od_w_in, od_w_out, od_s5_a_re, od_s5_a_im, od_s5_log_dt, od_s5_b_re, od_s5_b_im,
                              od_s5_c_re, od_s5_c_im, od_s5_d, od_glu_w, od_glu_b, od_gla_wa2, od_gla_ba, od_gla_norm)
            zs = jnp.zeros((bp, S5_GROUPS, S5_STATE), f32)
            xp, st_p = _odd_layer(xp, mp[0], mp[1], mp[2], gm, prm, bp, tp, zs, zs,
                                  jnp.zeros((bp, GLA_HEADS, GLA_DK, GLA_DV), f32))
            xs, st_s = _odd_layer(xs, ms[0], ms[1], ms[2], gm, prm, bs, ts, state_s5_re[i], state_s5_im[i], state_gla[i])
            names = ('s5_re', 's5_im', 'gla')
        for nm, a_p, a_s in zip(names, st_p, st_s):
            sp.setdefault(nm, []).append(a_p)
            ss.setdefault(nm, []).append(a_s)
        wr = jnp.zeros((D, LANES), f32).at[:, :MOE_GROUPS].set(moe_wc[l]).at[:, MOE_GROUPS:MOE_GROUPS + MOE_EXPERTS].set(moe_wf[l])
        br = jnp.zeros((1, LANES), f32).at[0, :MOE_GROUPS].set(moe_bc[l]).at[0, MOE_GROUPS:MOE_GROUPS + MOE_EXPERTS].set(moe_bf[l])
        xp = _moe(xp, mp[3], mp[4], mp[5], gf, wr, br, moe_w1, moe_w3, moe_w2, l)
        xs = _moe(xs, ms[3], ms[4], ms[5], gf, wr, br, moe_w1, moe_w3, moe_w2, l)
    order = ('cmp_k', 'cmp_v', 'sel_k', 'sel_v', 'win_k', 'win_v', 'ssd', 'conv', 's5_re', 's5_im', 'gla')
    outs = [xp, xs.reshape(bs, ts, D)]
    for nm in order:
        outs += [jnp.stack(sp[nm]), jnp.stack(ss[nm])]
    return tuple(outs)
```

```python
import functools
import math

import jax
import jax.numpy as jnp
from jax import lax
from jax.experimental import pallas as pl
from jax.experimental.pallas import tpu as pltpu

f32 = jnp.float32
bf16 = jnp.bfloat16
i32 = jnp.int32

D = 1024
PAGE = 128
SSD_HEADS, SSD_HD, SSD_INNER, SSD_GROUPS, SSD_STATE, SSD_CONV = 16, 64, 1024, 4, 64, 4
SSD_CONV_DIM = SSD_INNER + 2 * SSD_GROUPS * SSD_STATE
NSA_HEADS, NSA_KVH, NSA_REP, HD, NSA_BLOCK, NSA_TOPN, NSA_WINDOW = 16, 4, 4, 64, 64, 16, 512
ROPE_THETA = 10000.0
S5_CH, S5_GCH, S5_GROUPS, S5_STATE = 512, 16, 32, 64
S5_N = S5_GROUPS * S5_STATE
GLA_HEADS, GLA_DK, GLA_DV, GLA_RANK, GLA_TEMP = 4, 64, 128, 16, 16.0
MOE_GROUPS, MOE_PER_GROUP, MOE_EXPERTS, MOE_FF = 4, 8, 32, 256
MOE_ROWS = 256
MOE_MOVE_ROWS = 512
EPS = 1e-6
NEG = -1e30
BIG = 1e30
GONE = -3e38
LANES = 128
CHUNK = 128
VT_ROWS = HD + 16
FLASH_GROUP = 2
FIXED_STAB_MAX = 40.0
VMEM_LIMIT = 48 * 2**20


def _cp(sem, vmem=VMEM_LIMIT):
    return pltpu.CompilerParams(dimension_semantics=sem, vmem_limit_bytes=vmem)


def _sigmoid(x):
    return 1.0 / (1.0 + jnp.exp(-x))


def _softplus(x):
    return jnp.maximum(x, 0.0) + jnp.log1p(jnp.exp(-jnp.abs(x)))


def _dotf(a, b):
    return jnp.dot(a, b, preferred_element_type=f32)


def _dot_nt(a, b):
    return lax.dot_general(a, b, (((1,), (1,)), ((), ())), preferred_element_type=f32)


def _dot_tn(a, b):
    return lax.dot_general(a, b, (((0,), (0,)), ((), ())), preferred_element_type=f32)


def _split2(a):
    h = a.astype(bf16)
    return h, (a - h.astype(f32)).astype(bf16)


def _split3(a):
    h = a.astype(bf16)
    r = a - h.astype(f32)
    m = r.astype(bf16)
    return h, m, (r - m.astype(f32)).astype(bf16)


def _dot3_r(a, w):
    h, m, l = _split3(a)
    return _dotf(h, w) + _dotf(m, w) + _dotf(l, w)


def _dot3_l(w, a):
    h, m, l = _split3(a)
    return _dotf(w, h) + _dotf(w, m) + _dotf(w, l)


def _iota(shape, dim):
    return lax.broadcasted_iota(i32, shape, dim)


def _tri(n):
    return _iota((n, n), 0) >= _iota((n, n), 1)


def _modulate(x, sh, sc, g):
    ms = jnp.mean(x * x, axis=-1, keepdims=True)
    return (x * lax.rsqrt(ms + EPS) * g) * (1.0 + sc) + sh


def _mod_spec(mod, tm):
    if mod.shape[1] == 1:
        return pl.BlockSpec((None, 1, D), lambda b, i: (b, 0, 0))
    return pl.BlockSpec((None, tm, D), lambda b, i: (b, i, 0))


def _full(a):
    n = a.ndim
    return pl.BlockSpec(a.shape, lambda *_: (0,) * n)


def _ada_kernel(c_ref, w_ref, b_ref, o_ref):
    c = c_ref[...]
    a = c * _sigmoid(c)
    ah, al = _split2(a)
    wh, wl = _split2(w_ref[...])
    o_ref[...] = _dotf(ah, wh) + _dotf(al, wh) + _dotf(ah, wl) + b_ref[...]


def _ada(c_all, ada_w, ada_b):
    depth, bc, tn = ada_w.shape[0], c_all.shape[0], 1536
    return pl.pallas_call(
        _ada_kernel, grid=(depth, 6 * D // tn),
        in_specs=[pl.BlockSpec((bc, D), lambda l, j: (0, 0)),
                  pl.BlockSpec((None, D, tn), lambda l, j: (l, 0, j)),
                  pl.BlockSpec((None, 1, tn), lambda l, j: (l, 0, j))],
        out_specs=pl.BlockSpec((None, bc, tn), lambda l, j: (l, 0, j)),
        out_shape=jax.ShapeDtypeStruct((depth, bc, 6 * D), f32),
        compiler_params=_cp(("parallel", "parallel")), name="ada")(c_all, ada_w, ada_b.reshape(depth, 1, 6 * D))


def _proj_kernel(x_ref, sh_ref, sc_ref, g_ref, *refs, nseg):
    h = _modulate(x_ref[...], sh_ref[...], sc_ref[...], g_ref[...]).astype(bf16)
    for i in range(nseg):
        refs[nseg + i][...] = _dotf(h, refs[i][...])


def _mod_proj(x, sh, sc, g, ws, name):
    b, t, _ = x.shape
    tm = min(256, t)
    nseg = len(ws)
    return pl.pallas_call(
        functools.partial(_proj_kernel, nseg=nseg), grid=(b, t // tm),
        in_specs=[pl.BlockSpec((None, tm, D), lambda b, i: (b, i, 0)), _mod_spec(sh, tm), _mod_spec(sc, tm),
                  pl.BlockSpec((1, D), lambda b, i: (0, 0))] + [_full(w) for w in ws],
        out_specs=[pl.BlockSpec((None, tm, w.shape[1]), lambda b, i: (b, i, 0)) for w in ws],
        out_shape=[jax.ShapeDtypeStruct((b, t, w.shape[1]), f32) for w in ws],
        compiler_params=_cp(("parallel", "parallel")), name=name)(x, sh, sc, g, *ws)


def _outproj_kernel(x_ref, gate_ref, *refs, nseg):
    acc = _dotf(refs[0][...].astype(bf16), refs[nseg][...])
    for i in range(1, nseg):
        acc = acc + _dotf(refs[i][...].astype(bf16), refs[nseg + i][...])
    refs[2 * nseg][...] = x_ref[...] + gate_ref[...] * acc


def _out_proj(x, gate, acts, ws, name):
    b, t, _ = x.shape
    tm = min(512, t)
    nseg = len(acts)
    return pl.pallas_call(
        functools.partial(_outproj_kernel, nseg=nseg), grid=(b, t // tm),
        in_specs=[pl.BlockSpec((None, tm, D), lambda b, i: (b, i, 0)), _mod_spec(gate, tm)]
        + [pl.BlockSpec((None, tm, a.shape[2]), lambda b, i: (b, i, 0)) for a in acts] + [_full(w) for w in ws],
        out_specs=pl.BlockSpec((None, tm, D), lambda b, i: (b, i, 0)),
        out_shape=jax.ShapeDtypeStruct((b, t, D), f32),
        compiler_params=_cp(("parallel", "parallel")), name=name)(x, gate, *acts, *ws)


def _ssd_kernel(xbc_ref, dt_ref, z_ref, conv0_ref, h0_ref, cw_ref, cb_ref, dtb_ref, a_ref, dsk_ref, nrm_ref,
                y_ref, convn_ref, hn_ref, xpad, hst, ybuf, *, nc, t_valid):
    L = CHUNK
    c = pl.program_id(1)

    @pl.when(c == 0)
    def _():
        hst[...] = h0_ref[...]
        xpad[0:8, :] = conv0_ref[...]

    xpad[8:8 + L, :] = xbc_ref[...]
    acc = cb_ref[...] + cw_ref[0:1, :] * xpad[5:5 + L, :]
    for k in range(1, SSD_CONV):
        acc = acc + cw_ref[k:k + 1, :] * xpad[5 + k:5 + k + L, :]
    tv_last = t_valid - (nc - 1) * L
    convn_ref[...] = xpad[tv_last:tv_last + 8, :]
    xpad[0:8, :] = xpad[L:L + 8, :]

    xc = acc * _sigmoid(acc)
    xs = xc[:, :SSD_INNER]
    bm = xc[:, SSD_INNER:SSD_INNER + 256]
    cm = xc[:, SSD_INNER + 256:]
    dt = _softplus(dt_ref[...] + dtb_ref[...])
    if t_valid < nc * L:
        dt = jnp.where(c * L + _iota((L, LANES), 0) < t_valid, dt, 0.0)
    causal = _tri(L)
    trib = causal.astype(bf16)
    cs = _dot3_l(trib, dt * a_ref[...])
    cs_t, dt_t = cs.T, dt.T
    wend_t = jnp.exp(cs_t[:, L - 1:L] - cs_t) * dt_t
    ecs = jnp.exp(cs)
    xs_t = xs.T
    for g in range(SSD_GROUPS):
        bg = bm[:, g * 64:(g + 1) * 64].astype(bf16)
        cg = cm[:, g * 64:(g + 1) * 64].astype(bf16)
        gmat = _dot_nt(cg, bg)
        for r in range(SSD_HEADS // SSD_GROUPS):
            h = g * (SSD_HEADS // SSD_GROUPS) + r
            seg = cs[:, h:h + 1] - cs_t[h:h + 1, :]
            dec = jnp.where(causal, jnp.exp(jnp.where(causal, seg, 0.0)), 0.0)
            sc = (gmat * dec * dt_t[h:h + 1, :]).astype(bf16)
            hs = hst[h]
            yh = _dotf(sc, xs[:, h * 64:(h + 1) * 64].astype(bf16)) + _dot_nt(cg, hs.astype(bf16)) * ecs[:, h:h + 1]
            ybuf[:, h * 64:(h + 1) * 64] = yh
            xw = (xs_t[h * 64:(h + 1) * 64, :] * wend_t[h:h + 1, :]).astype(bf16)
            hst[h] = hs * ecs[L - 1:L, h:h + 1] + _dotf(xw, bg)
    zz = z_ref[...]
    y = (ybuf[...] + dsk_ref[...] * xs) * (zz * _sigmoid(zz))
    gw = SSD_INNER // SSD_GROUPS
    for g in range(SSD_GROUPS):
        s = y[:, g * gw:(g + 1) * gw]
        ms = jnp.mean(s * s, axis=-1, keepdims=True)
        y_ref[:, g * gw:(g + 1) * gw] = s * lax.rsqrt(ms + EPS) * nrm_ref[:, g * gw:(g + 1) * gw]

    @pl.when(c == nc - 1)
    def _():
        hn_ref[...] = hst[...]


def _ssd(xbc, sm, z, conv0, h0, prm, t_valid):
    b, t, _ = xbc.shape
    nc = t // CHUNK
    tok = lambda w: pl.BlockSpec((None, CHUNK, w), lambda b, c: (b, c, 0))
    return pl.pallas_call(
        functools.partial(_ssd_kernel, nc=nc, t_valid=t_valid), grid=(b, nc),
        in_specs=[tok(SSD_CONV_DIM), tok(LANES), tok(SSD_INNER),
                  pl.BlockSpec((None, 8, SSD_CONV_DIM), lambda b, c: (b, 0, 0)),
                  pl.BlockSpec((None, SSD_HEADS, SSD_HD, SSD_STATE), lambda b, c: (b, 0, 0, 0))]
        + [_full(p) for p in prm],
        out_specs=[tok(SSD_INNER), pl.BlockSpec((None, 8, SSD_CONV_DIM), lambda b, c: (b, 0, 0)),
                   pl.BlockSpec((None, SSD_HEADS, SSD_HD, SSD_STATE), lambda b, c: (b, 0, 0, 0))],
        out_shape=[jax.ShapeDtypeStruct((b, t, SSD_INNER), f32), jax.ShapeDtypeStruct((b, 8, SSD_CONV_DIM), f32),
                   jax.ShapeDtypeStruct((b, SSD_HEADS, SSD_HD, SSD_STATE), f32)],
        scratch_shapes=[pltpu.VMEM((CHUNK + 8, SSD_CONV_DIM), f32), pltpu.VMEM((SSD_HEADS, SSD_HD, SSD_STATE), f32),
                        pltpu.VMEM((CHUNK, SSD_INNER), f32)],
        compiler_params=_cp(("parallel", "arbitrary")), name="ssd")(xbc, sm, z, conv0, h0, *prm)


def _seg_rinv(x, seg_ref, exp_ref):
    x2 = x * x
    h, l = _split2(x2)
    ss = _dotf(h, seg_ref[...]) + _dotf(l, seg_ref[...])
    return _dot3_r(lax.rsqrt(ss * (1.0 / HD) + EPS), exp_ref[...])


def _rope(x, cosf, sinf):
    w = x.shape[1]
    first = (_iota(x.shape, 1) % HD) < (HD // 2)
    rot = jnp.where(first, pltpu.roll(x, w - HD // 2, 1), pltpu.roll(x, HD // 2, 1))
    return x * cosf + rot * sinf


def _nsaprep_kernel(q_ref, kv_ref, sm_ref, cos_ref, sin_ref, segq_ref, expq_ref, segk_ref, expk_ref, gq_ref, gk_ref,
                    vm_ref, cw_ref, *outs, tm, compress):
    qc_ref, qr_ref, kcmp_ref, vcmp_ref, ksel_ref, vsel_ref, kwin_ref, vwin_ref, gate_ref = outs[:9]
    i = pl.program_id(1)
    cos1, sin1 = cos_ref[...], sin_ref[...]
    q = q_ref[...]
    qn = q * _seg_rinv(q, segq_ref, expq_ref) * gq_ref[...]
    qc_ref[...] = (qn * 0.125).astype(bf16)
    qr_ref[...] = (_rope(qn, jnp.concatenate([cos1] * 8, axis=1), jnp.concatenate([sin1] * 8, axis=1)) * 0.125).astype(bf16)
    kv = kv_ref[...]
    kvn = kv * (_seg_rinv(kv, segk_ref, expk_ref) * gk_ref[...] + vm_ref[...])
    cos2, sin2 = jnp.concatenate([cos1] * 2, axis=1), jnp.concatenate([sin1] * 2, axis=1)
    kcmp, vcmp = kvn[:, 0:256], kvn[:, 256:512]
    ksel, vsel = _rope(kvn[:, 512:768], cos2, sin2), kvn[:, 768:1024]
    kwin, vwin = _rope(kvn[:, 1024:1280], cos2, sin2), kvn[:, 1280:1536]
    kcmp_ref[...] = kcmp
    vcmp_ref[...] = vcmp
    ksel_ref[...] = ksel
    vsel_ref[...] = vsel
    kwin_ref[...] = kwin
    vwin_ref[...] = vwin
    gate_ref[...] = _sigmoid(sm_ref[...])
    if compress:
        kaug_ref, vsa_ref, kwh_ref, vwa_ref, kch_ref, vch_ref, kn2_ref = outs[9:]
        sq = lambda a: jnp.square(a.astype(bf16).astype(f32))
        n2 = lambda a, seg: jnp.max(_dot3_r(sq(a), seg), axis=0, keepdims=True)
        kn2_ref[...] = jnp.broadcast_to(n2(ksel, segq_ref[0:256, :]) + n2(kwin, segq_ref[256:512, :]), (8, LANES))
        blk = (i * tm + _iota((tm, LANES), 0)) // NSA_BLOCK
        onehot = (_iota((tm, LANES), 1) == blk).astype(bf16)
        ones_t = (_iota((VT_ROWS - HD, tm), 0) == 0).astype(bf16)
        zero = jnp.zeros((tm, HD), bf16)
        kcw = (kcmp.reshape(tm // NSA_BLOCK, NSA_BLOCK, 256) * cw_ref[0][None]).sum(axis=1)
        vcw = (vcmp.reshape(tm // NSA_BLOCK, NSA_BLOCK, 256) * cw_ref[1][None]).sum(axis=1)
        vsel_t, vwin_t = vsel.T, vwin.T
        for g in range(NSA_KVH):
            sl = slice(g * HD, (g + 1) * HD)
            kaug_ref[g, :, 0:HD] = ksel[:, sl].astype(bf16)
            kaug_ref[g, :, HD:2 * HD] = zero
            kaug_ref[g, :, 2 * HD:] = onehot
            vsa_ref[g, 0:HD, :] = vsel_t[sl, :].astype(bf16)
            vsa_ref[g, HD:, :] = ones_t
            kwh_ref[g] = kwin[:, sl].astype(bf16)
            vwa_ref[g, 0:HD, :] = vwin_t[sl, :].astype(bf16)
            vwa_ref[g, HD:, :] = ones_t
            kch_ref[g] = kcw[:, sl]
            vch_ref[g] = vcw[:, sl]


def _nsa_prep(q, kv, sm, cos_t, sin_t, prm, compress):
    b, t, _ = q.shape
    tm = min(512, t)
    tok = lambda w: pl.BlockSpec((None, tm, w), lambda b, i: (b, i, 0))
    tab = pl.BlockSpec((tm, LANES), lambda b, i: (i, 0))
    hm = lambda w: pl.BlockSpec((None, NSA_KVH, tm, w), lambda b, i: (b, 0, i, 0))
    out_specs = [tok(D), tok(D)] + [tok(256)] * 6 + [tok(LANES)]
    out_shape = [jax.ShapeDtypeStruct((b, t, D), bf16)] * 2 + [jax.ShapeDtypeStruct((b, t, 256), f32)] * 6 \
        + [jax.ShapeDtypeStruct((b, t, LANES), f32)]
    if compress:
        nbt = tm // NSA_BLOCK
        cspec = pl.BlockSpec((None, NSA_KVH, nbt, HD), lambda b, i: (b, 0, i, 0))
        vt = pl.BlockSpec((None, NSA_KVH, VT_ROWS, tm), lambda b, i: (b, 0, 0, i))
        out_specs += [hm(256), vt, hm(HD), vt, cspec, cspec, pl.BlockSpec((None, None, 8, LANES), lambda b, i: (b, i, 0, 0))]
        out_shape += [jax.ShapeDtypeStruct((b, NSA_KVH, t, 256), bf16), jax.ShapeDtypeStruct((b, NSA_KVH, VT_ROWS, t), bf16),
                      jax.ShapeDtypeStruct((b, NSA_KVH, t, HD), bf16), jax.ShapeDtypeStruct((b, NSA_KVH, VT_ROWS, t), bf16),
                      jax.ShapeDtypeStruct((b, NSA_KVH, t // NSA_BLOCK, HD), f32),
                      jax.ShapeDtypeStruct((b, NSA_KVH, t // NSA_BLOCK, HD), f32),
                      jax.ShapeDtypeStruct((b, t // tm, 8, LANES), f32)]
    return pl.pallas_call(
        functools.partial(_nsaprep_kernel, tm=tm, compress=compress), grid=(b, t // tm),
        in_specs=[tok(D), tok(1536), tok(LANES), tab, tab] + [_full(p) for p in prm],
        out_specs=out_specs, out_shape=out_shape,
        compiler_params=_cp(("parallel", "parallel")), name="nsa_prep")(q, kv, sm, cos_t, sin_t, *prm)


def _cmp_branch(q_heads, kc, vc, tpos):
    nb, nq = kc.shape[0], tpos.shape[1]
    nrow = _iota((nb, nq), 0)
    ok = ((nrow + 1) * NSA_BLOCK - 1) <= tpos
    imp = jnp.zeros((nb, nq), f32)
    outs = []
    for q in q_heads:
        s = jnp.where(ok, _dot_nt(kc, q), NEG)
        m = jnp.max(s, axis=0, keepdims=True)
        e = jnp.where(ok, jnp.exp(s - m), 0.0)
        l = jnp.sum(e, axis=0, keepdims=True)
        p = e / jnp.where(l > 0.0, l, 1.0)
        imp = imp + p
        outs.append(_dot_tn(p.astype(bf16), vc))
    cur = tpos // NSA_BLOCK
    impm = jnp.where((nrow == cur) | (nrow == 0), BIG, jnp.where(nrow < cur, imp, NEG))

    def pick(_, v):
        mx = jnp.max(v, axis=0, keepdims=True)
        idx = jnp.min(jnp.where(v == mx, nrow, nb), axis=0, keepdims=True)
        return jnp.where(nrow == idx, GONE, v)

    taken = lax.fori_loop(0, NSA_TOPN, pick, impm) == GONE
    neg = jnp.where(taken & (impm > 0.5 * NEG), 0.0, NEG)
    return outs, neg


def _flash_t(lhs, k_ref, vt_ref, lo, hi, trow, tk, m_ref, acc_ref, window):
    m_ref[...] = jnp.full(m_ref.shape, NEG, f32)
    acc_ref[...] = jnp.zeros(acc_ref.shape, f32)

    def step(kt, masked):
        ks = pl.multiple_of(kt * tk, tk)
        s = _dot_nt(k_ref[pl.ds(ks, tk), :], lhs)
        if masked:
            kpos = ks + _iota((tk, 1), 0)
            keep = kpos <= trow
            if window:
                keep = keep & (kpos > trow - NSA_WINDOW)
            s = jnp.where(keep, s, NEG)
        m_old = m_ref[...]
        m_new = jnp.maximum(m_old, jnp.max(s, axis=0, keepdims=True))
        p = jnp.exp(s - m_new).astype(bf16)
        acc_ref[...] = jnp.exp(m_old - m_new) * acc_ref[...] + _dotf(vt_ref[:, pl.ds(ks, tk)], p)
        m_ref[...] = m_new

    def body(masked):
        def f(kt, _):
            step(kt, masked)
            return 0
        return f

    if window:
        lax.fori_loop(lo, hi, body(True), 0)
    else:
        lax.fori_loop(lo, hi - 1, body(False), 0)
        step(hi - 1, True)
    acc = acc_ref[...]
    return acc[0:HD, :] / acc[HD:HD + 1, :]


def _flash_fixed(lhs, k_ref, vt_ref, lo, hi, trow, tk, mrow, acc_ref, window):
    acc_ref[...] = jnp.zeros(acc_ref.shape, f32)

    def probs(kt, masked):
        ks = pl.multiple_of(kt * tk, tk)
        s = _dot_nt(k_ref[pl.ds(ks, tk), :], lhs)
        if masked:
            kpos = ks + _iota((tk, 1), 0)
            keep = kpos <= trow
            if window:
                keep = keep & (kpos > trow - NSA_WINDOW)
            s = jnp.where(keep, s, NEG)
        return jnp.exp(s - mrow).astype(bf16)

    def group(first, masks):
        pv = None
        for j, masked in enumerate(masks):
            ks = pl.multiple_of((first + j) * tk, tk)
            d = _dotf(vt_ref[:, pl.ds(ks, tk)], probs(first + j, masked))
            pv = d if pv is None else pv + d
        acc_ref[...] += pv

    def single(masked):
        def f(kt, _):
            group(kt, (masked,))
            return 0
        return f

    if window:
        full = hi - lo == 3

        @pl.when(full)
        def _():
            group(lo, (True, NSA_WINDOW != 2 * tk or lhs.shape[0] != NSA_REP * tk, True))

        @pl.when(jnp.logical_not(full))
        def _():
            lax.fori_loop(lo, hi, single(True), 0)
    else:
        ngrp = (hi - 1 - lo) // FLASH_GROUP

        def grp(j, _):
            group(lo + FLASH_GROUP * j, (False,) * FLASH_GROUP)
            return 0

        lax.fori_loop(0, ngrp, grp, 0)
        lax.fori_loop(lo + FLASH_GROUP * ngrp, hi - 1, single(False), 0)
        group(hi - 1, (True,))
    acc = acc_ref[...]
    return acc[0:HD, :] / acc[HD:HD + 1, :]


def _nsa_kernel(qc_ref, qr_ref, kc_ref, vc_ref, kaug_ref, vsa_ref, kw_ref, vwa_ref, gate_ref, gexp_ref, kb_ref, o_ref,
                lhs_ref, m_ref, acc_ref, os_ref, ow_ref, oc_ref, *, tq, tk):
    qi = pl.program_id(2)
    t0 = qi * tq
    tpos = t0 + _iota((1, tq), 1)
    qh = [qc_ref[:, r * HD:(r + 1) * HD] for r in range(NSA_REP)]
    nbp = kc_ref.shape[0]
    need = (t0 + tq) // NSA_BLOCK
    sizes = [r for r in (32, 64) if r < nbp] + [nbp]
    for j, nr in enumerate(sizes):
        fits = need <= nr if j == 0 else (need > sizes[j - 1]) if nr == nbp else (need > sizes[j - 1]) & (need <= nr)

        @pl.when(fits)
        def _():
            oc, neg = _cmp_branch(qh, kc_ref[0:nr, :].astype(bf16), vc_ref[0:nr, :].astype(bf16), tpos)
            if nr < nbp:
                neg = jnp.concatenate([neg, jnp.full((nbp - nr, tq), NEG, f32)], axis=0)
            negq = neg.T.astype(bf16)
            oc_ref[...] = jnp.concatenate(oc, axis=1)
            for r in range(NSA_REP):
                lhs_ref[r * tq:(r + 1) * tq, 2 * HD:] = negq

    for r in range(NSA_REP):
        rows = slice(r * tq, (r + 1) * tq)
        lhs_ref[rows, 0:HD] = qr_ref[:, r * HD:(r + 1) * HD]
        lhs_ref[rows, HD:2 * HD] = jnp.zeros((tq, HD), bf16)
    trow = t0 + _iota((1, NSA_REP * tq), 1) % tq
    hi = (t0 + tq) // tk
    lo = jnp.maximum(t0 - (NSA_WINDOW - 1), 0) // tk
    qf = lhs_ref[:, 0:HD].astype(f32)
    qh, ql = _split2(qf * qf)
    ones = jnp.ones((8, HD), bf16)
    qn = jnp.sqrt((_dot_nt(ones, qh) + _dot_nt(ones, ql))[0:1, :]) * 1.02 + 1e-6
    kb = kb_ref[...]
    for k_ref, vt_ref, kmax, out_ref, lo_b, win in ((kaug_ref, vsa_ref, kb[0:1, 0:1], os_ref, 0, False),
                                                     (kw_ref, vwa_ref, kb[0:1, 1:2], ow_ref, lo, True)):
        lhs = lhs_ref[:, 0:HD] if win else lhs_ref[...]
        mrow = qn * kmax
        safe = jnp.max(mrow) <= FIXED_STAB_MAX

        @pl.when(safe)
        def _():
            out_ref[...] = _flash_fixed(lhs, k_ref, vt_ref, lo_b, hi, trow, tk, mrow, acc_ref, win)

        @pl.when(jnp.logical_not(safe))
        def _():
            out_ref[...] = _flash_t(lhs, k_ref, vt_ref, lo_b, hi, trow, tk, m_ref, acc_ref, win)

    os_t, ow_t = os_ref[...], ow_ref[...]
    gates = gate_ref[...]
    unstack = lambda a: jnp.concatenate([a[:, r * tq:(r + 1) * tq] for r in range(NSA_REP)], axis=0).T
    o = _dot3_r(gates, gexp_ref[0]) * oc_ref[...] \
        + _dot3_r(gates, gexp_ref[1]) * unstack(os_t) + _dot3_r(gates, gexp_ref[2]) * unstack(ow_t)
    o_ref[...] = o.astype(bf16)


def _nsa_prompt(qc, qr, kch, vch, kaug, vsa, kwh, vwa, gates, gexp, kbound):
    b, t, _ = qc.shape
    tq = tk = min(256, t)
    nbp = kch.shape[2]
    qspec = pl.BlockSpec((None, tq, 256), lambda b, g, i: (b, i, g))
    kvspec = lambda n, w: pl.BlockSpec((None, None, n, w), lambda b, g, i: (b, g, 0, 0))
    return pl.pallas_call(
        functools.partial(_nsa_kernel, tq=tq, tk=tk), grid=(b, NSA_KVH, t // tq),
        in_specs=[qspec, qspec, kvspec(nbp, HD), kvspec(nbp, HD), kvspec(t, 256), kvspec(VT_ROWS, t), kvspec(t, HD),
                  kvspec(VT_ROWS, t), pl.BlockSpec((None, tq, LANES), lambda b, g, i: (b, i, 0)),
                  pl.BlockSpec((None, 3, LANES, 256), lambda b, g, i: (g, 0, 0, 0)), kvspec(8, LANES)],
        out_specs=qspec, out_shape=jax.ShapeDtypeStruct((b, t, D), bf16),
        scratch_shapes=[pltpu.VMEM((NSA_REP * tq, 256), bf16), pltpu.VMEM((1, NSA_REP * tq), f32),
                        pltpu.VMEM((VT_ROWS, NSA_REP * tq), f32), pltpu.VMEM((HD, NSA_REP * tq), f32),
                        pltpu.VMEM((HD, NSA_REP * tq), f32), pltpu.VMEM((tq, 256), f32)],
        compiler_params=_cp(("parallel", "parallel", "arbitrary")), name="nsa_prompt")(
            qc, qr, kch, vch, kaug, vsa, kwh, vwa, gates, gexp, kbound)


PCH = 8
NSLOT = 4


def _chunk_copies(cache_ref, li, pt_ref, b, c, buf, slot, sem):
    return [pltpu.make_async_copy(cache_ref.at[li, pt_ref[b, c * PCH + p]], buf.at[slot, p], sem.at[slot])
            for p in range(PCH)]


def _pagecmp_kernel(pt_ref, wk_ref, wv_ref, place_ref, ck_ref, cv_ref, kc_ref, vc_ref, kbuf, vbuf, sem, *, li, npg):
    b = pl.program_id(0)
    nch = npg // PCH

    def copies(c, slot):
        return _chunk_copies(ck_ref, li, pt_ref, b, c, kbuf, slot, sem) + _chunk_copies(cv_ref, li, pt_ref, b, c, vbuf, slot, sem)

    for c0 in range(min(NSLOT - 1, nch)):
        for cp in copies(c0, c0):
            cp.start()

    def body(c, _):
        slot = c % NSLOT
        ahead = c + NSLOT - 1

        @pl.when(ahead < nch)
        def _():
            for cp in copies(ahead, ahead % NSLOT):
                cp.start()

        for cp in copies(c, slot):
            cp.wait()
        ka = _dotf((kbuf[slot, 0] * wk_ref[...]).astype(bf16), place_ref[0])
        va = _dotf((vbuf[slot, 0] * wv_ref[...]).astype(bf16), place_ref[0])
        for p in range(1, PCH):
            ka = ka + _dotf((kbuf[slot, p] * wk_ref[...]).astype(bf16), place_ref[p])
            va = va + _dotf((vbuf[slot, p] * wv_ref[...]).astype(bf16), place_ref[p])
        kc_ref[c] = ka
        vc_ref[c] = va
        return 0

    lax.fori_loop(0, nch, body, 0)


def _page_compress(page_table, cache_k, cache_v, li, wk_t, wv_t):
    bs, npg = page_table.shape
    nch = npg // PCH
    place = (jnp.arange(LANES)[None, None, :]
             == 2 * jnp.arange(PCH)[:, None, None] + jnp.arange(PAGE)[None, :, None] // NSA_BLOCK).astype(bf16)
    out = pl.BlockSpec((None, nch, 256, LANES), lambda b, pt: (b, 0, 0, 0))
    shp = jax.ShapeDtypeStruct((bs, nch, 256, LANES), f32)
    anyspec = pl.BlockSpec(memory_space=pl.ANY)
    return pl.pallas_call(
        functools.partial(_pagecmp_kernel, li=li, npg=npg),
        grid_spec=pltpu.PrefetchScalarGridSpec(
            num_scalar_prefetch=1, grid=(bs,),
            in_specs=[pl.BlockSpec((256, PAGE), lambda b, pt: (0, 0)), pl.BlockSpec((256, PAGE), lambda b, pt: (0, 0)),
                      pl.BlockSpec((PCH, PAGE, LANES), lambda b, pt: (0, 0, 0)), anyspec, anyspec],
            out_specs=[out, out],
            scratch_shapes=[pltpu.VMEM((NSLOT, PCH, 256, PAGE), f32), pltpu.VMEM((NSLOT, PCH, 256, PAGE), f32),
                            pltpu.SemaphoreType.DMA((NSLOT,))]),
        out_shape=[shp, shp], compiler_params=_cp(("arbitrary",)), name="page_compress")(
            page_table, wk_t, wv_t, place, cache_k, cache_v)


def _ssel_kernel(qc_ref, kc_ref, vc_ref, kn_ref, vn_ref, w_ref, mrep_ref, oc_ref, neg_ref, *, past, ts):
    nbp = kc_ref.shape[1]
    nb_past = past // NSA_BLOCK
    col = _iota((1, LANES), 1)
    tpos = past + col % 8
    newrow = (_iota((nbp, 1), 0) == nb_past).astype(f32)
    tmask = (_iota((8, 1), 0) < ts).astype(f32)
    for g in range(NSA_KVH):
        sl = slice(g * HD, (g + 1) * HD)
        kc_new = jnp.sum(kn_ref[:, sl] * tmask * w_ref[0, 0:8, :], axis=0, keepdims=True)
        vc_new = jnp.sum(vn_ref[:, sl] * tmask * w_ref[1, 0:8, :], axis=0, keepdims=True)
        kc = (kc_ref[g] + newrow * kc_new).astype(bf16)
        vc = (vc_ref[g] + newrow * vc_new).astype(bf16)
        q = qc_ref[g]
        nrow = _iota((nbp, LANES), 0)
        ok = ((nrow + 1) * NSA_BLOCK - 1) <= tpos
        s = jnp.where(ok, _dot_nt(kc, q), NEG)
        m = jnp.max(s, axis=0, keepdims=True)
        e = jnp.where(ok, jnp.exp(s - m), 0.0)
        l = jnp.sum(e, axis=0, keepdims=True)
        p = e / jnp.where(l > 0.0, l, 1.0)
        oc_ref[g] = _dot_tn(p.astype(bf16), vc)
        imp = _dot3_r(p, mrep_ref[...])
        cur = tpos // NSA_BLOCK
        impm = jnp.where((nrow == cur) | (nrow == 0), BIG, jnp.where(nrow < cur, imp, NEG))

        def pick(_, v):
            mx = jnp.max(v, axis=0, keepdims=True)
            idx = jnp.min(jnp.where(v == mx, nrow, nbp), axis=0, keepdims=True)
            return jnp.where(nrow == idx, GONE, v)

        taken = lax.fori_loop(0, NSA_TOPN, pick, impm) == GONE
        neg_ref[g] = jnp.where(taken & (impm > 0.5 * NEG), 0.0, NEG)


def _sample_select(qc_st, kc_h, vc_h, kn, vn, cmp_w, mrep, past, ts):
    bs, _, nbp, _ = kc_h.shape
    b4 = lambda n, w: pl.BlockSpec((None, NSA_KVH, n, w), lambda b: (b, 0, 0, 0))
    return pl.pallas_call(
        functools.partial(_ssel_kernel, past=past, ts=ts), grid=(bs,),
        in_specs=[b4(LANES, HD), b4(nbp, HD), b4(nbp, HD), pl.BlockSpec((None, 8, 256), lambda b: (b, 0, 0)),
                  pl.BlockSpec((None, 8, 256), lambda b: (b, 0, 0)), _full(cmp_w), _full(mrep)],
        out_specs=[b4(LANES, HD), b4(nbp, LANES)],
        out_shape=[jax.ShapeDtypeStruct((bs, NSA_KVH, LANES, HD), f32), jax.ShapeDtypeStruct((bs, NSA_KVH, nbp, LANES), f32)],
        compiler_params=_cp(("parallel",)), name="sample_select")(qc_st, kc_h, vc_h, kn, vn, cmp_w, mrep)


def _new_rows_mask(ts):
    tt = _iota((LANES, LANES), 0) % 8
    lane = _iota((LANES, LANES), 1)
    return (lane <= tt) & (lane < ts)


def _diag_blocks(o_ref, acc, l):
    nr = LANES // NSA_KVH
    for g in range(NSA_KVH):
        o_ref[g] = acc[g * nr:(g + 1) * nr, g * HD:(g + 1) * HD] / l[g * nr:(g + 1) * nr, :]


def _spage_kernel(pt_ref, q_ref, slab_ref, e_ref, kn_ref, vn_ref, ck_ref, cv_ref, o_ref, kbuf, vbuf, sem, m_s, l_s, acc_s,
                  *, li, npg, ts):
    b = pl.program_id(0)
    nch = npg // PCH

    def copies(c, slot):
        return _chunk_copies(ck_ref, li, pt_ref, b, c, kbuf, slot, sem) + _chunk_copies(cv_ref, li, pt_ref, b, c, vbuf, slot, sem)

    for c0 in range(min(NSLOT - 1, nch)):
        for cp in copies(c0, c0):
            cp.start()
    q = q_ref[...]
    s = jnp.where(_new_rows_mask(ts), _dotf(q, kn_ref[...].astype(bf16)), NEG)
    m = jnp.max(s, axis=1, keepdims=True)
    p = jnp.exp(s - m)
    m_s[...] = m
    l_s[...] = jnp.sum(p, axis=1, keepdims=True)
    acc_s[...] = _dot_nt(p.astype(bf16), vn_ref[...].astype(bf16))

    def body(c, _):
        slot = c % NSLOT
        ahead = c + NSLOT - 1

        @pl.when(ahead < nch)
        def _():
            for cp in copies(ahead, ahead % NSLOT):
                cp.start()

        for cp in copies(c, slot):
            cp.wait()
        bias = _dotf(slab_ref[c], e_ref[...])
        s = jnp.concatenate([_dotf(q, kbuf[slot, p].astype(bf16)) for p in range(PCH)], axis=1) + bias
        m_old = m_s[...]
        m_new = jnp.maximum(m_old, jnp.max(s, axis=1, keepdims=True))
        p = jnp.exp(s - m_new)
        alpha = jnp.exp(m_old - m_new)
        l_s[...] = alpha * l_s[...] + jnp.sum(p, axis=1, keepdims=True)
        pb = p.astype(bf16)
        pv = _dot_nt(pb[:, 0:PAGE], vbuf[slot, 0].astype(bf16))
        for j in range(1, PCH):
            pv = pv + _dot_nt(pb[:, j * PAGE:(j + 1) * PAGE], vbuf[slot, j].astype(bf16))
        acc_s[...] = alpha * acc_s[...] + pv
        m_s[...] = m_new
        return 0

    lax.fori_loop(0, nch, body, 0)
    _diag_blocks(o_ref, acc_s[...], l_s[...])


def _sample_selected(page_table, qbd, slab, expand, kn_t, vn_t, cache_k, cache_v, li, ts):
    bs, npg = page_table.shape
    nch = npg // PCH
    per_b = lambda shp: pl.BlockSpec((None,) + shp, lambda b, pt: (b,) + (0,) * len(shp))
    anyspec = pl.BlockSpec(memory_space=pl.ANY)
    return pl.pallas_call(
        functools.partial(_spage_kernel, li=li, npg=npg, ts=ts),
        grid_spec=pltpu.PrefetchScalarGridSpec(
            num_scalar_prefetch=1, grid=(bs,),
            in_specs=[per_b((LANES, 256)), per_b((nch, LANES, LANES)),
                      pl.BlockSpec((LANES, PCH * PAGE), lambda b, pt: (0, 0)), per_b((256, LANES)), per_b((256, LANES)),
                      anyspec, anyspec],
            out_specs=per_b((NSA_KVH, LANES // NSA_KVH, HD)),
            scratch_shapes=[pltpu.VMEM((NSLOT, PCH, 256, PAGE), f32), pltpu.VMEM((NSLOT, PCH, 256, PAGE), f32),
                            pltpu.SemaphoreType.DMA((NSLOT,)), pltpu.VMEM((LANES, 1), f32), pltpu.VMEM((LANES, 1), f32),
                            pltpu.VMEM((LANES, 256), f32)]),
        out_shape=jax.ShapeDtypeStruct((bs, NSA_KVH, LANES // NSA_KVH, HD), f32),
        compiler_params=_cp(("arbitrary",)), name="sample_selected")(
            page_table, qbd, slab, expand, kn_t, vn_t, cache_k, cache_v)


def _swin_kernel(q_ref, wk_ref, wv_ref, kn_ref, vn_ref, o_ref, wko_ref, wvo_ref, *, ts, wb):
    q = q_ref[...]
    wk, wv, kn, vn = wk_ref[...], wv_ref[...], kn_ref[...], vn_ref[...]
    tt = _iota((LANES, wb), 0) % 8
    s1 = jnp.where(_iota((LANES, wb), 1) > tt + (wb - NSA_WINDOW), _dotf(q, wk.astype(bf16)), NEG)
    s2 = jnp.where(_new_rows_mask(ts), _dotf(q, kn.astype(bf16)), NEG)
    m = jnp.maximum(jnp.max(s1, axis=1, keepdims=True), jnp.max(s2, axis=1, keepdims=True))
    p1, p2 = jnp.exp(s1 - m), jnp.exp(s2 - m)
    l = jnp.sum(p1, axis=1, keepdims=True) + jnp.sum(p2, axis=1, keepdims=True)
    acc = _dot_nt(p1.astype(bf16), wv.astype(bf16)) + _dot_nt(p2.astype(bf16), vn.astype(bf16))
    _diag_blocks(o_ref, acc, l)
    tail = _iota((256, wb), 1) >= wb - ts
    pad = jnp.zeros((256, wb - LANES), f32)
    wko_ref[...] = jnp.where(tail, jnp.concatenate([pad, pltpu.roll(kn, LANES - ts, 1)], axis=1), pltpu.roll(wk, wb - ts, 1))
    wvo_ref[...] = jnp.where(tail, jnp.concatenate([pad, pltpu.roll(vn, LANES - ts, 1)], axis=1), pltpu.roll(wv, wb - ts, 1))


def _sample_window(qbd, win_k, win_v, li, kn_t, vn_t, ts):
    bs, wb = win_k.shape[1], win_k.shape[3]
    wspec = pl.BlockSpec((None, None, 256, wb), lambda b: (li, b, 0, 0))
    per_b = lambda shp: pl.BlockSpec((None,) + shp, lambda b: (b,) + (0,) * len(shp))
    return pl.pallas_call(
        functools.partial(_swin_kernel, ts=ts, wb=wb), grid=(bs,),
        in_specs=[per_b((LANES, 256)), wspec, wspec, per_b((256, LANES)), per_b((256, LANES))],
        out_specs=[per_b((NSA_KVH, LANES // NSA_KVH, HD)), per_b((256, wb)), per_b((256, wb))],
        out_shape=[jax.ShapeDtypeStruct((bs, NSA_KVH, LANES // NSA_KVH, HD), f32), jax.ShapeDtypeStruct((bs, 256, wb), f32),
                   jax.ShapeDtypeStruct((bs, 256, wb), f32)],
        compiler_params=_cp(("parallel",)), name="sample_window")(qbd, win_k, win_v, kn_t, vn_t)


def _gelu_tanh(x):
    return 0.5 * x * (1.0 + jnp.tanh(math.sqrt(2.0 / math.pi) * (x + 0.044715 * (x * x * x))))


def _s5_kernel(u_ref, h0r_ref, h0i_ref, perm_ref, permt_ref, bd_ref, pwr_ref, pwi_ref, cdr_ref, cdi_ref, d_ref, gw_ref,
               gb_ref, y_ref, sr_ref, si_ref, xr_s, xi_s, cr_s, ci_s, *, tc, srow):
    c = pl.program_id(1)

    @pl.when(c == 0)
    def _():
        cr_s[...] = h0r_ref[...]
        ci_s[...] = h0i_ref[...]

    ns = tc // 8
    u = u_ref[...]
    ub = _dotf(perm_ref[...], u.astype(bf16)).astype(bf16)
    hc, hn = S5_CH // 2, S5_N // 2
    for h in range(2):
        uh = ub[:, h * hc:(h + 1) * hc]
        xr_s[:, h * hn:(h + 1) * hn] = _dotf(uh, bd_ref[h * hc:(h + 1) * hc, h * hn:(h + 1) * hn])
        xi_s[:, h * hn:(h + 1) * hn] = _dotf(uh, bd_ref[h * hc:(h + 1) * hc, S5_N + h * hn:S5_N + (h + 1) * hn])
    ar, ai = pwr_ref[0:1, :], pwi_ref[0:1, :]

    def scan(t, carry):
        xr, xi = carry
        rs = pl.multiple_of(t * 8, 8)
        nr = ar * xr - ai * xi + xr_s[pl.ds(rs, 8), :]
        ni = ar * xi + ai * xr + xi_s[pl.ds(rs, 8), :]
        xr_s[pl.ds(rs, 8), :] = nr
        xi_s[pl.ds(rs, 8), :] = ni
        return nr, ni

    zero = jnp.zeros((8, S5_N), f32)
    fr, fi = lax.fori_loop(0, ns, scan, (zero, zero))
    asr, asi = pwr_ref[ns - 1:ns, :], pwi_ref[ns - 1:ns, :]
    cr, ci = cr_s[...], ci_s[...]
    er, ei = [], []
    for s in range(8):
        er.append(cr)
        ei.append(ci)
        cr, ci = asr * cr - asi * ci + fr[s:s + 1, :], asr * ci + asi * cr + fi[s:s + 1, :]
    cr_s[...] = cr
    ci_s[...] = ci
    er, ei = jnp.concatenate(er, axis=0), jnp.concatenate(ei, axis=0)

    def fix(t, _):
        rs = pl.multiple_of(t * 8, 8)
        pr, pi = pwr_ref[pl.ds(t, 1), :], pwi_ref[pl.ds(t, 1), :]
        xr_s[pl.ds(rs, 8), :] += pr * er - pi * ei
        xi_s[pl.ds(rs, 8), :] += pr * ei + pi * er
        return 0

    lax.fori_loop(0, ns, fix, 0)
    sr_ref[...] = xr_s[srow:srow + 8, :]
    si_ref[...] = xi_s[srow:srow + 8, :]
    yh = []
    for h in range(2):
        rows, cols = slice(h * hn, (h + 1) * hn), slice(h * hc, (h + 1) * hc)
        yh.append(_dotf(xr_s[:, rows].astype(bf16), cdr_ref[rows, cols]) + _dotf(xi_s[:, rows].astype(bf16), cdi_ref[rows, cols]))
    y = _dot3_l(permt_ref[...], jnp.concatenate(yh, axis=1)) + d_ref[...] * u
    y = _gelu_tanh(y)
    y_ref[...] = y * _sigmoid(_dotf(y.astype(bf16), gw_ref[...]) + gb_ref[...])


def _s5(u, h0r, h0i, prm, t_valid):
    b, t, _ = u.shape
    tc = min(256, t)
    ns = tc // 8
    assert (t_valid - 1) // tc == t // tc - 1
    r = (t_valid - 1) % tc
    prow = (r % ns) * 8 + r // ns
    srow = prow // 8 * 8
    rid = jnp.arange(tc)
    perm = (rid[None, :] == (rid[:, None] % 8) * ns + rid[:, None] // 8).astype(bf16)
    st = pl.BlockSpec((None, 8, S5_N), lambda b, c: (b, 0, 0))
    h0 = pl.BlockSpec((None, 1, S5_N), lambda b, c: (b, 0, 0))
    y, sr, si = pl.pallas_call(
        functools.partial(_s5_kernel, tc=tc, srow=srow), grid=(b, t // tc),
        in_specs=[pl.BlockSpec((None, tc, S5_CH), lambda b, c: (b, c, 0)), h0, h0, _full(perm), _full(perm)]
        + [_full(p) for p in prm],
        out_specs=[pl.BlockSpec((None, tc, S5_CH), lambda b, c: (b, c, 0)), st, st],
        out_shape=[jax.ShapeDtypeStruct((b, t, S5_CH), f32), jax.ShapeDtypeStruct((b, 8, S5_N), f32),
                   jax.ShapeDtypeStruct((b, 8, S5_N), f32)],
        scratch_shapes=[pltpu.VMEM((tc, S5_N), f32), pltpu.VMEM((tc, S5_N), f32), pltpu.VMEM((1, S5_N), f32),
                        pltpu.VMEM((1, S5_N), f32)],
        compiler_params=_cp(("parallel", "arbitrary")), name="s5")(u, h0r, h0i, perm, perm.T, *prm)
    return y, sr[:, prow % 8], si[:, prow % 8]


def _gla_kernel(q_ref, k_ref, v_ref, g_ref, sm_ref, s0_ref, wa_ref, ba_ref, gn_ref, o_ref, sn_ref, st, *, nc, t_valid):
    L = CHUNK
    c = pl.program_id(1)

    @pl.when(c == 0)
    def _():
        st[...] = s0_ref[...]

    x = _dotf(sm_ref[...].astype(bf16), wa_ref[...]) + ba_ref[...]
    la = (jnp.minimum(x, 0.0) - jnp.log1p(jnp.exp(-jnp.abs(x)))) * (1.0 / GLA_TEMP)
    k = k_ref[...]
    if t_valid < nc * L:
        valid = c * L + _iota((L, 1), 0) < t_valid
        la = jnp.where(valid, la, 0.0)
        k = jnp.where(valid, k, 0.0)
    causal = _tri(L)
    b = _dot3_l(causal.astype(bf16), la)
    bl = b[L - 1:L, :]
    qe = (q_ref[...] * (GLA_DK ** -0.5) * jnp.exp(b)).astype(bf16)
    ke = (k * jnp.exp(-b)).astype(bf16)
    kl = (k * jnp.exp(bl - b)).astype(bf16)
    ebl = jnp.exp(bl)
    v, gg, gn = v_ref[...], g_ref[...], gn_ref[...]
    for h in range(GLA_HEADS):
        sl = slice(h * GLA_DK, (h + 1) * GLA_DK)
        vl = slice(h * GLA_DV, (h + 1) * GLA_DV)
        att = jnp.where(causal, _dot_nt(qe[:, sl], ke[:, sl]), 0.0)
        vh = v[:, vl].astype(bf16)
        s_t = st[h]
        o = _dotf(att.astype(bf16), vh) + _dot_nt(qe[:, sl], s_t.astype(bf16))
        st[h] = s_t * ebl[:, sl] + _dot_tn(vh, kl[:, sl])
        ms = jnp.mean(o * o, axis=-1, keepdims=True)
        gh = gg[:, vl]
        o_ref[:, vl] = o * lax.rsqrt(ms + EPS) * gn * (gh * _sigmoid(gh))

    @pl.when(c == nc - 1)
    def _():
        sn_ref[...] = st[...]


def _gla(q, k, v, g, sm, s0t, prm, t_valid):
    b, t, _ = q.shape
    nc = t // CHUNK
    tok = lambda w: pl.BlockSpec((None, CHUNK, w), lambda b, c: (b, c, 0))
    st = pl.BlockSpec((None, GLA_HEADS, GLA_DV, GLA_DK), lambda b, c: (b, 0, 0, 0))
    return pl.pallas_call(
        functools.partial(_gla_kernel, nc=nc, t_valid=t_valid), grid=(b, nc),
        in_specs=[tok(256), tok(256), tok(512), tok(512), tok(LANES), st] + [_full(p) for p in prm],
        out_specs=[tok(512), st],
        out_shape=[jax.ShapeDtypeStruct((b, t, 512), f32), jax.ShapeDtypeStruct((b, GLA_HEADS, GLA_DV, GLA_DK), f32)],
        scratch_shapes=[pltpu.VMEM((GLA_HEADS, GLA_DV, GLA_DK), f32)],
        compiler_params=_cp(("parallel", "arbitrary")), name="gla")(q, k, v, g, sm, s0t, *prm)


def _router_kernel(x_ref, sh_ref, sc_ref, g_ref, w_ref, b_ref, h_ref, route_ref, cnt_ref):
    first = (pl.program_id(0) == 0) & (pl.program_id(1) == 0)
    h = _modulate(x_ref[...], sh_ref[...], sc_ref[...], g_ref[...])
    hb = lax.bitcast_convert_type(h.astype(bf16).astype(f32), jnp.uint32)
    h_ref[...] = (hb[:, :D // 2] >> 16) | hb[:, D // 2:]
    hh, hl = _split2(h)
    wh, wl = _split2(w_ref[...])
    lg = _dotf(hh, wh) + _dotf(hl, wh) + _dotf(hh, wl) + b_ref[...]
    lane = _iota(lg.shape, 1)
    big = 4 * LANES
    coarse = lane < MOE_GROUPS
    lc = jnp.where(coarse, lg, GONE)
    mx = jnp.max(lc, axis=1, keepdims=True)
    gsel = jnp.min(jnp.where(lc == mx, lane, big), axis=1, keepdims=True)
    gc = 1.0 / jnp.sum(jnp.where(coarse, jnp.exp(lg - mx), 0.0), axis=1, keepdims=True)
    base = MOE_GROUPS + gsel * MOE_PER_GROUP
    fine = (lane >= base) & (lane < base + MOE_PER_GROUP)
    mf = jnp.max(jnp.where(fine, lg, GONE), axis=1, keepdims=True)
    ef = jnp.where(fine, jnp.exp(lg - mf), 0.0)
    pf = ef / jnp.sum(ef, axis=1, keepdims=True)
    cand = jnp.where(fine, pf, -1.0)
    v1 = jnp.max(cand, axis=1, keepdims=True)
    i1 = jnp.min(jnp.where(cand == v1, lane, big), axis=1, keepdims=True)
    cand = jnp.where(lane == i1, -1.0, cand)
    v2 = jnp.max(cand, axis=1, keepdims=True)
    i2 = jnp.min(jnp.where(cand == v2, lane, big), axis=1, keepdims=True)
    e1, e2 = i1 - MOE_GROUPS, i2 - MOE_GROUPS
    w1, w2 = gc * v1 / (v1 + v2), gc * v2 / (v1 + v2)
    route_ref[...] = jnp.where(lane == 0, e1.astype(f32), jnp.where(lane == 1, e2.astype(f32),
                               jnp.where(lane == 2, w1, jnp.where(lane == 3, w2, 0.0))))
    cnt = jnp.sum((lane == e1).astype(f32) + (lane == e2).astype(f32), axis=0, keepdims=True)

    @pl.when(first)
    def _():
        cnt_ref[...] = jnp.zeros_like(cnt_ref)

    cnt_ref[...] += jnp.broadcast_to(cnt, cnt_ref.shape)


def _router(x, sh, sc, g, wr, br):
    b, t, _ = x.shape
    tm = min(256, t)
    return pl.pallas_call(
        _router_kernel, grid=(b, t // tm),
        in_specs=[pl.BlockSpec((None, tm, D), lambda b, i: (b, i, 0)), _mod_spec(sh, tm), _mod_spec(sc, tm),
                  pl.BlockSpec((1, D), lambda b, i: (0, 0)), _full(wr), _full(br)],
        out_specs=[pl.BlockSpec((None, tm, D // 2), lambda b, i: (b, i, 0)), pl.BlockSpec((None, tm, LANES), lambda b, i: (b, i, 0)),
                   pl.BlockSpec((8, LANES), lambda b, i: (0, 0))],
        out_shape=[jax.ShapeDtypeStruct((b, t, D // 2), jnp.uint32), jax.ShapeDtypeStruct((b, t, LANES), f32),
                   jax.ShapeDtypeStruct((8, LANES), f32)],
        compiler_params=_cp(("arbitrary", "arbitrary")), name="moe_router")(x, sh, sc, g, wr, br)


def _plan_kernel(route_ref, ps_ref, dest_ref, run_s, *, tm):
    @pl.when(pl.program_id(0) == 0)
    def _():
        run_s[...] = jnp.zeros_like(run_s)

    route = route_ref[...]
    lane = _iota((tm, LANES), 1).astype(f32)
    oh1 = (lane == route[:, 0:1]).astype(f32)
    oh2 = (lane == route[:, 1:2]).astype(f32)
    tot = oh1 + oh2
    strict = (_iota((tm, tm), 0) > _iota((tm, tm), 1)).astype(bf16)
    before = _dotf(strict, tot.astype(bf16)) + run_s[0:1, :] + ps_ref[0:1, :]
    d1 = jnp.sum(oh1 * before, axis=1, keepdims=True)
    d2 = jnp.sum(oh2 * before, axis=1, keepdims=True)
    lane_i = _iota((tm, LANES), 1)
    dest_ref[...] = jnp.where(lane_i == 0, d1, jnp.where(lane_i == 1, d2, 0.0)).astype(i32)
    run_s[...] += jnp.broadcast_to(jnp.sum(tot, axis=0, keepdims=True), run_s.shape)


def _plan(route, pstart):
    n = route.shape[0]
    tm = min(256, n)
    return pl.pallas_call(
        functools.partial(_plan_kernel, tm=tm), grid=(n // tm,),
        in_specs=[pl.BlockSpec((tm, LANES), lambda i: (i, 0)), pl.BlockSpec((8, LANES), lambda i: (0, 0))],
        out_specs=pl.BlockSpec((tm, LANES), lambda i: (i, 0)),
        out_shape=jax.ShapeDtypeStruct((n, LANES), i32),
        scratch_shapes=[pltpu.VMEM((8, LANES), f32)],
        compiler_params=_cp(("arbitrary",)), name="moe_plan")(route, pstart)


def _dispatch_kernel(dest_ref, h_ref, xs_in, xs_out, sem, *, tm):
    del xs_in

    def row_copy(r, d):
        return pltpu.make_async_copy(h_ref.at[pl.ds(r, 1), :], xs_out.at[pl.ds(d, 1), :], sem)

    def issue(r, _):
        row_copy(r, dest_ref[2 * r]).start()
        row_copy(r, dest_ref[2 * r + 1]).start()
        return 0

    lax.fori_loop(0, tm, issue, 0, unroll=8)
    for _ in range(2):
        pltpu.make_async_copy(h_ref, xs_out.at[pl.ds(0, tm), :], sem).wait()


def _dispatch(h, dest_flat, nrows):
    n, w = h.shape
    tm = min(MOE_MOVE_ROWS, n)
    return pl.pallas_call(
        functools.partial(_dispatch_kernel, tm=tm), grid=(n // tm,),
        in_specs=[pl.BlockSpec((2 * tm,), lambda i: (i,), memory_space=pltpu.SMEM),
                  pl.BlockSpec((tm, w), lambda i: (i, 0)), pl.BlockSpec(memory_space=pl.ANY)],
        out_specs=pl.BlockSpec(memory_space=pl.ANY),
        out_shape=jax.ShapeDtypeStruct((nrows, w), h.dtype),
        scratch_shapes=[pltpu.SemaphoreType.DMA(())],
        input_output_aliases={2: 0},
        compiler_params=_cp(("arbitrary",)), name="moe_dispatch")(dest_flat, h, jnp.zeros((nrows, w), h.dtype))


def _ffn_kernel(be_ref, nu_ref, x_ref, w1_ref, w3_ref, w2_ref, o_ref):
    i = pl.program_id(0)

    @pl.when(i < nu_ref[0])
    def _():
        xp = x_ref[...]
        x_lo = lax.bitcast_convert_type(xp << 16, f32).astype(bf16)
        x_hi = lax.bitcast_convert_type(xp & jnp.uint32(0xFFFF0000), f32).astype(bf16)
        half = D // 2
        a = _dotf(x_lo, w1_ref[0:half, :].astype(bf16)) + _dotf(x_hi, w1_ref[half:, :].astype(bf16))
        b = _dotf(x_lo, w3_ref[0:half, :].astype(bf16)) + _dotf(x_hi, w3_ref[half:, :].astype(bf16))
        hid = ((a * _sigmoid(a)) * b).astype(bf16)
        o_ref[...] = _dotf(hid, w2_ref[...].astype(bf16))

    @pl.when(i >= nu_ref[0])
    def _():
        o_ref[...] = jnp.zeros_like(o_ref)


def _ffn(xs, blk_e, nused, w1, w3, w2, l):
    nblk = xs.shape[0] // MOE_ROWS
    return pl.pallas_call(
        _ffn_kernel,
        grid_spec=pltpu.PrefetchScalarGridSpec(
            num_scalar_prefetch=2, grid=(nblk,),
            in_specs=[pl.BlockSpec((MOE_ROWS, D // 2), lambda i, be, nu: (i, 0)),
                      pl.BlockSpec((None, None, D, MOE_FF), lambda i, be, nu: (l, be[i], 0, 0)),
                      pl.BlockSpec((None, None, D, MOE_FF), lambda i, be, nu: (l, be[i], 0, 0)),
                      pl.BlockSpec((None, None, MOE_FF, D), lambda i, be, nu: (l, be[i], 0, 0))],
            out_specs=pl.BlockSpec((MOE_ROWS, D), lambda i, be, nu: (i, 0))),
        out_shape=jax.ShapeDtypeStruct((xs.shape[0], D), f32),
        compiler_params=_cp(("arbitrary",)), name="moe_ffn")(blk_e, nused, xs, w1, w3, w2)


def _combine_kernel(dest_ref, x_ref, gate_ref, route_ref, ys_ref, o_ref, buf_a, buf_b, sem, *, tm):
    def row_copy(d, buf, r):
        return pltpu.make_async_copy(ys_ref.at[pl.ds(d, 1), :], buf.at[pl.ds(r, 1), :], sem)

    def issue(r, _):
        row_copy(dest_ref[2 * r], buf_a, r).start()
        row_copy(dest_ref[2 * r + 1], buf_b, r).start()
        return 0

    lax.fori_loop(0, tm, issue, 0, unroll=8)
    for buf in (buf_a, buf_b):
        pltpu.make_async_copy(ys_ref.at[pl.ds(0, tm), :], buf, sem).wait()
    route = route_ref[...]
    o_ref[...] = x_ref[...] + gate_ref[...] * (route[:, 2:3] * buf_a[...] + route[:, 3:4] * buf_b[...])


def _combine(x, gate, route, ys, dest_flat):
    b, t, _ = x.shape
    tm = min(MOE_MOVE_ROWS, t)
    nt = t // tm
    return pl.pallas_call(
        functools.partial(_combine_kernel, tm=tm), grid=(b, nt),
        in_specs=[pl.BlockSpec((2 * tm,), lambda b, i: (b * nt + i,), memory_space=pltpu.SMEM),
                  pl.BlockSpec((None, tm, D), lambda b, i: (b, i, 0)), _mod_spec(gate, tm),
                  pl.BlockSpec((None, tm, LANES), lambda b, i: (b, i, 0)), pl.BlockSpec(memory_space=pl.ANY)],
        out_specs=pl.BlockSpec((None, tm, D), lambda b, i: (b, i, 0)),
        out_shape=jax.ShapeDtypeStruct((b, t, D), f32),
        scratch_shapes=[pltpu.VMEM((tm, D), f32), pltpu.VMEM((tm, D), f32), pltpu.SemaphoreType.DMA(())],
        compiler_params=_cp(("arbitrary", "arbitrary")), name="moe_combine")(dest_flat, x, gate, route, ys)


def _moe(x, sh, sc, gate, g, wr, br, w1, w3, w2, l):
    b, t, _ = x.shape
    n = b * t
    h, route, cnt = _router(x, sh, sc, g, wr, br)
    counts = cnt[0, :MOE_EXPERTS].astype(i32)
    pcounts = (counts + MOE_ROWS - 1) // MOE_ROWS * MOE_ROWS
    pends = jnp.cumsum(pcounts)
    pstarts = pends - pcounts
    nblk = -(-2 * n // MOE_ROWS) + MOE_EXPERTS
    blk_start = jnp.arange(nblk, dtype=i32) * MOE_ROWS
    blk_e = jnp.minimum(jnp.sum((pends[None, :] <= blk_start[:, None]).astype(i32), axis=1), MOE_EXPERTS - 1)
    nused = (pends[-1:] // MOE_ROWS).astype(i32)
    ps = jnp.zeros((8, LANES), f32).at[:, :MOE_EXPERTS].set(pstarts.astype(f32)[None])
    dest = _plan(route.reshape(n, LANES), ps)
    dest_flat = dest[:, :2].reshape(2 * n)
    xs = _dispatch(h.reshape(n, D // 2), dest_flat, nblk * MOE_ROWS)
    ys = _ffn(xs, blk_e, nused, w1, w3, w2, l)
    return _combine(x, gate, route, ys, dest_flat)


def _rope_tables(pos):
    half = HD // 2
    inv = ROPE_THETA ** (-jnp.arange(half, dtype=f32) / half)
    ang = pos.astype(f32)[:, None] * inv[None, :]
    cos, sin = jnp.cos(ang), jnp.sin(ang)
    return jnp.tile(jnp.concatenate([cos, cos], axis=1), (1, 2)), jnp.tile(jnp.concatenate([-sin, sin], axis=1), (1, 2))


def _seg_mats(width):
    nseg = width // HD
    seg = (jnp.arange(width)[:, None] // HD == jnp.arange(LANES)[None, :]).astype(bf16)
    return seg, seg.T


def _even_params(i, ev_w_in, ev_w_out, ev_conv_w, ev_conv_b, ev_dt_bias, ev_a_log, ev_d_skip, ev_ssd_norm, ev_q_norm,
                 ev_k_norm, ev_cmp_w):
    w = ev_w_in[i]
    o = [0, 1024, 2560, 2576, 3600, 5136, 5184]
    small = jnp.concatenate([w[:, o[2]:o[3]], w[:, o[5]:o[6]], jnp.zeros((D, 64), f32)], axis=1)
    ws = [w[:, o[0]:o[1]], w[:, o[1]:o[2]], w[:, o[3]:o[4]], w[:, o[4]:o[5]], small]
    pad = lambda v: jnp.zeros((1, LANES), f32).at[0, :v.shape[0]].set(v)
    ssd = [ev_conv_w[i], ev_conv_b[i][None], pad(ev_dt_bias[i]), pad(-jnp.exp(ev_a_log[i])),
           jnp.repeat(ev_d_skip[i], SSD_HD)[None], ev_ssd_norm[i][None]]
    segq, expq = _seg_mats(D)
    segk, expk = _seg_mats(1536)
    kn = ev_k_norm[i]
    z4 = jnp.zeros((256,), f32)
    gk = jnp.concatenate([jnp.tile(kn[0], 4), z4, jnp.tile(kn[1], 4), z4, jnp.tile(kn[2], 4), z4])[None]
    vm = jnp.concatenate([z4, z4 + 1, z4, z4 + 1, z4, z4 + 1])[None]
    cw = jnp.tile(ev_cmp_w[i], (1, 1, 4))
    prep = [segq, expq, segk, expk, jnp.tile(ev_q_norm[i], 16)[None], gk, vm, cw]
    lane = jnp.arange(LANES)[None, None, :, None]
    col = jnp.arange(256)[None, None, None, :]
    gidx = jnp.arange(NSA_KVH)[:, None, None, None]
    jidx = jnp.arange(3)[None, :, None, None]
    gexp = (lane == 16 + ((gidx * NSA_REP + col // HD) * 3 + jidx)).astype(bf16)
    wo = ev_w_out[i]
    return dict(ws=[a.astype(bf16) for a in ws], ssd=ssd, prep=prep, gexp=gexp, cmp_w=ev_cmp_w[i],
                wo=[wo[:SSD_INNER].astype(bf16), wo[SSD_INNER:].astype(bf16)])


def _odd_params(i, od_w_in, od_w_out, a_re, a_im, log_dt, b_re, b_im, c_re, c_im, d, glu_w, glu_b, wa2, ba, gnorm):
    w = od_w_in[i]
    o = [0, 512, 768, 1024, 1536, 2048, 2064]
    small = jnp.concatenate([w[:, o[5]:o[6]], jnp.zeros((D, LANES - GLA_RANK), f32)], axis=1)
    ws = [w[:, o[k]:o[k + 1]] for k in range(5)] + [small]
    are, aim = a_re[i], a_im[i]
    dt = jnp.exp(log_dt[i])[:, None]
    lr, li = are * dt, aim * dt
    ab_re, ab_im = jnp.exp(lr) * jnp.cos(li), jnp.exp(lr) * jnp.sin(li)
    den = are * are + aim * aim
    nr = ab_re - 1.0
    f_re = (nr * are + ab_im * aim) / den
    f_im = (ab_im * are - nr * aim) / den
    bb_re = f_re[..., None] * b_re[i] - f_im[..., None] * b_im[i]
    bb_im = f_re[..., None] * b_im[i] + f_im[..., None] * b_re[i]
    eye = jnp.eye(S5_GROUPS, dtype=f32)
    bdiag = lambda m: jnp.einsum('gpc,gh->gchp', m, eye).reshape(S5_CH, S5_N)
    cdiag = lambda m: jnp.einsum('gcp,gh->gphc', m, eye).reshape(S5_N, S5_CH)
    bd = jnp.concatenate([bdiag(bb_re), bdiag(bb_im)], axis=1).astype(bf16)
    npow = 32
    kk = jnp.arange(1, npow + 1, dtype=f32)[:, None, None]
    mag, ang = jnp.exp(lr[None] * kk), li[None] * kk
    pwr, pwi = (mag * jnp.cos(ang)).reshape(npow, S5_N), (mag * jnp.sin(ang)).reshape(npow, S5_N)
    s5 = [bd, pwr, pwi, cdiag(c_re[i]).astype(bf16), cdiag(-c_im[i]).astype(bf16), d[i][None],
          glu_w[i].astype(bf16), glu_b[i][None]]
    wa = jnp.zeros((LANES, 256), f32).at[:GLA_RANK].set(wa2[i]).astype(bf16)
    gla = [wa, ba[i][None], gnorm[i][None]]
    wo = od_w_out[i]
    return dict(ws=[a.astype(bf16) for a in ws], s5=s5, gla=gla, wo=[wo[:S5_CH].astype(bf16), wo[S5_CH:].astype(bf16)])


def _pad_t(a, t):
    return jnp.pad(a, ((0, 0), (0, t - a.shape[1])) + ((0, 0),) * (a.ndim - 2))


def _even_prompt(x, sh, sc, gate, g, prm):
    b, t, _ = x.shape
    z, xbc, q, kv, sm = _mod_proj(x, sh, sc, g, prm['ws'], "even_in")
    ya, convn, ssdn = _ssd(xbc, sm, z, jnp.zeros((b, 8, SSD_CONV_DIM), f32),
                           jnp.zeros((b, SSD_HEADS, SSD_HD, SSD_STATE), f32), prm['ssd'], t)
    cos_t, sin_t = _rope_tables(jnp.arange(t))
    (qc, qr, kcmp, vcmp, ksel, vsel, kwin, vwin, gates, kaug, vsa, kwh, vwa, kch, vch, kn2) = _nsa_prep(
        q, kv, sm, cos_t, sin_t, prm['prep'], True)
    nb = t // NSA_BLOCK
    padb = lambda a: jnp.pad(a, ((0, 0), (0, 0), (0, LANES - nb), (0, 0)))
    knorm = jnp.sqrt(jnp.max(kn2[:, :, 0, :2 * NSA_KVH], axis=1)).reshape(b, 2, NSA_KVH)
    kbound = jnp.zeros((b, NSA_KVH, 8, LANES), f32).at[:, :, :, 0:2].set(jnp.swapaxes(knorm, 1, 2)[:, :, None, :])
    ob = _nsa_prompt(qc, qr, padb(kch), padb(vch), kaug, vsa, kwh, vwa, gates, prm['gexp'], kbound)
    xn = _out_proj(x, gate, [ya, ob], prm['wo'], "even_out")
    r5 = lambda a: a.reshape(b, t, NSA_KVH, HD)
    keep = min(NSA_WINDOW, t)
    st = (r5(kcmp), r5(vcmp), r5(ksel), r5(vsel), r5(kwin)[:, t - keep:], r5(vwin)[:, t - keep:], ssdn, convn[:, 5:8])
    return xn, st


def _stack_q(qb, bs, ts):
    q = qb.reshape(bs, ts, NSA_KVH, NSA_REP, HD)
    q = jnp.pad(q, ((0, 0), (0, 8 - ts), (0, 0), (0, 0), (0, 0)))
    q = q.transpose(0, 2, 3, 1, 4).reshape(bs, NSA_KVH, NSA_REP * 8, HD)
    return jnp.pad(q, ((0, 0), (0, 0), (0, LANES - NSA_REP * 8), (0, 0)))


def _unstack_o(o, bs, ts):
    o = o[:, :, :NSA_REP * 8].reshape(bs, NSA_KVH, NSA_REP, 8, HD)[:, :, :, :ts]
    return o.transpose(0, 3, 1, 2, 4).reshape(1, bs * ts, D)


def _page_major(cache):
    l, p, r, h, d = cache.shape
    return jnp.transpose(cache, (0, 1, 3, 4, 2)).reshape(l, p, h * d, r)


def _even_sample(x, sh, sc, gate, g, prm, bs, ts, li, conv_state, ssd_state, page_table, cmp_k, cmp_v, sel_k, sel_v,
                 win_k, win_v):
    n = bs * ts
    npg = page_table.shape[1]
    past = npg * PAGE
    z, xbc, q, kv, sm = _mod_proj(x, sh, sc, g, prm['ws'], "even_in_s")
    seq = lambda a: _pad_t(a.reshape(bs, ts, a.shape[-1]), CHUNK)
    conv0 = jnp.pad(conv_state, ((0, 0), (5, 0), (0, 0)))
    ya, convn, ssdn = _ssd(seq(xbc), seq(sm), seq(z), conv0, ssd_state, prm['ssd'], ts)
    ya = ya[:, :ts].reshape(1, n, SSD_INNER)
    cos_t, sin_t = _rope_tables(past + jnp.arange(n) % ts)
    qc, qr, kcmp, vcmp, ksel, vsel, kwin, vwin, gates = _nsa_prep(q, kv, sm, cos_t, sin_t, prm['prep'], False)
    r8 = lambda a: _pad_t(a.reshape(bs, ts, 256), 8)
    qc_st = _stack_q(qc, bs, ts)
    nr = LANES // NSA_KVH
    qbd = jnp.einsum('bgid,gh->bgihd', _stack_q(qr, bs, ts)[:, :, :nr], jnp.eye(NSA_KVH, dtype=bf16)).reshape(bs, LANES, 256)
    rows_t = lambda a: jnp.pad(jnp.swapaxes(r8(a), 1, 2), ((0, 0), (0, 0), (0, LANES - 8)))
    cw = prm['cmp_w']
    w_t = lambda w: jnp.tile(w.T, (NSA_KVH, PAGE // NSA_BLOCK))
    kct, vct = _page_compress(page_table, _page_major(cmp_k), _page_major(cmp_v), li, w_t(cw[0]), w_t(cw[1]))
    nb_past = 2 * npg
    nbp = -(-(nb_past + 1) // 8) * 8
    hmaj = lambda a: jnp.pad(
        a[..., :2 * PCH].reshape(bs, -1, NSA_KVH, HD, 2 * PCH).transpose(0, 2, 1, 4, 3).reshape(bs, NSA_KVH, nb_past, HD),
        ((0, 0), (0, 0), (0, nbp - nb_past), (0, 0)))
    cidx = jnp.arange(LANES)
    mrep = ((cidx[:, None] < NSA_REP * 8) & (cidx[None, :] < NSA_REP * 8)
            & (cidx[:, None] % 8 == cidx[None, :] % 8)).astype(bf16)
    oc, neg = _sample_select(qc_st, hmaj(kct), hmaj(vct), r8(kcmp), r8(vcmp), cw, mrep, past, ts)
    nch, bpc = npg // PCH, 2 * PCH
    slab = neg[:, :, :nb_past, :nr].reshape(bs, NSA_KVH, nch, bpc, nr).transpose(0, 2, 1, 4, 3).reshape(bs, nch, LANES, bpc)
    slab = jnp.pad(slab, ((0, 0), (0, 0), (0, 0), (0, LANES - bpc))).astype(bf16)
    expand = (jnp.arange(LANES)[:, None] == jnp.arange(PCH * PAGE)[None, :] // NSA_BLOCK).astype(bf16)
    o_s = _sample_selected(page_table, qbd, slab, expand, rows_t(ksel), rows_t(vsel), _page_major(sel_k),
                           _page_major(sel_v), li, ts)
    o_w, wkt, wvt = _sample_window(qbd, _page_major(win_k), _page_major(win_v), li, rows_t(kwin), rows_t(vwin), ts)
    unmajor = lambda a: a.reshape(bs, NSA_KVH, HD, -1).transpose(0, 3, 1, 2)
    wkn, wvn = unmajor(wkt), unmajor(wvt)
    gt = gates[0, :, 16:16 + 3 * NSA_HEADS].reshape(1, n, NSA_HEADS, 3)
    gx = lambda j: jnp.repeat(gt[..., j], HD, axis=-1)
    ob = gx(0) * _unstack_o(oc, bs, ts) + gx(1) * _unstack_o(o_s, bs, ts) + gx(2) * _unstack_o(o_w, bs, ts)
    xn = _out_proj(x, gate, [ya, ob], prm['wo'], "even_out_s")
    r5 = lambda a: a.reshape(bs, ts, NSA_KVH, HD)
    st = (r5(kcmp), r5(vcmp), r5(ksel), r5(vsel), wkn, wvn, ssdn, convn[:, 5:8])
    return xn, st


def _odd_layer(x, sh, sc, gate, g, prm, bs, ts, s5r0, s5i0, gla0):
    u, q, k, v, gg, sm = _mod_proj(x, sh, sc, g, prm['ws'], "odd_in")
    tp = -(-ts // CHUNK) * CHUNK
    seq = lambda a: _pad_t(a.reshape(bs, ts, a.shape[-1]), tp)
    yc, sr, si = _s5(seq(u), s5r0.reshape(bs, 1, S5_N), s5i0.reshape(bs, 1, S5_N), prm['s5'], ts)
    og, gn = _gla(seq(q), seq(k), seq(v), seq(gg), seq(sm), jnp.swapaxes(gla0, 2, 3), prm['gla'], ts)
    unseq = lambda a: a[:, :ts].reshape(x.shape[0], x.shape[1], a.shape[-1])
    xn = _out_proj(x, gate, [unseq(yc), unseq(og)], prm['wo'], "odd_out")
    st = (sr.reshape(bs, S5_GROUPS, S5_STATE), si.reshape(bs, S5_GROUPS, S5_STATE), jnp.swapaxes(gn, 2, 3))
    return xn, st


def kernel(x_prompt, x_sample, cache_cmp_k, cache_cmp_v, cache_sel_k, cache_sel_v, cache_win_k, cache_win_v, state_ssd, state_conv, state_s5_re, state_s5_im, state_gla, page_table, c_prompt, c_sample, ada_w, ada_b, norm_mix, norm_ffn, ev_w_in, ev_w_out, ev_conv_w, ev_conv_b, ev_dt_bias, ev_a_log, ev_d_skip, ev_ssd_norm, ev_q_norm, ev_k_norm, ev_cmp_w, od_w_in, od_w_out, od_s5_a_re, od_s5_a_im, od_s5_log_dt, od_s5_b_re, od_s5_b_im, od_s5_c_re, od_s5_c_im, od_s5_d, od_glu_w, od_glu_b, od_gla_wa2, od_gla_ba, od_gla_norm, moe_wc, moe_bc, moe_wf, moe_bf, moe_w1, moe_w3, moe_w2):
    bp, tp, _ = x_prompt.shape
    bs, ts, _ = x_sample.shape
    ns = bs * ts
    depth = ada_w.shape[0]
    bc = -(-(bp + bs) // 8) * 8
    c_all = jnp.zeros((bc, D), f32).at[:bp].set(c_prompt).at[bp:bp + bs].set(c_sample)
    mods = _ada(c_all, ada_w, ada_b)
    xp, xs = x_prompt, x_sample.reshape(1, ns, D)
    sp = {}
    ss = {}
    for l in range(depth):
        i = l // 2
        mp = [m[:, None, :] for m in jnp.split(mods[l, :bp], 6, axis=-1)]
        ms = [jnp.repeat(m, ts, axis=0)[None] for m in jnp.split(mods[l, bp:bp + bs], 6, axis=-1)]
        gm, gf = norm_mix[l][None], norm_ffn[l][None]
        if l % 2 == 0:
            prm = _even_params(i, ev_w_in, ev_w_out, ev_conv_w, ev_conv_b, ev_dt_bias, ev_a_log, ev_d_skip, ev_ssd_norm,
                               ev_q_norm, ev_k_norm, ev_cmp_w)
            xp, st_p = _even_prompt(xp, mp[0], mp[1], mp[2], gm, prm)
            xs, st_s = _even_sample(xs, ms[0], ms[1], ms[2], gm, prm, bs, ts, i, state_conv[i], state_ssd[i], page_table,
                                    cache_cmp_k, cache_cmp_v, cache_sel_k, cache_sel_v, cache_win_k, cache_win_v)
            names = ('cmp_k', 'cmp_v', 'sel_k', 'sel_v', 'win_k', 'win_v', 'ssd', 'conv')
        else:
            prm = _odd_params(i, od_w_in, od_w_out, od_s5_a_re, od_s5_a_im, od_s5_log_dt, od_s5_b_re, od_s5_b_im,
                              od_s5_c_re, od_s5_c_im, od_s5_d, od_glu_w, od_glu_b, od_gla_wa2, od_gla_ba, od_gla_norm)
            zs = jnp.zeros((bp, S5_GROUPS, S5_STATE), f32)
            xp, st_p = _odd_layer(xp, mp[0], mp[1], mp[2], gm, prm, bp, tp, zs, zs,
                                  jnp.zeros((bp, GLA_HEADS, GLA_DK, GLA_DV), f32))
            xs, st_s = _odd_layer(xs, ms[0], ms[1], ms[2], gm, prm, bs, ts, state_s5_re[i], state_s5_im[i], state_gla[i])
            names = ('s5_re', 's5_im', 'gla')
        for nm, a_p, a_s in zip(names, st_p, st_s):
            sp.setdefault(nm, []).append(a_p)
            ss.setdefault(nm, []).append(a_s)
        wr = jnp.zeros((D, LANES), f32).at[:, :MOE_GROUPS].set(moe_wc[l]).at[:, MOE_GROUPS:MOE_GROUPS + MOE_EXPERTS].set(moe_wf[l])
        br = jnp.zeros((1, LANES), f32).at[0, :MOE_GROUPS].set(moe_bc[l]).at[0, MOE_GROUPS:MOE_GROUPS + MOE_EXPERTS].set(moe_bf[l])
        xp = _moe(xp, mp[3], mp[4], mp[5], gf, wr, br, moe_w1, moe_w3, moe_w2, l)
        xs = _moe(xs, ms[3], ms[4], ms[5], gf, wr, br, moe_w1, moe_w3, moe_w2, l)
    order = ('cmp_k', 'cmp_v', 'sel_k', 'sel_v', 'win_k', 'win_v', 'ssd', 'conv', 's5_re', 's5_im', 'gla')
    outs = [xp, xs.reshape(bs, ts, D)]
    for nm in order:
        outs += [jnp.stack(sp[nm]), jnp.stack(ss[nm])]
    return tuple(outs)
```
